```python
import math, functools
import jax
import jax.numpy as jnp
from jax import lax
import numpy as np

D_MODEL = 1024
BATCH = 4
SEQ = 4096
DEPTH = 2
DEC_BATCH = 32
DEC_SEQ = 4
PAST_LEN = 16384
PAGE_SIZE = 128

MIX_WIDTH = D_MODEL
GROUP_WIDTH = MIX_WIDTH // 4
CONV_W = 4
MOBA_HEADS = 4
MOBA_DH = GROUP_WIDTH // MOBA_HEADS
MOBA_BLOCK = 256
MOBA_TOPK = 3
MOBA_QBLOCK = 128
SSM_HEADS = 4
SSM_HEADDIM = GROUP_WIDTH // SSM_HEADS
SSM_GROUPS = 2
SSM_DSTATE = 128
SSM_CHUNK = 128
SSM_CONV_DIM = GROUP_WIDTH + 2 * SSM_GROUPS * SSM_DSTATE
HGRN_HEADS = 4
HGRN_DK = GROUP_WIDTH // HGRN_HEADS
HGRN_DV = GROUP_WIDTH // HGRN_HEADS
GDN_HEADS = 4
GDN_DK = GROUP_WIDTH // GDN_HEADS
GDN_DV = GROUP_WIDTH // GDN_HEADS
GDN_CONV_DIM = 3 * GROUP_WIDTH
LIN_CHUNK = 64
N_MEM = 256
X_HEADS = 4
X_DH = D_MODEL // X_HEADS
FFN_HIDDEN = -(-8 * D_MODEL // (3 * 256)) * 256
IN_SIZES = (3 * GROUP_WIDTH,
            GROUP_WIDTH, SSM_CONV_DIM, SSM_HEADS,
            GROUP_WIDTH, GROUP_WIDTH, GROUP_WIDTH, GROUP_WIDTH,
            GDN_CONV_DIM, GROUP_WIDTH, GDN_HEADS, GDN_HEADS)
IN_COLS = sum(IN_SIZES)

kernel_name = 'hybrid_moba_ssd_hgrn2_gdn_decoder_step'


def rms_norm(x, w, eps=1e-6):
    xf = x.astype(jnp.float32)
    y = xf * lax.rsqrt(jnp.mean(xf * xf, axis=-1, keepdims=True) + eps)
    return (y * w.astype(jnp.float32)).astype(x.dtype)


def l2_norm(x, eps=1e-6):
    return x * lax.rsqrt(jnp.sum(x * x, axis=-1, keepdims=True) + eps)


def split_cols(t, sizes):
    points = [int(s) for s in np.cumsum(sizes)[:-1]]
    return jnp.split(t, points, axis=-1)


def to_chunks(t, cs):
    b, L = t.shape[:2]
    return jnp.moveaxis(t.reshape(b, L // cs, cs, *t.shape[2:]), 1, 0)


def from_chunks(t):
    nc, b, cs = t.shape[:3]
    return jnp.moveaxis(t, 0, 1).reshape(b, nc * cs, *t.shape[3:])


def causal_conv(u, prefix, w, bias):
    L = u.shape[1]
    up = jnp.concatenate([prefix.astype(u.dtype), u], axis=1)
    out = w[0] * up[:, 0:L]
    for j in range(1, CONV_W):
        out = out + w[j] * up[:, j:j + L]
    if bias is not None:
        out = out + bias
    return out, up[:, L:]


def moba_prompt(slopes, q, k, v):
    b, S, H, dh = q.shape
    nb = -(-S // MOBA_BLOCK)
    pad = nb * MOBA_BLOCK - S
    padw = ((0, 0), (0, pad), (0, 0), (0, 0))
    kb = jnp.pad(k, padw).reshape(b, nb, MOBA_BLOCK, H, dh).transpose(0, 3, 1, 2, 4)
    vb = jnp.pad(v, padw).reshape(b, nb, MOBA_BLOCK, H, dh).transpose(0, 3, 1, 2, 4)
    kmean = jnp.mean(kb, axis=3)
    n_sel = min(MOBA_TOPK, nb - 1)
    nq = S // MOBA_QBLOCK
    qb = q.reshape(b, nq, MOBA_QBLOCK, H, dh).transpose(1, 0, 3, 2, 4)
    scale = dh ** -0.5
    m = slopes[None, :, None, None]
    bi = jnp.arange(b)[:, None, None, None]
    hi = jnp.arange(H)[None, :, None, None]

    def one_block(args):
        qblk, qi = args
        start = qi * MOBA_QBLOCK
        t = start + jnp.arange(MOBA_QBLOCK)
        cb = start // MOBA_BLOCK
        own_k = lax.dynamic_index_in_dim(kb, cb, axis=2, keepdims=False)
        own_v = lax.dynamic_index_in_dim(vb, cb, axis=2, keepdims=False)
        dist = t[:, None] - (cb * MOBA_BLOCK + jnp.arange(MOBA_BLOCK))[None, :]
        sc_own = jnp.einsum('bhqd,bhkd->bhqk', qblk, own_k) * scale - m * dist
        sc_own = jnp.where(dist >= 0, sc_own, -jnp.inf)
        if n_sel == 0:
            p = jax.nn.softmax(sc_own, axis=-1)
            return jnp.einsum('bhqk,bhkd->bhqd', p, own_v)
        gate = jnp.einsum('bhqd,bhnd->bhqn', qblk, kmean)
        gate = jnp.where(jnp.arange(nb) < cb, gate, -jnp.inf)
        _, idx = lax.top_k(gate, n_sel)
        valid = idx < cb
        ks = kb[bi, hi, idx]
        vs = vb[bi, hi, idx]
        s_sel = idx[..., None] * MOBA_BLOCK + jnp.arange(MOBA_BLOCK)
        sc_sel = (jnp.einsum('bhqd,bhqnkd->bhqnk', qblk, ks) * scale
                  - m[..., None] * (t[None, None, :, None, None] - s_sel))
        sc_sel = jnp.where(valid[..., None], sc_sel, -jnp.inf)
        sc = jnp.concatenate([sc_sel.reshape(b, H, MOBA_QBLOCK, n_sel * MOBA_BLOCK), sc_own], axis=-1)
        p = jax.nn.softmax(sc, axis=-1)
        p_sel = p[..., :n_sel * MOBA_BLOCK].reshape(b, H, MOBA_QBLOCK, n_sel, MOBA_BLOCK)
        p_own = p[..., n_sel * MOBA_BLOCK:]
        return (jnp.einsum('bhqnk,bhqnkd->bhqd', p_sel, vs)
                + jnp.einsum('bhqk,bhkd->bhqd', p_own, own_v))

    o = lax.map(one_block, (qb, jnp.arange(nq)))
    return o.transpose(1, 0, 3, 2, 4).reshape(b, S, H * dh)


def moba_sample(k_cache, v_cache, layer, page_table, slopes, q, k_new, v_new):
    db, T, H, dh = q.shape
    n_pages = page_table.shape[1]
    past = n_pages * PAGE_SIZE
    ppb = MOBA_BLOCK // PAGE_SIZE
    nbp = past // MOBA_BLOCK
    r0 = nbp * MOBA_BLOCK
    scale = dh ** -0.5
    m = slopes[None, :, None, None]
    t = past + jnp.arange(T)
    k_past = k_cache[layer, page_table].reshape(db, past, H, dh)
    v_tail = v_cache[layer, page_table[:, nbp * ppb:]].reshape(db, past - r0, H, dh)
    own_k = jnp.concatenate([k_past[:, r0:], k_new.astype(k_past.dtype)], axis=1)
    own_v = jnp.concatenate([v_tail, v_new.astype(v_tail.dtype)], axis=1)
    dist = t[:, None] - (r0 + jnp.arange(past - r0 + T))[None, :]
    sc_own = jnp.einsum('bqhd,bkhd->bhqk', q, own_k) * scale - m * dist
    sc_own = jnp.where(dist >= 0, sc_own, -jnp.inf)
    n_sel = min(MOBA_TOPK, nbp)
    if n_sel == 0:
        p = jax.nn.softmax(sc_own.astype(jnp.float32), axis=-1)
        o = jnp.einsum('bhqk,bkhd->bhqd', p, own_v)
        return o.transpose(0, 2, 1, 3).reshape(db, T, H * dh)
    kblocks = k_past[:, :r0].reshape(db, nbp, MOBA_BLOCK, H, dh)
    kmean = jnp.mean(kblocks.astype(jnp.float32), axis=2)
    gate = jnp.einsum('bqhd,bnhd->bhqn', q, kmean)
    _, idx = lax.top_k(gate, n_sel)
    bi = jnp.arange(db)[:, None, None, None]
    hi = jnp.arange(H)[None, :, None, None]
    ks = kblocks[bi, idx, :, hi]
    logical = idx[..., None] * ppb + jnp.arange(ppb)
    phys = page_table[bi[..., None], logical]
    vs = v_cache[layer, phys, :, hi[..., None]].reshape(db, H, T, n_sel, MOBA_BLOCK, dh)
    s_sel = idx[..., None] * MOBA_BLOCK + jnp.arange(MOBA_BLOCK)
    sc_sel = (jnp.einsum('bqhd,bhqnkd->bhqnk', q, ks) * scale
              - m[..., None] * (t[None, None, :, None, None] - s_sel))
    sc = jnp.concatenate([sc_sel.reshape(db, H, T, n_sel * MOBA_BLOCK), sc_own], axis=-1)
    p = jax.nn.softmax(sc.astype(jnp.float32), axis=-1)
    p_sel = p[..., :n_sel * MOBA_BLOCK].reshape(db, H, T, n_sel, MOBA_BLOCK)
    p_own = p[..., n_sel * MOBA_BLOCK:]
    o = jnp.einsum('bhqnk,bhqnkd->bhqd', p_sel, vs) + jnp.einsum('bhqk,bkhd->bhqd', p_own, own_v)
    return o.transpose(0, 2, 1, 3).reshape(db, T, H * dh)


def ssd_scan(xdt, a, bm, cm, h0):
    cs = min(SSM_CHUNK, xdt.shape[1])
    mask = jnp.tril(jnp.ones((cs, cs), dtype=bool))[None, :, :, None]

    def step(h, inp):
        xc, ac, bc, cc = inp
        acum = jnp.cumsum(ac, axis=1)
        diff = acum[:, :, None] - acum[:, None, :]
        dec = jnp.exp(jnp.where(mask, diff, -jnp.inf))
        y = (jnp.einsum('blhn,bshn,blsh,bshp->blhp', cc, bc, dec, xc)
             + jnp.einsum('blhn,bhpn,blh->blhp', cc, h, jnp.exp(acum)))
        h = (jnp.exp(acum[:, -1])[..., None, None] * h
             + jnp.einsum('bshn,bsh,bshp->bhpn', bc, jnp.exp(acum[:, -1:] - acum), xc))
        return h, y

    h1, y = lax.scan(step, h0, (to_chunks(xdt, cs), to_chunks(a, cs), to_chunks(bm, cs), to_chunks(cm, cs)))
    return from_chunks(y), h1


def mamba2_mixer(z, xbc, dt_raw, conv0, h0, conv_w, conv_b, dt_bias, a_log, d_skip, norm_w):
    b, L, _ = xbc.shape
    xbc_c, conv1 = causal_conv(xbc, conv0, conv_w, conv_b)
    xbc_c = jax.nn.silu(xbc_c)
    xs, bm, cm = split_cols(xbc_c, (GROUP_WIDTH, SSM_GROUPS * SSM_DSTATE, SSM_GROUPS * SSM_DSTATE))
    xs = xs.reshape(b, L, SSM_HEADS, SSM_HEADDIM)
    rep = SSM_HEADS // SSM_GROUPS
    bm = jnp.repeat(bm.reshape(b, L, SSM_GROUPS, SSM_DSTATE), rep, axis=2)
    cm = jnp.repeat(cm.reshape(b, L, SSM_GROUPS, SSM_DSTATE), rep, axis=2)
    dt = jax.nn.softplus(dt_raw + dt_bias)
    a = -jnp.exp(a_log.astype(jnp.float32))
    y, h1 = ssd_scan(xs * dt[..., None], dt * a, bm, cm, h0)
    y = (y + d_skip[:, None] * xs).reshape(b, L, GROUP_WIDTH)
    return rms_norm(y * jax.nn.silu(z), norm_w), conv1, h1


def gla_chunked(q, k, v, log_f, s0):
    cs = min(LIN_CHUNK, q.shape[1])
    mask = jnp.tril(jnp.ones((cs, cs), dtype=bool))[None, :, :, None, None]

    def step(S, inp):
        qc, kc, vc, gc = inp
        bcum = jnp.cumsum(gc, axis=1)
        diff = bcum[:, :, None] - bcum[:, None, :]
        dec = jnp.exp(jnp.where(mask, diff, -jnp.inf))
        att = jnp.einsum('blhk,bshk,blshk->blsh', qc, kc, dec)
        o = (jnp.einsum('blsh,bshv->blhv', att, vc)
             + jnp.einsum('blhk,bhkv->blhv', qc * jnp.exp(bcum), S))
        b_last = bcum[:, -1:]
        S = (jnp.exp(b_last[:, 0])[..., None] * S
             + jnp.einsum('bshk,bshv->bhkv', kc * jnp.exp(b_last - bcum), vc))
        return S, o

    s1, o = lax.scan(step, s0, (to_chunks(q, cs), to_chunks(k, cs), to_chunks(v, cs), to_chunks(log_f, cs)))
    return from_chunks(o), s1


def hgrn2_mixer(q, fx, i, g, s0, lb, norm_w):
    b, L, _ = q.shape
    shp = (b, L, HGRN_HEADS, HGRN_DK)
    q = q.reshape(shp)
    fx = fx.reshape(shp)
    lb = lb.reshape(HGRN_HEADS, HGRN_DK)
    log_f = jnp.logaddexp(jnp.log(lb), jnp.log1p(-lb) + jax.nn.log_sigmoid(fx))
    k = (1.0 - lb) * jax.nn.sigmoid(-fx)
    v = i.reshape(b, L, HGRN_HEADS, HGRN_DV)
    o, s1 = gla_chunked(q, k, v, log_f, s0)
    o = rms_norm(o, norm_w) * jax.nn.silu(g.reshape(b, L, HGRN_HEADS, HGRN_DV))
    return o.reshape(b, L, GROUP_WIDTH), s1


def gated_delta_chunked(q, k, v, beta, log_g, s0):
    cs = min(LIN_CHUNK, q.shape[1])
    incl = jnp.tril(jnp.ones((cs, cs), dtype=bool))
    strict = jnp.tril(jnp.ones((cs, cs), dtype=bool), -1)
    eye = jnp.eye(cs, dtype=jnp.float32)

    def step(S, inp):
        qc, kc, vc, bc, gc = inp
        gam = jnp.cumsum(gc, axis=1)
        gam_h = jnp.moveaxis(gam, 2, 1)
        diff = gam_h[:, :, :, None] - gam_h[:, :, None, :]
        dec_incl = jnp.exp(jnp.where(incl, diff, -jnp.inf))
        dec_strict = jnp.where(strict, dec_incl, 0.0)
        kb = kc * bc[..., None]
        a_mat = jnp.einsum('blhk,bshk->bhls', kb, kc) * dec_strict
        t_mat = lax.linalg.triangular_solve(eye + a_mat, jnp.broadcast_to(eye, a_mat.shape),
                                            left_side=True, lower=True, unit_diagonal=True)
        u = jnp.einsum('bhls,bshv->bhlv', t_mat, vc * bc[..., None])
        w = jnp.einsum('bhls,bshk->bhlk', t_mat, kb * jnp.exp(gam)[..., None])
        v_new = u - jnp.einsum('bhlk,bhkv->bhlv', w, S)
        qk = jnp.einsum('blhk,bshk->bhls', qc, kc) * dec_incl
        o = (jnp.einsum('bhls,bhsv->blhv', qk, v_new)
             + jnp.einsum('blhk,bhkv->blhv', qc * jnp.exp(gam)[..., None], S))
        k_dec = kc * jnp.exp(gam[:, -1:] - gam)[..., None]
        S = jnp.exp(gam_h[:, :, -1])[..., None, None] * S + jnp.einsum('blhk,bhlv->bhkv', k_dec, v_new)
        return S, o

    s1, o = lax.scan(step, s0, (to_chunks(q, cs), to_chunks(k, cs), to_chunks(v, cs),
                                to_chunks(beta, cs), to_chunks(log_g, cs)))
    return from_chunks(o), s1


def gdn_mixer(qkv, z, beta_raw, a_raw, conv0, s0, conv_w, a_log, dt_bias, norm_w):
    b, L, _ = qkv.shape
    qkv_c, conv1 = causal_conv(qkv, conv0, conv_w, None)
    q, k, v = jnp.split(jax.nn.silu(qkv_c), 3, axis=-1)
    q = l2_norm(q.reshape(b, L, GDN_HEADS, GDN_DK)) * GDN_DK ** -0.5
    k = l2_norm(k.reshape(b, L, GDN_HEADS, GDN_DK))
    v = v.reshape(b, L, GDN_HEADS, GDN_DV)
    beta = jax.nn.sigmoid(beta_raw)
    log_g = -jnp.exp(a_log.astype(jnp.float32)) * jax.nn.softplus(a_raw + dt_bias)
    o, s1 = gated_delta_chunked(q, k, v, beta, log_g, s0)
    o = rms_norm(o, norm_w) * jax.nn.silu(z.reshape(b, L, GDN_HEADS, GDN_DV))
    return o.reshape(b, L, GROUP_WIDTH), conv1, s1


def memory_kv(mem, norm_w, w_kv):
    b, M, _ = mem.shape
    k, v = jnp.split(rms_norm(mem, norm_w) @ w_kv, 2, axis=-1)
    return k.reshape(b, M, X_HEADS, X_DH), v.reshape(b, M, X_HEADS, X_DH)


def memory_cross_attention(xn, w_q, k, v, w_o):
    b, L, _ = xn.shape
    q = (xn @ w_q).reshape(b, L, X_HEADS, X_DH).astype(jnp.float32)
    s = jnp.einsum('blhd,bmhd->bhlm', q, k.astype(jnp.float32)) * X_DH ** -0.5
    p = jax.nn.softmax(s, axis=-1)
    o = jnp.einsum('bhlm,bmhd->blhd', p, v.astype(jnp.float32)).reshape(b, L, X_HEADS * X_DH)
    return o.astype(xn.dtype) @ w_o


def swiglu(xn, w_gu, w_down):
    g, u = jnp.split(xn @ w_gu, 2, axis=-1)
    return (jax.nn.silu(g) * u) @ w_down


def hybrid_layer(x, attend, mem_k, mem_v, ssm_conv0, ssm0, hgrn0, gdn_conv0, gdn0, lb, lp):
    f32 = jnp.float32
    b, L, _ = x.shape
    xn = rms_norm(x, lp['n_mix_pre'])
    (m_qkv, s_z, s_xbc, s_dt, h_q, h_f, h_i, h_g,
     g_qkv, g_z, g_b, g_a) = split_cols((xn @ lp['w_in']).astype(f32), IN_SIZES)
    q, k, v = [t.reshape(b, L, MOBA_HEADS, MOBA_DH) for t in jnp.split(m_qkv, 3, axis=-1)]
    o_a = attend(q, k, v)
    o_b, ssm_conv1, ssm1 = mamba2_mixer(s_z, s_xbc, s_dt, ssm_conv0.astype(f32), ssm0.astype(f32),
                                        lp['ssm_conv_w'], lp['ssm_conv_b'], lp['ssm_dt_bias'],
                                        lp['ssm_a_log'], lp['ssm_d'], lp['ssm_norm'])
    o_c, hgrn1 = hgrn2_mixer(h_q, h_f, h_i, h_g, hgrn0.astype(f32), lb, lp['hgrn_norm'])
    o_d, gdn_conv1, gdn1 = gdn_mixer(g_qkv, g_z, g_b, g_a, gdn_conv0.astype(f32), gdn0.astype(f32),
                                     lp['gdn_conv_w'], lp['gdn_a_log'], lp['gdn_dt_bias'], lp['gdn_norm'])
    mixed = jnp.concatenate([o_a, o_b, o_c, o_d], axis=-1).astype(x.dtype) @ lp['w_out']
    x = x + rms_norm(mixed, lp['n_mix_post'])
    xn = rms_norm(x, lp['n_x_pre'])
    x = x + rms_norm(memory_cross_attention(xn, lp['w_xq'], mem_k, mem_v, lp['w_xo']), lp['n_x_post'])
    xn = rms_norm(x, lp['n_f_pre'])
    x = x + rms_norm(swiglu(xn, lp['w_gu'], lp['w_down']), lp['n_f_post'])
    return x, (k, v, ssm_conv1, ssm1, hgrn1, gdn_conv1, gdn1)


def stack_field(outs, i):
    return jnp.stack([o[i] for o in outs], axis=0)


def setup_inputs(seed: int = 0) -> dict:
    key = jax.random.key(seed)
    ks = iter(list(jax.random.split(key, 64)))
    f32 = jnp.float32
    n_pages = PAST_LEN // PAGE_SIZE
    n_pool = (DEC_BATCH * n_pages * 5) // 4

    def nrm(shape, scale=1.0):
        return scale * jax.random.normal(next(ks), shape, f32)

    def gain(shape):
        return 1.0 + 0.05 * jax.random.normal(next(ks), shape, f32)

    def a_log_init(shape):
        return jnp.log(jax.random.uniform(next(ks), shape, f32, 1.0, 16.0))

    def dt_bias_init(shape):
        dt = jnp.exp(jax.random.uniform(next(ks), shape, f32, math.log(1e-3), math.log(1e-1)))
        return dt + jnp.log(-jnp.expm1(-dt))

    page_table = jax.random.permutation(next(ks), n_pool)[:DEC_BATCH * n_pages]
    page_table = page_table.reshape(DEC_BATCH, n_pages).astype(jnp.int32)
    return {
        'x_prompt': nrm((BATCH, SEQ, D_MODEL)),
        'x_sample': nrm((DEC_BATCH, DEC_SEQ, D_MODEL)),
        'mem_prompt': nrm((BATCH, N_MEM, D_MODEL)),
        'cache_moba_k': nrm((DEPTH, n_pool, PAGE_SIZE, MOBA_HEADS, MOBA_DH)),
        'cache_moba_v': nrm((DEPTH, n_pool, PAGE_SIZE, MOBA_HEADS, MOBA_DH)),
        'page_table': page_table,
        'cache_mem_k': nrm((DEPTH, DEC_BATCH, N_MEM, X_HEADS, X_DH)),
        'cache_mem_v': nrm((DEPTH, DEC_BATCH, N_MEM, X_HEADS, X_DH)),
        'state_ssm_conv': nrm((DEPTH, DEC_BATCH, CONV_W - 1, SSM_CONV_DIM)),
        'state_ssm': nrm((DEPTH, DEC_BATCH, SSM_HEADS, SSM_HEADDIM, SSM_DSTATE), 0.1),
        'state_hgrn': nrm((DEPTH, DEC_BATCH, HGRN_HEADS, HGRN_DK, HGRN_DV), 0.5),
        'state_gdn_conv': nrm((DEPTH, DEC_BATCH, CONV_W - 1, GDN_CONV_DIM)),
        'state_gdn': nrm((DEPTH, DEC_BATCH, GDN_HEADS, GDN_DK, GDN_DV), 0.1),
        'n_mix_pre': gain((DEPTH, D_MODEL)),
        'n_mix_post': gain((DEPTH, D_MODEL)),
        'w_in': nrm((DEPTH, D_MODEL, IN_COLS), D_MODEL ** -0.5),
        'w_out': nrm((DEPTH, MIX_WIDTH, D_MODEL), MIX_WIDTH ** -0.5),
        'ssm_conv_w': nrm((DEPTH, CONV_W, SSM_CONV_DIM), CONV_W ** -0.5),
        'ssm_conv_b': nrm((DEPTH, SSM_CONV_DIM), 0.02),
        'ssm_dt_bias': dt_bias_init((DEPTH, SSM_HEADS)),
        'ssm_a_log': a_log_init((DEPTH, SSM_HEADS)),
        'ssm_d': gain((DEPTH, SSM_HEADS)),
        'ssm_norm': gain((DEPTH, GROUP_WIDTH)),
        'hgrn_lb_raw': nrm((DEPTH, HGRN_HEADS * HGRN_DK), 0.5),
        'hgrn_norm': gain((DEPTH, HGRN_DV)),
        'gdn_conv_w': nrm((DEPTH, CONV_W, GDN_CONV_DIM), CONV_W ** -0.5),
        'gdn_a_log': a_log_init((DEPTH, GDN_HEADS)),
        'gdn_dt_bias': dt_bias_init((DEPTH, GDN_HEADS)),
        'gdn_norm': gain((DEPTH, GDN_DV)),
        'n_x_pre': gain((DEPTH, D_MODEL)),
        'n_x_post': gain((DEPTH, D_MODEL)),
        'mem_norm': gain((DEPTH, D_MODEL)),
        'w_xq': nrm((DEPTH, D_MODEL, X_HEADS * X_DH), D_MODEL ** -0.5),
        'w_xkv': nrm((DEPTH, D_MODEL, 2 * X_HEADS * X_DH), D_MODEL ** -0.5),
        'w_xo': nrm((DEPTH, X_HEADS * X_DH, D_MODEL), D_MODEL ** -0.5),
        'n_f_pre': gain((DEPTH, D_MODEL)),
        'n_f_post': gain((DEPTH, D_MODEL)),
        'w_gu': nrm((DEPTH, D_MODEL, 2 * FFN_HIDDEN), D_MODEL ** -0.5),
        'w_down': nrm((DEPTH, FFN_HIDDEN, D_MODEL), FFN_HIDDEN ** -0.5),
    }


def reference(x_prompt, x_sample, mem_prompt, cache_moba_k, cache_moba_v, page_table,
              cache_mem_k, cache_mem_v, state_ssm_conv, state_ssm, state_hgrn, state_gdn_conv, state_gdn,
              n_mix_pre, n_mix_post, w_in, w_out, ssm_conv_w, ssm_conv_b, ssm_dt_bias, ssm_a_log, ssm_d,
              ssm_norm, hgrn_lb_raw, hgrn_norm, gdn_conv_w, gdn_a_log, gdn_dt_bias, gdn_norm,
              n_x_pre, n_x_post, mem_norm, w_xq, w_xkv, w_xo, n_f_pre, n_f_post, w_gu, w_down):
    f32 = jnp.float32
    slopes = jnp.exp2(-8.0 * jnp.arange(1, MOBA_HEADS + 1, dtype=f32) / MOBA_HEADS)
    lb_all = jnp.cumsum(jax.nn.softmax(hgrn_lb_raw.astype(f32), axis=0), axis=0)
    lb_all = lb_all - lb_all[:1]
    bp = x_prompt.shape[0]
    yp, ys = x_prompt, x_sample
    outs_p, outs_s, mem_ks, mem_vs = [], [], [], []
    for l in range(DEPTH):
        lp = {'n_mix_pre': n_mix_pre[l], 'n_mix_post': n_mix_post[l], 'w_in': w_in[l], 'w_out': w_out[l],
              'ssm_conv_w': ssm_conv_w[l], 'ssm_conv_b': ssm_conv_b[l], 'ssm_dt_bias': ssm_dt_bias[l],
              'ssm_a_log': ssm_a_log[l], 'ssm_d': ssm_d[l], 'ssm_norm': ssm_norm[l], 'hgrn_norm': hgrn_norm[l],
              'gdn_conv_w': gdn_conv_w[l], 'gdn_a_log': gdn_a_log[l], 'gdn_dt_bias': gdn_dt_bias[l],
              'gdn_norm': gdn_norm[l], 'n_x_pre': n_x_pre[l], 'n_x_post': n_x_post[l], 'w_xq': w_xq[l],
              'w_xo': w_xo[l], 'n_f_pre': n_f_pre[l], 'n_f_post': n_f_post[l], 'w_gu': w_gu[l],
              'w_down': w_down[l]}
        mk, mv = memory_kv(mem_prompt, mem_norm[l], w_xkv[l])
        yp, st_p = hybrid_layer(
            yp, functools.partial(moba_prompt, slopes), mk, mv,
            jnp.zeros((bp, CONV_W - 1, SSM_CONV_DIM), f32),
            jnp.zeros((bp, SSM_HEADS, SSM_HEADDIM, SSM_DSTATE), f32),
            jnp.zeros((bp, HGRN_HEADS, HGRN_DK, HGRN_DV), f32),
            jnp.zeros((bp, CONV_W - 1, GDN_CONV_DIM), f32),
            jnp.zeros((bp, GDN_HEADS, GDN_DK, GDN_DV), f32),
            lb_all[l], lp)
        outs_p.append(st_p)
        mem_ks.append(mk)
        mem_vs.append(mv)
        ys, st_s = hybrid_layer(
            ys, functools.partial(moba_sample, cache_moba_k, cache_moba_v, l, page_table, slopes),
            cache_mem_k[l], cache_mem_v[l], state_ssm_conv[l], state_ssm[l], state_hgrn[l],
            state_gdn_conv[l], state_gdn[l], lb_all[l], lp)
        outs_s.append(st_s)
    moba_k_prompt = stack_field(outs_p, 0)
    moba_v_prompt = stack_field(outs_p, 1)
    moba_k_sample = stack_field(outs_s, 0)
    moba_v_sample = stack_field(outs_s, 1)
    mem_k_prompt = jnp.stack(mem_ks, axis=0)
    mem_v_prompt = jnp.stack(mem_vs, axis=0)
    ssm_conv_prompt = stack_field(outs_p, 2)
    ssm_conv_sample = stack_field(outs_s, 2)
    ssm_prompt = stack_field(outs_p, 3)
    ssm_sample = stack_field(outs_s, 3)
    hgrn_prompt = stack_field(outs_p, 4)
    hgrn_sample = stack_field(outs_s, 4)
    gdn_conv_prompt = stack_field(outs_p, 5)
    gdn_conv_sample = stack_field(outs_s, 5)
    gdn_prompt = stack_field(outs_p, 6)
    gdn_sample = stack_field(outs_s, 6)
    return (yp, ys, moba_k_prompt, moba_k_sample, moba_v_prompt, moba_v_sample, mem_k_prompt, mem_v_prompt,
            ssm_conv_prompt, ssm_conv_sample, ssm_prompt, ssm_sample, hgrn_prompt, hgrn_sample,
            gdn_conv_prompt, gdn_conv_sample, gdn_prompt, gdn_sample)
```

```python
import functools

import jax
import jax.numpy as jnp
from jax import lax
from jax.experimental import pallas as pl
from jax.experimental.pallas import tpu as pltpu

F32 = jnp.float32
BF16 = jnp.bfloat16

LANES = 128
SUBLANES = 8
VMEM_LIMIT_BYTES = 56 * 1024 * 1024

GROUP = 256
HEADS = 4
HEAD_DIM = 64
CONV_W = 4
SSM_DSTATE = 128
SSM_CHUNK = 128
LIN_CHUNK = 64
MOBA_BLOCK = 256
MOBA_TOPK = 3
PAGE_SIZE = 128
X_HEADS = 4
N_SMALL = 12
COL_MOBA = 0
COL_SSM_Z = 3
COL_SSM_X = 4
COL_HGRN = 7
COL_GDN = 11
N_WIDE = 15 * GROUP
COL_SMALL = N_WIDE // LANES
EPS = 1e-6
NEG_INF = float("-inf")
ALIBI_SLOPES = tuple(2.0 ** (-8.0 * (h + 1) / HEADS) for h in range(HEADS))


def _rms(x, w):
    return x * lax.rsqrt(jnp.mean(x * x, axis=-1, keepdims=True) + EPS) * w


def _sigmoid(x):
    return 1.0 / (1.0 + jnp.exp(-x))


def _silu(x):
    return x * _sigmoid(x)


def _softplus(x):
    return jnp.maximum(x, 0.0) + jnp.log(1.0 + jnp.exp(-jnp.abs(x)))


def _dot(a, b):
    return jnp.dot(a.astype(BF16), b.astype(BF16), preferred_element_type=F32)


def _dot_nt(a, b):
    return lax.dot_general(a.astype(BF16), b.astype(BF16), (((1,), (1,)), ((), ())),
                           preferred_element_type=F32)


def _dot_f32(a, b):
    return jnp.dot(a, b, preferred_element_type=F32, precision=lax.Precision.HIGHEST)


def _params(*sem):
    return pltpu.CompilerParams(dimension_semantics=sem, vmem_limit_bytes=VMEM_LIMIT_BYTES)


def _resident(shape):
    return pl.BlockSpec(shape, lambda *_: (0,) * len(shape), pipeline_mode=pl.Buffered(1))


def _norm_matmul_kernel(x_ref, nw_ref, w_ref, o_ref, *, full_precision):
    xn = _rms(x_ref[...], nw_ref[...])
    o_ref[...] = _dot_f32(xn, w_ref[...]) if full_precision else _dot(xn, w_ref[...])


def norm_matmul(x, norm_w, w, tm):
    n, d = x.shape
    c = w.shape[1]
    return pl.pallas_call(
        functools.partial(_norm_matmul_kernel, full_precision=(w.dtype == F32)),
        grid=(n // tm,),
        in_specs=[pl.BlockSpec((tm, d), lambda i: (i, 0)),
                  _resident((1, d)),
                  _resident((d, c))],
        out_specs=pl.BlockSpec((tm, c), lambda i: (i, 0)),
        out_shape=jax.ShapeDtypeStruct((n, c), F32),
        compiler_params=_params("parallel"),
        name="norm_matmul",
    )(x, norm_w.reshape(1, d), w)


def _mix_out_and_query(x, mixed, nw, wout_ref, wxq_ref):
    x = x + _rms(_dot(mixed, wout_ref[...]), nw[0:1])
    return x, _dot(_rms(x, nw[1:2]), wxq_ref[...])


def _memory_attention(q, mk, mv):
    xdh = q.shape[1] // X_HEADS
    heads = []
    for h in range(X_HEADS):
        sl = slice(h * xdh, (h + 1) * xdh)
        s = _dot_nt(q[:, sl], mk[:, sl]) * (xdh ** -0.5)
        p = jnp.exp(s - jnp.max(s, axis=-1, keepdims=True))
        heads.append(_dot(p, mv[:, sl]) / jnp.sum(p, axis=-1, keepdims=True))
    return jnp.concatenate(heads, axis=-1)


def _attn_out_and_ffn(x, att, nw, wxo_ref, wgu_ref, wdown_ref):
    x = x + _rms(_dot(att, wxo_ref[...]), nw[2:3])
    gu = _dot(_rms(x, nw[3:4]), wgu_ref[...])
    hid = gu.shape[1] // 2
    act = _silu(gu[:, :hid]) * gu[:, hid:]
    return x + _rms(_dot(act, wdown_ref[...]), nw[4:5])


def _post_mixer_kernel(x_ref, oa_ref, ob_ref, oc_ref, od_ref, mk_ref, mv_ref, norms_ref, wout_ref,
                       wxq_ref, wxo_ref, wgu_ref, wdown_ref, o_ref):
    nw = norms_ref[...]
    mixed = jnp.concatenate([oa_ref[0], ob_ref[0], oc_ref[0], od_ref[0]], axis=-1)
    x, q = _mix_out_and_query(x_ref[0], mixed, nw, wout_ref, wxq_ref)
    att = _memory_attention(q, mk_ref[0], mv_ref[0])
    o_ref[0] = _attn_out_and_ffn(x, att, nw, wxo_ref, wgu_ref, wdown_ref)


def post_mixer(x, mix_parts, mk, mv, norms, w_out, w_xq, w_xo, w_gu, w_down, tm):
    b, L, d = x.shape
    m = mk.shape[1]
    row = pl.BlockSpec((1, tm, d), lambda i, j: (i, j, 0))
    part = pl.BlockSpec((1, tm, GROUP), lambda i, j: (i, j, 0))
    mem = pl.BlockSpec((1, m, d), lambda i, j: (i, 0, 0))
    return pl.pallas_call(
        _post_mixer_kernel,
        grid=(b, L // tm),
        in_specs=[row, part, part, part, part, mem, mem, _resident(norms.shape),
                  _resident(w_out.shape), _resident(w_xq.shape), _resident(w_xo.shape),
                  _resident(w_gu.shape), _resident(w_down.shape)],
        out_specs=row,
        out_shape=jax.ShapeDtypeStruct((b, L, d), F32),
        compiler_params=_params("parallel", "parallel"),
        name="post_mixer",
    )(x, *mix_parts, mk, mv, norms, w_out, w_xq, w_xo, w_gu, w_down)


def _sample_pre_kernel(x_ref, oa_ref, ob_ref, oc_ref, od_ref, norms_ref, wout_ref, wxq_ref,
                       x1_ref, q_ref):
    mixed = jnp.concatenate([oa_ref[...], ob_ref[...], oc_ref[...], od_ref[...]], axis=-1)
    x1, q = _mix_out_and_query(x_ref[...], mixed, norms_ref[...], wout_ref, wxq_ref)
    x1_ref[...] = x1
    q_ref[...] = q


def _sample_attn_kernel(q_ref, mk_ref, mv_ref, o_ref, q8, *, t):
    q8[0:t, :] = q_ref[0]
    q8[t:, :] = jnp.zeros((q8.shape[0] - t, q8.shape[1]), F32)
    o_ref[0] = _memory_attention(q8[...], mk_ref[0], mv_ref[0])[0:t]


def _sample_post_kernel(x_ref, att_ref, norms_ref, wxo_ref, wgu_ref, wdown_ref, o_ref):
    o_ref[...] = _attn_out_and_ffn(x_ref[...], att_ref[...], norms_ref[...], wxo_ref, wgu_ref,
                                   wdown_ref)


def post_mixer_sample(x, mix_parts, mk, mv, norms, w_out, w_xq, w_xo, w_gu, w_down):
    b, t, d = x.shape
    n = b * t
    m = mk.shape[1]
    flat = lambda a: a.reshape(n, a.shape[-1])
    full = lambda shp: pl.BlockSpec(shp, lambda i: (0,) * len(shp))
    x1, q = pl.pallas_call(
        _sample_pre_kernel,
        grid=(1,),
        in_specs=[full((n, d))] + [full((n, GROUP))] * 4
                 + [full(norms.shape), _resident(w_out.shape), _resident(w_xq.shape)],
        out_specs=[full((n, d)), full((n, d))],
        out_shape=[jax.ShapeDtypeStruct((n, d), F32)] * 2,
        compiler_params=_params("arbitrary"),
        name="sample_pre",
    )(flat(x), *[flat(p) for p in mix_parts], norms, w_out, w_xq)
    seq = pl.BlockSpec((1, t, d), lambda i: (i, 0, 0))
    mem = pl.BlockSpec((1, m, d), lambda i: (i, 0, 0))
    att = pl.pallas_call(
        functools.partial(_sample_attn_kernel, t=t),
        grid=(b,),
        in_specs=[seq, mem, mem],
        out_specs=seq,
        out_shape=jax.ShapeDtypeStruct((b, t, d), F32),
        scratch_shapes=[pltpu.VMEM((SUBLANES, d), F32)],
        compiler_params=_params("parallel"),
        name="sample_attn",
    )(q.reshape(b, t, d), mk, mv)
    out = pl.pallas_call(
        _sample_post_kernel,
        grid=(1,),
        in_specs=[full((n, d)), full((n, d)), full(norms.shape), _resident(w_xo.shape),
                  _resident(w_gu.shape), _resident(w_down.shape)],
        out_specs=full((n, d)),
        out_shape=jax.ShapeDtypeStruct((n, d), F32),
        compiler_params=_params("arbitrary"),
        name="sample_post",
    )(x1, flat(att), norms, w_xo, w_gu, w_down)
    return out.reshape(b, t, d)


def _tril(n, strict=False):
    r = lax.broadcasted_iota(jnp.int32, (n, n), 0)
    c = lax.broadcasted_iota(jnp.int32, (n, n), 1)
    return (r > c) if strict else (r >= c)


def _head_expand(n_rows, n_cols, width, offset=0):
    r = lax.broadcasted_iota(jnp.int32, (n_rows, n_cols), 0)
    c = lax.broadcasted_iota(jnp.int32, (n_rows, n_cols), 1)
    return (r == (c // width) + offset).astype(F32)


def _stage_rows(dst, src_ref, l_blk, row0=0):
    n = dst.shape[0]
    dst[row0:row0 + l_blk, :] = src_ref[0]
    if row0 + l_blk < n:
        dst[row0 + l_blk:n, :] = jnp.zeros((n - row0 - l_blk, dst.shape[1]), F32)


def _causal_conv(buf, cw_ref, cs):
    out = cw_ref[0:1, :] * buf[5:5 + cs, :]
    for j in range(1, CONV_W):
        out = out + cw_ref[j:j + 1, :] * buf[5 + j:5 + j + cs, :]
    return out


def _ssd_kernel(z_ref, x_ref, b_ref, c_ref, sm_ref, conv0_ref, h0_ref, cw_ref, cb_ref, vec_ref,
                o_ref, conv1_ref, h1_ref, buf, smbuf, hst, *, cs, l_blk, n_chunks):
    c_idx = pl.program_id(1)

    @pl.when(c_idx == 0)
    def _():
        buf[5:8, :] = conv0_ref[0]
        hst[...] = h0_ref[0]

    _stage_rows(buf.at[:, 0:GROUP], x_ref, l_blk, 8)
    _stage_rows(buf.at[:, GROUP:2 * GROUP], b_ref, l_blk, 8)
    _stage_rows(buf.at[:, 2 * GROUP:3 * GROUP], c_ref, l_blk, 8)
    _stage_rows(smbuf, sm_ref, l_blk)

    xbc = _silu(_causal_conv(buf, cw_ref, cs) + cb_ref[...])
    conv_tail = buf[5 + l_blk:8 + l_blk, :]
    buf[5:8, :] = conv_tail
    xs = xbc[:, 0:GROUP]
    bm = xbc[:, GROUP:2 * GROUP]
    cm = xbc[:, 2 * GROUP:3 * GROUP]

    dt = _softplus(smbuf[...] + vec_ref[0:1, 0:LANES])
    if l_blk < cs:
        rows = lax.broadcasted_iota(jnp.int32, (cs, LANES), 0)
        dt = jnp.where(rows < l_blk, dt, 0.0)
    a = dt * (-jnp.exp(vec_ref[1:2, 0:LANES]))
    acum = _dot_f32(_tril(cs).astype(F32), a)
    expand = _head_expand(LANES, GROUP, HEAD_DIM)
    dt_w = _dot_f32(dt, expand)
    acum_w = _dot_f32(acum, expand)
    a_last = acum_w[cs - 1:cs, :]
    xdt = xs * dt_w
    e_acum = jnp.exp(acum_w)
    xw = xdt * jnp.exp(a_last - acum_w)
    e_last = jnp.exp(a_last)
    acum_t = acum.T
    causal = _tril(cs)

    ys = []
    for h in range(HEADS):
        g = h // (HEADS // 2)
        hs = slice(h * HEAD_DIM, (h + 1) * HEAD_DIM)
        gs = slice(g * SSM_DSTATE, (g + 1) * SSM_DSTATE)
        st = hst[h]
        dec = jnp.exp(jnp.where(causal, acum[:, h:h + 1] - acum_t[h:h + 1, :], NEG_INF))
        y = _dot(_dot_nt(cm[:, gs], bm[:, gs]) * dec, xdt[:, hs])
        y = y + _dot_nt(cm[:, gs], st) * e_acum[:, hs]
        ys.append(y)
        hst[h] = e_last[:, h * HEAD_DIM:h * HEAD_DIM + 1] * st + _dot(xw[:, hs].T, bm[:, gs])
    y = jnp.concatenate(ys, axis=-1) + vec_ref[2:3, :] * xs
    o_ref[0] = _rms(y[0:l_blk] * _silu(z_ref[0]), vec_ref[3:4, :])

    @pl.when(c_idx == n_chunks - 1)
    def _():
        conv1_ref[0] = conv_tail
        h1_ref[0] = hst[...]


def ssd_mixer(proj, conv0, h0, conv_w, conv_b, dt_bias, a_log, d_skip, norm_w):
    b, L, _ = proj.shape
    cs = SSM_CHUNK
    l_blk = min(cs, L)
    n_chunks = L // l_blk
    vec = jnp.zeros((8, GROUP), F32)
    vec = vec.at[0, :HEADS].set(dt_bias).at[1, :HEADS].set(a_log)
    vec = vec.at[2].set(jnp.repeat(d_skip, HEAD_DIM)).at[3].set(norm_w)
    col = lambda k: pl.BlockSpec((1, l_blk, GROUP), lambda i, j, k=k: (i, j, k))
    per_b = lambda shp: pl.BlockSpec((1,) + shp, lambda i, j: (i,) + (0,) * len(shp))
    c3 = 3 * GROUP
    return pl.pallas_call(
        functools.partial(_ssd_kernel, cs=cs, l_blk=l_blk, n_chunks=n_chunks),
        grid=(b, n_chunks),
        in_specs=[col(COL_SSM_Z), col(COL_SSM_X), col(COL_SSM_X + 1), col(COL_SSM_X + 2),
                  pl.BlockSpec((1, l_blk, LANES), lambda i, j: (i, j, COL_SMALL)),
                  per_b((CONV_W - 1, c3)), per_b((HEADS, HEAD_DIM, SSM_DSTATE)),
                  _resident((CONV_W, c3)), _resident((1, c3)), _resident((8, GROUP))],
        out_specs=[pl.BlockSpec((1, l_blk, GROUP), lambda i, j: (i, j, 0)),
                   per_b((CONV_W - 1, c3)), per_b((HEADS, HEAD_DIM, SSM_DSTATE))],
        out_shape=[jax.ShapeDtypeStruct((b, L, GROUP), F32),
                   jax.ShapeDtypeStruct((b, CONV_W - 1, c3), F32),
                   jax.ShapeDtypeStruct((b, HEADS, HEAD_DIM, SSM_DSTATE), F32)],
        scratch_shapes=[pltpu.VMEM((cs + 8, c3), F32), pltpu.VMEM((cs, LANES), F32),
                        pltpu.VMEM((HEADS, HEAD_DIM, SSM_DSTATE), F32)],
        compiler_params=_params("parallel", "arbitrary"),
        name="ssd_mixer",
    )(proj, proj, proj, proj, proj, conv0, h0, conv_w, conv_b.reshape(1, c3), vec)


def _dot_tn(a, b):
    return lax.dot_general(a.astype(BF16), b.astype(BF16), (((0,), (0,)), ((), ())),
                           preferred_element_type=F32)


def _head_blocks(n, width):
    r = lax.broadcasted_iota(jnp.int32, (n, n), 0)
    c = lax.broadcasted_iota(jnp.int32, (n, n), 1)
    return ((r // width) == (c // width)).astype(F32)


def _head_rms_gate(o, gate, nw_row, l_blk):
    ms = _dot_f32(o * o, _head_blocks(GROUP, HEAD_DIM)) * (1.0 / HEAD_DIM)
    return (o * lax.rsqrt(ms + EPS) * nw_row)[0:l_blk] * _silu(gate)


def _hgrn_kernel(q_ref, f_ref, i_ref, g_ref, s0_ref, lbraw_ref, nw_ref, o_ref, s1_ref,
                 stage, sst, *, cs, l_blk, n_chunks, layer):
    c_idx = pl.program_id(1)

    @pl.when(c_idx == 0)
    def _():
        sst[...] = s0_ref[0]

    if l_blk < cs:
        _stage_rows(stage.at[0], q_ref, l_blk)
        _stage_rows(stage.at[1], f_ref, l_blk)
        _stage_rows(stage.at[2], i_ref, l_blk)
        q, fx, v = stage[0], stage[1], stage[2]
    else:
        q, fx, v = q_ref[0], f_ref[0], i_ref[0]

    raw = lbraw_ref[...]
    e = jnp.exp(raw - jnp.max(raw, axis=0, keepdims=True))
    sm = e / jnp.sum(e, axis=0, keepdims=True)
    lb = jnp.zeros((1, GROUP), F32)
    for i in range(1, layer + 1):
        lb = lb + sm[i:i + 1, :]

    log_sig = jnp.minimum(fx, 0.0) - jnp.log1p(jnp.exp(-jnp.abs(fx)))
    la = jnp.log(lb)
    lbb = jnp.log1p(-lb) + log_sig
    log_f = jnp.maximum(la, lbb) + jnp.log1p(jnp.exp(-jnp.abs(la - lbb)))
    k = (1.0 - lb) * _sigmoid(-fx)
    if l_blk < cs:
        rows = lax.broadcasted_iota(jnp.int32, (cs, GROUP), 0)
        log_f = jnp.where(rows < l_blk, log_f, 0.0)
        k = jnp.where(rows < l_blk, k, 0.0)

    bcum = _dot_f32(_tril(cs).astype(F32), log_f)
    b_mid = bcum[cs // 2 - 1:cs // 2, :]
    b_last = bcum[cs - 1:cs, :]
    qe = q * jnp.exp(bcum - b_mid)
    ke = k * jnp.exp(b_mid - bcum)
    qs = q * jnp.exp(bcum)
    kd = k * jnp.exp(b_last - bcum)
    e_last = jnp.exp(b_last)
    causal = _tril(cs)

    os_ = []
    for h in range(HEADS):
        hs = slice(h * HEAD_DIM, (h + 1) * HEAD_DIM)
        st_t = sst[h]
        att = jnp.where(causal, _dot_nt(qe[:, hs], ke[:, hs]), 0.0)
        os_.append(_dot(att, v[:, hs]) + _dot_nt(qs[:, hs], st_t))
        sst[h] = e_last[:, hs] * st_t + _dot_tn(v[:, hs], kd[:, hs])
    o = jnp.concatenate(os_, axis=-1)
    o_ref[0] = _head_rms_gate(o, g_ref[0], nw_ref[...], l_blk)

    @pl.when(c_idx == n_chunks - 1)
    def _():
        s1_ref[0] = sst[...]


def hgrn_mixer(proj, s0, lb_raw, norm_w, layer):
    b, L, _ = proj.shape
    cs = LIN_CHUNK
    l_blk = min(cs, L)
    n_chunks = L // l_blk
    col = lambda k: pl.BlockSpec((1, l_blk, GROUP), lambda i, j, k=k: (i, j, k))
    st = pl.BlockSpec((1, HEADS, HEAD_DIM, HEAD_DIM), lambda i, j: (i, 0, 0, 0))
    return pl.pallas_call(
        functools.partial(_hgrn_kernel, cs=cs, l_blk=l_blk, n_chunks=n_chunks, layer=layer),
        grid=(b, n_chunks),
        in_specs=[col(COL_HGRN), col(COL_HGRN + 1), col(COL_HGRN + 2), col(COL_HGRN + 3), st,
                  _resident(lb_raw.shape), _resident((1, GROUP))],
        out_specs=[pl.BlockSpec((1, l_blk, GROUP), lambda i, j: (i, j, 0)), st],
        out_shape=[jax.ShapeDtypeStruct((b, L, GROUP), F32),
                   jax.ShapeDtypeStruct((b, HEADS, HEAD_DIM, HEAD_DIM), F32)],
        scratch_shapes=[pltpu.VMEM((3, cs, GROUP), F32),
                        pltpu.VMEM((HEADS, HEAD_DIM, HEAD_DIM), F32)],
        compiler_params=_params("parallel", "arbitrary"),
        name="hgrn_mixer",
    )(proj, proj, proj, proj, s0, lb_raw, jnp.tile(norm_w, HEADS).reshape(1, GROUP))


def _unit_lower_inverse_minus_eye(a):
    n = a.shape[0]
    r = lax.broadcasted_iota(jnp.int32, (n, n), 0)
    c = lax.broadcasted_iota(jnp.int32, (n, n), 1)
    dx = jnp.zeros((n, n), F32)
    s = 1
    while s < n:
        lower_left = jnp.logical_and((r // (2 * s)) == (c // (2 * s)),
                                     jnp.logical_and((r // s) % 2 == 1, (c // s) % 2 == 0))
        b = jnp.where(lower_left, a, 0.0)
        m = b + _dot(dx, b)
        dx = dx - m - _dot(m, dx)
        s *= 2
    return dx


def _gdn_kernel(q_ref, k_ref, v_ref, z_ref, sm_ref, conv0_ref, s0_ref, cw_ref, vec_ref, nw_ref,
                o_ref, conv1_ref, s1_ref, buf, smbuf, sst, *, cs, l_blk, n_chunks):
    c_idx = pl.program_id(1)

    @pl.when(c_idx == 0)
    def _():
        buf[5:8, :] = conv0_ref[0]
        sst[...] = s0_ref[0]

    _stage_rows(buf.at[:, 0:GROUP], q_ref, l_blk, 8)
    _stage_rows(buf.at[:, GROUP:2 * GROUP], k_ref, l_blk, 8)
    _stage_rows(buf.at[:, 2 * GROUP:3 * GROUP], v_ref, l_blk, 8)
    _stage_rows(smbuf, sm_ref, l_blk)

    qkv = _silu(_causal_conv(buf, cw_ref, cs))
    conv_tail = buf[5 + l_blk:8 + l_blk, :]
    buf[5:8, :] = conv_tail
    q = qkv[:, 0:GROUP]
    k = qkv[:, GROUP:2 * GROUP]
    v = qkv[:, 2 * GROUP:3 * GROUP]
    blocks = _head_blocks(GROUP, HEAD_DIM)
    q = q * lax.rsqrt(_dot_f32(q * q, blocks) + EPS) * (HEAD_DIM ** -0.5)
    k = k * lax.rsqrt(_dot_f32(k * k, blocks) + EPS)

    sm = smbuf[...]
    beta = _sigmoid(sm)
    log_g = -jnp.exp(vec_ref[1:2, :]) * _softplus(sm + vec_ref[0:1, :])
    if l_blk < cs:
        rows = lax.broadcasted_iota(jnp.int32, (cs, LANES), 0)
        beta = jnp.where(rows < l_blk, beta, 0.0)
        log_g = jnp.where(rows < l_blk, log_g, 0.0)
    gam = _dot_f32(_tril(cs).astype(F32), log_g)
    beta_w = _dot_f32(beta, _head_expand(LANES, GROUP, HEAD_DIM, HEADS))
    gam_w = _dot_f32(gam, _head_expand(LANES, GROUP, HEAD_DIM, 2 * HEADS))
    g_last = gam_w[cs - 1:cs, :]
    e_gam = jnp.exp(gam_w)
    kb = k * beta_w
    vb = v * beta_w
    kbg = kb * e_gam
    qg = q * e_gam
    k_dec = k * jnp.exp(g_last - gam_w)
    e_last = jnp.exp(g_last)
    gam_t = gam.T
    incl = _tril(cs)
    strict = _tril(cs, strict=True)

    os_ = []
    for h in range(HEADS):
        hs = slice(h * HEAD_DIM, (h + 1) * HEAD_DIM)
        r = 2 * HEADS + h
        st = sst[h]
        dec_incl = jnp.exp(jnp.where(incl, gam[:, r:r + 1] - gam_t[r:r + 1, :], NEG_INF))
        a_mat = jnp.where(strict, _dot_nt(kb[:, hs], k[:, hs]) * dec_incl, 0.0)
        tx = _unit_lower_inverse_minus_eye(a_mat)
        u = vb[:, hs] + _dot(tx, vb[:, hs])
        w = kbg[:, hs] + _dot(tx, kbg[:, hs])
        v_new = u - _dot(w, st)
        qk = _dot_nt(q[:, hs], k[:, hs]) * dec_incl
        os_.append(_dot(qk, v_new) + _dot(qg[:, hs], st))
        sst[h] = e_last[:, h * HEAD_DIM:h * HEAD_DIM + 1] * st + _dot_tn(k_dec[:, hs], v_new)
    o = jnp.concatenate(os_, axis=-1)
    o_ref[0] = _head_rms_gate(o, z_ref[0], nw_ref[...], l_blk)

    @pl.when(c_idx == n_chunks - 1)
    def _():
        conv1_ref[0] = conv_tail
        s1_ref[0] = sst[...]


def gdn_mixer(proj, conv0, s0, conv_w, a_log, dt_bias, norm_w):
    b, L, _ = proj.shape
    cs = LIN_CHUNK
    l_blk = min(cs, L)
    n_chunks = L // l_blk
    vec = jnp.zeros((8, LANES), F32)
    vec = vec.at[0, 2 * HEADS:3 * HEADS].set(dt_bias).at[1, 2 * HEADS:3 * HEADS].set(a_log)
    col = lambda k: pl.BlockSpec((1, l_blk, GROUP), lambda i, j, k=k: (i, j, k))
    per_b = lambda shp: pl.BlockSpec((1,) + shp, lambda i, j: (i,) + (0,) * len(shp))
    c3 = 3 * GROUP
    return pl.pallas_call(
        functools.partial(_gdn_kernel, cs=cs, l_blk=l_blk, n_chunks=n_chunks),
        grid=(b, n_chunks),
        in_specs=[col(COL_GDN), col(COL_GDN + 1), col(COL_GDN + 2), col(COL_GDN + 3),
                  pl.BlockSpec((1, l_blk, LANES), lambda i, j: (i, j, COL_SMALL)),
                  per_b((CONV_W - 1, c3)), per_b((HEADS, HEAD_DIM, HEAD_DIM)),
                  _resident((CONV_W, c3)), _resident((8, LANES)), _resident((1, GROUP))],
        out_specs=[pl.BlockSpec((1, l_blk, GROUP), lambda i, j: (i, j, 0)),
                   per_b((CONV_W - 1, c3)), per_b((HEADS, HEAD_DIM, HEAD_DIM))],
        out_shape=[jax.ShapeDtypeStruct((b, L, GROUP), F32),
                   jax.ShapeDtypeStruct((b, CONV_W - 1, c3), F32),
                   jax.ShapeDtypeStruct((b, HEADS, HEAD_DIM, HEAD_DIM), F32)],
        scratch_shapes=[pltpu.VMEM((cs + 8, c3), F32), pltpu.VMEM((cs, LANES), F32),
                        pltpu.VMEM((HEADS, HEAD_DIM, HEAD_DIM), F32)],
        compiler_params=_params("parallel", "arbitrary"),
        name="gdn_mixer",
    )(proj, proj, proj, proj, proj, conv0, s0, conv_w, vec,
      jnp.tile(norm_w, HEADS).reshape(1, GROUP))


def _moba_prompt_kernel(q_ref, k_ref, v_ref, o_ref, ks, vts, kmean, sel_t, o_t, *, nb):
    blk = MOBA_BLOCK
    qi = pl.program_id(1)

    @pl.when(qi == 0)
    def _():
        for j in range(nb):
            kj = k_ref[0, j * blk:(j + 1) * blk, :]
            kmean[j:j + 1, :] = jnp.mean(kj, axis=0, keepdims=True)
            kjb = kj.astype(BF16)
            for h in range(HEADS):
                ks[h, j] = kjb[:, h * HEAD_DIM:(h + 1) * HEAD_DIM]
            vts[j] = v_ref[0, j * blk:(j + 1) * blk, :].T.astype(BF16)

    q_t = (q_ref[0] * (HEAD_DIM ** -0.5)).T
    q_tb = q_t.astype(BF16)
    blk_row = lax.broadcasted_iota(jnp.int32, (nb, blk), 0)
    rel = (lax.broadcasted_iota(jnp.int32, (blk, blk), 1)
           - lax.broadcasted_iota(jnp.int32, (blk, blk), 0)).astype(F32)

    for h in range(HEADS):
        hs = slice(h * HEAD_DIM, (h + 1) * HEAD_DIM)
        slope = ALIBI_SLOPES[h]
        gate = jnp.where(blk_row < qi, _dot_f32(kmean[:, hs], q_t[hs, :]), NEG_INF)
        sel = jnp.full((nb, blk), NEG_INF, F32)
        for _ in range(MOBA_TOPK):
            top = jnp.max(gate, axis=0, keepdims=True)
            first = jnp.min(jnp.where(gate == top, blk_row, nb), axis=0, keepdims=True)
            pick = blk_row == first
            sel = jnp.where(jnp.logical_and(pick, blk_row < qi), 0.0, sel)
            gate = jnp.where(pick, NEG_INF, gate)
        sel_t[h] = sel

        qh = q_tb[hs, :]
        bias = -slope * rel
        s = _dot(ks[h, qi], qh)
        s = jnp.where(rel >= 0.0, s + bias, NEG_INF)
        m0 = jnp.max(s, axis=0, keepdims=True)
        p = jnp.exp(s - m0)
        l0 = jnp.sum(p, axis=0, keepdims=True)
        acc0 = _dot(vts[qi, hs, :], p)

        def body(j, carry, h=h, qh=qh, bias=bias, slope=slope):
            m, l, acc = carry
            off = (qi - j).astype(F32) * (slope * blk)
            s = _dot(ks[h, j], qh) + bias + (sel_t[h, pl.ds(j, 1), :] - off)
            m_new = jnp.maximum(m, jnp.max(s, axis=0, keepdims=True))
            alpha = jnp.exp(m - m_new)
            p = jnp.exp(s - m_new)
            l = alpha * l + jnp.sum(p, axis=0, keepdims=True)
            acc = alpha * acc + _dot(vts[j, h * HEAD_DIM:(h + 1) * HEAD_DIM, :], p)
            return m_new, l, acc

        _, l, acc = lax.fori_loop(0, qi, body, (m0, l0, acc0))
        o_t[hs, :] = acc / l
    o_ref[0] = o_t[...].T


def moba_prompt(proj):
    b, S, _ = proj.shape
    blk = MOBA_BLOCK
    nb = S // blk
    seq = lambda k: pl.BlockSpec((1, S, GROUP), lambda i, j, k=k: (i, 0, k))
    return pl.pallas_call(
        functools.partial(_moba_prompt_kernel, nb=nb),
        grid=(b, nb),
        in_specs=[pl.BlockSpec((1, blk, GROUP), lambda i, j: (i, j, COL_MOBA)),
                  seq(COL_MOBA + 1), seq(COL_MOBA + 2)],
        out_specs=pl.BlockSpec((1, blk, GROUP), lambda i, j: (i, j, 0)),
        out_shape=jax.ShapeDtypeStruct((b, S, GROUP), F32),
        scratch_shapes=[pltpu.VMEM((HEADS, nb, blk, HEAD_DIM), BF16),
                        pltpu.VMEM((nb, GROUP, blk), BF16),
                        pltpu.VMEM((nb, GROUP), F32),
                        pltpu.VMEM((HEADS, nb, blk), F32),
                        pltpu.VMEM((GROUP, blk), F32)],
        compiler_params=_params("parallel", "arbitrary"),
        name="moba_prompt",
    )(proj, proj, proj)


PAGES_PER_BLOCK = MOBA_BLOCK // PAGE_SIZE
SELECT_PAGE_BUFFERS = 8


def _moba_select_kernel(pt_ref, q_ref, kc_ref, idx_ref, pages, kmean, q8, sems, *,
                        layer, n_pages, t):
    b = pl.program_id(0)
    nbuf = SELECT_PAGE_BUFFERS
    n_blocks = n_pages // PAGES_PER_BLOCK
    blocks_per_iter = nbuf // PAGES_PER_BLOCK

    def page_copy(p, slot):
        return pltpu.make_async_copy(kc_ref.at[layer, pt_ref[b, p]], pages.at[slot], sems.at[slot])

    for s in range(nbuf):
        page_copy(s, s).start()

    def body(it, carry):
        for u in range(blocks_per_iter):
            n = it * blocks_per_iter + u
            tot = jnp.zeros((1, GROUP), F32)
            for pp in range(PAGES_PER_BLOCK):
                slot = u * PAGES_PER_BLOCK + pp
                p = n * PAGES_PER_BLOCK + pp
                page_copy(p, slot).wait()
                tot = tot + jnp.sum(pages[slot], axis=0, keepdims=True)

                @pl.when(p + nbuf < n_pages)
                def _():
                    page_copy(p + nbuf, slot).start()
            kmean[pl.ds(n, 1), :] = tot * (1.0 / MOBA_BLOCK)
        return carry

    lax.fori_loop(0, n_blocks // blocks_per_iter, body, 0)

    q8[0:t, :] = q_ref[0]
    q8[t:, :] = jnp.zeros((SUBLANES - t, GROUP), F32)
    blk_lane = lax.broadcasted_iota(jnp.int32, (SUBLANES, n_blocks), 1)
    out_lane = lax.broadcasted_iota(jnp.int32, (SUBLANES, LANES), 1)
    res = jnp.zeros((SUBLANES, LANES), jnp.int32)
    for h in range(HEADS):
        hs = slice(h * HEAD_DIM, (h + 1) * HEAD_DIM)
        gate = lax.dot_general(q8[:, hs], kmean[:, hs], (((1,), (1,)), ((), ())),
                               preferred_element_type=F32, precision=lax.Precision.HIGHEST)
        for r in range(MOBA_TOPK):
            top = jnp.max(gate, axis=1, keepdims=True)
            first = jnp.min(jnp.where(gate == top, blk_lane, n_blocks), axis=1, keepdims=True)
            res = jnp.where(out_lane == h * MOBA_TOPK + r, first, res)
            gate = jnp.where(blk_lane == first, NEG_INF, gate)
    idx_ref[0] = res


def _moba_sample_attn_kernel(pt_ref, idx_ref, q_ref, kn_ref, vn_ref, kc_ref, vc_ref, o_ref,
                             kbuf, vbuf, st8, o8, ksem, vsem, *, layer, past, t):
    b = pl.program_id(0)
    n_sel = MOBA_TOPK * MOBA_BLOCK

    def block_of(tok, h, r):
        return idx_ref[b, tok * (HEADS * MOBA_TOPK) + h * MOBA_TOPK + r]

    def copies(tok, h, slot):
        out = []
        for r in range(MOBA_TOPK):
            blk = block_of(tok, h, r)
            for pp in range(PAGES_PER_BLOCK):
                phys = pt_ref[b, blk * PAGES_PER_BLOCK + pp]
                rows = pl.ds((r * PAGES_PER_BLOCK + pp) * PAGE_SIZE, PAGE_SIZE)
                out.append(pltpu.make_async_copy(kc_ref.at[layer, phys], kbuf.at[slot, rows],
                                                 ksem.at[slot]))
                out.append(pltpu.make_async_copy(vc_ref.at[layer, phys], vbuf.at[slot, rows],
                                                 vsem.at[slot]))
        return out

    pairs = [(tok, h) for h in range(HEADS) for tok in range(t)]
    for c in copies(*pairs[0], 0):
        c.start()

    for i, ref in enumerate((q_ref, kn_ref, vn_ref)):
        st8[i, 0:t, :] = ref[0]
        st8[i, t:, :] = jnp.zeros((SUBLANES - t, GROUP), F32)

    lane = lax.broadcasted_iota(jnp.int32, (1, n_sel), 1)
    row = lax.broadcasted_iota(jnp.int32, (SUBLANES, 1), 0)
    for i, (tok, h) in enumerate(pairs):
        slot = i % 2
        if i + 1 < len(pairs):
            for c in copies(*pairs[i + 1], 1 - slot):
                c.start()
        for c in copies(tok, h, slot):
            c.wait()
        hs = slice(h * HEAD_DIM, (h + 1) * HEAD_DIM)
        slope = ALIBI_SLOPES[h]
        qrow = st8[0, tok:tok + 1, hs] * (HEAD_DIM ** -0.5)
        s_sel = _dot_nt(jnp.broadcast_to(qrow, (SUBLANES, HEAD_DIM)), kbuf[slot, :, hs])[0:1]
        blk = jnp.where(lane < MOBA_BLOCK, block_of(tok, h, 0),
                        jnp.where(lane < 2 * MOBA_BLOCK, block_of(tok, h, 1), block_of(tok, h, 2)))
        pos = blk * MOBA_BLOCK + (lane % MOBA_BLOCK)
        s_sel = s_sel - slope * (past + tok - pos).astype(F32)
        s_own = jnp.sum(st8[1, :, hs] * qrow, axis=1, keepdims=True)
        s_own = jnp.where(row <= tok, s_own - slope * (tok - row).astype(F32), NEG_INF)
        m = jnp.maximum(jnp.max(s_sel, axis=1, keepdims=True), jnp.max(s_own, axis=0, keepdims=True))
        p_sel = jnp.exp(s_sel - m)
        p_own = jnp.exp(s_own - m)
        l = jnp.sum(p_sel, axis=1, keepdims=True) + jnp.sum(p_own, axis=0, keepdims=True)
        o = _dot(jnp.broadcast_to(p_sel, (SUBLANES, n_sel)), vbuf[slot, :, hs])[0:1]
        o = o + jnp.sum(p_own * st8[2, :, hs], axis=0, keepdims=True)
        o8[tok:tok + 1, hs] = o / l
    o_ref[0] = o8[0:t, :]


def moba_sample(proj, k_cache, v_cache, page_table, layer):
    db, t, _ = proj.shape
    n_pages = page_table.shape[1]
    past = n_pages * PAGE_SIZE
    assert past % MOBA_BLOCK == 0 and n_pages % SELECT_PAGE_BUFFERS == 0 and t <= SUBLANES
    assert past // MOBA_BLOCK >= MOBA_TOPK
    tok = lambda k: pl.BlockSpec((1, t, GROUP), lambda i, *_: (i, 0, k))
    hbm = pl.BlockSpec(memory_space=pl.ANY)
    idx = pl.pallas_call(
        functools.partial(_moba_select_kernel, layer=layer, n_pages=n_pages, t=t),
        grid_spec=pltpu.PrefetchScalarGridSpec(
            num_scalar_prefetch=1, grid=(db,),
            in_specs=[tok(COL_MOBA), hbm],
            out_specs=pl.BlockSpec((1, SUBLANES, LANES), lambda i, *_: (i, 0, 0)),
            scratch_shapes=[pltpu.VMEM((SELECT_PAGE_BUFFERS, PAGE_SIZE, GROUP), F32),
                            pltpu.VMEM((n_pages // PAGES_PER_BLOCK, GROUP), F32),
                            pltpu.VMEM((SUBLANES, GROUP), F32),
                            pltpu.SemaphoreType.DMA((SELECT_PAGE_BUFFERS,))]),
        out_shape=jax.ShapeDtypeStruct((db, SUBLANES, LANES), jnp.int32),
        compiler_params=_params("arbitrary"),
        name="moba_sample_select",
    )(page_table, proj, k_cache)
    idx = idx[:, :t, :HEADS * MOBA_TOPK].reshape(db, t * HEADS * MOBA_TOPK)
    return pl.pallas_call(
        functools.partial(_moba_sample_attn_kernel, layer=layer, past=past, t=t),
        grid_spec=pltpu.PrefetchScalarGridSpec(
            num_scalar_prefetch=2, grid=(db,),
            in_specs=[tok(COL_MOBA), tok(COL_MOBA + 1), tok(COL_MOBA + 2), hbm, hbm],
            out_specs=pl.BlockSpec((1, t, GROUP), lambda i, *_: (i, 0, 0)),
            scratch_shapes=[pltpu.VMEM((2, MOBA_TOPK * MOBA_BLOCK, GROUP), F32),
                            pltpu.VMEM((2, MOBA_TOPK * MOBA_BLOCK, GROUP), F32),
                            pltpu.VMEM((3, SUBLANES, GROUP), F32),
                            pltpu.VMEM((SUBLANES, GROUP), F32),
                            pltpu.SemaphoreType.DMA((2,)), pltpu.SemaphoreType.DMA((2,))]),
        out_shape=jax.ShapeDtypeStruct((db, t, GROUP), F32),
        compiler_params=_params("arbitrary"),
        name="moba_sample_attn",
    )(page_table, idx, proj, proj, proj, k_cache, v_cache)


def _rearrange_w_in(w):
    d = w.shape[0]
    dt0 = COL_HGRN * GROUP
    ba0 = dt0 + HEADS + (COL_GDN + 4 - COL_HGRN) * GROUP
    wide = jnp.concatenate([w[:, :dt0], w[:, dt0 + HEADS:ba0]], axis=1)
    narrow = jnp.concatenate([w[:, dt0:dt0 + HEADS], w[:, ba0:ba0 + 2 * HEADS],
                              jnp.zeros((d, LANES - N_SMALL), w.dtype)], axis=1)
    return jnp.concatenate([wide, narrow], axis=1)


def _layer(x, mem_k, mem_v, states, lp, layer, attend, prompt):
    b, L, d = x.shape
    ssm_conv0, ssm0, hgrn0, gdn_conv0, gdn0 = states
    w_in = lp['w_in'].astype(BF16) if prompt else lp['w_in']
    proj = norm_matmul(x.reshape(b * L, d), lp['n_mix_pre'], w_in, min(256, b * L)).reshape(b, L, -1)
    o_a = attend(proj)
    o_b, ssm_conv1, ssm1 = ssd_mixer(proj, ssm_conv0, ssm0, lp['ssm_conv_w'], lp['ssm_conv_b'],
                                     lp['ssm_dt_bias'], lp['ssm_a_log'], lp['ssm_d'], lp['ssm_norm'])
    o_c, hgrn1 = hgrn_mixer(proj, jnp.swapaxes(hgrn0, -1, -2), lp['hgrn_lb_raw'], lp['hgrn_norm'], layer)
    o_d, gdn_conv1, gdn1 = gdn_mixer(proj, gdn_conv0, gdn0, lp['gdn_conv_w'], lp['gdn_a_log'],
                                     lp['gdn_dt_bias'], lp['gdn_norm'])
    post = post_mixer if prompt else post_mixer_sample
    args = (x, (o_a, o_b, o_c, o_d), mem_k, mem_v, lp['norms'], lp['w_out'], lp['w_xq'], lp['w_xo'],
            lp['w_gu'], lp['w_down'])
    x = post(*args, 256) if prompt else post(*args)
    k = proj[..., GROUP:2 * GROUP].reshape(b, L, HEADS, HEAD_DIM)
    v = proj[..., 2 * GROUP:3 * GROUP].reshape(b, L, HEADS, HEAD_DIM)
    return x, (k, v, ssm_conv1, ssm1, jnp.swapaxes(hgrn1, -1, -2), gdn_conv1, gdn1)


def kernel(x_prompt, x_sample, mem_prompt, cache_moba_k, cache_moba_v, page_table, cache_mem_k, cache_mem_v, state_ssm_conv, state_ssm, state_hgrn, state_gdn_conv, state_gdn, n_mix_pre, n_mix_post, w_in, w_out, ssm_conv_w, ssm_conv_b, ssm_dt_bias, ssm_a_log, ssm_d, ssm_norm, hgrn_lb_raw, hgrn_norm, gdn_conv_w, gdn_a_log, gdn_dt_bias, gdn_norm, n_x_pre, n_x_post, mem_norm, w_xq, w_xkv, w_xo, n_f_pre, n_f_post, w_gu, w_down):
    depth = w_in.shape[0]
    bp, _, d = x_prompt.shape
    db = x_sample.shape[0]
    n_mem = mem_prompt.shape[1]
    kc = cache_moba_k.reshape(*cache_moba_k.shape[:3], GROUP)
    vc = cache_moba_v.reshape(*cache_moba_v.shape[:3], GROUP)
    zeros_p = (jnp.zeros((bp,) + state_ssm_conv.shape[2:], F32), jnp.zeros((bp,) + state_ssm.shape[2:], F32),
               jnp.zeros((bp,) + state_hgrn.shape[2:], F32), jnp.zeros((bp,) + state_gdn_conv.shape[2:], F32),
               jnp.zeros((bp,) + state_gdn.shape[2:], F32))
    yp, ys = x_prompt, x_sample
    outs_p, outs_s, mem_ks, mem_vs = [], [], [], []
    for l in range(depth):
        norms = jnp.zeros((SUBLANES, d), F32)
        for i, nrm in enumerate((n_mix_post, n_x_pre, n_x_post, n_f_pre, n_f_post)):
            norms = norms.at[i].set(nrm[l])
        lp = {'n_mix_pre': n_mix_pre[l], 'w_in': _rearrange_w_in(w_in[l]), 'norms': norms,
              'w_out': w_out[l].astype(BF16), 'w_xq': w_xq[l].astype(BF16), 'w_xo': w_xo[l].astype(BF16),
              'w_gu': w_gu[l].astype(BF16), 'w_down': w_down[l].astype(BF16),
              'ssm_conv_w': ssm_conv_w[l], 'ssm_conv_b': ssm_conv_b[l], 'ssm_dt_bias': ssm_dt_bias[l],
              'ssm_a_log': ssm_a_log[l], 'ssm_d': ssm_d[l], 'ssm_norm': ssm_norm[l],
              'hgrn_lb_raw': hgrn_lb_raw, 'hgrn_norm': hgrn_norm[l], 'gdn_conv_w': gdn_conv_w[l],
              'gdn_a_log': gdn_a_log[l], 'gdn_dt_bias': gdn_dt_bias[l], 'gdn_norm': gdn_norm[l]}
        mkv = norm_matmul(mem_prompt.reshape(bp * n_mem, d), mem_norm[l], w_xkv[l].astype(BF16), 256)
        mk = mkv[:, :d].reshape(bp, n_mem, d)
        mv = mkv[:, d:].reshape(bp, n_mem, d)
        yp, st_p = _layer(yp, mk.astype(BF16), mv.astype(BF16), zeros_p, lp, l, moba_prompt, True)
        outs_p.append(st_p)
        mem_ks.append(mk.reshape(bp, n_mem, X_HEADS, d // X_HEADS))
        mem_vs.append(mv.reshape(bp, n_mem, X_HEADS, d // X_HEADS))
        states_s = (state_ssm_conv[l], state_ssm[l], state_hgrn[l], state_gdn_conv[l], state_gdn[l])
        attend_s = functools.partial(moba_sample, k_cache=kc, v_cache=vc, page_table=page_table, layer=l)
        ys, st_s = _layer(ys, cache_mem_k[l].reshape(db, n_mem, d).astype(BF16),
                          cache_mem_v[l].reshape(db, n_mem, d).astype(BF16), states_s, lp, l, attend_s, False)
        outs_s.append(st_s)
    stack = lambda outs, i: jnp.stack([o[i] for o in outs], axis=0)
    return (yp, ys, stack(outs_p, 0), stack(outs_s, 0), stack(outs_p, 1), stack(outs_s, 1),
            jnp.stack(mem_ks, axis=0), jnp.stack(mem_vs, axis=0),
            stack(outs_p, 2), stack(outs_s, 2), stack(outs_p, 3), stack(outs_s, 3),
            stack(outs_p, 4), stack(outs_s, 4), stack(outs_p, 5), stack(outs_s, 5),
            stack(outs_p, 6), stack(outs_s, 6))
```

```python
import functools

import numpy as np
import jax
import jax.numpy as jnp
from jax import lax
from jax.experimental import pallas as pl
from jax.experimental.pallas import tpu as pltpu

F32 = jnp.float32
BF16 = jnp.bfloat16

LANES = 128
SUBLANES = 8
VMEM_LIMIT_BYTES = 56 * 1024 * 1024

GROUP = 256
HEADS = 4
HEAD_DIM = 64
CONV_W = 4
SSM_DSTATE = 128
SSM_CHUNK = 128
LIN_CHUNK = 64
MOBA_BLOCK = 256
MOBA_TOPK = 3
PAGE_SIZE = 128
X_HEADS = 4
N_SMALL = 12
COL_MOBA = 0
COL_SSM_Z = 3
COL_SSM_X = 4
COL_HGRN = 7
COL_GDN = 11
N_WIDE = 15 * GROUP
COL_SMALL = N_WIDE // LANES
EPS = 1e-6
NEG_INF = float("-inf")
ALIBI_SLOPES = tuple(2.0 ** (-8.0 * (h + 1) / HEADS) for h in range(HEADS))


def _rms(x, w):
    return x * lax.rsqrt(jnp.mean(x * x, axis=-1, keepdims=True) + EPS) * w


def _sigmoid(x):
    return 1.0 / (1.0 + jnp.exp(-x))


def _silu(x):
    return x * _sigmoid(x)


def _softplus(x):
    return jnp.maximum(x, 0.0) + jnp.log(1.0 + jnp.exp(-jnp.abs(x)))


def _dot(a, b):
    return jnp.dot(a.astype(BF16), b.astype(BF16), preferred_element_type=F32)


def _dot_nt(a, b):
    return lax.dot_general(a.astype(BF16), b.astype(BF16), (((1,), (1,)), ((), ())),
                           preferred_element_type=F32)


def _dot_f32(a, b):
    return jnp.dot(a, b, preferred_element_type=F32, precision=lax.Precision.HIGHEST)


def _params(*sem):
    return pltpu.CompilerParams(dimension_semantics=sem, vmem_limit_bytes=VMEM_LIMIT_BYTES)


def _resident(shape):
    return pl.BlockSpec(shape, lambda *_: (0,) * len(shape), pipeline_mode=pl.Buffered(1))


def _norm_matmul_kernel(x_ref, nw_ref, w_ref, o_ref, *, full_precision):
    xn = _rms(x_ref[...], nw_ref[...])
    o_ref[...] = _dot_f32(xn, w_ref[...]) if full_precision else _dot(xn, w_ref[...])


def norm_matmul(x, norm_w, w, tm):
    n, d = x.shape
    c = w.shape[1]
    return pl.pallas_call(
        functools.partial(_norm_matmul_kernel, full_precision=(w.dtype == F32)),
        grid=(n // tm,),
        in_specs=[pl.BlockSpec((tm, d), lambda i: (i, 0)),
                  _resident((1, d)),
                  _resident((d, c))],
        out_specs=pl.BlockSpec((tm, c), lambda i: (i, 0)),
        out_shape=jax.ShapeDtypeStruct((n, c), F32),
        compiler_params=_params("parallel"),
        name="norm_matmul",
    )(x, norm_w.reshape(1, d), w)


def _mix_out_and_query(x, mixed, nw, wout_ref, wxq_ref):
    x = x + _rms(_dot(mixed, wout_ref[...]), nw[0:1])
    return x, _dot(_rms(x, nw[1:2]), wxq_ref[...])


def _memory_attention(q, mk, mv):
    xdh = q.shape[1] // X_HEADS
    heads = []
    for h in range(X_HEADS):
        sl = slice(h * xdh, (h + 1) * xdh)
        s = _dot_nt(q[:, sl], mk[:, sl]) * (xdh ** -0.5)
        p = jnp.exp(s - jnp.max(s, axis=-1, keepdims=True))
        heads.append(_dot(p, mv[:, sl]) / jnp.sum(p, axis=-1, keepdims=True))
    return jnp.concatenate(heads, axis=-1)


def _attn_out_and_ffn(x, att, nw, wxo_ref, wgu_ref, wdown_ref):
    x = x + _rms(_dot(att, wxo_ref[...]), nw[2:3])
    gu = _dot(_rms(x, nw[3:4]), wgu_ref[...])
    hid = gu.shape[1] // 2
    act = _silu(gu[:, :hid]) * gu[:, hid:]
    return x + _rms(_dot(act, wdown_ref[...]), nw[4:5])


def _post_mixer_kernel(x_ref, oa_ref, ob_ref, oc_ref, od_ref, mk_ref, mv_ref, norms_ref, wout_ref,
                       wxq_ref, wxo_ref, wgu_ref, wdown_ref, o_ref):
    nw = norms_ref[...]
    mixed = jnp.concatenate([oa_ref[0], ob_ref[0], oc_ref[0], od_ref[0]], axis=-1)
    x, q = _mix_out_and_query(x_ref[0], mixed, nw, wout_ref, wxq_ref)
    att = _memory_attention(q, mk_ref[0], mv_ref[0])
    o_ref[0] = _attn_out_and_ffn(x, att, nw, wxo_ref, wgu_ref, wdown_ref)


def post_mixer(x, mix_parts, mk, mv, norms, w_out, w_xq, w_xo, w_gu, w_down, tm):
    b, L, d = x.shape
    m = mk.shape[1]
    row = pl.BlockSpec((1, tm, d), lambda i, j: (i, j, 0))
    part = pl.BlockSpec((1, tm, GROUP), lambda i, j: (i, j, 0))
    mem = pl.BlockSpec((1, m, d), lambda i, j: (i, 0, 0))
    return pl.pallas_call(
        _post_mixer_kernel,
        grid=(b, L // tm),
        in_specs=[row, part, part, part, part, mem, mem, _resident(norms.shape),
                  _resident(w_out.shape), _resident(w_xq.shape), _resident(w_xo.shape),
                  _resident(w_gu.shape), _resident(w_down.shape)],
        out_specs=row,
        out_shape=jax.ShapeDtypeStruct((b, L, d), F32),
        compiler_params=_params("parallel", "parallel"),
        name="post_mixer",
    )(x, *mix_parts, mk, mv, norms, w_out, w_xq, w_xo, w_gu, w_down)


def _sample_pre_kernel(x_ref, oa_ref, ob_ref, oc_ref, od_ref, norms_ref, wout_ref, wxq_ref,
                       x1_ref, q_ref):
    mixed = jnp.concatenate([oa_ref[...], ob_ref[...], oc_ref[...], od_ref[...]], axis=-1)
    x1, q = _mix_out_and_query(x_ref[...], mixed, norms_ref[...], wout_ref, wxq_ref)
    x1_ref[...] = x1
    q_ref[...] = q


def _sample_attn_kernel(q_ref, mk_ref, mv_ref, o_ref, q8, *, t):
    q8[0:t, :] = q_ref[0]
    q8[t:, :] = jnp.zeros((q8.shape[0] - t, q8.shape[1]), F32)
    o_ref[0] = _memory_attention(q8[...], mk_ref[0], mv_ref[0])[0:t]


def _sample_post_kernel(x_ref, att_ref, norms_ref, wxo_ref, wgu_ref, wdown_ref, o_ref):
    o_ref[...] = _attn_out_and_ffn(x_ref[...], att_ref[...], norms_ref[...], wxo_ref, wgu_ref,
                                   wdown_ref)


def post_mixer_sample(x, mix_parts, mk, mv, norms, w_out, w_xq, w_xo, w_gu, w_down):
    b, t, d = x.shape
    n = b * t
    m = mk.shape[1]
    flat = lambda a: a.reshape(n, a.shape[-1])
    full = lambda shp: pl.BlockSpec(shp, lambda i: (0,) * len(shp))
    x1, q = pl.pallas_call(
        _sample_pre_kernel,
        grid=(1,),
        in_specs=[full((n, d))] + [full((n, GROUP))] * 4
                 + [full(norms.shape), _resident(w_out.shape), _resident(w_xq.shape)],
        out_specs=[full((n, d)), full((n, d))],
        out_shape=[jax.ShapeDtypeStruct((n, d), F32)] * 2,
        compiler_params=_params("arbitrary"),
        name="sample_pre",
    )(flat(x), *[flat(p) for p in mix_parts], norms, w_out, w_xq)
    seq = pl.BlockSpec((1, t, d), lambda i: (i, 0, 0))
    mem = pl.BlockSpec((1, m, d), lambda i: (i, 0, 0))
    att = pl.pallas_call(
        functools.partial(_sample_attn_kernel, t=t),
        grid=(b,),
        in_specs=[seq, mem, mem],
        out_specs=seq,
        out_shape=jax.ShapeDtypeStruct((b, t, d), F32),
        scratch_shapes=[pltpu.VMEM((SUBLANES, d), F32)],
        compiler_params=_params("parallel"),
        name="sample_attn",
    )(q.reshape(b, t, d), mk, mv)
    out = pl.pallas_call(
        _sample_post_kernel,
        grid=(1,),
        in_specs=[full((n, d)), full((n, d)), full(norms.shape), _resident(w_xo.shape),
                  _resident(w_gu.shape), _resident(w_down.shape)],
        out_specs=full((n, d)),
        out_shape=jax.ShapeDtypeStruct((n, d), F32),
        compiler_params=_params("arbitrary"),
        name="sample_post",
    )(x1, flat(att), norms, w_xo, w_gu, w_down)
    return out.reshape(b, t, d)


def _tril(n, strict=False):
    r = lax.broadcasted_iota(jnp.int32, (n, n), 0)
    c = lax.broadcasted_iota(jnp.int32, (n, n), 1)
    return (r > c) if strict else (r >= c)


def _head_expand(n_rows, n_cols, width, offset=0):
    r = lax.broadcasted_iota(jnp.int32, (n_rows, n_cols), 0)
    c = lax.broadcasted_iota(jnp.int32, (n_rows, n_cols), 1)
    return (r == (c // width) + offset).astype(F32)


def _stage_rows(dst, src_ref, l_blk, row0=0):
    n = dst.shape[0]
    dst[row0:row0 + l_blk, :] = src_ref[0]
    if row0 + l_blk < n:
        dst[row0 + l_blk:n, :] = jnp.zeros((n - row0 - l_blk, dst.shape[1]), F32)


def _causal_conv(buf, cw_ref, cs):
    out = cw_ref[0:1, :] * buf[5:5 + cs, :]
    for j in range(1, CONV_W):
        out = out + cw_ref[j:j + 1, :] * buf[5 + j:5 + j + cs, :]
    return out


def _ssd_kernel(z_ref, x_ref, b_ref, c_ref, sm_ref, conv0_ref, h0_ref, cw_ref, cb_ref, vec_ref,
                o_ref, conv1_ref, h1_ref, buf, smbuf, hst, *, cs, l_blk, n_chunks):
    c_idx = pl.program_id(1)

    @pl.when(c_idx == 0)
    def _():
        buf[5:8, :] = conv0_ref[0]
        hst[...] = h0_ref[0]

    _stage_rows(buf.at[:, 0:GROUP], x_ref, l_blk, 8)
    _stage_rows(buf.at[:, GROUP:2 * GROUP], b_ref, l_blk, 8)
    _stage_rows(buf.at[:, 2 * GROUP:3 * GROUP], c_ref, l_blk, 8)
    _stage_rows(smbuf, sm_ref, l_blk)

    xbc = _silu(_causal_conv(buf, cw_ref, cs) + cb_ref[...])
    conv_tail = buf[5 + l_blk:8 + l_blk, :]
    buf[5:8, :] = conv_tail
    xs = xbc[:, 0:GROUP]
    bm = xbc[:, GROUP:2 * GROUP]
    cm = xbc[:, 2 * GROUP:3 * GROUP]

    dt = _softplus(smbuf[...] + vec_ref[0:1, 0:LANES])
    if l_blk < cs:
        rows = lax.broadcasted_iota(jnp.int32, (cs, LANES), 0)
        dt = jnp.where(rows < l_blk, dt, 0.0)
    a = dt * (-jnp.exp(vec_ref[1:2, 0:LANES]))
    acum = _dot_f32(_tril(cs).astype(F32), a)
    expand = _head_expand(LANES, GROUP, HEAD_DIM)
    dt_w = _dot_f32(dt, expand)
    acum_w = _dot_f32(acum, expand)
    a_last = acum_w[cs - 1:cs, :]
    xdt = xs * dt_w
    e_acum = jnp.exp(acum_w)
    xw = xdt * jnp.exp(a_last - acum_w)
    e_last = jnp.exp(a_last)
    acum_t = acum.T
    causal = _tril(cs)

    ys = []
    for h in range(HEADS):
        g = h // (HEADS // 2)
        hs = slice(h * HEAD_DIM, (h + 1) * HEAD_DIM)
        gs = slice(g * SSM_DSTATE, (g + 1) * SSM_DSTATE)
        st = hst[h]
        dec = jnp.exp(jnp.where(causal, acum[:, h:h + 1] - acum_t[h:h + 1, :], NEG_INF))
        y = _dot(_dot_nt(cm[:, gs], bm[:, gs]) * dec, xdt[:, hs])
        y = y + _dot_nt(cm[:, gs], st) * e_acum[:, hs]
        ys.append(y)
        hst[h] = e_last[:, h * HEAD_DIM:h * HEAD_DIM + 1] * st + _dot(xw[:, hs].T, bm[:, gs])
    y = jnp.concatenate(ys, axis=-1) + vec_ref[2:3, :] * xs
    o_ref[0] = _rms(y[0:l_blk] * _silu(z_ref[0]), vec_ref[3:4, :])

    @pl.when(c_idx == n_chunks - 1)
    def _():
        conv1_ref[0] = conv_tail
        h1_ref[0] = hst[...]


def ssd_mixer(proj, conv0, h0, conv_w, conv_b, dt_bias, a_log, d_skip, norm_w):
    b, L, _ = proj.shape
    cs = SSM_CHUNK
    l_blk = min(cs, L)
    n_chunks = L // l_blk
    vec = jnp.zeros((8, GROUP), F32)
    vec = vec.at[0, :HEADS].set(dt_bias).at[1, :HEADS].set(a_log)
    vec = vec.at[2].set(jnp.repeat(d_skip, HEAD_DIM)).at[3].set(norm_w)
    col = lambda k: pl.BlockSpec((1, l_blk, GROUP), lambda i, j, k=k: (i, j, k))
    per_b = lambda shp: pl.BlockSpec((1,) + shp, lambda i, j: (i,) + (0,) * len(shp))
    c3 = 3 * GROUP
    return pl.pallas_call(
        functools.partial(_ssd_kernel, cs=cs, l_blk=l_blk, n_chunks=n_chunks),
        grid=(b, n_chunks),
        in_specs=[col(COL_SSM_Z), col(COL_SSM_X), col(COL_SSM_X + 1), col(COL_SSM_X + 2),
                  pl.BlockSpec((1, l_blk, LANES), lambda i, j: (i, j, COL_SMALL)),
                  per_b((CONV_W - 1, c3)), per_b((HEADS, HEAD_DIM, SSM_DSTATE)),
                  _resident((CONV_W, c3)), _resident((1, c3)), _resident((8, GROUP))],
        out_specs=[pl.BlockSpec((1, l_blk, GROUP), lambda i, j: (i, j, 0)),
                   per_b((CONV_W - 1, c3)), per_b((HEADS, HEAD_DIM, SSM_DSTATE))],
        out_shape=[jax.ShapeDtypeStruct((b, L, GROUP), F32),
                   jax.ShapeDtypeStruct((b, CONV_W - 1, c3), F32),
                   jax.ShapeDtypeStruct((b, HEADS, HEAD_DIM, SSM_DSTATE), F32)],
        scratch_shapes=[pltpu.VMEM((cs + 8, c3), F32), pltpu.VMEM((cs, LANES), F32),
                        pltpu.VMEM((HEADS, HEAD_DIM, SSM_DSTATE), F32)],
        compiler_params=_params("parallel", "arbitrary"),
        name="ssd_mixer",
    )(proj, proj, proj, proj, proj, conv0, h0, conv_w, conv_b.reshape(1, c3), vec)


def _dot_tn(a, b):
    return lax.dot_general(a.astype(BF16), b.astype(BF16), (((0,), (0,)), ((), ())),
                           preferred_element_type=F32)


def _head_blocks(n, width):
    r = lax.broadcasted_iota(jnp.int32, (n, n), 0)
    c = lax.broadcasted_iota(jnp.int32, (n, n), 1)
    return ((r // width) == (c // width)).astype(F32)


def _dot_split(x, m01, passes, m01_left=False):
    m = m01.astype(BF16)
    acc = None
    for _ in range(passes):
        hi = x.astype(BF16)
        part = jnp.dot(m, hi, preferred_element_type=F32) if m01_left else jnp.dot(hi, m, preferred_element_type=F32)
        acc = part if acc is None else acc + part
        x = x - hi.astype(F32)
    return acc


def _head_rms_gate(o, gate, nw_row, l_blk):
    ms = _dot_split(o * o, _head_blocks(GROUP, HEAD_DIM), 2) * (1.0 / HEAD_DIM)
    return (o * lax.rsqrt(ms + EPS) * nw_row)[0:l_blk] * _silu(gate)


def _hgrn_kernel(q_ref, f_ref, i_ref, g_ref, s0_ref, lbraw_ref, nw_ref, o_ref, s1_ref,
                 stage, sst, *, cs, l_blk, n_chunks, layer):
    c_idx = pl.program_id(1)

    @pl.when(c_idx == 0)
    def _():
        sst[...] = s0_ref[0]

    if l_blk < cs:
        _stage_rows(stage.at[0], q_ref, l_blk)
        _stage_rows(stage.at[1], f_ref, l_blk)
        _stage_rows(stage.at[2], i_ref, l_blk)
        q, fx, v = stage[0], stage[1], stage[2]
    else:
        q, fx, v = q_ref[0], f_ref[0], i_ref[0]

    raw = lbraw_ref[...]
    e = jnp.exp(raw - jnp.max(raw, axis=0, keepdims=True))
    sm = e / jnp.sum(e, axis=0, keepdims=True)
    lb = jnp.zeros((1, GROUP), F32)
    for i in range(1, layer + 1):
        lb = lb + sm[i:i + 1, :]

    log_sig = jnp.minimum(fx, 0.0) - jnp.log1p(jnp.exp(-jnp.abs(fx)))
    la = jnp.log(lb)
    lbb = jnp.log1p(-lb) + log_sig
    log_f = jnp.maximum(la, lbb) + jnp.log1p(jnp.exp(-jnp.abs(la - lbb)))
    k = (1.0 - lb) * _sigmoid(-fx)
    if l_blk < cs:
        rows = lax.broadcasted_iota(jnp.int32, (cs, GROUP), 0)
        log_f = jnp.where(rows < l_blk, log_f, 0.0)
        k = jnp.where(rows < l_blk, k, 0.0)

    bcum = _dot_f32(_tril(cs).astype(F32), log_f)
    b_mid = bcum[cs // 2 - 1:cs // 2, :]
    b_last = bcum[cs - 1:cs, :]
    qe = q * jnp.exp(bcum - b_mid)
    ke = k * jnp.exp(b_mid - bcum)
    qs = q * jnp.exp(bcum)
    kd = k * jnp.exp(b_last - bcum)
    e_last = jnp.exp(b_last)
    causal = _tril(cs)

    os_ = []
    for h in range(HEADS):
        hs = slice(h * HEAD_DIM, (h + 1) * HEAD_DIM)
        st_t = sst[h]
        att = jnp.where(causal, _dot_nt(qe[:, hs], ke[:, hs]), 0.0)
        os_.append(_dot(att, v[:, hs]) + _dot_nt(qs[:, hs], st_t))
        sst[h] = e_last[:, hs] * st_t + _dot_tn(v[:, hs], kd[:, hs])
    o = jnp.concatenate(os_, axis=-1)
    o_ref[0] = _head_rms_gate(o, g_ref[0], nw_ref[...], l_blk)

    @pl.when(c_idx == n_chunks - 1)
    def _():
        s1_ref[0] = sst[...]


def hgrn_mixer(proj, s0, lb_raw, norm_w, layer):
    b, L, _ = proj.shape
    cs = LIN_CHUNK
    l_blk = min(cs, L)
    n_chunks = L // l_blk
    col = lambda k: pl.BlockSpec((1, l_blk, GROUP), lambda i, j, k=k: (i, j, k))
    st = pl.BlockSpec((1, HEADS, HEAD_DIM, HEAD_DIM), lambda i, j: (i, 0, 0, 0))
    return pl.pallas_call(
        functools.partial(_hgrn_kernel, cs=cs, l_blk=l_blk, n_chunks=n_chunks, layer=layer),
        grid=(b, n_chunks),
        in_specs=[col(COL_HGRN), col(COL_HGRN + 1), col(COL_HGRN + 2), col(COL_HGRN + 3), st,
                  _resident(lb_raw.shape), _resident((1, GROUP))],
        out_specs=[pl.BlockSpec((1, l_blk, GROUP), lambda i, j: (i, j, 0)), st],
        out_shape=[jax.ShapeDtypeStruct((b, L, GROUP), F32),
                   jax.ShapeDtypeStruct((b, HEADS, HEAD_DIM, HEAD_DIM), F32)],
        scratch_shapes=[pltpu.VMEM((3, cs, GROUP), F32),
                        pltpu.VMEM((HEADS, HEAD_DIM, HEAD_DIM), F32)],
        compiler_params=_params("parallel", "arbitrary"),
        name="hgrn_mixer",
    )(proj, proj, proj, proj, s0, lb_raw, jnp.tile(norm_w, HEADS).reshape(1, GROUP))


def _doubling_level_masks(n, block):
    r = np.arange(n)[:, None]
    c = np.arange(n)[None, :]
    out, s = [], 1
    while s < block:
        out.append((r // (2 * s) == c // (2 * s)) & ((r // s) % 2 == 1) & ((c // s) % 2 == 0))
        s *= 2
    return np.stack(out).astype(np.float32)


def _unit_lower_inverse_minus_eye(a, block=None, level_masks_ref=None):
    n = a.shape[0]
    block = n if block is None else block
    r = lax.broadcasted_iota(jnp.int32, (n, n), 0)
    c = lax.broadcasted_iota(jnp.int32, (n, n), 1)
    dx = None
    s, level = 1, 0
    while s < block:
        if level_masks_ref is None:
            lower_left = jnp.logical_and((r // (2 * s)) == (c // (2 * s)),
                                         jnp.logical_and((r // s) % 2 == 1, (c // s) % 2 == 0))
            b = jnp.where(lower_left, a, 0.0)
        else:
            b = a * level_masks_ref[level]
        if dx is None:
            dx = -b
        else:
            m = b + _dot(dx, b)
            dx = dx - m - _dot(m, dx)
        s *= 2
        level += 1
    return dx


def _gdn_kernel(q_ref, k_ref, v_ref, z_ref, sm_ref, conv0_ref, s0_ref, cw_ref, vec_ref, nw_ref,
                o_ref, conv1_ref, s1_ref, buf, smbuf, sst, *, cs, l_blk, n_chunks):
    c_idx = pl.program_id(1)

    @pl.when(c_idx == 0)
    def _():
        buf[5:8, :] = conv0_ref[0]
        sst[...] = s0_ref[0]

    _stage_rows(buf.at[:, 0:GROUP], q_ref, l_blk, 8)
    _stage_rows(buf.at[:, GROUP:2 * GROUP], k_ref, l_blk, 8)
    _stage_rows(buf.at[:, 2 * GROUP:3 * GROUP], v_ref, l_blk, 8)
    _stage_rows(smbuf, sm_ref, l_blk)

    qkv = _silu(_causal_conv(buf, cw_ref, cs))
    conv_tail = buf[5 + l_blk:8 + l_blk, :]
    buf[5:8, :] = conv_tail
    q = qkv[:, 0:GROUP]
    k = qkv[:, GROUP:2 * GROUP]
    v = qkv[:, 2 * GROUP:3 * GROUP]
    blocks = _head_blocks(GROUP, HEAD_DIM)
    q = q * lax.rsqrt(_dot_f32(q * q, blocks) + EPS) * (HEAD_DIM ** -0.5)
    k = k * lax.rsqrt(_dot_f32(k * k, blocks) + EPS)

    sm = smbuf[...]
    beta = _sigmoid(sm)
    log_g = -jnp.exp(vec_ref[1:2, :]) * _softplus(sm + vec_ref[0:1, :])
    if l_blk < cs:
        rows = lax.broadcasted_iota(jnp.int32, (cs, LANES), 0)
        beta = jnp.where(rows < l_blk, beta, 0.0)
        log_g = jnp.where(rows < l_blk, log_g, 0.0)
    gam = _dot_f32(_tril(cs).astype(F32), log_g)
    beta_w = _dot_f32(beta, _head_expand(LANES, GROUP, HEAD_DIM, HEADS))
    gam_w = _dot_f32(gam, _head_expand(LANES, GROUP, HEAD_DIM, 2 * HEADS))
    g_last = gam_w[cs - 1:cs, :]
    e_gam = jnp.exp(gam_w)
    kb = k * beta_w
    vb = v * beta_w
    kbg = kb * e_gam
    qg = q * e_gam
    k_dec = k * jnp.exp(g_last - gam_w)
    e_last = jnp.exp(g_last)
    gam_t = gam.T
    incl = _tril(cs)
    strict = _tril(cs, strict=True)

    os_ = []
    for h in range(HEADS):
        hs = slice(h * HEAD_DIM, (h + 1) * HEAD_DIM)
        r = 2 * HEADS + h
        st = sst[h]
        dec_incl = jnp.exp(jnp.where(incl, gam[:, r:r + 1] - gam_t[r:r + 1, :], NEG_INF))
        a_mat = jnp.where(strict, _dot_nt(kb[:, hs], k[:, hs]) * dec_incl, 0.0)
        tx = _unit_lower_inverse_minus_eye(a_mat)
        u = vb[:, hs] + _dot(tx, vb[:, hs])
        w = kbg[:, hs] + _dot(tx, kbg[:, hs])
        v_new = u - _dot(w, st)
        qk = _dot_nt(q[:, hs], k[:, hs]) * dec_incl
        os_.append(_dot(qk, v_new) + _dot(qg[:, hs], st))
        sst[h] = e_last[:, h * HEAD_DIM:h * HEAD_DIM + 1] * st + _dot_tn(k_dec[:, hs], v_new)
    o = jnp.concatenate(os_, axis=-1)
    o_ref[0] = _head_rms_gate(o, z_ref[0], nw_ref[...], l_blk)

    @pl.when(c_idx == n_chunks - 1)
    def _():
        conv1_ref[0] = conv_tail
        s1_ref[0] = sst[...]


def gdn_mixer(proj, conv0, s0, conv_w, a_log, dt_bias, norm_w):
    b, L, _ = proj.shape
    cs = LIN_CHUNK
    l_blk = min(cs, L)
    n_chunks = L // l_blk
    vec = jnp.zeros((8, LANES), F32)
    vec = vec.at[0, 2 * HEADS:3 * HEADS].set(dt_bias).at[1, 2 * HEADS:3 * HEADS].set(a_log)
    col = lambda k: pl.BlockSpec((1, l_blk, GROUP), lambda i, j, k=k: (i, j, k))
    per_b = lambda shp: pl.BlockSpec((1,) + shp, lambda i, j: (i,) + (0,) * len(shp))
    c3 = 3 * GROUP
    return pl.pallas_call(
        functools.partial(_gdn_kernel, cs=cs, l_blk=l_blk, n_chunks=n_chunks),
        grid=(b, n_chunks),
        in_specs=[col(COL_GDN), col(COL_GDN + 1), col(COL_GDN + 2), col(COL_GDN + 3),
                  pl.BlockSpec((1, l_blk, LANES), lambda i, j: (i, j, COL_SMALL)),
                  per_b((CONV_W - 1, c3)), per_b((HEADS, HEAD_DIM, HEAD_DIM)),
                  _resident((CONV_W, c3)), _resident((8, LANES)), _resident((1, GROUP))],
        out_specs=[pl.BlockSpec((1, l_blk, GROUP), lambda i, j: (i, j, 0)),
                   per_b((CONV_W - 1, c3)), per_b((HEADS, HEAD_DIM, HEAD_DIM))],
        out_shape=[jax.ShapeDtypeStruct((b, L, GROUP), F32),
                   jax.ShapeDtypeStruct((b, CONV_W - 1, c3), F32),
                   jax.ShapeDtypeStruct((b, HEADS, HEAD_DIM, HEAD_DIM), F32)],
        scratch_shapes=[pltpu.VMEM((cs + 8, c3), F32), pltpu.VMEM((cs, LANES), F32),
                        pltpu.VMEM((HEADS, HEAD_DIM, HEAD_DIM), F32)],
        compiler_params=_params("parallel", "arbitrary"),
        name="gdn_mixer",
    )(proj, proj, proj, proj, proj, conv0, s0, conv_w, vec,
      jnp.tile(norm_w, HEADS).reshape(1, GROUP))


GDN_PREP_CHUNKS = 4


def _gdn_prep_kernel(q_ref, k_ref, v_ref, pq_ref, pk_ref, pv_ref, sm_ref, conv0_ref, cw_ref, vec_ref,
                     lvl_ref, u_ref, w_ref, qg_ref, kd_ref, qk_ref, aux_ref, buf, *, cs, cps):
    j = pl.program_id(1)
    rows = cs * cps

    @pl.when(j == 0)
    def _():
        buf[5:8, :] = conv0_ref[0]

    @pl.when(j > 0)
    def _():
        for i, ref in enumerate((pq_ref, pk_ref, pv_ref)):
            buf[5:8, i * GROUP:(i + 1) * GROUP] = ref[0, SUBLANES - 3:SUBLANES, :]

    for i, ref in enumerate((q_ref, k_ref, v_ref)):
        buf[8:8 + rows, i * GROUP:(i + 1) * GROUP] = ref[0]

    qkv = _silu(_causal_conv(buf, cw_ref, rows))
    q = qkv[:, 0:GROUP]
    k = qkv[:, GROUP:2 * GROUP]
    v = qkv[:, 2 * GROUP:3 * GROUP]
    blocks = _head_blocks(GROUP, HEAD_DIM)
    q = q * lax.rsqrt(_dot_split(q * q, blocks, 2) + EPS) * (HEAD_DIM ** -0.5)
    k = k * lax.rsqrt(_dot_split(k * k, blocks, 2) + EPS)

    sm = sm_ref[0]
    beta = _sigmoid(sm)
    log_g = -jnp.exp(vec_ref[1:2, :]) * _softplus(sm + vec_ref[0:1, :])
    r = lax.broadcasted_iota(jnp.int32, (rows, rows), 0)
    c = lax.broadcasted_iota(jnp.int32, (rows, rows), 1)
    chunk_tril = jnp.logical_and(r >= c, (r // cs) == (c // cs)).astype(F32)
    gam = _dot_split(log_g, chunk_tril, 3, m01_left=True)
    beta_w = _dot_split(beta, _head_expand(LANES, GROUP, HEAD_DIM, HEADS), 3)
    gam_w = _dot_split(gam, _head_expand(LANES, GROUP, HEAD_DIM, 2 * HEADS), 3)
    g_last = jnp.concatenate(
        [jnp.broadcast_to(gam_w[(i + 1) * cs - 1:(i + 1) * cs, :], (cs, GROUP)) for i in range(cps)], axis=0)
    e_gam = jnp.exp(gam_w)
    kb = k * beta_w
    vb = v * beta_w
    kbg = kb * e_gam
    qg_ref[0] = q * e_gam
    kd_ref[0] = k * jnp.exp(g_last - gam_w)
    for i in range(cps):
        aux_ref[0, i * SUBLANES:(i + 1) * SUBLANES, :] = jnp.exp(g_last[i * cs:i * cs + SUBLANES, :])
    gam_t = gam.T
    incl = chunk_tril > 0.0
    strict = jnp.logical_and(incl, r > c)

    us, ws, qks = [], [], []
    for h in range(HEADS):
        hs = slice(h * HEAD_DIM, (h + 1) * HEAD_DIM)
        lane = 2 * HEADS + h
        dec_incl = jnp.exp(jnp.where(incl, gam[:, lane:lane + 1] - gam_t[lane:lane + 1, :], NEG_INF))
        a_mat = jnp.where(strict, _dot_nt(kb[:, hs], k[:, hs]) * dec_incl, 0.0)
        dx = _unit_lower_inverse_minus_eye(a_mat, cs, lvl_ref)
        uw = _dot(dx, jnp.concatenate([vb[:, hs], kbg[:, hs]], axis=-1))
        us.append(vb[:, hs] + uw[:, 0:HEAD_DIM])
        ws.append(kbg[:, hs] + uw[:, HEAD_DIM:2 * HEAD_DIM])
        qk = _dot_nt(q[:, hs], k[:, hs]) * dec_incl
        qks.append(jnp.concatenate([qk[i * cs:(i + 1) * cs, i * cs:(i + 1) * cs] for i in range(cps)], axis=0))
    u_ref[0] = jnp.concatenate(us, axis=-1)
    w_ref[0] = jnp.concatenate(ws, axis=-1)
    qk_ref[0] = jnp.concatenate(qks, axis=-1)


def _gdn_scan_kernel(u_ref, w_ref, qg_ref, kd_ref, qk_ref, aux_ref, z_ref, s0_ref, nw_ref,
                     o_ref, s1_ref, sst, *, nb, n_chunks):
    c_idx = pl.program_id(0)
    blocks = _head_blocks(GROUP, HEAD_DIM)

    @pl.when(c_idx == 0)
    def _():
        sst[...] = jnp.zeros(sst.shape, F32)
        for bi in range(nb):
            for h in range(HEADS):
                hs = slice(h * HEAD_DIM, (h + 1) * HEAD_DIM)
                sst[bi, hs, hs] = s0_ref[bi, h]

    for bi in range(nb):
        st = sst[bi]
        v_new = u_ref[bi] - _dot(w_ref[bi], st)
        v_bd = jnp.concatenate([v_new] * HEADS, axis=0) * blocks
        o = _dot(qk_ref[bi], v_bd) + _dot(qg_ref[bi], st)
        sst[bi] = aux_ref[bi, 0:1, :] * st + _dot_tn(kd_ref[bi], v_new) * blocks
        o_ref[bi] = _head_rms_gate(o, z_ref[bi], nw_ref[...], o.shape[0])

    @pl.when(c_idx == n_chunks - 1)
    def _():
        for bi in range(nb):
            for h in range(HEADS):
                hs = slice(h * HEAD_DIM, (h + 1) * HEAD_DIM)
                s1_ref[bi, h] = sst[bi, hs, hs]


def gdn_mixer_long(proj, conv0, s0, conv_w, a_log, dt_bias, norm_w):
    b, L, _ = proj.shape
    cs = LIN_CHUNK
    cps = GDN_PREP_CHUNKS
    rows = cs * cps
    n_chunks = L // cs
    vec = jnp.zeros((8, LANES), F32)
    vec = vec.at[0, 2 * HEADS:3 * HEADS].set(dt_bias).at[1, 2 * HEADS:3 * HEADS].set(a_log)
    c3 = 3 * GROUP
    col = lambda k: pl.BlockSpec((1, rows, GROUP), lambda i, j, k=k: (i, j, k))
    prev = lambda k: pl.BlockSpec(
        (1, SUBLANES, GROUP), lambda i, j, k=k: (i, jnp.maximum(j * (rows // SUBLANES) - 1, 0), k))
    out = pl.BlockSpec((1, rows, GROUP), lambda i, j: (i, j, 0))
    wide = jax.ShapeDtypeStruct((b, L, GROUP), F32)
    level_masks = jnp.asarray(_doubling_level_masks(rows, cs))
    u, w, qg, kd, qk, aux = pl.pallas_call(
        functools.partial(_gdn_prep_kernel, cs=cs, cps=cps),
        grid=(b, L // rows),
        in_specs=[col(COL_GDN), col(COL_GDN + 1), col(COL_GDN + 2),
                  prev(COL_GDN), prev(COL_GDN + 1), prev(COL_GDN + 2),
                  pl.BlockSpec((1, rows, LANES), lambda i, j: (i, j, COL_SMALL)),
                  pl.BlockSpec((1, CONV_W - 1, c3), lambda i, j: (i, 0, 0)),
                  _resident((CONV_W, c3)), _resident((8, LANES)), _resident(level_masks.shape)],
        out_specs=[out, out, out, out, out,
                   pl.BlockSpec((1, cps * SUBLANES, GROUP), lambda i, j: (i, j, 0))],
        out_shape=[wide, wide, wide, wide, wide,
                   jax.ShapeDtypeStruct((b, n_chunks * SUBLANES, GROUP), F32)],
        scratch_shapes=[pltpu.VMEM((rows + 8, c3), F32)],
        compiler_params=_params("parallel", "parallel"),
        name="gdn_prep",
    )(proj, proj, proj, proj, proj, proj, proj, conv0, conv_w, vec, level_masks)
    blk = pl.BlockSpec((b, cs, GROUP), lambda c: (0, c, 0))
    state = pl.BlockSpec((b, HEADS, HEAD_DIM, HEAD_DIM), lambda c: (0, 0, 0, 0))
    o, s1 = pl.pallas_call(
        functools.partial(_gdn_scan_kernel, nb=b, n_chunks=n_chunks),
        grid=(n_chunks,),
        in_specs=[blk, blk, blk, blk, blk,
                  pl.BlockSpec((b, SUBLANES, GROUP), lambda c: (0, c, 0)),
                  pl.BlockSpec((b, cs, GROUP), lambda c: (0, c, COL_GDN + 3)),
                  state, _resident((1, GROUP))],
        out_specs=[blk, state],
        out_shape=[wide, jax.ShapeDtypeStruct((b, HEADS, HEAD_DIM, HEAD_DIM), F32)],
        scratch_shapes=[pltpu.VMEM((b, GROUP, GROUP), F32)],
        compiler_params=_params("arbitrary"),
        name="gdn_scan",
    )(u, w, qg, kd, qk, aux, proj, s0, jnp.tile(norm_w, HEADS).reshape(1, GROUP))
    conv1 = proj[:, L - (CONV_W - 1):, COL_GDN * GROUP:(COL_GDN + 3) * GROUP]
    return o, conv1, s1


def _moba_prompt_kernel(q_ref, k_ref, v_ref, o_ref, ks, vts, kmean, sel_t, o_t, q_tb, rel, ml, *, nb):
    blk = MOBA_BLOCK
    qi = pl.program_id(1)

    @pl.when(qi == 0)
    def _():
        for j in range(nb):
            kj = k_ref[0, j * blk:(j + 1) * blk, :]
            kmean[j:j + 1, :] = jnp.mean(kj, axis=0, keepdims=True)
            kjb = kj.astype(BF16)
            for h in range(HEADS):
                ks[h, j] = kjb[:, h * HEAD_DIM:(h + 1) * HEAD_DIM]
            vts[j] = v_ref[0, j * blk:(j + 1) * blk, :].T.astype(BF16)

    q_t = (q_ref[0] * (HEAD_DIM ** -0.5)).T
    q_tb[...] = q_t.astype(BF16)
    blk_row = lax.broadcasted_iota(jnp.int32, (nb, blk), 0)
    rel[...] = (lax.broadcasted_iota(jnp.int32, (blk, blk), 1)
                - lax.broadcasted_iota(jnp.int32, (blk, blk), 0)).astype(F32)

    for h in range(HEADS):
        hs = slice(h * HEAD_DIM, (h + 1) * HEAD_DIM)
        slope = ALIBI_SLOPES[h]
        gate = jnp.where(blk_row < qi, _dot_f32(kmean[:, hs], q_t[hs, :]), NEG_INF)
        sel = jnp.full((nb, blk), NEG_INF, F32)
        for _ in range(MOBA_TOPK):
            top = jnp.max(gate, axis=0, keepdims=True)
            first = jnp.min(jnp.where(gate == top, blk_row, nb), axis=0, keepdims=True)
            pick = blk_row == first
            sel = jnp.where(jnp.logical_and(pick, blk_row < qi), 0.0, sel)
            gate = jnp.where(pick, NEG_INF, gate)
        sel_t[h] = sel

        s = _dot(ks[h, qi], q_tb[hs, :])
        s = jnp.where(rel[...] >= 0.0, s - slope * rel[...], NEG_INF)
        m0 = jnp.max(s, axis=0, keepdims=True)
        p = jnp.exp(s - m0)
        ml[h, 0:1, :] = m0
        ml[h, 1:2, :] = jnp.sum(p, axis=0, keepdims=True)
        o_t[hs, :] = _dot(vts[qi, hs, :], p)

    def body(j, carry):
        for h in range(HEADS):
            hs = slice(h * HEAD_DIM, (h + 1) * HEAD_DIM)
            slope = ALIBI_SLOPES[h]
            off = (qi - j).astype(F32) * (slope * blk)
            s = _dot(ks[h, j], q_tb[hs, :]) - slope * rel[...] + (sel_t[h, pl.ds(j, 1), :] - off)
            m = ml[h, 0:1, :]
            m_new = jnp.maximum(m, jnp.max(s, axis=0, keepdims=True))
            alpha = jnp.exp(m - m_new)
            p = jnp.exp(s - m_new)
            ml[h, 0:1, :] = m_new
            ml[h, 1:2, :] = alpha * ml[h, 1:2, :] + jnp.sum(p, axis=0, keepdims=True)
            o_t[hs, :] = alpha * o_t[hs, :] + _dot(vts[j, hs, :], p)
        return carry

    lax.fori_loop(0, qi, body, 0)
    for h in range(HEADS):
        hs = slice(h * HEAD_DIM, (h + 1) * HEAD_DIM)
        o_t[hs, :] = o_t[hs, :] / ml[h, 1:2, :]
    o_ref[0] = o_t[...].T


def moba_prompt(proj):
    b, S, _ = proj.shape
    blk = MOBA_BLOCK
    nb = S // blk
    seq = lambda k: pl.BlockSpec((1, S, GROUP), lambda i, j, k=k: (i, 0, k))
    return pl.pallas_call(
        functools.partial(_moba_prompt_kernel, nb=nb),
        grid=(b, nb),
        in_specs=[pl.BlockSpec((1, blk, GROUP), lambda i, j: (i, j, COL_MOBA)),
                  seq(COL_MOBA + 1), seq(COL_MOBA + 2)],
        out_specs=pl.BlockSpec((1, blk, GROUP), lambda i, j: (i, j, 0)),
        out_shape=jax.ShapeDtypeStruct((b, S, GROUP), F32),
        scratch_shapes=[pltpu.VMEM((HEADS, nb, blk, HEAD_DIM), BF16),
                        pltpu.VMEM((nb, GROUP, blk), BF16),
                        pltpu.VMEM((nb, GROUP), F32),
                        pltpu.VMEM((HEADS, nb, blk), F32),
                        pltpu.VMEM((GROUP, blk), F32),
                        pltpu.VMEM((GROUP, blk), BF16),
                        pltpu.VMEM((blk, blk), F32),
                        pltpu.VMEM((HEADS, SUBLANES, blk), F32)],
        compiler_params=_params("parallel", "arbitrary"),
        name="moba_prompt",
    )(proj, proj, proj)


PAGES_PER_BLOCK = MOBA_BLOCK // PAGE_SIZE
SELECT_PAGE_BUFFERS = 32


def _moba_select_kernel(pt_ref, q_ref, kc_ref, idx_ref, pages, kmean_t, q8, sems, *,
                        layer, n_pages, t):
    b = pl.program_id(0)
    nbuf = SELECT_PAGE_BUFFERS
    n_blocks = n_pages // PAGES_PER_BLOCK

    def page_copy(p, slot):
        return pltpu.make_async_copy(kc_ref.at[layer, pt_ref[b, p]], pages.at[slot], sems.at[slot])

    for s in range(nbuf):
        page_copy(s, s).start()

    kmean_t[...] = jnp.zeros(kmean_t.shape, F32)
    blk_of_lane = lax.broadcasted_iota(jnp.int32, kmean_t.shape, 1)

    def body(n, carry):
        tot = jnp.zeros((GROUP, PAGE_SIZE), F32)
        for pp in range(PAGES_PER_BLOCK):
            p = n * PAGES_PER_BLOCK + pp
            slot = p % nbuf
            page_copy(p, slot).wait()
            tot = tot + pages[slot]

            @pl.when(p + nbuf < n_pages)
            def _():
                page_copy(p + nbuf, slot).start()
        mean = jnp.sum(tot, axis=1, keepdims=True) * (1.0 / MOBA_BLOCK)
        kmean_t[...] = jnp.where(blk_of_lane == n, mean, kmean_t[...])
        return carry

    lax.fori_loop(0, n_blocks, body, 0)

    q8[0:t, :] = q_ref[0]
    q8[t:, :] = jnp.zeros((SUBLANES - t, GROUP), F32)
    blk_lane = lax.broadcasted_iota(jnp.int32, (SUBLANES, n_blocks), 1)
    out_lane = lax.broadcasted_iota(jnp.int32, (SUBLANES, LANES), 1)
    res = jnp.zeros((SUBLANES, LANES), jnp.int32)
    for h in range(HEADS):
        hs = slice(h * HEAD_DIM, (h + 1) * HEAD_DIM)
        gate = _dot_f32(q8[:, hs], kmean_t[hs, 0:n_blocks])
        for r in range(MOBA_TOPK):
            top = jnp.max(gate, axis=1, keepdims=True)
            first = jnp.min(jnp.where(gate == top, blk_lane, n_blocks), axis=1, keepdims=True)
            res = jnp.where(out_lane == h * MOBA_TOPK + r, first, res)
            gate = jnp.where(blk_lane == first, NEG_INF, gate)
    idx_ref[0] = res


def _moba_sample_attn_kernel(pt_ref, idx_ref, q_ref, kn_ref, vn_ref, kc_ref, vc_ref, o_ref,
                             kbuf, vbuf, st8, o8, ksem, vsem, *, layer, past, t):
    b = pl.program_id(0)
    n_seq = pl.num_programs(0)
    n_sel = MOBA_TOPK * MOBA_BLOCK
    pairs = [(tok, h) for h in range(HEADS) for tok in range(t)]

    def block_of(seq, tok, h, r):
        return idx_ref[seq, tok * (HEADS * MOBA_TOPK) + h * MOBA_TOPK + r]

    def copies(seq, i, half):
        tok, h = pairs[i]
        out = []
        for r in range(MOBA_TOPK):
            blk = block_of(seq, tok, h, r)
            for pp in range(PAGES_PER_BLOCK):
                phys = pt_ref[seq, blk * PAGES_PER_BLOCK + pp]
                lanes = pl.ds((r * PAGES_PER_BLOCK + pp) * PAGE_SIZE, PAGE_SIZE)
                out.append(pltpu.make_async_copy(kc_ref.at[layer, phys, h], kbuf.at[half, i, :, lanes],
                                                 ksem.at[half, i]))
                out.append(pltpu.make_async_copy(vc_ref.at[layer, phys, h], vbuf.at[half, i, :, lanes],
                                                 vsem.at[half, i]))
        return out

    def start_all(seq, half):
        for i in range(len(pairs)):
            for c in copies(seq, i, half):
                c.start()

    half = b % 2

    @pl.when(b == 0)
    def _():
        start_all(b, half)

    @pl.when(b + 1 < n_seq)
    def _():
        start_all(b + 1, 1 - half)

    for i, ref in enumerate((q_ref, kn_ref, vn_ref)):
        st8[i, 0:t, :] = ref[0]
        st8[i, t:, :] = jnp.zeros((SUBLANES - t, GROUP), F32)

    lane = lax.broadcasted_iota(jnp.int32, (1, n_sel), 1)
    row = lax.broadcasted_iota(jnp.int32, (SUBLANES, 1), 0)
    for i, (tok, h) in enumerate(pairs):
        for c in copies(b, i, half):
            c.wait()
        hs = slice(h * HEAD_DIM, (h + 1) * HEAD_DIM)
        slope = ALIBI_SLOPES[h]
        qrow = st8[0, tok:tok + 1, hs] * (HEAD_DIM ** -0.5)
        s_sel = _dot(jnp.broadcast_to(qrow, (SUBLANES, HEAD_DIM)), kbuf[half, i])[0:1]
        blk = jnp.where(lane < MOBA_BLOCK, block_of(b, tok, h, 0),
                        jnp.where(lane < 2 * MOBA_BLOCK, block_of(b, tok, h, 1), block_of(b, tok, h, 2)))
        pos = blk * MOBA_BLOCK + (lane % MOBA_BLOCK)
        s_sel = s_sel - slope * (past + tok - pos).astype(F32)
        s_own = jnp.sum(st8[1, :, hs] * qrow, axis=1, keepdims=True)
        s_own = jnp.where(row <= tok, s_own - slope * (tok - row).astype(F32), NEG_INF)
        m = jnp.maximum(jnp.max(s_sel, axis=1, keepdims=True), jnp.max(s_own, axis=0, keepdims=True))
        p_sel = jnp.exp(s_sel - m)
        p_own = jnp.exp(s_own - m)
        l = jnp.sum(p_sel, axis=1, keepdims=True) + jnp.sum(p_own, axis=0, keepdims=True)
        o = _dot_nt(jnp.broadcast_to(p_sel, (SUBLANES, n_sel)), vbuf[half, i])[0:1]
        o = o + jnp.sum(p_own * st8[2, :, hs], axis=0, keepdims=True)
        o8[tok:tok + 1, hs] = o / l
    o_ref[0] = o8[0:t, :]


def moba_sample(proj, k_cache, v_cache, page_table, layer):
    db, t, _ = proj.shape
    depth, pool = k_cache.shape[:2]
    n_pages = page_table.shape[1]
    n_blocks = n_pages // PAGES_PER_BLOCK
    past = n_pages * PAGE_SIZE
    assert past % MOBA_BLOCK == 0 and n_pages >= SELECT_PAGE_BUFFERS and t <= SUBLANES
    assert MOBA_TOPK <= n_blocks <= LANES
    tok = lambda k: pl.BlockSpec((1, t, GROUP), lambda i, *_: (i, 0, k))
    hbm = pl.BlockSpec(memory_space=pl.ANY)
    n_pairs = t * HEADS
    idx = pl.pallas_call(
        functools.partial(_moba_select_kernel, layer=layer, n_pages=n_pages, t=t),
        grid_spec=pltpu.PrefetchScalarGridSpec(
            num_scalar_prefetch=1, grid=(db,),
            in_specs=[tok(COL_MOBA), hbm],
            out_specs=pl.BlockSpec((1, SUBLANES, LANES), lambda i, *_: (i, 0, 0)),
            scratch_shapes=[pltpu.VMEM((SELECT_PAGE_BUFFERS, GROUP, PAGE_SIZE), F32),
                            pltpu.VMEM((GROUP, LANES), F32),
                            pltpu.VMEM((SUBLANES, GROUP), F32),
                            pltpu.SemaphoreType.DMA((SELECT_PAGE_BUFFERS,))]),
        out_shape=jax.ShapeDtypeStruct((db, SUBLANES, LANES), jnp.int32),
        compiler_params=_params("arbitrary"),
        name="moba_sample_select",
    )(page_table, proj, k_cache.reshape(depth, pool, GROUP, PAGE_SIZE))
    idx = idx[:, :t, :HEADS * MOBA_TOPK].reshape(db, t * HEADS * MOBA_TOPK)
    return pl.pallas_call(
        functools.partial(_moba_sample_attn_kernel, layer=layer, past=past, t=t),
        grid_spec=pltpu.PrefetchScalarGridSpec(
            num_scalar_prefetch=2, grid=(db,),
            in_specs=[tok(COL_MOBA), tok(COL_MOBA + 1), tok(COL_MOBA + 2), hbm, hbm],
            out_specs=pl.BlockSpec((1, t, GROUP), lambda i, *_: (i, 0, 0)),
            scratch_shapes=[pltpu.VMEM((2, n_pairs, HEAD_DIM, MOBA_TOPK * MOBA_BLOCK), F32),
                            pltpu.VMEM((2, n_pairs, HEAD_DIM, MOBA_TOPK * MOBA_BLOCK), F32),
                            pltpu.VMEM((3, SUBLANES, GROUP), F32),
                            pltpu.VMEM((SUBLANES, GROUP), F32),
                            pltpu.SemaphoreType.DMA((2, n_pairs)),
                            pltpu.SemaphoreType.DMA((2, n_pairs))]),
        out_shape=jax.ShapeDtypeStruct((db, t, GROUP), F32),
        compiler_params=_params("arbitrary"),
        name="moba_sample_attn",
    )(page_table, idx, proj, proj, proj, k_cache, v_cache)


def _rearrange_w_in(w):
    d = w.shape[0]
    dt0 = COL_HGRN * GROUP
    ba0 = dt0 + HEADS + (COL_GDN + 4 - COL_HGRN) * GROUP
    wide = jnp.concatenate([w[:, :dt0], w[:, dt0 + HEADS:ba0]], axis=1)
    narrow = jnp.concatenate([w[:, dt0:dt0 + HEADS], w[:, ba0:ba0 + 2 * HEADS],
                              jnp.zeros((d, LANES - N_SMALL), w.dtype)], axis=1)
    return jnp.concatenate([wide, narrow], axis=1)


def _layer(x, mem_k, mem_v, states, lp, layer, attend, prompt):
    b, L, d = x.shape
    ssm_conv0, ssm0, hgrn0, gdn_conv0, gdn0 = states
    w_in = lp['w_in'].astype(BF16) if prompt else lp['w_in']
    proj = norm_matmul(x.reshape(b * L, d), lp['n_mix_pre'], w_in, min(256, b * L)).reshape(b, L, -1)
    o_a = attend(proj)
    o_b, ssm_conv1, ssm1 = ssd_mixer(proj, ssm_conv0, ssm0, lp['ssm_conv_w'], lp['ssm_conv_b'],
                                     lp['ssm_dt_bias'], lp['ssm_a_log'], lp['ssm_d'], lp['ssm_norm'])
    o_c, hgrn1 = hgrn_mixer(proj, jnp.swapaxes(hgrn0, -1, -2), lp['hgrn_lb_raw'], lp['hgrn_norm'], layer)
    gdn = gdn_mixer_long if L % (GDN_PREP_CHUNKS * LIN_CHUNK) == 0 else gdn_mixer
    o_d, gdn_conv1, gdn1 = gdn(proj, gdn_conv0, gdn0, lp['gdn_conv_w'], lp['gdn_a_log'],
                               lp['gdn_dt_bias'], lp['gdn_norm'])
    post = post_mixer if prompt else post_mixer_sample
    args = (x, (o_a, o_b, o_c, o_d), mem_k, mem_v, lp['norms'], lp['w_out'], lp['w_xq'], lp['w_xo'],
            lp['w_gu'], lp['w_down'])
    x = post(*args, 256) if prompt else post(*args)
    k = proj[..., GROUP:2 * GROUP].reshape(b, L, HEADS, HEAD_DIM)
    v = proj[..., 2 * GROUP:3 * GROUP].reshape(b, L, HEADS, HEAD_DIM)
    return x, (k, v, ssm_conv1, ssm1, jnp.swapaxes(hgrn1, -1, -2), gdn_conv1, gdn1)


def kernel(x_prompt, x_sample, mem_prompt, cache_moba_k, cache_moba_v, page_table, cache_mem_k, cache_mem_v, state_ssm_conv, state_ssm, state_hgrn, state_gdn_conv, state_gdn, n_mix_pre, n_mix_post, w_in, w_out, ssm_conv_w, ssm_conv_b, ssm_dt_bias, ssm_a_log, ssm_d, ssm_norm, hgrn_lb_raw, hgrn_norm, gdn_conv_w, gdn_a_log, gdn_dt_bias, gdn_norm, n_x_pre, n_x_post, mem_norm, w_xq, w_xkv, w_xo, n_f_pre, n_f_post, w_gu, w_down):
    depth = w_in.shape[0]
    bp, _, d = x_prompt.shape
    db = x_sample.shape[0]
    n_mem = mem_prompt.shape[1]
    kc = jnp.transpose(cache_moba_k, (0, 1, 3, 4, 2))
    vc = jnp.transpose(cache_moba_v, (0, 1, 3, 4, 2))
    zeros_p = (jnp.zeros((bp,) + state_ssm_conv.shape[2:], F32), jnp.zeros((bp,) + state_ssm.shape[2:], F32),
               jnp.zeros((bp,) + state_hgrn.shape[2:], F32), jnp.zeros((bp,) + state_gdn_conv.shape[2:], F32),
               jnp.zeros((bp,) + state_gdn.shape[2:], F32))
    yp, ys = x_prompt, x_sample
    outs_p, outs_s, mem_ks, mem_vs = [], [], [], []
    for l in range(depth):
        norms = jnp.zeros((SUBLANES, d), F32)
        for i, nrm in enumerate((n_mix_post, n_x_pre, n_x_post, n_f_pre, n_f_post)):
            norms = norms.at[i].set(nrm[l])
        lp = {'n_mix_pre': n_mix_pre[l], 'w_in': _rearrange_w_in(w_in[l]), 'norms': norms,
              'w_out': w_out[l].astype(BF16), 'w_xq': w_xq[l].astype(BF16), 'w_xo': w_xo[l].astype(BF16),
              'w_gu': w_gu[l].astype(BF16), 'w_down': w_down[l].astype(BF16),
              'ssm_conv_w': ssm_conv_w[l], 'ssm_conv_b': ssm_conv_b[l], 'ssm_dt_bias': ssm_dt_bias[l],
              'ssm_a_log': ssm_a_log[l], 'ssm_d': ssm_d[l], 'ssm_norm': ssm_norm[l],
              'hgrn_lb_raw': hgrn_lb_raw, 'hgrn_norm': hgrn_norm[l], 'gdn_conv_w': gdn_conv_w[l],
              'gdn_a_log': gdn_a_log[l], 'gdn_dt_bias': gdn_dt_bias[l], 'gdn_norm': gdn_norm[l]}
        mkv = norm_matmul(mem_prompt.reshape(bp * n_mem, d), mem_norm[l], w_xkv[l].astype(BF16), 256)
        mk = mkv[:, :d].reshape(bp, n_mem, d)
        mv = mkv[:, d:].reshape(bp, n_mem, d)
        yp, st_p = _layer(yp, mk.astype(BF16), mv.astype(BF16), zeros_p, lp, l, moba_prompt, True)
        outs_p.append(st_p)
        mem_ks.append(mk.reshape(bp, n_mem, X_HEADS, d // X_HEADS))
        mem_vs.append(mv.reshape(bp, n_mem, X_HEADS, d // X_HEADS))
        states_s = (state_ssm_conv[l], state_ssm[l], state_hgrn[l], state_gdn_conv[l], state_gdn[l])
        attend_s = functools.partial(moba_sample, k_cache=kc, v_cache=vc, page_table=page_table, layer=l)
        ys, st_s = _layer(ys, cache_mem_k[l].reshape(db, n_mem, d).astype(BF16),
                          cache_mem_v[l].reshape(db, n_mem, d).astype(BF16), states_s, lp, l, attend_s, False)
        outs_s.append(st_s)
    stack = lambda outs, i: jnp.stack([o[i] for o in outs], axis=0)
    return (yp, ys, stack(outs_p, 0), stack(outs_s, 0), stack(outs_p, 1), stack(outs_s, 1),
            jnp.stack(mem_ks, axis=0), jnp.stack(mem_vs, axis=0),
            stack(outs_p, 2), stack(outs_s, 2), stack(outs_p, 3), stack(outs_s, 3),
            stack(outs_p, 4), stack(outs_s, 4), stack(outs_p, 5), stack(outs_s, 5),
            stack(outs_p, 6), stack(outs_s, 6))
```

```python
import functools

import numpy as np
import jax
import jax.numpy as jnp
from jax import lax
from jax.experimental import pallas as pl
from jax.experimental.pallas import tpu as pltpu

F32 = jnp.float32
BF16 = jnp.bfloat16

LANES = 128
SUBLANES = 8
VMEM_LIMIT_BYTES = 56 * 1024 * 1024

GROUP = 256
HEADS = 4
HEAD_DIM = 64
CONV_W = 4
SSM_DSTATE = 128
SSM_CHUNK = 128
LIN_CHUNK = 64
MOBA_BLOCK = 256
MOBA_TOPK = 3
PAGE_SIZE = 128
X_HEADS = 4
N_SMALL = 12
COL_MOBA = 0
COL_SSM_Z = 3
COL_SSM_X = 4
COL_HGRN = 7
COL_GDN = 11
N_WIDE = 15 * GROUP
COL_SMALL = N_WIDE // LANES
EPS = 1e-6
NEG_INF = float("-inf")
ALIBI_SLOPES = tuple(2.0 ** (-8.0 * (h + 1) / HEADS) for h in range(HEADS))


def _rms(x, w):
    return x * lax.rsqrt(jnp.mean(x * x, axis=-1, keepdims=True) + EPS) * w


def _sigmoid(x):
    return 1.0 / (1.0 + jnp.exp(-x))


def _silu(x):
    return x * _sigmoid(x)


def _softplus(x):
    return jnp.maximum(x, 0.0) + jnp.log(1.0 + jnp.exp(-jnp.abs(x)))


def _dot(a, b):
    return jnp.dot(a.astype(BF16), b.astype(BF16), preferred_element_type=F32)


def _dot_nt(a, b):
    return lax.dot_general(a.astype(BF16), b.astype(BF16), (((1,), (1,)), ((), ())),
                           preferred_element_type=F32)


def _dot_f32(a, b):
    return jnp.dot(a, b, preferred_element_type=F32, precision=lax.Precision.HIGHEST)


def _params(*sem, flags=None):
    return pltpu.CompilerParams(dimension_semantics=sem, vmem_limit_bytes=VMEM_LIMIT_BYTES, flags=flags)


def _resident(shape):
    return pl.BlockSpec(shape, lambda *_: (0,) * len(shape), pipeline_mode=pl.Buffered(1))


def _norm_matmul_kernel(x_ref, nw_ref, w_ref, o_ref, *, full_precision):
    xn = _rms(x_ref[...], nw_ref[...])
    o_ref[...] = _dot_f32(xn, w_ref[...]) if full_precision else _dot(xn, w_ref[...])


def norm_matmul(x, norm_w, w, tm):
    n, d = x.shape
    c = w.shape[1]
    return pl.pallas_call(
        functools.partial(_norm_matmul_kernel, full_precision=(w.dtype == F32)),
        grid=(n // tm,),
        in_specs=[pl.BlockSpec((tm, d), lambda i: (i, 0)),
                  _resident((1, d)),
                  _resident((d, c))],
        out_specs=pl.BlockSpec((tm, c), lambda i: (i, 0)),
        out_shape=jax.ShapeDtypeStruct((n, c), F32),
        compiler_params=_params("parallel"),
        name="norm_matmul",
    )(x, norm_w.reshape(1, d), w)


def _mix_out_and_query(x, mixed, nw, wout_ref, wxq_ref):
    x = x + _rms(_dot(mixed, wout_ref[...]), nw[0:1])
    return x, _dot(_rms(x, nw[1:2]), wxq_ref[...])


def _memory_attention(q, mk, mv):
    xdh = q.shape[1] // X_HEADS
    heads = []
    for h in range(X_HEADS):
        sl = slice(h * xdh, (h + 1) * xdh)
        s = _dot_nt(q[:, sl], mk[:, sl]) * (xdh ** -0.5)
        p = jnp.exp(s - jnp.max(s, axis=-1, keepdims=True))
        heads.append(_dot(p, mv[:, sl]) / jnp.sum(p, axis=-1, keepdims=True))
    return jnp.concatenate(heads, axis=-1)


def _attn_out_and_ffn(x, att, nw, wxo_ref, wgu_ref, wdown_ref):
    x = x + _rms(_dot(att, wxo_ref[...]), nw[2:3])
    gu = _dot(_rms(x, nw[3:4]), wgu_ref[...])
    hid = gu.shape[1] // 2
    act = _silu(gu[:, :hid]) * gu[:, hid:]
    return x + _rms(_dot(act, wdown_ref[...]), nw[4:5])


def _post_mixer_kernel(x_ref, oa_ref, ob_ref, oc_ref, od_ref, mk_ref, mv_ref, norms_ref, wout_ref,
                       wxq_ref, wxo_ref, wgu_ref, wdown_ref, o_ref):
    nw = norms_ref[...]
    mixed = jnp.concatenate([oa_ref[0], ob_ref[0], oc_ref[0], od_ref[0]], axis=-1)
    x, q = _mix_out_and_query(x_ref[0], mixed, nw, wout_ref, wxq_ref)
    att = _memory_attention(q, mk_ref[0], mv_ref[0])
    o_ref[0] = _attn_out_and_ffn(x, att, nw, wxo_ref, wgu_ref, wdown_ref)


def post_mixer(x, mix_parts, mk, mv, norms, w_out, w_xq, w_xo, w_gu, w_down, tm):
    b, L, d = x.shape
    m = mk.shape[1]
    row = pl.BlockSpec((1, tm, d), lambda i, j: (i, j, 0))
    part = pl.BlockSpec((1, tm, GROUP), lambda i, j: (i, j, 0))
    mem = pl.BlockSpec((1, m, d), lambda i, j: (i, 0, 0))
    return pl.pallas_call(
        _post_mixer_kernel,
        grid=(b, L // tm),
        in_specs=[row, part, part, part, part, mem, mem, _resident(norms.shape),
                  _resident(w_out.shape), _resident(w_xq.shape), _resident(w_xo.shape),
                  _resident(w_gu.shape), _resident(w_down.shape)],
        out_specs=row,
        out_shape=jax.ShapeDtypeStruct((b, L, d), F32),
        compiler_params=_params("parallel", "parallel"),
        name="post_mixer",
    )(x, *mix_parts, mk, mv, norms, w_out, w_xq, w_xo, w_gu, w_down)


def _sample_pre_kernel(x_ref, oa_ref, ob_ref, oc_ref, od_ref, norms_ref, wout_ref, wxq_ref,
                       x1_ref, q_ref):
    mixed = jnp.concatenate([oa_ref[...], ob_ref[...], oc_ref[...], od_ref[...]], axis=-1)
    x1, q = _mix_out_and_query(x_ref[...], mixed, norms_ref[...], wout_ref, wxq_ref)
    x1_ref[...] = x1
    q_ref[...] = q


def _sample_attn_kernel(q_ref, mk_ref, mv_ref, o_ref, q8, *, t):
    q8[0:t, :] = q_ref[0]
    q8[t:, :] = jnp.zeros((q8.shape[0] - t, q8.shape[1]), F32)
    o_ref[0] = _memory_attention(q8[...], mk_ref[0], mv_ref[0])[0:t]


def _sample_post_kernel(x_ref, att_ref, norms_ref, wxo_ref, wgu_ref, wdown_ref, o_ref):
    o_ref[...] = _attn_out_and_ffn(x_ref[...], att_ref[...], norms_ref[...], wxo_ref, wgu_ref,
                                   wdown_ref)


def post_mixer_sample(x, mix_parts, mk, mv, norms, w_out, w_xq, w_xo, w_gu, w_down):
    b, t, d = x.shape
    n = b * t
    m = mk.shape[1]
    flat = lambda a: a.reshape(n, a.shape[-1])
    full = lambda shp: pl.BlockSpec(shp, lambda i: (0,) * len(shp))
    x1, q = pl.pallas_call(
        _sample_pre_kernel,
        grid=(1,),
        in_specs=[full((n, d))] + [full((n, GROUP))] * 4
                 + [full(norms.shape), _resident(w_out.shape), _resident(w_xq.shape)],
        out_specs=[full((n, d)), full((n, d))],
        out_shape=[jax.ShapeDtypeStruct((n, d), F32)] * 2,
        compiler_params=_params("arbitrary"),
        name="sample_pre",
    )(flat(x), *[flat(p) for p in mix_parts], norms, w_out, w_xq)
    seq = pl.BlockSpec((1, t, d), lambda i: (i, 0, 0))
    mem = pl.BlockSpec((1, m, d), lambda i: (i, 0, 0))
    att = pl.pallas_call(
        functools.partial(_sample_attn_kernel, t=t),
        grid=(b,),
        in_specs=[seq, mem, mem],
        out_specs=seq,
        out_shape=jax.ShapeDtypeStruct((b, t, d), F32),
        scratch_shapes=[pltpu.VMEM((SUBLANES, d), F32)],
        compiler_params=_params("parallel"),
        name="sample_attn",
    )(q.reshape(b, t, d), mk, mv)
    out = pl.pallas_call(
        _sample_post_kernel,
        grid=(1,),
        in_specs=[full((n, d)), full((n, d)), full(norms.shape), _resident(w_xo.shape),
                  _resident(w_gu.shape), _resident(w_down.shape)],
        out_specs=full((n, d)),
        out_shape=jax.ShapeDtypeStruct((n, d), F32),
        compiler_params=_params("arbitrary"),
        name="sample_post",
    )(x1, flat(att), norms, w_xo, w_gu, w_down)
    return out.reshape(b, t, d)


def _tril(n, strict=False):
    r = lax.broadcasted_iota(jnp.int32, (n, n), 0)
    c = lax.broadcasted_iota(jnp.int32, (n, n), 1)
    return (r > c) if strict else (r >= c)


def _head_expand(n_rows, n_cols, width, offset=0):
    r = lax.broadcasted_iota(jnp.int32, (n_rows, n_cols), 0)
    c = lax.broadcasted_iota(jnp.int32, (n_rows, n_cols), 1)
    return (r == (c // width) + offset).astype(F32)


def _stage_rows(dst, src_ref, l_blk, row0=0):
    n = dst.shape[0]
    dst[row0:row0 + l_blk, :] = src_ref[0]
    if row0 + l_blk < n:
        dst[row0 + l_blk:n, :] = jnp.zeros((n - row0 - l_blk, dst.shape[1]), F32)


def _causal_conv(buf, cw_ref, cs):
    out = cw_ref[0:1, :] * buf[5:5 + cs, :]
    for j in range(1, CONV_W):
        out = out + cw_ref[j:j + 1, :] * buf[5 + j:5 + j + cs, :]
    return out


def _ssd_kernel(z_ref, x_ref, b_ref, c_ref, sm_ref, conv0_ref, h0_ref, cw_ref, cb_ref, vec_ref,
                o_ref, conv1_ref, h1_ref, buf, smbuf, hst, *, cs, l_blk, n_chunks):
    c_idx = pl.program_id(1)

    @pl.when(c_idx == 0)
    def _():
        buf[5:8, :] = conv0_ref[0]
        hst[...] = h0_ref[0]

    _stage_rows(buf.at[:, 0:GROUP], x_ref, l_blk, 8)
    _stage_rows(buf.at[:, GROUP:2 * GROUP], b_ref, l_blk, 8)
    _stage_rows(buf.at[:, 2 * GROUP:3 * GROUP], c_ref, l_blk, 8)
    _stage_rows(smbuf, sm_ref, l_blk)

    xbc = _silu(_causal_conv(buf, cw_ref, cs) + cb_ref[...])
    conv_tail = buf[5 + l_blk:8 + l_blk, :]
    buf[5:8, :] = conv_tail
    xs = xbc[:, 0:GROUP]
    bm = xbc[:, GROUP:2 * GROUP]
    cm = xbc[:, 2 * GROUP:3 * GROUP]

    dt = _softplus(smbuf[...] + vec_ref[0:1, 0:LANES])
    if l_blk < cs:
        rows = lax.broadcasted_iota(jnp.int32, (cs, LANES), 0)
        dt = jnp.where(rows < l_blk, dt, 0.0)
    a = dt * (-jnp.exp(vec_ref[1:2, 0:LANES]))
    acum = _dot_f32(_tril(cs).astype(F32), a)
    expand = _head_expand(LANES, GROUP, HEAD_DIM)
    dt_w = _dot_f32(dt, expand)
    acum_w = _dot_f32(acum, expand)
    a_last = acum_w[cs - 1:cs, :]
    xdt = xs * dt_w
    e_acum = jnp.exp(acum_w)
    xw = xdt * jnp.exp(a_last - acum_w)
    e_last = jnp.exp(a_last)
    acum_t = acum.T
    causal = _tril(cs)

    ys = []
    for h in range(HEADS):
        g = h // (HEADS // 2)
        hs = slice(h * HEAD_DIM, (h + 1) * HEAD_DIM)
        gs = slice(g * SSM_DSTATE, (g + 1) * SSM_DSTATE)
        st = hst[h]
        dec = jnp.exp(jnp.where(causal, acum[:, h:h + 1] - acum_t[h:h + 1, :], NEG_INF))
        y = _dot(_dot_nt(cm[:, gs], bm[:, gs]) * dec, xdt[:, hs])
        y = y + _dot_nt(cm[:, gs], st) * e_acum[:, hs]
        ys.append(y)
        hst[h] = e_last[:, h * HEAD_DIM:h * HEAD_DIM + 1] * st + _dot(xw[:, hs].T, bm[:, gs])
    y = jnp.concatenate(ys, axis=-1) + vec_ref[2:3, :] * xs
    o_ref[0] = _rms(y[0:l_blk] * _silu(z_ref[0]), vec_ref[3:4, :])

    @pl.when(c_idx == n_chunks - 1)
    def _():
        conv1_ref[0] = conv_tail
        h1_ref[0] = hst[...]


def ssd_mixer(proj, conv0, h0, conv_w, conv_b, dt_bias, a_log, d_skip, norm_w):
    b, L, _ = proj.shape
    cs = SSM_CHUNK
    l_blk = min(cs, L)
    n_chunks = L // l_blk
    vec = jnp.zeros((8, GROUP), F32)
    vec = vec.at[0, :HEADS].set(dt_bias).at[1, :HEADS].set(a_log)
    vec = vec.at[2].set(jnp.repeat(d_skip, HEAD_DIM)).at[3].set(norm_w)
    col = lambda k: pl.BlockSpec((1, l_blk, GROUP), lambda i, j, k=k: (i, j, k))
    per_b = lambda shp: pl.BlockSpec((1,) + shp, lambda i, j: (i,) + (0,) * len(shp))
    c3 = 3 * GROUP
    return pl.pallas_call(
        functools.partial(_ssd_kernel, cs=cs, l_blk=l_blk, n_chunks=n_chunks),
        grid=(b, n_chunks),
        in_specs=[col(COL_SSM_Z), col(COL_SSM_X), col(COL_SSM_X + 1), col(COL_SSM_X + 2),
                  pl.BlockSpec((1, l_blk, LANES), lambda i, j: (i, j, COL_SMALL)),
                  per_b((CONV_W - 1, c3)), per_b((HEADS, HEAD_DIM, SSM_DSTATE)),
                  _resident((CONV_W, c3)), _resident((1, c3)), _resident((8, GROUP))],
        out_specs=[pl.BlockSpec((1, l_blk, GROUP), lambda i, j: (i, j, 0)),
                   per_b((CONV_W - 1, c3)), per_b((HEADS, HEAD_DIM, SSM_DSTATE))],
        out_shape=[jax.ShapeDtypeStruct((b, L, GROUP), F32),
                   jax.ShapeDtypeStruct((b, CONV_W - 1, c3), F32),
                   jax.ShapeDtypeStruct((b, HEADS, HEAD_DIM, SSM_DSTATE), F32)],
        scratch_shapes=[pltpu.VMEM((cs + 8, c3), F32), pltpu.VMEM((cs, LANES), F32),
                        pltpu.VMEM((HEADS, HEAD_DIM, SSM_DSTATE), F32)],
        compiler_params=_params("parallel", "arbitrary"),
        name="ssd_mixer",
    )(proj, proj, proj, proj, proj, conv0, h0, conv_w, conv_b.reshape(1, c3), vec)


def _dot_tn(a, b):
    return lax.dot_general(a.astype(BF16), b.astype(BF16), (((0,), (0,)), ((), ())),
                           preferred_element_type=F32)


def _head_blocks(n, width):
    r = lax.broadcasted_iota(jnp.int32, (n, n), 0)
    c = lax.broadcasted_iota(jnp.int32, (n, n), 1)
    return ((r // width) == (c // width)).astype(F32)


def _dot_split(x, m01, passes, m01_left=False):
    m = m01.astype(BF16)
    acc = None
    for _ in range(passes):
        hi = x.astype(BF16)
        part = jnp.dot(m, hi, preferred_element_type=F32) if m01_left else jnp.dot(hi, m, preferred_element_type=F32)
        acc = part if acc is None else acc + part
        x = x - hi.astype(F32)
    return acc


def _head_rms_gate(o, gate, nw_row, l_blk):
    ms = _dot_split(o * o, _head_blocks(GROUP, HEAD_DIM), 2) * (1.0 / HEAD_DIM)
    return (o * lax.rsqrt(ms + EPS) * nw_row)[0:l_blk] * _silu(gate)


def _hgrn_kernel(q_ref, f_ref, i_ref, g_ref, s0_ref, lbraw_ref, nw_ref, o_ref, s1_ref,
                 stage, sst, *, cs, l_blk, n_chunks, layer):
    c_idx = pl.program_id(1)

    @pl.when(c_idx == 0)
    def _():
        sst[...] = s0_ref[0]

    if l_blk < cs:
        _stage_rows(stage.at[0], q_ref, l_blk)
        _stage_rows(stage.at[1], f_ref, l_blk)
        _stage_rows(stage.at[2], i_ref, l_blk)
        q, fx, v = stage[0], stage[1], stage[2]
    else:
        q, fx, v = q_ref[0], f_ref[0], i_ref[0]

    raw = lbraw_ref[...]
    e = jnp.exp(raw - jnp.max(raw, axis=0, keepdims=True))
    sm = e / jnp.sum(e, axis=0, keepdims=True)
    lb = jnp.zeros((1, GROUP), F32)
    for i in range(1, layer + 1):
        lb = lb + sm[i:i + 1, :]

    log_sig = jnp.minimum(fx, 0.0) - jnp.log1p(jnp.exp(-jnp.abs(fx)))
    la = jnp.log(lb)
    lbb = jnp.log1p(-lb) + log_sig
    log_f = jnp.maximum(la, lbb) + jnp.log1p(jnp.exp(-jnp.abs(la - lbb)))
    k = (1.0 - lb) * _sigmoid(-fx)
    if l_blk < cs:
        rows = lax.broadcasted_iota(jnp.int32, (cs, GROUP), 0)
        log_f = jnp.where(rows < l_blk, log_f, 0.0)
        k = jnp.where(rows < l_blk, k, 0.0)

    bcum = _dot_f32(_tril(cs).astype(F32), log_f)
    b_mid = bcum[cs // 2 - 1:cs // 2, :]
    b_last = bcum[cs - 1:cs, :]
    qe = q * jnp.exp(bcum - b_mid)
    ke = k * jnp.exp(b_mid - bcum)
    qs = q * jnp.exp(bcum)
    kd = k * jnp.exp(b_last - bcum)
    e_last = jnp.exp(b_last)
    causal = _tril(cs)

    os_ = []
    for h in range(HEADS):
        hs = slice(h * HEAD_DIM, (h + 1) * HEAD_DIM)
        st_t = sst[h]
        att = jnp.where(causal, _dot_nt(qe[:, hs], ke[:, hs]), 0.0)
        os_.append(_dot(att, v[:, hs]) + _dot_nt(qs[:, hs], st_t))
        sst[h] = e_last[:, hs] * st_t + _dot_tn(v[:, hs], kd[:, hs])
    o = jnp.concatenate(os_, axis=-1)
    o_ref[0] = _head_rms_gate(o, g_ref[0], nw_ref[...], l_blk)

    @pl.when(c_idx == n_chunks - 1)
    def _():
        s1_ref[0] = sst[...]


def hgrn_mixer(proj, s0, lb_raw, norm_w, layer):
    b, L, _ = proj.shape
    cs = LIN_CHUNK
    l_blk = min(cs, L)
    n_chunks = L // l_blk
    col = lambda k: pl.BlockSpec((1, l_blk, GROUP), lambda i, j, k=k: (i, j, k))
    st = pl.BlockSpec((1, HEADS, HEAD_DIM, HEAD_DIM), lambda i, j: (i, 0, 0, 0))
    return pl.pallas_call(
        functools.partial(_hgrn_kernel, cs=cs, l_blk=l_blk, n_chunks=n_chunks, layer=layer),
        grid=(b, n_chunks),
        in_specs=[col(COL_HGRN), col(COL_HGRN + 1), col(COL_HGRN + 2), col(COL_HGRN + 3), st,
                  _resident(lb_raw.shape), _resident((1, GROUP))],
        out_specs=[pl.BlockSpec((1, l_blk, GROUP), lambda i, j: (i, j, 0)), st],
        out_shape=[jax.ShapeDtypeStruct((b, L, GROUP), F32),
                   jax.ShapeDtypeStruct((b, HEADS, HEAD_DIM, HEAD_DIM), F32)],
        scratch_shapes=[pltpu.VMEM((3, cs, GROUP), F32),
                        pltpu.VMEM((HEADS, HEAD_DIM, HEAD_DIM), F32)],
        compiler_params=_params("parallel", "arbitrary"),
        name="hgrn_mixer",
    )(proj, proj, proj, proj, s0, lb_raw, jnp.tile(norm_w, HEADS).reshape(1, GROUP))


LIN_PREP_CHUNKS = 4


def _chunk_rows(x, cs, cps, row):
    return jnp.concatenate(
        [jnp.broadcast_to(x[i * cs + row:i * cs + row + 1, :], (cs, x.shape[1])) for i in range(cps)], axis=0)


def _hgrn_prep_kernel(q_ref, f_ref, i_ref, lbraw_ref, oi_ref, qs_ref, kd_ref, aux_ref, *, cs, cps, layer):
    rows = cs * cps
    q, fx, v = q_ref[0], f_ref[0], i_ref[0]
    raw = lbraw_ref[...]
    e = jnp.exp(raw - jnp.max(raw, axis=0, keepdims=True))
    sm = e / jnp.sum(e, axis=0, keepdims=True)
    lb = jnp.zeros((1, GROUP), F32)
    for i in range(1, layer + 1):
        lb = lb + sm[i:i + 1, :]
    log_sig = jnp.minimum(fx, 0.0) - jnp.log1p(jnp.exp(-jnp.abs(fx)))
    la = jnp.log(lb)
    lbb = jnp.log1p(-lb) + log_sig
    log_f = jnp.maximum(la, lbb) + jnp.log1p(jnp.exp(-jnp.abs(la - lbb)))
    k = (1.0 - lb) * _sigmoid(-fx)

    r = lax.broadcasted_iota(jnp.int32, (rows, rows), 0)
    c = lax.broadcasted_iota(jnp.int32, (rows, rows), 1)
    incl = jnp.logical_and(r >= c, (r // cs) == (c // cs))
    bcum = _dot_split(log_f, incl.astype(F32), 3, m01_left=True)
    b_mid = _chunk_rows(bcum, cs, cps, cs // 2 - 1)
    b_last = _chunk_rows(bcum, cs, cps, cs - 1)
    qe = q * jnp.exp(bcum - b_mid)
    ke = k * jnp.exp(b_mid - bcum)
    qs_ref[0] = q * jnp.exp(bcum)
    kd_ref[0] = k * jnp.exp(b_last - bcum)
    for i in range(cps):
        aux_ref[0, i * SUBLANES:(i + 1) * SUBLANES, :] = jnp.exp(b_last[i * cs:i * cs + SUBLANES, :])
    os_ = []
    for h in range(HEADS):
        hs = slice(h * HEAD_DIM, (h + 1) * HEAD_DIM)
        att = jnp.where(incl, _dot_nt(qe[:, hs], ke[:, hs]), 0.0)
        os_.append(_dot(att, v[:, hs]))
    oi_ref[0] = jnp.concatenate(os_, axis=-1)


def _hgrn_scan_kernel(oi_ref, qs_ref, kd_ref, aux_ref, v_ref, g_ref, s0_ref, nw_ref, o_ref, s1_ref, sst,
                      *, nb, n_chunks):
    c_idx = pl.program_id(1)
    blocks = _head_blocks(GROUP, HEAD_DIM)

    @pl.when(c_idx == 0)
    def _():
        sst[...] = jnp.zeros(sst.shape, F32)
        for bi in range(nb):
            for h in range(HEADS):
                hs = slice(h * HEAD_DIM, (h + 1) * HEAD_DIM)
                sst[bi, hs, hs] = s0_ref[bi, h]

    for bi in range(nb):
        st_t = sst[bi]
        o = oi_ref[bi] + _dot_nt(qs_ref[bi], st_t)
        sst[bi] = aux_ref[bi, 0:1, :] * st_t + _dot_tn(v_ref[bi], kd_ref[bi]) * blocks
        o_ref[bi] = _head_rms_gate(o, g_ref[bi], nw_ref[...], o.shape[0])

    @pl.when(c_idx == n_chunks - 1)
    def _():
        for bi in range(nb):
            for h in range(HEADS):
                hs = slice(h * HEAD_DIM, (h + 1) * HEAD_DIM)
                s1_ref[bi, h] = sst[bi, hs, hs]


def hgrn_mixer_long(proj, s0, lb_raw, norm_w, layer):
    b, L, _ = proj.shape
    cs = LIN_CHUNK
    cps = LIN_PREP_CHUNKS
    rows = cs * cps
    n_chunks = L // cs
    nb = 4 if b % 4 == 0 else 1
    col = lambda k: pl.BlockSpec((1, rows, GROUP), lambda i, j, k=k: (i, j, k))
    out = pl.BlockSpec((1, rows, GROUP), lambda i, j: (i, j, 0))
    wide = jax.ShapeDtypeStruct((b, L, GROUP), F32)
    oi, qs, kd, aux = pl.pallas_call(
        functools.partial(_hgrn_prep_kernel, cs=cs, cps=cps, layer=layer),
        grid=(b, L // rows),
        in_specs=[col(COL_HGRN), col(COL_HGRN + 1), col(COL_HGRN + 2), _resident(lb_raw.shape)],
        out_specs=[out, out, out, pl.BlockSpec((1, cps * SUBLANES, GROUP), lambda i, j: (i, j, 0))],
        out_shape=[wide, wide, wide, jax.ShapeDtypeStruct((b, n_chunks * SUBLANES, GROUP), F32)],
        compiler_params=_params("parallel", "parallel"),
        name="hgrn_prep",
    )(proj, proj, proj, lb_raw)
    blk = pl.BlockSpec((nb, cs, GROUP), lambda i, c: (i, c, 0))
    pcol = lambda k: pl.BlockSpec((nb, cs, GROUP), lambda i, c, k=k: (i, c, k))
    state = pl.BlockSpec((nb, HEADS, HEAD_DIM, HEAD_DIM), lambda i, c: (i, 0, 0, 0))
    return pl.pallas_call(
        functools.partial(_hgrn_scan_kernel, nb=nb, n_chunks=n_chunks),
        grid=(b // nb, n_chunks),
        in_specs=[blk, blk, blk, pl.BlockSpec((nb, SUBLANES, GROUP), lambda i, c: (i, c, 0)),
                  pcol(COL_HGRN + 2), pcol(COL_HGRN + 3), state, _resident((1, GROUP))],
        out_specs=[blk, state],
        out_shape=[wide, jax.ShapeDtypeStruct((b, HEADS, HEAD_DIM, HEAD_DIM), F32)],
        scratch_shapes=[pltpu.VMEM((nb, GROUP, GROUP), F32)],
        compiler_params=_params("parallel", "arbitrary"),
        name="hgrn_scan",
    )(oi, qs, kd, aux, proj, proj, s0, jnp.tile(norm_w, HEADS).reshape(1, GROUP))


def _doubling_level_masks(n, block):
    r = np.arange(n)[:, None]
    c = np.arange(n)[None, :]
    out, s = [], 1
    while s < block:
        out.append((r // (2 * s) == c // (2 * s)) & ((r // s) % 2 == 1) & ((c // s) % 2 == 0))
        s *= 2
    return np.stack(out).astype(np.float32)


def _unit_lower_inverse_minus_eye(a, block=None, level_masks_ref=None):
    n = a.shape[0]
    block = n if block is None else block
    r = lax.broadcasted_iota(jnp.int32, (n, n), 0)
    c = lax.broadcasted_iota(jnp.int32, (n, n), 1)
    dx = None
    s, level = 1, 0
    while s < block:
        if level_masks_ref is None:
            lower_left = jnp.logical_and((r // (2 * s)) == (c // (2 * s)),
                                         jnp.logical_and((r // s) % 2 == 1, (c // s) % 2 == 0))
            b = jnp.where(lower_left, a, 0.0)
        else:
            b = a * level_masks_ref[level]
        if dx is None:
            dx = -b
        else:
            m = b + _dot(dx, b)
            dx = dx - m - _dot(m, dx)
        s *= 2
        level += 1
    return dx


def _gdn_kernel(q_ref, k_ref, v_ref, z_ref, sm_ref, conv0_ref, s0_ref, cw_ref, vec_ref, nw_ref,
                o_ref, conv1_ref, s1_ref, buf, smbuf, sst, *, cs, l_blk, n_chunks):
    c_idx = pl.program_id(1)

    @pl.when(c_idx == 0)
    def _():
        buf[5:8, :] = conv0_ref[0]
        sst[...] = s0_ref[0]

    _stage_rows(buf.at[:, 0:GROUP], q_ref, l_blk, 8)
    _stage_rows(buf.at[:, GROUP:2 * GROUP], k_ref, l_blk, 8)
    _stage_rows(buf.at[:, 2 * GROUP:3 * GROUP], v_ref, l_blk, 8)
    _stage_rows(smbuf, sm_ref, l_blk)

    qkv = _silu(_causal_conv(buf, cw_ref, cs))
    conv_tail = buf[5 + l_blk:8 + l_blk, :]
    buf[5:8, :] = conv_tail
    q = qkv[:, 0:GROUP]
    k = qkv[:, GROUP:2 * GROUP]
    v = qkv[:, 2 * GROUP:3 * GROUP]
    blocks = _head_blocks(GROUP, HEAD_DIM)
    q = q * lax.rsqrt(_dot_f32(q * q, blocks) + EPS) * (HEAD_DIM ** -0.5)
    k = k * lax.rsqrt(_dot_f32(k * k, blocks) + EPS)

    sm = smbuf[...]
    beta = _sigmoid(sm)
    log_g = -jnp.exp(vec_ref[1:2, :]) * _softplus(sm + vec_ref[0:1, :])
    if l_blk < cs:
        rows = lax.broadcasted_iota(jnp.int32, (cs, LANES), 0)
        beta = jnp.where(rows < l_blk, beta, 0.0)
        log_g = jnp.where(rows < l_blk, log_g, 0.0)
    gam = _dot_f32(_tril(cs).astype(F32), log_g)
    beta_w = _dot_f32(beta, _head_expand(LANES, GROUP, HEAD_DIM, HEADS))
    gam_w = _dot_f32(gam, _head_expand(LANES, GROUP, HEAD_DIM, 2 * HEADS))
    g_last = gam_w[cs - 1:cs, :]
    e_gam = jnp.exp(gam_w)
    kb = k * beta_w
    vb = v * beta_w
    kbg = kb * e_gam
    qg = q * e_gam
    k_dec = k * jnp.exp(g_last - gam_w)
    e_last = jnp.exp(g_last)
    gam_t = gam.T
    incl = _tril(cs)
    strict = _tril(cs, strict=True)

    os_ = []
    for h in range(HEADS):
        hs = slice(h * HEAD_DIM, (h + 1) * HEAD_DIM)
        r = 2 * HEADS + h
        st = sst[h]
        dec_incl = jnp.exp(jnp.where(incl, gam[:, r:r + 1] - gam_t[r:r + 1, :], NEG_INF))
        a_mat = jnp.where(strict, _dot_nt(kb[:, hs], k[:, hs]) * dec_incl, 0.0)
        tx = _unit_lower_inverse_minus_eye(a_mat)
        u = vb[:, hs] + _dot(tx, vb[:, hs])
        w = kbg[:, hs] + _dot(tx, kbg[:, hs])
        v_new = u - _dot(w, st)
        qk = _dot_nt(q[:, hs], k[:, hs]) * dec_incl
        os_.append(_dot(qk, v_new) + _dot(qg[:, hs], st))
        sst[h] = e_last[:, h * HEAD_DIM:h * HEAD_DIM + 1] * st + _dot_tn(k_dec[:, hs], v_new)
    o = jnp.concatenate(os_, axis=-1)
    o_ref[0] = _head_rms_gate(o, z_ref[0], nw_ref[...], l_blk)

    @pl.when(c_idx == n_chunks - 1)
    def _():
        conv1_ref[0] = conv_tail
        s1_ref[0] = sst[...]


def gdn_mixer(proj, conv0, s0, conv_w, a_log, dt_bias, norm_w):
    b, L, _ = proj.shape
    cs = LIN_CHUNK
    l_blk = min(cs, L)
    n_chunks = L // l_blk
    vec = jnp.zeros((8, LANES), F32)
    vec = vec.at[0, 2 * HEADS:3 * HEADS].set(dt_bias).at[1, 2 * HEADS:3 * HEADS].set(a_log)
    col = lambda k: pl.BlockSpec((1, l_blk, GROUP), lambda i, j, k=k: (i, j, k))
    per_b = lambda shp: pl.BlockSpec((1,) + shp, lambda i, j: (i,) + (0,) * len(shp))
    c3 = 3 * GROUP
    return pl.pallas_call(
        functools.partial(_gdn_kernel, cs=cs, l_blk=l_blk, n_chunks=n_chunks),
        grid=(b, n_chunks),
        in_specs=[col(COL_GDN), col(COL_GDN + 1), col(COL_GDN + 2), col(COL_GDN + 3),
                  pl.BlockSpec((1, l_blk, LANES), lambda i, j: (i, j, COL_SMALL)),
                  per_b((CONV_W - 1, c3)), per_b((HEADS, HEAD_DIM, HEAD_DIM)),
                  _resident((CONV_W, c3)), _resident((8, LANES)), _resident((1, GROUP))],
        out_specs=[pl.BlockSpec((1, l_blk, GROUP), lambda i, j: (i, j, 0)),
                   per_b((CONV_W - 1, c3)), per_b((HEADS, HEAD_DIM, HEAD_DIM))],
        out_shape=[jax.ShapeDtypeStruct((b, L, GROUP), F32),
                   jax.ShapeDtypeStruct((b, CONV_W - 1, c3), F32),
                   jax.ShapeDtypeStruct((b, HEADS, HEAD_DIM, HEAD_DIM), F32)],
        scratch_shapes=[pltpu.VMEM((cs + 8, c3), F32), pltpu.VMEM((cs, LANES), F32),
                        pltpu.VMEM((HEADS, HEAD_DIM, HEAD_DIM), F32)],
        compiler_params=_params("parallel", "arbitrary"),
        name="gdn_mixer",
    )(proj, proj, proj, proj, proj, conv0, s0, conv_w, vec,
      jnp.tile(norm_w, HEADS).reshape(1, GROUP))


GDN_PREP_CHUNKS = 4


def _gdn_prep_kernel(q_ref, k_ref, v_ref, pq_ref, pk_ref, pv_ref, sm_ref, conv0_ref, cw_ref, vec_ref,
                     lvl_ref, u_ref, w_ref, qg_ref, kd_ref, qk_ref, aux_ref, buf, *, cs, cps):
    j = pl.program_id(1)
    rows = cs * cps

    @pl.when(j == 0)
    def _():
        buf[5:8, :] = conv0_ref[0]

    @pl.when(j > 0)
    def _():
        for i, ref in enumerate((pq_ref, pk_ref, pv_ref)):
            buf[5:8, i * GROUP:(i + 1) * GROUP] = ref[0, SUBLANES - 3:SUBLANES, :]

    for i, ref in enumerate((q_ref, k_ref, v_ref)):
        buf[8:8 + rows, i * GROUP:(i + 1) * GROUP] = ref[0]

    qkv = _silu(_causal_conv(buf, cw_ref, rows))
    q = qkv[:, 0:GROUP]
    k = qkv[:, GROUP:2 * GROUP]
    v = qkv[:, 2 * GROUP:3 * GROUP]
    blocks = _head_blocks(GROUP, HEAD_DIM)
    q = q * lax.rsqrt(_dot_split(q * q, blocks, 2) + EPS) * (HEAD_DIM ** -0.5)
    k = k * lax.rsqrt(_dot_split(k * k, blocks, 2) + EPS)

    sm = sm_ref[0]
    beta = _sigmoid(sm)
    log_g = -jnp.exp(vec_ref[1:2, :]) * _softplus(sm + vec_ref[0:1, :])
    r = lax.broadcasted_iota(jnp.int32, (rows, rows), 0)
    c = lax.broadcasted_iota(jnp.int32, (rows, rows), 1)
    chunk_tril = jnp.logical_and(r >= c, (r // cs) == (c // cs)).astype(F32)
    gam = _dot_split(log_g, chunk_tril, 3, m01_left=True)
    beta_w = _dot_split(beta, _head_expand(LANES, GROUP, HEAD_DIM, HEADS), 3)
    gam_w = _dot_split(gam, _head_expand(LANES, GROUP, HEAD_DIM, 2 * HEADS), 3)
    g_last = jnp.concatenate(
        [jnp.broadcast_to(gam_w[(i + 1) * cs - 1:(i + 1) * cs, :], (cs, GROUP)) for i in range(cps)], axis=0)
    e_gam = jnp.exp(gam_w)
    kb = k * beta_w
    vb = v * beta_w
    kbg = kb * e_gam
    qg_ref[0] = q * e_gam
    kd_ref[0] = k * jnp.exp(g_last - gam_w)
    for i in range(cps):
        aux_ref[0, i * SUBLANES:(i + 1) * SUBLANES, :] = jnp.exp(g_last[i * cs:i * cs + SUBLANES, :])
    gam_t = gam.T
    incl = chunk_tril > 0.0
    strict = jnp.logical_and(incl, r > c)

    us, ws, qks = [], [], []
    for h in range(HEADS):
        hs = slice(h * HEAD_DIM, (h + 1) * HEAD_DIM)
        lane = 2 * HEADS + h
        dec_incl = jnp.exp(jnp.where(incl, gam[:, lane:lane + 1] - gam_t[lane:lane + 1, :], NEG_INF))
        a_mat = jnp.where(strict, _dot_nt(kb[:, hs], k[:, hs]) * dec_incl, 0.0)
        dx = _unit_lower_inverse_minus_eye(a_mat, cs, lvl_ref)
        uw = _dot(dx, jnp.concatenate([vb[:, hs], kbg[:, hs]], axis=-1))
        us.append(vb[:, hs] + uw[:, 0:HEAD_DIM])
        ws.append(kbg[:, hs] + uw[:, HEAD_DIM:2 * HEAD_DIM])
        qk = _dot_nt(q[:, hs], k[:, hs]) * dec_incl
        qks.append(jnp.concatenate([qk[i * cs:(i + 1) * cs, i * cs:(i + 1) * cs] for i in range(cps)], axis=0))
    u_ref[0] = jnp.concatenate(us, axis=-1)
    w_ref[0] = jnp.concatenate(ws, axis=-1)
    qk_ref[0] = jnp.concatenate(qks, axis=-1)


def _gdn_scan_kernel(u_ref, w_ref, qg_ref, kd_ref, qk_ref, aux_ref, z_ref, s0_ref, nw_ref,
                     o_ref, s1_ref, sst, *, nb, n_chunks):
    c_idx = pl.program_id(0)
    blocks = _head_blocks(GROUP, HEAD_DIM)

    @pl.when(c_idx == 0)
    def _():
        sst[...] = jnp.zeros(sst.shape, F32)
        for bi in range(nb):
            for h in range(HEADS):
                hs = slice(h * HEAD_DIM, (h + 1) * HEAD_DIM)
                sst[bi, hs, hs] = s0_ref[bi, h]

    for bi in range(nb):
        st = sst[bi]
        v_new = u_ref[bi] - _dot(w_ref[bi], st)
        v_bd = jnp.concatenate([v_new] * HEADS, axis=0) * blocks
        o = _dot(qk_ref[bi], v_bd) + _dot(qg_ref[bi], st)
        sst[bi] = aux_ref[bi, 0:1, :] * st + _dot_tn(kd_ref[bi], v_new) * blocks
        o_ref[bi] = _head_rms_gate(o, z_ref[bi], nw_ref[...], o.shape[0])

    @pl.when(c_idx == n_chunks - 1)
    def _():
        for bi in range(nb):
            for h in range(HEADS):
                hs = slice(h * HEAD_DIM, (h + 1) * HEAD_DIM)
                s1_ref[bi, h] = sst[bi, hs, hs]


def gdn_mixer_long(proj, conv0, s0, conv_w, a_log, dt_bias, norm_w):
    b, L, _ = proj.shape
    cs = LIN_CHUNK
    cps = GDN_PREP_CHUNKS
    rows = cs * cps
    n_chunks = L // cs
    vec = jnp.zeros((8, LANES), F32)
    vec = vec.at[0, 2 * HEADS:3 * HEADS].set(dt_bias).at[1, 2 * HEADS:3 * HEADS].set(a_log)
    c3 = 3 * GROUP
    col = lambda k: pl.BlockSpec((1, rows, GROUP), lambda i, j, k=k: (i, j, k))
    prev = lambda k: pl.BlockSpec(
        (1, SUBLANES, GROUP), lambda i, j, k=k: (i, jnp.maximum(j * (rows // SUBLANES) - 1, 0), k))
    out = pl.BlockSpec((1, rows, GROUP), lambda i, j: (i, j, 0))
    wide = jax.ShapeDtypeStruct((b, L, GROUP), F32)
    level_masks = jnp.asarray(_doubling_level_masks(rows, cs))
    u, w, qg, kd, qk, aux = pl.pallas_call(
        functools.partial(_gdn_prep_kernel, cs=cs, cps=cps),
        grid=(b, L // rows),
        in_specs=[col(COL_GDN), col(COL_GDN + 1), col(COL_GDN + 2),
                  prev(COL_GDN), prev(COL_GDN + 1), prev(COL_GDN + 2),
                  pl.BlockSpec((1, rows, LANES), lambda i, j: (i, j, COL_SMALL)),
                  pl.BlockSpec((1, CONV_W - 1, c3), lambda i, j: (i, 0, 0)),
                  _resident((CONV_W, c3)), _resident((8, LANES)), _resident(level_masks.shape)],
        out_specs=[out, out, out, out, out,
                   pl.BlockSpec((1, cps * SUBLANES, GROUP), lambda i, j: (i, j, 0))],
        out_shape=[wide, wide, wide, wide, wide,
                   jax.ShapeDtypeStruct((b, n_chunks * SUBLANES, GROUP), F32)],
        scratch_shapes=[pltpu.VMEM((rows + 8, c3), F32)],
        compiler_params=_params("parallel", "parallel"),
        name="gdn_prep",
    )(proj, proj, proj, proj, proj, proj, proj, conv0, conv_w, vec, level_masks)
    blk = pl.BlockSpec((b, cs, GROUP), lambda c: (0, c, 0))
    state = pl.BlockSpec((b, HEADS, HEAD_DIM, HEAD_DIM), lambda c: (0, 0, 0, 0))
    o, s1 = pl.pallas_call(
        functools.partial(_gdn_scan_kernel, nb=b, n_chunks=n_chunks),
        grid=(n_chunks,),
        in_specs=[blk, blk, blk, blk, blk,
                  pl.BlockSpec((b, SUBLANES, GROUP), lambda c: (0, c, 0)),
                  pl.BlockSpec((b, cs, GROUP), lambda c: (0, c, COL_GDN + 3)),
                  state, _resident((1, GROUP))],
        out_specs=[blk, state],
        out_shape=[wide, jax.ShapeDtypeStruct((b, HEADS, HEAD_DIM, HEAD_DIM), F32)],
        scratch_shapes=[pltpu.VMEM((b, GROUP, GROUP), F32)],
        compiler_params=_params("arbitrary"),
        name="gdn_scan",
    )(u, w, qg, kd, qk, aux, proj, s0, jnp.tile(norm_w, HEADS).reshape(1, GROUP))
    conv1 = proj[:, L - (CONV_W - 1):, COL_GDN * GROUP:(COL_GDN + 3) * GROUP]
    return o, conv1, s1


def _moba_prompt_kernel(q_ref, k_ref, v_ref, o_ref, ks, vts, kmean, sel_t, o_t, q_tb, rel, ml,
                        s_buf, p_buf, *, nb):
    blk = MOBA_BLOCK
    qi = pl.program_id(1)

    @pl.when(qi == 0)
    def _():
        for j in range(nb):
            kj = k_ref[0, j * blk:(j + 1) * blk, :]
            kmean[j:j + 1, :] = jnp.mean(kj, axis=0, keepdims=True)
            kjb = kj.astype(BF16)
            for h in range(HEADS):
                ks[h, j] = kjb[:, h * HEAD_DIM:(h + 1) * HEAD_DIM]
            vts[j] = v_ref[0, j * blk:(j + 1) * blk, :].T.astype(BF16)

    q_t = (q_ref[0] * (HEAD_DIM ** -0.5)).T
    q_tb[...] = q_t.astype(BF16)
    blk_row = lax.broadcasted_iota(jnp.int32, (nb, blk), 0)
    rel[...] = (lax.broadcasted_iota(jnp.int32, (blk, blk), 1)
                - lax.broadcasted_iota(jnp.int32, (blk, blk), 0)).astype(F32)

    for h in range(HEADS):
        hs = slice(h * HEAD_DIM, (h + 1) * HEAD_DIM)
        slope = ALIBI_SLOPES[h]
        gate = jnp.where(blk_row < qi, _dot_f32(kmean[:, hs], q_t[hs, :]), NEG_INF)
        sel = jnp.full((nb, blk), NEG_INF, F32)
        for _ in range(MOBA_TOPK):
            top = jnp.max(gate, axis=0, keepdims=True)
            first = jnp.min(jnp.where(gate == top, blk_row, nb), axis=0, keepdims=True)
            pick = blk_row == first
            sel = jnp.where(jnp.logical_and(pick, blk_row < qi), 0.0, sel)
            gate = jnp.where(pick, NEG_INF, gate)
        sel_t[h] = sel
        ml[h, 0:1, :] = jnp.full((1, blk), NEG_INF, F32)
        ml[h, 1:2, :] = jnp.zeros((1, blk), F32)
    o_t[...] = jnp.zeros(o_t.shape, F32)

    def block_step(j, own):
        for h in range(HEADS):
            hs = slice(h * HEAD_DIM, (h + 1) * HEAD_DIM)
            s_buf[h] = _dot(ks[h, j], q_tb[hs, :])
        for h in range(HEADS):
            slope = ALIBI_SLOPES[h]
            if own:
                s = jnp.where(rel[...] >= 0.0, s_buf[h] - slope * rel[...], NEG_INF)
            else:
                off = (qi - j).astype(F32) * (slope * blk)
                s = s_buf[h] - slope * rel[...] + (sel_t[h, pl.ds(j, 1), :] - off)
            m = ml[h, 0:1, :]
            m_new = jnp.maximum(m, jnp.max(s, axis=0, keepdims=True))
            p = jnp.exp(s - m_new)
            p_buf[h] = p.astype(BF16)
            alpha = jnp.exp(m - m_new)
            ml[h, 0:1, :] = m_new
            ml[h, 1:2, :] = alpha * ml[h, 1:2, :] + jnp.sum(p, axis=0, keepdims=True)
            ml[h, 2:3, :] = alpha
        for h in range(HEADS):
            hs = slice(h * HEAD_DIM, (h + 1) * HEAD_DIM)
            o_t[hs, :] = (ml[h, 2:3, :] * o_t[hs, :]
                          + jnp.dot(vts[j, hs, :], p_buf[h], preferred_element_type=F32))

    block_step(qi, True)

    def body(j, carry):
        block_step(j, False)
        return carry

    lax.fori_loop(0, qi, body, 0)
    for h in range(HEADS):
        hs = slice(h * HEAD_DIM, (h + 1) * HEAD_DIM)
        o_t[hs, :] = o_t[hs, :] / ml[h, 1:2, :]
    o_ref[0] = o_t[...].T


def moba_prompt(proj):
    b, S, _ = proj.shape
    blk = MOBA_BLOCK
    nb = S // blk
    seq = lambda k: pl.BlockSpec((1, S, GROUP), lambda i, j, k=k: (i, 0, k))
    return pl.pallas_call(
        functools.partial(_moba_prompt_kernel, nb=nb),
        grid=(b, nb),
        in_specs=[pl.BlockSpec((1, blk, GROUP), lambda i, j: (i, j, COL_MOBA)),
                  seq(COL_MOBA + 1), seq(COL_MOBA + 2)],
        out_specs=pl.BlockSpec((1, blk, GROUP), lambda i, j: (i, j, 0)),
        out_shape=jax.ShapeDtypeStruct((b, S, GROUP), F32),
        scratch_shapes=[pltpu.VMEM((HEADS, nb, blk, HEAD_DIM), BF16),
                        pltpu.VMEM((nb, GROUP, blk), BF16),
                        pltpu.VMEM((nb, GROUP), F32),
                        pltpu.VMEM((HEADS, nb, blk), F32),
                        pltpu.VMEM((GROUP, blk), F32),
                        pltpu.VMEM((GROUP, blk), BF16),
                        pltpu.VMEM((blk, blk), F32),
                        pltpu.VMEM((HEADS, SUBLANES, blk), F32),
                        pltpu.VMEM((HEADS, blk, blk), F32),
                        pltpu.VMEM((HEADS, blk, blk), BF16)],
        compiler_params=_params("parallel", "arbitrary"),
        name="moba_prompt",
    )(proj, proj, proj)


PAGES_PER_BLOCK = MOBA_BLOCK // PAGE_SIZE
SELECT_PAGE_BUFFERS = 32
SELECT_UNROLL = 4


def _moba_select_kernel(pt_ref, q_ref, kc_ref, idx_ref, pages, kmean_t, q8, sems, *,
                        layer, n_pages, t):
    b = pl.program_id(0)
    nbuf = SELECT_PAGE_BUFFERS
    n_blocks = n_pages // PAGES_PER_BLOCK

    def page_copy(p, slot):
        return pltpu.make_async_copy(kc_ref.at[layer, pt_ref[b, p]], pages.at[slot], sems.at[slot])

    for s in range(nbuf):
        page_copy(s, s).start()

    kmean_t[...] = jnp.zeros(kmean_t.shape, F32)
    blk_of_lane = lax.broadcasted_iota(jnp.int32, kmean_t.shape, 1)

    def body(n, carry):
        tot = jnp.zeros((GROUP, PAGE_SIZE), F32)
        for pp in range(PAGES_PER_BLOCK):
            p = n * PAGES_PER_BLOCK + pp
            slot = p % nbuf
            page_copy(p, slot).wait()
            tot = tot + pages[slot]

            @pl.when(p + nbuf < n_pages)
            def _():
                page_copy(p + nbuf, slot).start()
        mean = jnp.sum(tot, axis=1, keepdims=True) * (1.0 / MOBA_BLOCK)
        kmean_t[...] = jnp.where(blk_of_lane == n, mean, kmean_t[...])
        return carry

    lax.fori_loop(0, n_blocks, body, 0, unroll=SELECT_UNROLL)

    q8[0:t, :] = q_ref[0]
    q8[t:, :] = jnp.zeros((SUBLANES - t, GROUP), F32)
    blk_lane = lax.broadcasted_iota(jnp.int32, (SUBLANES, n_blocks), 1)
    out_lane = lax.broadcasted_iota(jnp.int32, (SUBLANES, LANES), 1)
    res = jnp.zeros((SUBLANES, LANES), jnp.int32)
    for h in range(HEADS):
        hs = slice(h * HEAD_DIM, (h + 1) * HEAD_DIM)
        gate = _dot_f32(q8[:, hs], kmean_t[hs, 0:n_blocks])
        for r in range(MOBA_TOPK):
            top = jnp.max(gate, axis=1, keepdims=True)
            first = jnp.min(jnp.where(gate == top, blk_lane, n_blocks), axis=1, keepdims=True)
            res = jnp.where(out_lane == h * MOBA_TOPK + r, first, res)
            gate = jnp.where(blk_lane == first, NEG_INF, gate)
    idx_ref[0] = res


def _moba_sample_attn_kernel(pt_ref, idx_ref, q_ref, kn_ref, vn_ref, kc_ref, vc_ref, o_ref,
                             kbuf, vbuf, st8, o8, ksem, vsem, *, layer, past, t):
    b = pl.program_id(0)
    n_seq = pl.num_programs(0)
    n_sel = MOBA_TOPK * MOBA_BLOCK
    pairs = [(tok, h) for h in range(HEADS) for tok in range(t)]

    def block_of(seq, tok, h, r):
        return idx_ref[seq, tok * (HEADS * MOBA_TOPK) + h * MOBA_TOPK + r]

    def copies(seq, i, half):
        tok, h = pairs[i]
        out = []
        for r in range(MOBA_TOPK):
            blk = block_of(seq, tok, h, r)
            for pp in range(PAGES_PER_BLOCK):
                phys = pt_ref[seq, blk * PAGES_PER_BLOCK + pp]
                lanes = pl.ds((r * PAGES_PER_BLOCK + pp) * PAGE_SIZE, PAGE_SIZE)
                out.append(pltpu.make_async_copy(kc_ref.at[layer, phys, h], kbuf.at[half, i, :, lanes],
                                                 ksem.at[half, i]))
                out.append(pltpu.make_async_copy(vc_ref.at[layer, phys, h], vbuf.at[half, i, :, lanes],
                                                 vsem.at[half, i]))
        return out

    def start_all(seq, half):
        for i in range(len(pairs)):
            for c in copies(seq, i, half):
                c.start()

    half = b % 2

    @pl.when(b == 0)
    def _():
        start_all(b, half)

    @pl.when(b + 1 < n_seq)
    def _():
        start_all(b + 1, 1 - half)

    for i, ref in enumerate((q_ref, kn_ref, vn_ref)):
        st8[i, 0:t, :] = ref[0]
        st8[i, t:, :] = jnp.zeros((SUBLANES - t, GROUP), F32)

    lane = lax.broadcasted_iota(jnp.int32, (1, n_sel), 1)
    row = lax.broadcasted_iota(jnp.int32, (SUBLANES, 1), 0)
    for i in range(len(pairs)):
        for c in copies(b, i, half):
            c.wait()
    for i, (tok, h) in enumerate(pairs):
        hs = slice(h * HEAD_DIM, (h + 1) * HEAD_DIM)
        slope = ALIBI_SLOPES[h]
        qrow = st8[0, tok:tok + 1, hs] * (HEAD_DIM ** -0.5)
        s_sel = _dot(jnp.broadcast_to(qrow, (SUBLANES, HEAD_DIM)), kbuf[half, i])[0:1]
        blk = jnp.where(lane < MOBA_BLOCK, block_of(b, tok, h, 0),
                        jnp.where(lane < 2 * MOBA_BLOCK, block_of(b, tok, h, 1), block_of(b, tok, h, 2)))
        pos = blk * MOBA_BLOCK + (lane % MOBA_BLOCK)
        s_sel = s_sel - slope * (past + tok - pos).astype(F32)
        s_own = jnp.sum(st8[1, :, hs] * qrow, axis=1, keepdims=True)
        s_own = jnp.where(row <= tok, s_own - slope * (tok - row).astype(F32), NEG_INF)
        m = jnp.maximum(jnp.max(s_sel, axis=1, keepdims=True), jnp.max(s_own, axis=0, keepdims=True))
        p_sel = jnp.exp(s_sel - m)
        p_own = jnp.exp(s_own - m)
        l = jnp.sum(p_sel, axis=1, keepdims=True) + jnp.sum(p_own, axis=0, keepdims=True)
        o = _dot_nt(jnp.broadcast_to(p_sel, (SUBLANES, n_sel)), vbuf[half, i])[0:1]
        o = o + jnp.sum(p_own * st8[2, :, hs], axis=0, keepdims=True)
        o8[tok:tok + 1, hs] = o / l
    o_ref[0] = o8[0:t, :]


def moba_sample(proj, k_cache, v_cache, page_table, layer):
    db, t, _ = proj.shape
    depth, pool = k_cache.shape[:2]
    n_pages = page_table.shape[1]
    n_blocks = n_pages // PAGES_PER_BLOCK
    past = n_pages * PAGE_SIZE
    assert past % MOBA_BLOCK == 0 and n_pages >= SELECT_PAGE_BUFFERS and t <= SUBLANES
    assert MOBA_TOPK <= n_blocks <= LANES
    tok = lambda k: pl.BlockSpec((1, t, GROUP), lambda i, *_: (i, 0, k))
    hbm = pl.BlockSpec(memory_space=pl.ANY)
    n_pairs = t * HEADS
    idx = pl.pallas_call(
        functools.partial(_moba_select_kernel, layer=layer, n_pages=n_pages, t=t),
        grid_spec=pltpu.PrefetchScalarGridSpec(
            num_scalar_prefetch=1, grid=(db,),
            in_specs=[tok(COL_MOBA), hbm],
            out_specs=pl.BlockSpec((1, SUBLANES, LANES), lambda i, *_: (i, 0, 0)),
            scratch_shapes=[pltpu.VMEM((SELECT_PAGE_BUFFERS, GROUP, PAGE_SIZE), F32),
                            pltpu.VMEM((GROUP, LANES), F32),
                            pltpu.VMEM((SUBLANES, GROUP), F32),
                            pltpu.SemaphoreType.DMA((SELECT_PAGE_BUFFERS,))]),
        out_shape=jax.ShapeDtypeStruct((db, SUBLANES, LANES), jnp.int32),
        compiler_params=_params("arbitrary"),
        name="moba_sample_select",
    )(page_table, proj, k_cache.reshape(depth, pool, GROUP, PAGE_SIZE))
    idx = idx[:, :t, :HEADS * MOBA_TOPK].reshape(db, t * HEADS * MOBA_TOPK)
    return pl.pallas_call(
        functools.partial(_moba_sample_attn_kernel, layer=layer, past=past, t=t),
        grid_spec=pltpu.PrefetchScalarGridSpec(
            num_scalar_prefetch=2, grid=(db,),
            in_specs=[tok(COL_MOBA), tok(COL_MOBA + 1), tok(COL_MOBA + 2), hbm, hbm],
            out_specs=pl.BlockSpec((1, t, GROUP), lambda i, *_: (i, 0, 0)),
            scratch_shapes=[pltpu.VMEM((2, n_pairs, HEAD_DIM, MOBA_TOPK * MOBA_BLOCK), F32),
                            pltpu.VMEM((2, n_pairs, HEAD_DIM, MOBA_TOPK * MOBA_BLOCK), F32),
                            pltpu.VMEM((3, SUBLANES, GROUP), F32),
                            pltpu.VMEM((SUBLANES, GROUP), F32),
                            pltpu.SemaphoreType.DMA((2, n_pairs)),
                            pltpu.SemaphoreType.DMA((2, n_pairs))]),
        out_shape=jax.ShapeDtypeStruct((db, t, GROUP), F32),
        compiler_params=_params("arbitrary"),
        name="moba_sample_attn",
    )(page_table, idx, proj, proj, proj, k_cache, v_cache)


def _rearrange_w_in(w):
    d = w.shape[0]
    dt0 = COL_HGRN * GROUP
    ba0 = dt0 + HEADS + (COL_GDN + 4 - COL_HGRN) * GROUP
    wide = jnp.concatenate([w[:, :dt0], w[:, dt0 + HEADS:ba0]], axis=1)
    narrow = jnp.concatenate([w[:, dt0:dt0 + HEADS], w[:, ba0:ba0 + 2 * HEADS],
                              jnp.zeros((d, LANES - N_SMALL), w.dtype)], axis=1)
    return jnp.concatenate([wide, narrow], axis=1)


def _layer(x, mem_k, mem_v, states, lp, layer, attend, prompt):
    b, L, d = x.shape
    ssm_conv0, ssm0, hgrn0, gdn_conv0, gdn0 = states
    w_in = lp['w_in'].astype(BF16) if prompt else lp['w_in']
    proj = norm_matmul(x.reshape(b * L, d), lp['n_mix_pre'], w_in, min(256, b * L)).reshape(b, L, -1)
    o_a = attend(proj)
    o_b, ssm_conv1, ssm1 = ssd_mixer(proj, ssm_conv0, ssm0, lp['ssm_conv_w'], lp['ssm_conv_b'],
                                     lp['ssm_dt_bias'], lp['ssm_a_log'], lp['ssm_d'], lp['ssm_norm'])
    hgrn = hgrn_mixer_long if L % (LIN_PREP_CHUNKS * LIN_CHUNK) == 0 else hgrn_mixer
    o_c, hgrn1 = hgrn(proj, jnp.swapaxes(hgrn0, -1, -2), lp['hgrn_lb_raw'], lp['hgrn_norm'], layer)
    gdn = gdn_mixer_long if L % (GDN_PREP_CHUNKS * LIN_CHUNK) == 0 else gdn_mixer
    o_d, gdn_conv1, gdn1 = gdn(proj, gdn_conv0, gdn0, lp['gdn_conv_w'], lp['gdn_a_log'],
                               lp['gdn_dt_bias'], lp['gdn_norm'])
    post = post_mixer if prompt else post_mixer_sample
    args = (x, (o_a, o_b, o_c, o_d), mem_k, mem_v, lp['norms'], lp['w_out'], lp['w_xq'], lp['w_xo'],
            lp['w_gu'], lp['w_down'])
    x = post(*args, 256) if prompt else post(*args)
    k = proj[..., GROUP:2 * GROUP].reshape(b, L, HEADS, HEAD_DIM)
    v = proj[..., 2 * GROUP:3 * GROUP].reshape(b, L, HEADS, HEAD_DIM)
    return x, (k, v, ssm_conv1, ssm1, jnp.swapaxes(hgrn1, -1, -2), gdn_conv1, gdn1)


def kernel(x_prompt, x_sample, mem_prompt, cache_moba_k, cache_moba_v, page_table, cache_mem_k, cache_mem_v, state_ssm_conv, state_ssm, state_hgrn, state_gdn_conv, state_gdn, n_mix_pre, n_mix_post, w_in, w_out, ssm_conv_w, ssm_conv_b, ssm_dt_bias, ssm_a_log, ssm_d, ssm_norm, hgrn_lb_raw, hgrn_norm, gdn_conv_w, gdn_a_log, gdn_dt_bias, gdn_norm, n_x_pre, n_x_post, mem_norm, w_xq, w_xkv, w_xo, n_f_pre, n_f_post, w_gu, w_down):
    depth = w_in.shape[0]
    bp, _, d = x_prompt.shape
    db = x_sample.shape[0]
    n_mem = mem_prompt.shape[1]
    kc = jnp.transpose(cache_moba_k, (0, 1, 3, 4, 2))
    vc = jnp.transpose(cache_moba_v, (0, 1, 3, 4, 2))
    zeros_p = (jnp.zeros((bp,) + state_ssm_conv.shape[2:], F32), jnp.zeros((bp,) + state_ssm.shape[2:], F32),
               jnp.zeros((bp,) + state_hgrn.shape[2:], F32), jnp.zeros((bp,) + state_gdn_conv.shape[2:], F32),
               jnp.zeros((bp,) + state_gdn.shape[2:], F32))
    yp, ys = x_prompt, x_sample
    outs_p, outs_s, mem_ks, mem_vs = [], [], [], []
    for l in range(depth):
        norms = jnp.zeros((SUBLANES, d), F32)
        for i, nrm in enumerate((n_mix_post, n_x_pre, n_x_post, n_f_pre, n_f_post)):
            norms = norms.at[i].set(nrm[l])
        lp = {'n_mix_pre': n_mix_pre[l], 'w_in': _rearrange_w_in(w_in[l]), 'norms': norms,
              'w_out': w_out[l].astype(BF16), 'w_xq': w_xq[l].astype(BF16), 'w_xo': w_xo[l].astype(BF16),
              'w_gu': w_gu[l].astype(BF16), 'w_down': w_down[l].astype(BF16),
              'ssm_conv_w': ssm_conv_w[l], 'ssm_conv_b': ssm_conv_b[l], 'ssm_dt_bias': ssm_dt_bias[l],
              'ssm_a_log': ssm_a_log[l], 'ssm_d': ssm_d[l], 'ssm_norm': ssm_norm[l],
              'hgrn_lb_raw': hgrn_lb_raw, 'hgrn_norm': hgrn_norm[l], 'gdn_conv_w': gdn_conv_w[l],
              'gdn_a_log': gdn_a_log[l], 'gdn_dt_bias': gdn_dt_bias[l], 'gdn_norm': gdn_norm[l]}
        mkv = norm_matmul(mem_prompt.reshape(bp * n_mem, d), mem_norm[l], w_xkv[l].astype(BF16), 256)
        mk = mkv[:, :d].reshape(bp, n_mem, d)
        mv = mkv[:, d:].reshape(bp, n_mem, d)
        yp, st_p = _layer(yp, mk.astype(BF16), mv.astype(BF16), zeros_p, lp, l, moba_prompt, True)
        outs_p.append(st_p)
        mem_ks.append(mk.reshape(bp, n_mem, X_HEADS, d // X_HEADS))
        mem_vs.append(mv.reshape(bp, n_mem, X_HEADS, d // X_HEADS))
        states_s = (state_ssm_conv[l], state_ssm[l], state_hgrn[l], state_gdn_conv[l], state_gdn[l])
        attend_s = functools.partial(moba_sample, k_cache=kc, v_cache=vc, page_table=page_table, layer=l)
        ys, st_s = _layer(ys, cache_mem_k[l].reshape(db, n_mem, d).astype(BF16),
                          cache_mem_v[l].reshape(db, n_mem, d).astype(BF16), states_s, lp, l, attend_s, False)
        outs_s.append(st_s)
    stack = lambda outs, i: jnp.stack([o[i] for o in outs], axis=0)
    return (yp, ys, stack(outs_p, 0), stack(outs_s, 0), stack(outs_p, 1), stack(outs_s, 1),
            jnp.stack(mem_ks, axis=0), jnp.stack(mem_vs, axis=0),
            stack(outs_p, 2), stack(outs_s, 2), stack(outs_p, 3), stack(outs_s, 3),
            stack(outs_p, 4), stack(outs_s, 4), stack(outs_p, 5), stack(outs_s, 5),
            stack(outs_p, 6), stack(outs_s, 6))
```

```python
import functools

import numpy as np
import jax
import jax.numpy as jnp
from jax import lax
from jax.experimental import pallas as pl
from jax.experimental.pallas import tpu as pltpu

F32 = jnp.float32
BF16 = jnp.bfloat16

LANES = 128
SUBLANES = 8
VMEM_LIMIT_BYTES = 56 * 1024 * 1024

GROUP = 256
HEADS = 4
HEAD_DIM = 64
CONV_W = 4
SSM_DSTATE = 128
SSM_CHUNK = 128
LIN_CHUNK = 64
MOBA_BLOCK = 256
MOBA_TOPK = 3
PAGE_SIZE = 128
X_HEADS = 4
N_SMALL = 12
COL_MOBA = 0
COL_SSM_Z = 3
COL_SSM_X = 4
COL_HGRN = 7
COL_GDN = 11
N_WIDE = 15 * GROUP
COL_SMALL = N_WIDE // LANES
EPS = 1e-6
NEG_INF = float("-inf")
LOG2_E = 1.4426950408889634
ALIBI_SLOPES = tuple(2.0 ** (-8.0 * (h + 1) / HEADS) for h in range(HEADS))


def _rms(x, w):
    return x * lax.rsqrt(jnp.mean(x * x, axis=-1, keepdims=True) + EPS) * w


def _sigmoid(x):
    return 1.0 / (1.0 + jnp.exp(-x))


def _silu(x):
    return x * _sigmoid(x)


def _softplus(x):
    return jnp.maximum(x, 0.0) + jnp.log(1.0 + jnp.exp(-jnp.abs(x)))


def _dot(a, b):
    return jnp.dot(a.astype(BF16), b.astype(BF16), preferred_element_type=F32)


def _dot_nt(a, b):
    return lax.dot_general(a.astype(BF16), b.astype(BF16), (((1,), (1,)), ((), ())),
                           preferred_element_type=F32)


def _dot_f32(a, b):
    return jnp.dot(a, b, preferred_element_type=F32, precision=lax.Precision.HIGHEST)


def _params(*sem, flags=None):
    return pltpu.CompilerParams(dimension_semantics=sem, vmem_limit_bytes=VMEM_LIMIT_BYTES, flags=flags)


def _resident(shape):
    return pl.BlockSpec(shape, lambda *_: (0,) * len(shape), pipeline_mode=pl.Buffered(1))


def _norm_matmul_kernel(x_ref, nw_ref, w_ref, o_ref, *, full_precision):
    xn = _rms(x_ref[...], nw_ref[...])
    o_ref[...] = _dot_f32(xn, w_ref[...]) if full_precision else _dot(xn, w_ref[...])


def norm_matmul(x, norm_w, w, tm):
    n, d = x.shape
    c = w.shape[1]
    return pl.pallas_call(
        functools.partial(_norm_matmul_kernel, full_precision=(w.dtype == F32)),
        grid=(n // tm,),
        in_specs=[pl.BlockSpec((tm, d), lambda i: (i, 0)),
                  _resident((1, d)),
                  _resident((d, c))],
        out_specs=pl.BlockSpec((tm, c), lambda i: (i, 0)),
        out_shape=jax.ShapeDtypeStruct((n, c), F32),
        compiler_params=_params("parallel"),
        name="norm_matmul",
    )(x, norm_w.reshape(1, d), w)


def _mix_out_and_query(x, mixed, nw, wout_ref, wxq_ref):
    x = x + _rms(_dot(mixed, wout_ref[...]), nw[0:1])
    return x, _dot(_rms(x, nw[1:2]), wxq_ref[...])


def _memory_attention(q, head_k, head_v):
    xdh = q.shape[1] // X_HEADS
    heads = []
    for h in range(X_HEADS):
        s = _dot_nt(q[:, h * xdh:(h + 1) * xdh], head_k(h)) * (xdh ** -0.5)
        p = jnp.exp(s - jnp.max(s, axis=-1, keepdims=True))
        heads.append(_dot(p, head_v(h)) / jnp.sum(p, axis=-1, keepdims=True))
    return jnp.concatenate(heads, axis=-1)


def _lane_heads(ref):
    xdh = ref.shape[-1] // X_HEADS
    return lambda h: ref[0, :, h * xdh:(h + 1) * xdh]


def _attn_out_and_ffn(x, att, nw, wxo_ref, wgu_ref, wdown_ref):
    x = x + _rms(_dot(att, wxo_ref[...]), nw[2:3])
    gu = _dot(_rms(x, nw[3:4]), wgu_ref[...])
    hid = gu.shape[1] // 2
    act = _silu(gu[:, :hid]) * gu[:, hid:]
    return x + _rms(_dot(act, wdown_ref[...]), nw[4:5])


def _post_mixer_kernel(x_ref, oa_ref, ob_ref, oc_ref, od_ref, mk_ref, mv_ref, norms_ref, wout_ref,
                       wxq_ref, wxo_ref, wgu_ref, wdown_ref, o_ref):
    nw = norms_ref[...]
    mixed = jnp.concatenate([oa_ref[0], ob_ref[0], oc_ref[0], od_ref[0]], axis=-1)
    x, q = _mix_out_and_query(x_ref[0], mixed, nw, wout_ref, wxq_ref)
    att = _memory_attention(q, _lane_heads(mk_ref), _lane_heads(mv_ref))
    o_ref[0] = _attn_out_and_ffn(x, att, nw, wxo_ref, wgu_ref, wdown_ref)


def post_mixer(x, mix_parts, mk, mv, norms, w_out, w_xq, w_xo, w_gu, w_down, tm):
    b, L, d = x.shape
    m = mk.shape[1]
    row = pl.BlockSpec((1, tm, d), lambda i, j: (i, j, 0))
    part = pl.BlockSpec((1, tm, GROUP), lambda i, j: (i, j, 0))
    mem = pl.BlockSpec((1, m, d), lambda i, j: (i, 0, 0))
    return pl.pallas_call(
        _post_mixer_kernel,
        grid=(b, L // tm),
        in_specs=[row, part, part, part, part, mem, mem, _resident(norms.shape),
                  _resident(w_out.shape), _resident(w_xq.shape), _resident(w_xo.shape),
                  _resident(w_gu.shape), _resident(w_down.shape)],
        out_specs=row,
        out_shape=jax.ShapeDtypeStruct((b, L, d), F32),
        compiler_params=_params("parallel", "parallel"),
        name="post_mixer",
    )(x, *mix_parts, mk, mv, norms, w_out, w_xq, w_xo, w_gu, w_down)


def _sample_pre_kernel(x_ref, oa_ref, ob_ref, oc_ref, od_ref, norms_ref, wout_ref, wxq_ref,
                       x1_ref, q_ref):
    mixed = jnp.concatenate([oa_ref[...], ob_ref[...], oc_ref[...], od_ref[...]], axis=-1)
    x1, q = _mix_out_and_query(x_ref[...], mixed, norms_ref[...], wout_ref, wxq_ref)
    x1_ref[...] = x1
    q_ref[...] = q


def _sample_attn_kernel(q_ref, mk_ref, mv_ref, o_ref, q8, *, t):
    q8[0:t, :] = q_ref[0]
    q8[t:, :] = jnp.zeros((q8.shape[0] - t, q8.shape[1]), F32)
    o_ref[0] = _memory_attention(q8[...], _lane_heads(mk_ref), _lane_heads(mv_ref))[0:t]


def _sample_post_kernel(x_ref, att_ref, norms_ref, wxo_ref, wgu_ref, wdown_ref, o_ref):
    o_ref[...] = _attn_out_and_ffn(x_ref[...], att_ref[...], norms_ref[...], wxo_ref, wgu_ref,
                                   wdown_ref)


def post_mixer_sample(x, mix_parts, mk, mv, norms, w_out, w_xq, w_xo, w_gu, w_down):
    b, t, d = x.shape
    n = b * t
    m = mk.shape[1]
    flat = lambda a: a.reshape(n, a.shape[-1])
    full = lambda shp: pl.BlockSpec(shp, lambda i: (0,) * len(shp))
    x1, q = pl.pallas_call(
        _sample_pre_kernel,
        grid=(1,),
        in_specs=[full((n, d))] + [full((n, GROUP))] * 4
                 + [full(norms.shape), _resident(w_out.shape), _resident(w_xq.shape)],
        out_specs=[full((n, d)), full((n, d))],
        out_shape=[jax.ShapeDtypeStruct((n, d), F32)] * 2,
        compiler_params=_params("arbitrary"),
        name="sample_pre",
    )(flat(x), *[flat(p) for p in mix_parts], norms, w_out, w_xq)
    seq = pl.BlockSpec((1, t, d), lambda i: (i, 0, 0))
    mem = pl.BlockSpec((1, m, d), lambda i: (i, 0, 0))
    att = pl.pallas_call(
        functools.partial(_sample_attn_kernel, t=t),
        grid=(b,),
        in_specs=[seq, mem, mem],
        out_specs=seq,
        out_shape=jax.ShapeDtypeStruct((b, t, d), F32),
        scratch_shapes=[pltpu.VMEM((SUBLANES, d), F32)],
        compiler_params=_params("parallel"),
        name="sample_attn",
    )(q.reshape(b, t, d), mk, mv)
    out = pl.pallas_call(
        _sample_post_kernel,
        grid=(1,),
        in_specs=[full((n, d)), full((n, d)), full(norms.shape), _resident(w_xo.shape),
                  _resident(w_gu.shape), _resident(w_down.shape)],
        out_specs=full((n, d)),
        out_shape=jax.ShapeDtypeStruct((n, d), F32),
        compiler_params=_params("arbitrary"),
        name="sample_post",
    )(x1, flat(att), norms, w_xo, w_gu, w_down)
    return out.reshape(b, t, d)


def _tril(n, strict=False):
    r = lax.broadcasted_iota(jnp.int32, (n, n), 0)
    c = lax.broadcasted_iota(jnp.int32, (n, n), 1)
    return (r > c) if strict else (r >= c)


def _head_expand(n_rows, n_cols, width, offset=0):
    r = lax.broadcasted_iota(jnp.int32, (n_rows, n_cols), 0)
    c = lax.broadcasted_iota(jnp.int32, (n_rows, n_cols), 1)
    return (r == (c // width) + offset).astype(F32)


def _stage_rows(dst, src_ref, l_blk, row0=0):
    n = dst.shape[0]
    dst[row0:row0 + l_blk, :] = src_ref[0]
    if row0 + l_blk < n:
        dst[row0 + l_blk:n, :] = jnp.zeros((n - row0 - l_blk, dst.shape[1]), F32)


def _causal_conv(buf, cw_ref, cs, base=0):
    out = cw_ref[0:1, :] * buf[base + 5:base + 5 + cs, :]
    for j in range(1, CONV_W):
        out = out + cw_ref[j:j + 1, :] * buf[base + 5 + j:base + 5 + j + cs, :]
    return out


def _ssd_kernel(z_ref, x_ref, b_ref, c_ref, sm_ref, conv0_ref, h0_ref, cw_ref, cb_ref, vec_ref,
                o_ref, conv1_ref, h1_ref, buf, smbuf, hst, *, cs, l_blk, n_chunks):
    c_idx = pl.program_id(1)

    @pl.when(c_idx == 0)
    def _():
        buf[5:8, :] = conv0_ref[0]
        hst[...] = h0_ref[0]

    _stage_rows(buf.at[:, 0:GROUP], x_ref, l_blk, 8)
    _stage_rows(buf.at[:, GROUP:2 * GROUP], b_ref, l_blk, 8)
    _stage_rows(buf.at[:, 2 * GROUP:3 * GROUP], c_ref, l_blk, 8)
    _stage_rows(smbuf, sm_ref, l_blk)

    xbc = _silu(_causal_conv(buf, cw_ref, cs) + cb_ref[...])
    conv_tail = buf[5 + l_blk:8 + l_blk, :]
    buf[5:8, :] = conv_tail
    xs = xbc[:, 0:GROUP]
    bm = xbc[:, GROUP:2 * GROUP]
    cm = xbc[:, 2 * GROUP:3 * GROUP]

    dt = _softplus(smbuf[...] + vec_ref[0:1, 0:LANES])
    if l_blk < cs:
        rows = lax.broadcasted_iota(jnp.int32, (cs, LANES), 0)
        dt = jnp.where(rows < l_blk, dt, 0.0)
    a = dt * (-jnp.exp(vec_ref[1:2, 0:LANES]))
    acum = _dot_f32(_tril(cs).astype(F32), a)
    expand = _head_expand(LANES, GROUP, HEAD_DIM)
    dt_w = _dot_f32(dt, expand)
    acum_w = _dot_f32(acum, expand)
    a_last = acum_w[cs - 1:cs, :]
    xdt = xs * dt_w
    e_acum = jnp.exp(acum_w)
    xw = xdt * jnp.exp(a_last - acum_w)
    e_last = jnp.exp(a_last)
    acum_t = acum.T
    causal = _tril(cs)

    ys = []
    for h in range(HEADS):
        g = h // (HEADS // 2)
        hs = slice(h * HEAD_DIM, (h + 1) * HEAD_DIM)
        gs = slice(g * SSM_DSTATE, (g + 1) * SSM_DSTATE)
        st = hst[h]
        dec = jnp.exp(jnp.where(causal, acum[:, h:h + 1] - acum_t[h:h + 1, :], NEG_INF))
        y = _dot(_dot_nt(cm[:, gs], bm[:, gs]) * dec, xdt[:, hs])
        y = y + _dot_nt(cm[:, gs], st) * e_acum[:, hs]
        ys.append(y)
        hst[h] = e_last[:, h * HEAD_DIM:h * HEAD_DIM + 1] * st + _dot(xw[:, hs].T, bm[:, gs])
    y = jnp.concatenate(ys, axis=-1) + vec_ref[2:3, :] * xs
    o_ref[0] = _rms(y[0:l_blk] * _silu(z_ref[0]), vec_ref[3:4, :])

    @pl.when(c_idx == n_chunks - 1)
    def _():
        conv1_ref[0] = conv_tail
        h1_ref[0] = hst[...]


def ssd_mixer(proj, conv0, h0, conv_w, conv_b, dt_bias, a_log, d_skip, norm_w):
    b, L, _ = proj.shape
    cs = SSM_CHUNK
    l_blk = min(cs, L)
    n_chunks = L // l_blk
    vec = jnp.zeros((8, GROUP), F32)
    vec = vec.at[0, :HEADS].set(dt_bias).at[1, :HEADS].set(a_log)
    vec = vec.at[2].set(jnp.repeat(d_skip, HEAD_DIM)).at[3].set(norm_w)
    col = lambda k: pl.BlockSpec((1, l_blk, GROUP), lambda i, j, k=k: (i, j, k))
    per_b = lambda shp: pl.BlockSpec((1,) + shp, lambda i, j: (i,) + (0,) * len(shp))
    c3 = 3 * GROUP
    return pl.pallas_call(
        functools.partial(_ssd_kernel, cs=cs, l_blk=l_blk, n_chunks=n_chunks),
        grid=(b, n_chunks),
        in_specs=[col(COL_SSM_Z), col(COL_SSM_X), col(COL_SSM_X + 1), col(COL_SSM_X + 2),
                  pl.BlockSpec((1, l_blk, LANES), lambda i, j: (i, j, COL_SMALL)),
                  per_b((CONV_W - 1, c3)), per_b((HEADS, HEAD_DIM, SSM_DSTATE)),
                  _resident((CONV_W, c3)), _resident((1, c3)), _resident((8, GROUP))],
        out_specs=[pl.BlockSpec((1, l_blk, GROUP), lambda i, j: (i, j, 0)),
                   per_b((CONV_W - 1, c3)), per_b((HEADS, HEAD_DIM, SSM_DSTATE))],
        out_shape=[jax.ShapeDtypeStruct((b, L, GROUP), F32),
                   jax.ShapeDtypeStruct((b, CONV_W - 1, c3), F32),
                   jax.ShapeDtypeStruct((b, HEADS, HEAD_DIM, SSM_DSTATE), F32)],
        scratch_shapes=[pltpu.VMEM((cs + 8, c3), F32), pltpu.VMEM((cs, LANES), F32),
                        pltpu.VMEM((HEADS, HEAD_DIM, SSM_DSTATE), F32)],
        compiler_params=_params("parallel", "arbitrary"),
        name="ssd_mixer",
    )(proj, proj, proj, proj, proj, conv0, h0, conv_w, conv_b.reshape(1, c3), vec)


def _dot_tn(a, b):
    return lax.dot_general(a.astype(BF16), b.astype(BF16), (((0,), (0,)), ((), ())),
                           preferred_element_type=F32)


def _head_blocks(n, width):
    r = lax.broadcasted_iota(jnp.int32, (n, n), 0)
    c = lax.broadcasted_iota(jnp.int32, (n, n), 1)
    return ((r // width) == (c // width)).astype(F32)


def _dot_split(x, m01, passes, m01_left=False):
    m = m01.astype(BF16)
    acc = None
    for _ in range(passes):
        hi = x.astype(BF16)
        part = jnp.dot(m, hi, preferred_element_type=F32) if m01_left else jnp.dot(hi, m, preferred_element_type=F32)
        acc = part if acc is None else acc + part
        x = x - hi.astype(F32)
    return acc


def _head_rms_gate(o, gate, nw_row, l_blk):
    ms = _dot_split(o * o, _head_blocks(GROUP, HEAD_DIM), 2) * (1.0 / HEAD_DIM)
    return (o * lax.rsqrt(ms + EPS) * nw_row)[0:l_blk] * _silu(gate)


def _hgrn_kernel(q_ref, f_ref, i_ref, g_ref, s0_ref, lbraw_ref, nw_ref, o_ref, s1_ref,
                 stage, sst, *, cs, l_blk, n_chunks, layer):
    c_idx = pl.program_id(1)

    @pl.when(c_idx == 0)
    def _():
        sst[...] = s0_ref[0]

    if l_blk < cs:
        _stage_rows(stage.at[0], q_ref, l_blk)
        _stage_rows(stage.at[1], f_ref, l_blk)
        _stage_rows(stage.at[2], i_ref, l_blk)
        q, fx, v = stage[0], stage[1], stage[2]
    else:
        q, fx, v = q_ref[0], f_ref[0], i_ref[0]

    raw = lbraw_ref[...]
    e = jnp.exp(raw - jnp.max(raw, axis=0, keepdims=True))
    sm = e / jnp.sum(e, axis=0, keepdims=True)
    lb = jnp.zeros((1, GROUP), F32)
    for i in range(1, layer + 1):
        lb = lb + sm[i:i + 1, :]

    log_sig = jnp.minimum(fx, 0.0) - jnp.log1p(jnp.exp(-jnp.abs(fx)))
    la = jnp.log(lb)
    lbb = jnp.log1p(-lb) + log_sig
    log_f = jnp.maximum(la, lbb) + jnp.log1p(jnp.exp(-jnp.abs(la - lbb)))
    k = (1.0 - lb) * _sigmoid(-fx)
    if l_blk < cs:
        rows = lax.broadcasted_iota(jnp.int32, (cs, GROUP), 0)
        log_f = jnp.where(rows < l_blk, log_f, 0.0)
        k = jnp.where(rows < l_blk, k, 0.0)

    bcum = _dot_f32(_tril(cs).astype(F32), log_f)
    b_mid = bcum[cs // 2 - 1:cs // 2, :]
    b_last = bcum[cs - 1:cs, :]
    qe = q * jnp.exp(bcum - b_mid)
    ke = k * jnp.exp(b_mid - bcum)
    qs = q * jnp.exp(bcum)
    kd = k * jnp.exp(b_last - bcum)
    e_last = jnp.exp(b_last)
    causal = _tril(cs)

    os_ = []
    for h in range(HEADS):
        hs = slice(h * HEAD_DIM, (h + 1) * HEAD_DIM)
        st_t = sst[h]
        att = jnp.where(causal, _dot_nt(qe[:, hs], ke[:, hs]), 0.0)
        os_.append(_dot(att, v[:, hs]) + _dot_nt(qs[:, hs], st_t))
        sst[h] = e_last[:, hs] * st_t + _dot_tn(v[:, hs], kd[:, hs])
    o = jnp.concatenate(os_, axis=-1)
    o_ref[0] = _head_rms_gate(o, g_ref[0], nw_ref[...], l_blk)

    @pl.when(c_idx == n_chunks - 1)
    def _():
        s1_ref[0] = sst[...]


def hgrn_mixer(proj, s0, lb_raw, norm_w, layer):
    b, L, _ = proj.shape
    cs = LIN_CHUNK
    l_blk = min(cs, L)
    n_chunks = L // l_blk
    col = lambda k: pl.BlockSpec((1, l_blk, GROUP), lambda i, j, k=k: (i, j, k))
    st = pl.BlockSpec((1, HEADS, HEAD_DIM, HEAD_DIM), lambda i, j: (i, 0, 0, 0))
    return pl.pallas_call(
        functools.partial(_hgrn_kernel, cs=cs, l_blk=l_blk, n_chunks=n_chunks, layer=layer),
        grid=(b, n_chunks),
        in_specs=[col(COL_HGRN), col(COL_HGRN + 1), col(COL_HGRN + 2), col(COL_HGRN + 3), st,
                  _resident(lb_raw.shape), _resident((1, GROUP))],
        out_specs=[pl.BlockSpec((1, l_blk, GROUP), lambda i, j: (i, j, 0)), st],
        out_shape=[jax.ShapeDtypeStruct((b, L, GROUP), F32),
                   jax.ShapeDtypeStruct((b, HEADS, HEAD_DIM, HEAD_DIM), F32)],
        scratch_shapes=[pltpu.VMEM((3, cs, GROUP), F32),
                        pltpu.VMEM((HEADS, HEAD_DIM, HEAD_DIM), F32)],
        compiler_params=_params("parallel", "arbitrary"),
        name="hgrn_mixer",
    )(proj, proj, proj, proj, s0, lb_raw, jnp.tile(norm_w, HEADS).reshape(1, GROUP))


LIN_PREP_CHUNKS = 4


def _chunk_rows(x, cs, cps, row):
    return jnp.concatenate(
        [jnp.broadcast_to(x[i * cs + row:i * cs + row + 1, :], (cs, x.shape[1])) for i in range(cps)], axis=0)


def _hgrn_prep_kernel(q_ref, f_ref, i_ref, lbraw_ref, oi_ref, qs_ref, kd_ref, aux_ref, *, cs, cps, layer):
    rows = cs * cps
    q, fx, v = q_ref[0], f_ref[0], i_ref[0]
    raw = lbraw_ref[...]
    e = jnp.exp(raw - jnp.max(raw, axis=0, keepdims=True))
    sm = e / jnp.sum(e, axis=0, keepdims=True)
    lb = jnp.zeros((1, GROUP), F32)
    for i in range(1, layer + 1):
        lb = lb + sm[i:i + 1, :]
    log_sig = jnp.minimum(fx, 0.0) - jnp.log1p(jnp.exp(-jnp.abs(fx)))
    la = jnp.log(lb)
    lbb = jnp.log1p(-lb) + log_sig
    log_f = jnp.maximum(la, lbb) + jnp.log1p(jnp.exp(-jnp.abs(la - lbb)))
    k = (1.0 - lb) * _sigmoid(-fx)

    r = lax.broadcasted_iota(jnp.int32, (rows, rows), 0)
    c = lax.broadcasted_iota(jnp.int32, (rows, rows), 1)
    incl = jnp.logical_and(r >= c, (r // cs) == (c // cs))
    bcum = _dot_split(log_f, incl.astype(F32), 3, m01_left=True)
    b_mid = _chunk_rows(bcum, cs, cps, cs // 2 - 1)
    b_last = _chunk_rows(bcum, cs, cps, cs - 1)
    qe = q * jnp.exp(bcum - b_mid)
    ke = k * jnp.exp(b_mid - bcum)
    qs_ref[0] = q * jnp.exp(bcum)
    kd_ref[0] = k * jnp.exp(b_last - bcum)
    for i in range(cps):
        aux_ref[0, i * SUBLANES:(i + 1) * SUBLANES, :] = jnp.exp(b_last[i * cs:i * cs + SUBLANES, :])
    os_ = []
    for h in range(HEADS):
        hs = slice(h * HEAD_DIM, (h + 1) * HEAD_DIM)
        att = jnp.where(incl, _dot_nt(qe[:, hs], ke[:, hs]), 0.0)
        os_.append(_dot(att, v[:, hs]))
    oi_ref[0] = jnp.concatenate(os_, axis=-1)


def _hgrn_scan_kernel(oi_ref, qs_ref, kd_ref, aux_ref, v_ref, g_ref, s0_ref, nw_ref, o_ref, s1_ref, sst,
                      *, nb, n_chunks):
    c_idx = pl.program_id(1)
    blocks = _head_blocks(GROUP, HEAD_DIM)

    @pl.when(c_idx == 0)
    def _():
        sst[...] = jnp.zeros(sst.shape, F32)
        for bi in range(nb):
            for h in range(HEADS):
                hs = slice(h * HEAD_DIM, (h + 1) * HEAD_DIM)
                sst[bi, hs, hs] = s0_ref[bi, h]

    for bi in range(nb):
        st_t = sst[bi]
        o = oi_ref[bi] + _dot_nt(qs_ref[bi], st_t)
        sst[bi] = aux_ref[bi, 0:1, :] * st_t + _dot_tn(v_ref[bi], kd_ref[bi]) * blocks
        o_ref[bi] = _head_rms_gate(o, g_ref[bi], nw_ref[...], o.shape[0])

    @pl.when(c_idx == n_chunks - 1)
    def _():
        for bi in range(nb):
            for h in range(HEADS):
                hs = slice(h * HEAD_DIM, (h + 1) * HEAD_DIM)
                s1_ref[bi, h] = sst[bi, hs, hs]


def hgrn_mixer_long(proj, s0, lb_raw, norm_w, layer):
    b, L, _ = proj.shape
    cs = LIN_CHUNK
    cps = LIN_PREP_CHUNKS
    rows = cs * cps
    n_chunks = L // cs
    nb = 4 if b % 4 == 0 else 1
    col = lambda k: pl.BlockSpec((1, rows, GROUP), lambda i, j, k=k: (i, j, k))
    out = pl.BlockSpec((1, rows, GROUP), lambda i, j: (i, j, 0))
    wide = jax.ShapeDtypeStruct((b, L, GROUP), F32)
    oi, qs, kd, aux = pl.pallas_call(
        functools.partial(_hgrn_prep_kernel, cs=cs, cps=cps, layer=layer),
        grid=(b, L // rows),
        in_specs=[col(COL_HGRN), col(COL_HGRN + 1), col(COL_HGRN + 2), _resident(lb_raw.shape)],
        out_specs=[out, out, out, pl.BlockSpec((1, cps * SUBLANES, GROUP), lambda i, j: (i, j, 0))],
        out_shape=[wide, wide, wide, jax.ShapeDtypeStruct((b, n_chunks * SUBLANES, GROUP), F32)],
        compiler_params=_params("parallel", "parallel"),
        name="hgrn_prep",
    )(proj, proj, proj, lb_raw)
    blk = pl.BlockSpec((nb, cs, GROUP), lambda i, c: (i, c, 0))
    pcol = lambda k: pl.BlockSpec((nb, cs, GROUP), lambda i, c, k=k: (i, c, k))
    state = pl.BlockSpec((nb, HEADS, HEAD_DIM, HEAD_DIM), lambda i, c: (i, 0, 0, 0))
    return pl.pallas_call(
        functools.partial(_hgrn_scan_kernel, nb=nb, n_chunks=n_chunks),
        grid=(b // nb, n_chunks),
        in_specs=[blk, blk, blk, pl.BlockSpec((nb, SUBLANES, GROUP), lambda i, c: (i, c, 0)),
                  pcol(COL_HGRN + 2), pcol(COL_HGRN + 3), state, _resident((1, GROUP))],
        out_specs=[blk, state],
        out_shape=[wide, jax.ShapeDtypeStruct((b, HEADS, HEAD_DIM, HEAD_DIM), F32)],
        scratch_shapes=[pltpu.VMEM((nb, GROUP, GROUP), F32)],
        compiler_params=_params("parallel", "arbitrary"),
        name="hgrn_scan",
    )(oi, qs, kd, aux, proj, proj, s0, jnp.tile(norm_w, HEADS).reshape(1, GROUP))


def _doubling_level_masks(n, block):
    r = np.arange(n)[:, None]
    c = np.arange(n)[None, :]
    out, s = [], 1
    while s < block:
        out.append((r // (2 * s) == c // (2 * s)) & ((r // s) % 2 == 1) & ((c // s) % 2 == 0))
        s *= 2
    return np.stack(out).astype(np.float32)


def _unit_lower_inverse_minus_eye(a, block=None, level_masks_ref=None):
    n = a.shape[0]
    block = n if block is None else block
    r = lax.broadcasted_iota(jnp.int32, (n, n), 0)
    c = lax.broadcasted_iota(jnp.int32, (n, n), 1)
    dx = None
    s, level = 1, 0
    while s < block:
        if level_masks_ref is None:
            lower_left = jnp.logical_and((r // (2 * s)) == (c // (2 * s)),
                                         jnp.logical_and((r // s) % 2 == 1, (c // s) % 2 == 0))
            b = jnp.where(lower_left, a, 0.0)
        else:
            b = a * level_masks_ref[level]
        if dx is None:
            dx = -b
        else:
            m = b + _dot(dx, b)
            dx = dx - m - _dot(m, dx)
        s *= 2
        level += 1
    return dx


def _gdn_kernel(q_ref, k_ref, v_ref, z_ref, sm_ref, conv0_ref, s0_ref, cw_ref, vec_ref, nw_ref,
                o_ref, conv1_ref, s1_ref, buf, smbuf, sst, *, cs, l_blk, n_chunks):
    c_idx = pl.program_id(1)

    @pl.when(c_idx == 0)
    def _():
        buf[5:8, :] = conv0_ref[0]
        sst[...] = s0_ref[0]

    _stage_rows(buf.at[:, 0:GROUP], q_ref, l_blk, 8)
    _stage_rows(buf.at[:, GROUP:2 * GROUP], k_ref, l_blk, 8)
    _stage_rows(buf.at[:, 2 * GROUP:3 * GROUP], v_ref, l_blk, 8)
    _stage_rows(smbuf, sm_ref, l_blk)

    qkv = _silu(_causal_conv(buf, cw_ref, cs))
    conv_tail = buf[5 + l_blk:8 + l_blk, :]
    buf[5:8, :] = conv_tail
    q = qkv[:, 0:GROUP]
    k = qkv[:, GROUP:2 * GROUP]
    v = qkv[:, 2 * GROUP:3 * GROUP]
    blocks = _head_blocks(GROUP, HEAD_DIM)
    q = q * lax.rsqrt(_dot_f32(q * q, blocks) + EPS) * (HEAD_DIM ** -0.5)
    k = k * lax.rsqrt(_dot_f32(k * k, blocks) + EPS)

    sm = smbuf[...]
    beta = _sigmoid(sm)
    log_g = -jnp.exp(vec_ref[1:2, :]) * _softplus(sm + vec_ref[0:1, :])
    if l_blk < cs:
        rows = lax.broadcasted_iota(jnp.int32, (cs, LANES), 0)
        beta = jnp.where(rows < l_blk, beta, 0.0)
        log_g = jnp.where(rows < l_blk, log_g, 0.0)
    gam = _dot_f32(_tril(cs).astype(F32), log_g)
    beta_w = _dot_f32(beta, _head_expand(LANES, GROUP, HEAD_DIM, HEADS))
    gam_w = _dot_f32(gam, _head_expand(LANES, GROUP, HEAD_DIM, 2 * HEADS))
    g_last = gam_w[cs - 1:cs, :]
    e_gam = jnp.exp(gam_w)
    kb = k * beta_w
    vb = v * beta_w
    kbg = kb * e_gam
    qg = q * e_gam
    k_dec = k * jnp.exp(g_last - gam_w)
    e_last = jnp.exp(g_last)
    gam_t = gam.T
    incl = _tril(cs)
    strict = _tril(cs, strict=True)

    os_ = []
    for h in range(HEADS):
        hs = slice(h * HEAD_DIM, (h + 1) * HEAD_DIM)
        r = 2 * HEADS + h
        st = sst[h]
        dec_incl = jnp.exp(jnp.where(incl, gam[:, r:r + 1] - gam_t[r:r + 1, :], NEG_INF))
        a_mat = jnp.where(strict, _dot_nt(kb[:, hs], k[:, hs]) * dec_incl, 0.0)
        tx = _unit_lower_inverse_minus_eye(a_mat)
        u = vb[:, hs] + _dot(tx, vb[:, hs])
        w = kbg[:, hs] + _dot(tx, kbg[:, hs])
        v_new = u - _dot(w, st)
        qk = _dot_nt(q[:, hs], k[:, hs]) * dec_incl
        os_.append(_dot(qk, v_new) + _dot(qg[:, hs], st))
        sst[h] = e_last[:, h * HEAD_DIM:h * HEAD_DIM + 1] * st + _dot_tn(k_dec[:, hs], v_new)
    o = jnp.concatenate(os_, axis=-1)
    o_ref[0] = _head_rms_gate(o, z_ref[0], nw_ref[...], l_blk)

    @pl.when(c_idx == n_chunks - 1)
    def _():
        conv1_ref[0] = conv_tail
        s1_ref[0] = sst[...]


def gdn_mixer(proj, conv0, s0, conv_w, a_log, dt_bias, norm_w):
    b, L, _ = proj.shape
    cs = LIN_CHUNK
    l_blk = min(cs, L)
    n_chunks = L // l_blk
    vec = jnp.zeros((8, LANES), F32)
    vec = vec.at[0, 2 * HEADS:3 * HEADS].set(dt_bias).at[1, 2 * HEADS:3 * HEADS].set(a_log)
    col = lambda k: pl.BlockSpec((1, l_blk, GROUP), lambda i, j, k=k: (i, j, k))
    per_b = lambda shp: pl.BlockSpec((1,) + shp, lambda i, j: (i,) + (0,) * len(shp))
    c3 = 3 * GROUP
    return pl.pallas_call(
        functools.partial(_gdn_kernel, cs=cs, l_blk=l_blk, n_chunks=n_chunks),
        grid=(b, n_chunks),
        in_specs=[col(COL_GDN), col(COL_GDN + 1), col(COL_GDN + 2), col(COL_GDN + 3),
                  pl.BlockSpec((1, l_blk, LANES), lambda i, j: (i, j, COL_SMALL)),
                  per_b((CONV_W - 1, c3)), per_b((HEADS, HEAD_DIM, HEAD_DIM)),
                  _resident((CONV_W, c3)), _resident((8, LANES)), _resident((1, GROUP))],
        out_specs=[pl.BlockSpec((1, l_blk, GROUP), lambda i, j: (i, j, 0)),
                   per_b((CONV_W - 1, c3)), per_b((HEADS, HEAD_DIM, HEAD_DIM))],
        out_shape=[jax.ShapeDtypeStruct((b, L, GROUP), F32),
                   jax.ShapeDtypeStruct((b, CONV_W - 1, c3), F32),
                   jax.ShapeDtypeStruct((b, HEADS, HEAD_DIM, HEAD_DIM), F32)],
        scratch_shapes=[pltpu.VMEM((cs + 8, c3), F32), pltpu.VMEM((cs, LANES), F32),
                        pltpu.VMEM((HEADS, HEAD_DIM, HEAD_DIM), F32)],
        compiler_params=_params("parallel", "arbitrary"),
        name="gdn_mixer",
    )(proj, proj, proj, proj, proj, conv0, s0, conv_w, vec,
      jnp.tile(norm_w, HEADS).reshape(1, GROUP))


GDN_PREP_CHUNKS = 4


def _gdn_prep_kernel(q_ref, k_ref, v_ref, pq_ref, pk_ref, pv_ref, sm_ref, conv0_ref, cw_ref, vec_ref,
                     lvl_ref, u_ref, w_ref, qg_ref, kd_ref, qk_ref, aux_ref, buf, *, cs, cps, l_valid):
    j = pl.program_id(1)
    rows = cs * cps

    if l_valid is None:
        @pl.when(j == 0)
        def _():
            buf[5:8, :] = conv0_ref[0]

        @pl.when(j > 0)
        def _():
            for i, ref in enumerate((pq_ref, pk_ref, pv_ref)):
                buf[5:8, i * GROUP:(i + 1) * GROUP] = ref[0, SUBLANES - 3:SUBLANES, :]

        for i, ref in enumerate((q_ref, k_ref, v_ref)):
            buf[8:8 + rows, i * GROUP:(i + 1) * GROUP] = ref[0]
        qkv = _silu(_causal_conv(buf, cw_ref, rows))
    else:
        stride = cs + SUBLANES
        parts = []
        for n in range(cps):
            buf[n * stride + 5:n * stride + 8, :] = conv0_ref[0, n * (CONV_W - 1):(n + 1) * (CONV_W - 1), :]
            for i, ref in enumerate((q_ref, k_ref, v_ref)):
                buf[n * stride + 8:n * stride + 8 + cs, i * GROUP:(i + 1) * GROUP] = ref[0, n * cs:(n + 1) * cs, :]
            parts.append(_causal_conv(buf, cw_ref, cs, n * stride))
        qkv = _silu(jnp.concatenate(parts, axis=0))
    q = qkv[:, 0:GROUP]
    k = qkv[:, GROUP:2 * GROUP]
    v = qkv[:, 2 * GROUP:3 * GROUP]
    blocks = _head_blocks(GROUP, HEAD_DIM)
    q = q * lax.rsqrt(_dot_split(q * q, blocks, 2) + EPS) * (HEAD_DIM ** -0.5)
    k = k * lax.rsqrt(_dot_split(k * k, blocks, 2) + EPS)

    sm = sm_ref[0]
    beta = _sigmoid(sm)
    log_g = -jnp.exp(vec_ref[1:2, :]) * _softplus(sm + vec_ref[0:1, :])
    if l_valid is not None:
        valid = lax.broadcasted_iota(jnp.int32, (rows, LANES), 0) % cs < l_valid
        beta = jnp.where(valid, beta, 0.0)
        log_g = jnp.where(valid, log_g, 0.0)
    r = lax.broadcasted_iota(jnp.int32, (rows, rows), 0)
    c = lax.broadcasted_iota(jnp.int32, (rows, rows), 1)
    chunk_tril = jnp.logical_and(r >= c, (r // cs) == (c // cs)).astype(F32)
    gam = _dot_split(log_g, chunk_tril, 3, m01_left=True)
    beta_w = _dot_split(beta, _head_expand(LANES, GROUP, HEAD_DIM, HEADS), 3)
    gam_w = _dot_split(gam, _head_expand(LANES, GROUP, HEAD_DIM, 2 * HEADS), 3)
    g_last = jnp.concatenate(
        [jnp.broadcast_to(gam_w[(i + 1) * cs - 1:(i + 1) * cs, :], (cs, GROUP)) for i in range(cps)], axis=0)
    e_gam = jnp.exp(gam_w)
    kb = k * beta_w
    vb = v * beta_w
    kbg = kb * e_gam
    qg_ref[0] = q * e_gam
    kd_ref[0] = k * jnp.exp(g_last - gam_w)
    for i in range(cps):
        aux_ref[0, i * SUBLANES:(i + 1) * SUBLANES, :] = jnp.exp(g_last[i * cs:i * cs + SUBLANES, :])
    gam_t = gam.T
    incl = chunk_tril > 0.0
    strict = jnp.logical_and(incl, r > c)

    us, ws, qks = [], [], []
    for h in range(HEADS):
        hs = slice(h * HEAD_DIM, (h + 1) * HEAD_DIM)
        lane = 2 * HEADS + h
        dec_incl = jnp.exp(jnp.where(incl, gam[:, lane:lane + 1] - gam_t[lane:lane + 1, :], NEG_INF))
        a_mat = jnp.where(strict, _dot_nt(kb[:, hs], k[:, hs]) * dec_incl, 0.0)
        dx = _unit_lower_inverse_minus_eye(a_mat, cs, lvl_ref)
        uw = _dot(dx, jnp.concatenate([vb[:, hs], kbg[:, hs]], axis=-1))
        us.append(vb[:, hs] + uw[:, 0:HEAD_DIM])
        ws.append(kbg[:, hs] + uw[:, HEAD_DIM:2 * HEAD_DIM])
        qk = _dot_nt(q[:, hs], k[:, hs]) * dec_incl
        qks.append(jnp.concatenate([qk[i * cs:(i + 1) * cs, i * cs:(i + 1) * cs] for i in range(cps)], axis=0))
    u_ref[0] = jnp.concatenate(us, axis=-1)
    w_ref[0] = jnp.concatenate(ws, axis=-1)
    qk_ref[0] = jnp.concatenate(qks, axis=-1)


def _gdn_scan_kernel(u_ref, w_ref, qg_ref, kd_ref, qk_ref, aux_ref, z_ref, s0_ref, nw_ref,
                     o_ref, s1_ref, sst, *, nb, n_chunks):
    c_idx = pl.program_id(1)
    blocks = _head_blocks(GROUP, HEAD_DIM)

    @pl.when(c_idx == 0)
    def _():
        sst[...] = jnp.zeros(sst.shape, F32)
        for bi in range(nb):
            for h in range(HEADS):
                hs = slice(h * HEAD_DIM, (h + 1) * HEAD_DIM)
                sst[bi, hs, hs] = s0_ref[bi, h]

    for bi in range(nb):
        st = sst[bi]
        v_new = u_ref[bi] - _dot(w_ref[bi], st)
        v_bd = jnp.concatenate([v_new] * HEADS, axis=0) * blocks
        o = _dot(qk_ref[bi], v_bd) + _dot(qg_ref[bi], st)
        sst[bi] = aux_ref[bi, 0:1, :] * st + _dot_tn(kd_ref[bi], v_new) * blocks
        o_ref[bi] = _head_rms_gate(o, z_ref[bi], nw_ref[...], o.shape[0])

    @pl.when(c_idx == n_chunks - 1)
    def _():
        for bi in range(nb):
            for h in range(HEADS):
                hs = slice(h * HEAD_DIM, (h + 1) * HEAD_DIM)
                s1_ref[bi, h] = sst[bi, hs, hs]


def _gdn_prep_call(proj, conv0, conv_w, a_log, dt_bias, l_valid):
    g, L, _ = proj.shape
    cs = LIN_CHUNK
    cps = GDN_PREP_CHUNKS
    rows = cs * cps
    assert L % rows == 0 and (l_valid is None or L == rows)
    vec = jnp.zeros((8, LANES), F32)
    vec = vec.at[0, 2 * HEADS:3 * HEADS].set(dt_bias).at[1, 2 * HEADS:3 * HEADS].set(a_log)
    c3 = 3 * GROUP
    col = lambda k: pl.BlockSpec((1, rows, GROUP), lambda i, j, k=k: (i, j, k))
    prev = lambda k: pl.BlockSpec(
        (1, SUBLANES, GROUP), lambda i, j, k=k: (i, jnp.maximum(j * (rows // SUBLANES) - 1, 0), k))
    out = pl.BlockSpec((1, rows, GROUP), lambda i, j: (i, j, 0))
    wide = jax.ShapeDtypeStruct((g, L, GROUP), F32)
    level_masks = jnp.asarray(_doubling_level_masks(rows, cs))
    return pl.pallas_call(
        functools.partial(_gdn_prep_kernel, cs=cs, cps=cps, l_valid=l_valid),
        grid=(g, L // rows),
        in_specs=[col(COL_GDN), col(COL_GDN + 1), col(COL_GDN + 2),
                  prev(COL_GDN), prev(COL_GDN + 1), prev(COL_GDN + 2),
                  pl.BlockSpec((1, rows, LANES), lambda i, j: (i, j, COL_SMALL)),
                  pl.BlockSpec((1,) + conv0.shape[1:], lambda i, j: (i, 0, 0)),
                  _resident((CONV_W, c3)), _resident((8, LANES)), _resident(level_masks.shape)],
        out_specs=[out, out, out, out, out,
                   pl.BlockSpec((1, cps * SUBLANES, GROUP), lambda i, j: (i, j, 0))],
        out_shape=[wide, wide, wide, wide, wide,
                   jax.ShapeDtypeStruct((g, L // cs * SUBLANES, GROUP), F32)],
        scratch_shapes=[pltpu.VMEM((cps * (cs + SUBLANES), c3), F32)],
        compiler_params=_params("parallel", "parallel"),
        name="gdn_prep",
    )(proj, proj, proj, proj, proj, proj, proj, conv0, conv_w, vec, level_masks)


GDN_SCAN_SEQS = 4


def _gdn_scan_call(prep, proj, s0, norm_w):
    b, L, _ = proj.shape
    cs = LIN_CHUNK
    nb = GDN_SCAN_SEQS
    n_chunks = L // cs
    assert b % nb == 0 and L % cs == 0
    blk = pl.BlockSpec((nb, cs, GROUP), lambda i, c: (i, c, 0))
    state = pl.BlockSpec((nb, HEADS, HEAD_DIM, HEAD_DIM), lambda i, c: (i, 0, 0, 0))
    return pl.pallas_call(
        functools.partial(_gdn_scan_kernel, nb=nb, n_chunks=n_chunks),
        grid=(b // nb, n_chunks),
        in_specs=[blk, blk, blk, blk, blk,
                  pl.BlockSpec((nb, SUBLANES, GROUP), lambda i, c: (i, c, 0)),
                  pl.BlockSpec((nb, cs, GROUP), lambda i, c: (i, c, COL_GDN + 3)),
                  state, _resident((1, GROUP))],
        out_specs=[blk, state],
        out_shape=[jax.ShapeDtypeStruct((b, L, GROUP), F32),
                   jax.ShapeDtypeStruct((b, HEADS, HEAD_DIM, HEAD_DIM), F32)],
        scratch_shapes=[pltpu.VMEM((nb, GROUP, GROUP), F32)],
        compiler_params=_params("parallel", "arbitrary"),
        name="gdn_scan",
    )(*prep, proj, s0, jnp.tile(norm_w, HEADS).reshape(1, GROUP))


def gdn_mixer_long(proj, conv0, s0, conv_w, a_log, dt_bias, norm_w):
    L = proj.shape[1]
    prep = _gdn_prep_call(proj, conv0, conv_w, a_log, dt_bias, None)
    o, s1 = _gdn_scan_call(prep, proj, s0, norm_w)
    conv1 = proj[:, L - (CONV_W - 1):, COL_GDN * GROUP:(COL_GDN + 3) * GROUP]
    return o, conv1, s1


def gdn_mixer_short(proj, conv0, s0, conv_w, a_log, dt_bias, norm_w):
    b, L, cols = proj.shape
    cs = LIN_CHUNK
    cps = GDN_PREP_CHUNKS
    assert CONV_W - 1 <= L <= cs and b % cps == 0
    padded = jnp.pad(proj, ((0, 0), (0, cs - L), (0, 0)))
    prep = _gdn_prep_call(padded.reshape(b // cps, cps * cs, cols),
                          conv0.reshape(b // cps, cps * (CONV_W - 1), 3 * GROUP), conv_w, a_log, dt_bias, L)
    prep = [p.reshape(b, -1, GROUP) for p in prep]
    o, s1 = _gdn_scan_call(prep, padded, s0, norm_w)
    conv1 = proj[:, L - (CONV_W - 1):, COL_GDN * GROUP:(COL_GDN + 3) * GROUP]
    return o[:, :L], conv1, s1


def _moba_prompt_kernel(q_ref, k_ref, v_ref, o_ref, ks, vts, kmean, sel_t, o_t, q_tb, alibi, ml,
                        s_buf, p_buf, *, nb):
    blk = MOBA_BLOCK
    qi = pl.program_id(1)

    @pl.when(qi == 0)
    def _():
        for j in range(nb):
            kj = k_ref[0, j * blk:(j + 1) * blk, :]
            kmean[j:j + 1, :] = jnp.mean(kj, axis=0, keepdims=True)
            kjb = kj.astype(BF16)
            for h in range(HEADS):
                ks[h, j] = kjb[:, h * HEAD_DIM:(h + 1) * HEAD_DIM]
            vts[j] = v_ref[0, j * blk:(j + 1) * blk, :].T.astype(BF16)

    q_t = (q_ref[0] * (HEAD_DIM ** -0.5)).T
    q_tb[...] = (q_t * LOG2_E).astype(BF16)
    blk_row = lax.broadcasted_iota(jnp.int32, (nb, blk), 0)
    rel = (lax.broadcasted_iota(jnp.int32, (blk, blk), 1)
           - lax.broadcasted_iota(jnp.int32, (blk, blk), 0)).astype(F32)
    for h in range(HEADS):
        alibi[h] = rel * (-ALIBI_SLOPES[h] * LOG2_E)

    for h in range(HEADS):
        hs = slice(h * HEAD_DIM, (h + 1) * HEAD_DIM)
        slope = ALIBI_SLOPES[h]
        gate = jnp.where(blk_row < qi, _dot_f32(kmean[:, hs], q_t[hs, :]), NEG_INF)
        sel = jnp.full((nb, blk), NEG_INF, F32)
        for _ in range(MOBA_TOPK):
            top = jnp.max(gate, axis=0, keepdims=True)
            first = jnp.min(jnp.where(gate == top, blk_row, nb), axis=0, keepdims=True)
            pick = blk_row == first
            sel = jnp.where(jnp.logical_and(pick, blk_row < qi), 0.0, sel)
            gate = jnp.where(pick, NEG_INF, gate)
        sel_t[h] = sel
        ml[h, 0:1, :] = jnp.full((1, blk), NEG_INF, F32)
        ml[h, 1:2, :] = jnp.zeros((1, blk), F32)
    o_t[...] = jnp.zeros(o_t.shape, F32)

    def block_step(j, own):
        for h in range(HEADS):
            hs = slice(h * HEAD_DIM, (h + 1) * HEAD_DIM)
            s_buf[h] = _dot(ks[h, j], q_tb[hs, :])
        for h in range(HEADS):
            if own:
                causal = (lax.broadcasted_iota(jnp.int32, (blk, blk), 1)
                          >= lax.broadcasted_iota(jnp.int32, (blk, blk), 0))
                s = jnp.where(causal, s_buf[h] + alibi[h], NEG_INF)
            else:
                off = (qi - j).astype(F32) * (ALIBI_SLOPES[h] * LOG2_E * blk)
                s = s_buf[h] + alibi[h] + (sel_t[h, pl.ds(j, 1), :] - off)
            m = ml[h, 0:1, :]
            m_new = jnp.maximum(m, jnp.max(s, axis=0, keepdims=True))
            p = jnp.exp2(s - m_new)
            p_buf[h] = p.astype(BF16)
            alpha = jnp.exp2(m - m_new)
            ml[h, 0:1, :] = m_new
            ml[h, 1:2, :] = alpha * ml[h, 1:2, :] + jnp.sum(p, axis=0, keepdims=True)
            ml[h, 2:3, :] = alpha
        for h in range(HEADS):
            hs = slice(h * HEAD_DIM, (h + 1) * HEAD_DIM)
            o_t[hs, :] = (ml[h, 2:3, :] * o_t[hs, :]
                          + jnp.dot(vts[j, hs, :], p_buf[h], preferred_element_type=F32))

    block_step(qi, True)

    def body(j, carry):
        block_step(j, False)
        return carry

    lax.fori_loop(0, qi, body, 0)
    for h in range(HEADS):
        hs = slice(h * HEAD_DIM, (h + 1) * HEAD_DIM)
        o_t[hs, :] = o_t[hs, :] / ml[h, 1:2, :]
    o_ref[0] = o_t[...].T


def moba_prompt(proj):
    b, S, _ = proj.shape
    blk = MOBA_BLOCK
    nb = S // blk
    seq = lambda k: pl.BlockSpec((1, S, GROUP), lambda i, j, k=k: (i, 0, k))
    return pl.pallas_call(
        functools.partial(_moba_prompt_kernel, nb=nb),
        grid=(b, nb),
        in_specs=[pl.BlockSpec((1, blk, GROUP), lambda i, j: (i, j, COL_MOBA)),
                  seq(COL_MOBA + 1), seq(COL_MOBA + 2)],
        out_specs=pl.BlockSpec((1, blk, GROUP), lambda i, j: (i, j, 0)),
        out_shape=jax.ShapeDtypeStruct((b, S, GROUP), F32),
        scratch_shapes=[pltpu.VMEM((HEADS, nb, blk, HEAD_DIM), BF16),
                        pltpu.VMEM((nb, GROUP, blk), BF16),
                        pltpu.VMEM((nb, GROUP), F32),
                        pltpu.VMEM((HEADS, nb, blk), F32),
                        pltpu.VMEM((GROUP, blk), F32),
                        pltpu.VMEM((GROUP, blk), BF16),
                        pltpu.VMEM((HEADS, blk, blk), F32),
                        pltpu.VMEM((HEADS, SUBLANES, blk), F32),
                        pltpu.VMEM((HEADS, blk, blk), F32),
                        pltpu.VMEM((HEADS, blk, blk), BF16)],
        compiler_params=_params("parallel", "arbitrary"),
        name="moba_prompt",
    )(proj, proj, proj)


PAGES_PER_BLOCK = MOBA_BLOCK // PAGE_SIZE
SELECT_PAGE_BUFFERS = 64
SELECT_UNROLL = 4


def _moba_select_kernel(pt_ref, q_ref, kc_ref, idx_ref, pages, kmean_t, q8, sems, *,
                        layer, n_pages, t):
    b = pl.program_id(0)
    nbuf = SELECT_PAGE_BUFFERS
    n_blocks = n_pages // PAGES_PER_BLOCK

    def page_copy(p, slot):
        return pltpu.make_async_copy(kc_ref.at[layer, pt_ref[b, p]], pages.at[slot], sems.at[slot])

    for s in range(nbuf):
        page_copy(s, s).start()

    kmean_t[...] = jnp.zeros(kmean_t.shape, F32)
    blk_of_lane = lax.broadcasted_iota(jnp.int32, kmean_t.shape, 1)

    def body(n, carry):
        tot = jnp.zeros((GROUP, PAGE_SIZE), F32)
        for pp in range(PAGES_PER_BLOCK):
            p = n * PAGES_PER_BLOCK + pp
            slot = p % nbuf
            page_copy(p, slot).wait()
            tot = tot + pages[slot]

            @pl.when(p + nbuf < n_pages)
            def _():
                page_copy(p + nbuf, slot).start()
        mean = jnp.sum(tot, axis=1, keepdims=True) * (1.0 / MOBA_BLOCK)
        kmean_t[...] = jnp.where(blk_of_lane == n, mean, kmean_t[...])
        return carry

    lax.fori_loop(0, n_blocks, body, 0, unroll=SELECT_UNROLL)

    q8[0:t, :] = q_ref[0]
    q8[t:, :] = jnp.zeros((SUBLANES - t, GROUP), F32)
    blk_lane = lax.broadcasted_iota(jnp.int32, (SUBLANES, n_blocks), 1)
    out_lane = lax.broadcasted_iota(jnp.int32, (SUBLANES, LANES), 1)
    res = jnp.zeros((SUBLANES, LANES), jnp.int32)
    for h in range(HEADS):
        hs = slice(h * HEAD_DIM, (h + 1) * HEAD_DIM)
        gate = _dot_f32(q8[:, hs], kmean_t[hs, 0:n_blocks])
        for r in range(MOBA_TOPK):
            top = jnp.max(gate, axis=1, keepdims=True)
            first = jnp.min(jnp.where(gate == top, blk_lane, n_blocks), axis=1, keepdims=True)
            res = jnp.where(out_lane == h * MOBA_TOPK + r, first, res)
            gate = jnp.where(blk_lane == first, NEG_INF, gate)
    idx_ref[0] = res


def _moba_sample_attn_kernel(pt_ref, idx_ref, q_ref, kn_ref, vn_ref, kc_ref, vc_ref, o_ref,
                             kbuf, vbuf, st8, o8, ksem, vsem, *, layer, past, t):
    b = pl.program_id(0)
    n_seq = pl.num_programs(0)
    n_sel = MOBA_TOPK * MOBA_BLOCK
    pairs = [(tok, h) for h in range(HEADS) for tok in range(t)]

    def block_of(seq, tok, h, r):
        return idx_ref[seq, tok * (HEADS * MOBA_TOPK) + h * MOBA_TOPK + r]

    def copies(seq, i, half):
        tok, h = pairs[i]
        out = []
        for r in range(MOBA_TOPK):
            blk = block_of(seq, tok, h, r)
            for pp in range(PAGES_PER_BLOCK):
                phys = pt_ref[seq, blk * PAGES_PER_BLOCK + pp]
                lanes = pl.ds((r * PAGES_PER_BLOCK + pp) * PAGE_SIZE, PAGE_SIZE)
                out.append(pltpu.make_async_copy(kc_ref.at[layer, phys, h], kbuf.at[half, i, :, lanes],
                                                 ksem.at[half, i]))
                out.append(pltpu.make_async_copy(vc_ref.at[layer, phys, h], vbuf.at[half, i, :, lanes],
                                                 vsem.at[half, i]))
        return out

    def start_all(seq, half):
        for i in range(len(pairs)):
            for c in copies(seq, i, half):
                c.start()

    half = b % 2

    @pl.when(b == 0)
    def _():
        start_all(b, half)

    @pl.when(b + 1 < n_seq)
    def _():
        start_all(b + 1, 1 - half)

    for i, ref in enumerate((q_ref, kn_ref, vn_ref)):
        st8[i, 0:t, :] = ref[0]
        st8[i, t:, :] = jnp.zeros((SUBLANES - t, GROUP), F32)

    lane = lax.broadcasted_iota(jnp.int32, (1, n_sel), 1)
    row = lax.broadcasted_iota(jnp.int32, (SUBLANES, 1), 0)
    for i in range(len(pairs)):
        for c in copies(b, i, half):
            c.wait()
    for i, (tok, h) in enumerate(pairs):
        hs = slice(h * HEAD_DIM, (h + 1) * HEAD_DIM)
        slope = ALIBI_SLOPES[h]
        qrow = st8[0, tok:tok + 1, hs] * (HEAD_DIM ** -0.5)
        s_sel = _dot(jnp.broadcast_to(qrow, (SUBLANES, HEAD_DIM)), kbuf[half, i])[0:1]
        blk = jnp.where(lane < MOBA_BLOCK, block_of(b, tok, h, 0),
                        jnp.where(lane < 2 * MOBA_BLOCK, block_of(b, tok, h, 1), block_of(b, tok, h, 2)))
        pos = blk * MOBA_BLOCK + (lane % MOBA_BLOCK)
        s_sel = s_sel - slope * (past + tok - pos).astype(F32)
        s_own = jnp.sum(st8[1, :, hs] * qrow, axis=1, keepdims=True)
        s_own = jnp.where(row <= tok, s_own - slope * (tok - row).astype(F32), NEG_INF)
        m = jnp.maximum(jnp.max(s_sel, axis=1, keepdims=True), jnp.max(s_own, axis=0, keepdims=True))
        p_sel = jnp.exp(s_sel - m)
        p_own = jnp.exp(s_own - m)
        l = jnp.sum(p_sel, axis=1, keepdims=True) + jnp.sum(p_own, axis=0, keepdims=True)
        o = _dot_nt(jnp.broadcast_to(p_sel, (SUBLANES, n_sel)), vbuf[half, i])[0:1]
        o = o + jnp.sum(p_own * st8[2, :, hs], axis=0, keepdims=True)
        o8[tok:tok + 1, hs] = o / l
    o_ref[0] = o8[0:t, :]


def moba_sample(proj, k_cache, v_cache, page_table, layer):
    db, t, _ = proj.shape
    depth, pool = k_cache.shape[:2]
    n_pages = page_table.shape[1]
    n_blocks = n_pages // PAGES_PER_BLOCK
    past = n_pages * PAGE_SIZE
    assert past % MOBA_BLOCK == 0 and n_pages >= SELECT_PAGE_BUFFERS and t <= SUBLANES
    assert MOBA_TOPK <= n_blocks <= LANES
    tok = lambda k: pl.BlockSpec((1, t, GROUP), lambda i, *_: (i, 0, k))
    hbm = pl.BlockSpec(memory_space=pl.ANY)
    n_pairs = t * HEADS
    idx = pl.pallas_call(
        functools.partial(_moba_select_kernel, layer=layer, n_pages=n_pages, t=t),
        grid_spec=pltpu.PrefetchScalarGridSpec(
            num_scalar_prefetch=1, grid=(db,),
            in_specs=[tok(COL_MOBA), hbm],
            out_specs=pl.BlockSpec((1, SUBLANES, LANES), lambda i, *_: (i, 0, 0)),
            scratch_shapes=[pltpu.VMEM((SELECT_PAGE_BUFFERS, GROUP, PAGE_SIZE), F32),
                            pltpu.VMEM((GROUP, LANES), F32),
                            pltpu.VMEM((SUBLANES, GROUP), F32),
                            pltpu.SemaphoreType.DMA((SELECT_PAGE_BUFFERS,))]),
        out_shape=jax.ShapeDtypeStruct((db, SUBLANES, LANES), jnp.int32),
        compiler_params=_params("arbitrary"),
        name="moba_sample_select",
    )(page_table, proj, k_cache.reshape(depth, pool, GROUP, PAGE_SIZE))
    idx = idx[:, :t, :HEADS * MOBA_TOPK].reshape(db, t * HEADS * MOBA_TOPK)
    return pl.pallas_call(
        functools.partial(_moba_sample_attn_kernel, layer=layer, past=past, t=t),
        grid_spec=pltpu.PrefetchScalarGridSpec(
            num_scalar_prefetch=2, grid=(db,),
            in_specs=[tok(COL_MOBA), tok(COL_MOBA + 1), tok(COL_MOBA + 2), hbm, hbm],
            out_specs=pl.BlockSpec((1, t, GROUP), lambda i, *_: (i, 0, 0)),
            scratch_shapes=[pltpu.VMEM((2, n_pairs, HEAD_DIM, MOBA_TOPK * MOBA_BLOCK), F32),
                            pltpu.VMEM((2, n_pairs, HEAD_DIM, MOBA_TOPK * MOBA_BLOCK), F32),
                            pltpu.VMEM((3, SUBLANES, GROUP), F32),
                            pltpu.VMEM((SUBLANES, GROUP), F32),
                            pltpu.SemaphoreType.DMA((2, n_pairs)),
                            pltpu.SemaphoreType.DMA((2, n_pairs))]),
        out_shape=jax.ShapeDtypeStruct((db, t, GROUP), F32),
        compiler_params=_params("arbitrary"),
        name="moba_sample_attn",
    )(page_table, idx, proj, proj, proj, k_cache, v_cache)


def _rearrange_w_in(w):
    d = w.shape[0]
    dt0 = COL_HGRN * GROUP
    ba0 = dt0 + HEADS + (COL_GDN + 4 - COL_HGRN) * GROUP
    wide = jnp.concatenate([w[:, :dt0], w[:, dt0 + HEADS:ba0]], axis=1)
    narrow = jnp.concatenate([w[:, dt0:dt0 + HEADS], w[:, ba0:ba0 + 2 * HEADS],
                              jnp.zeros((d, LANES - N_SMALL), w.dtype)], axis=1)
    return jnp.concatenate([wide, narrow], axis=1)


def _layer(x, mem_k, mem_v, states, lp, layer, attend, prompt):
    b, L, d = x.shape
    ssm_conv0, ssm0, hgrn0, gdn_conv0, gdn0 = states
    w_in = lp['w_in'].astype(BF16) if prompt else lp['w_in']
    proj = norm_matmul(x.reshape(b * L, d), lp['n_mix_pre'], w_in, min(256, b * L)).reshape(b, L, -1)
    o_a = attend(proj)
    o_b, ssm_conv1, ssm1 = ssd_mixer(proj, ssm_conv0, ssm0, lp['ssm_conv_w'], lp['ssm_conv_b'],
                                     lp['ssm_dt_bias'], lp['ssm_a_log'], lp['ssm_d'], lp['ssm_norm'])
    hgrn = hgrn_mixer_long if L % (LIN_PREP_CHUNKS * LIN_CHUNK) == 0 else hgrn_mixer
    o_c, hgrn1 = hgrn(proj, jnp.swapaxes(hgrn0, -1, -2), lp['hgrn_lb_raw'], lp['hgrn_norm'], layer)
    gdn = gdn_mixer_long if L % (GDN_PREP_CHUNKS * LIN_CHUNK) == 0 else gdn_mixer_short
    o_d, gdn_conv1, gdn1 = gdn(proj, gdn_conv0, gdn0, lp['gdn_conv_w'], lp['gdn_a_log'],
                               lp['gdn_dt_bias'], lp['gdn_norm'])
    post = post_mixer if prompt else post_mixer_sample
    args = (x, (o_a, o_b, o_c, o_d), mem_k, mem_v, lp['norms'], lp['w_out'], lp['w_xq'], lp['w_xo'],
            lp['w_gu'], lp['w_down'])
    x = post(*args, 256) if prompt else post(*args)
    k = proj[..., GROUP:2 * GROUP].reshape(b, L, HEADS, HEAD_DIM)
    v = proj[..., 2 * GROUP:3 * GROUP].reshape(b, L, HEADS, HEAD_DIM)
    return x, (k, v, ssm_conv1, ssm1, jnp.swapaxes(hgrn1, -1, -2), gdn_conv1, gdn1)


def kernel(x_prompt, x_sample, mem_prompt, cache_moba_k, cache_moba_v, page_table, cache_mem_k, cache_mem_v, state_ssm_conv, state_ssm, state_hgrn, state_gdn_conv, state_gdn, n_mix_pre, n_mix_post, w_in, w_out, ssm_conv_w, ssm_conv_b, ssm_dt_bias, ssm_a_log, ssm_d, ssm_norm, hgrn_lb_raw, hgrn_norm, gdn_conv_w, gdn_a_log, gdn_dt_bias, gdn_norm, n_x_pre, n_x_post, mem_norm, w_xq, w_xkv, w_xo, n_f_pre, n_f_post, w_gu, w_down):
    depth = w_in.shape[0]
    bp, _, d = x_prompt.shape
    db = x_sample.shape[0]
    n_mem = mem_prompt.shape[1]
    kc = jnp.transpose(cache_moba_k, (0, 1, 3, 4, 2))
    vc = jnp.transpose(cache_moba_v, (0, 1, 3, 4, 2))
    zeros_p = (jnp.zeros((bp,) + state_ssm_conv.shape[2:], F32), jnp.zeros((bp,) + state_ssm.shape[2:], F32),
               jnp.zeros((bp,) + state_hgrn.shape[2:], F32), jnp.zeros((bp,) + state_gdn_conv.shape[2:], F32),
               jnp.zeros((bp,) + state_gdn.shape[2:], F32))
    yp, ys = x_prompt, x_sample
    outs_p, outs_s, mem_ks, mem_vs = [], [], [], []
    for l in range(depth):
        norms = jnp.zeros((SUBLANES, d), F32)
        for i, nrm in enumerate((n_mix_post, n_x_pre, n_x_post, n_f_pre, n_f_post)):
            norms = norms.at[i].set(nrm[l])
        lp = {'n_mix_pre': n_mix_pre[l], 'w_in': _rearrange_w_in(w_in[l]), 'norms': norms,
              'w_out': w_out[l].astype(BF16), 'w_xq': w_xq[l].astype(BF16), 'w_xo': w_xo[l].astype(BF16),
              'w_gu': w_gu[l].astype(BF16), 'w_down': w_down[l].astype(BF16),
              'ssm_conv_w': ssm_conv_w[l], 'ssm_conv_b': ssm_conv_b[l], 'ssm_dt_bias': ssm_dt_bias[l],
              'ssm_a_log': ssm_a_log[l], 'ssm_d': ssm_d[l], 'ssm_norm': ssm_norm[l],
              'hgrn_lb_raw': hgrn_lb_raw, 'hgrn_norm': hgrn_norm[l], 'gdn_conv_w': gdn_conv_w[l],
              'gdn_a_log': gdn_a_log[l], 'gdn_dt_bias': gdn_dt_bias[l], 'gdn_norm': gdn_norm[l]}
        mkv = norm_matmul(mem_prompt.reshape(bp * n_mem, d), mem_norm[l], w_xkv[l].astype(BF16), 256)
        mk = mkv[:, :d].reshape(bp, n_mem, d)
        mv = mkv[:, d:].reshape(bp, n_mem, d)
        yp, st_p = _layer(yp, mk.astype(BF16), mv.astype(BF16), zeros_p, lp, l, moba_prompt, True)
        outs_p.append(st_p)
        mem_ks.append(mk.reshape(bp, n_mem, X_HEADS, d // X_HEADS))
        mem_vs.append(mv.reshape(bp, n_mem, X_HEADS, d // X_HEADS))
        states_s = (state_ssm_conv[l], state_ssm[l], state_hgrn[l], state_gdn_conv[l], state_gdn[l])
        attend_s = functools.partial(moba_sample, k_cache=kc, v_cache=vc, page_table=page_table, layer=l)
        ys, st_s = _layer(ys, cache_mem_k[l].reshape(db, n_mem, d).astype(BF16),
                          cache_mem_v[l].reshape(db, n_mem, d).astype(BF16), states_s, lp, l, attend_s, False)
        outs_s.append(st_s)
    stack = lambda outs, i: jnp.stack([o[i] for o in outs], axis=0)
    return (yp, ys, stack(outs_p, 0), stack(outs_s, 0), stack(outs_p, 1), stack(outs_s, 1),
            jnp.stack(mem_ks, axis=0), jnp.stack(mem_vs, axis=0),
            stack(outs_p, 2), stack(outs_s, 2), stack(outs_p, 3), stack(outs_s, 3),
            stack(outs_p, 4), stack(outs_s, 4), stack(outs_p, 5), stack(outs_s, 5),
            stack(outs_p, 6), stack(outs_s, 6))
```

```python
import functools

import numpy as np
import jax
import jax.numpy as jnp
from jax import lax
from jax.experimental import pallas as pl
from jax.experimental.pallas import tpu as pltpu

F32 = jnp.float32
BF16 = jnp.bfloat16

LANES = 128
SUBLANES = 8
VMEM_LIMIT_BYTES = 56 * 1024 * 1024

GROUP = 256
HEADS = 4
HEAD_DIM = 64
CONV_W = 4
SSM_DSTATE = 128
SSM_CHUNK = 128
LIN_CHUNK = 64
MOBA_BLOCK = 256
MOBA_TOPK = 3
PAGE_SIZE = 128
X_HEADS = 4
N_SMALL = 12
COL_MOBA = 0
COL_SSM_Z = 3
COL_SSM_X = 4
COL_HGRN = 7
COL_GDN = 11
N_WIDE = 15 * GROUP
COL_SMALL = N_WIDE // LANES
EPS = 1e-6
NEG_INF = float("-inf")
LOG2_E = 1.4426950408889634
ALIBI_SLOPES = tuple(2.0 ** (-8.0 * (h + 1) / HEADS) for h in range(HEADS))


def _rms(x, w):
    return x * lax.rsqrt(jnp.mean(x * x, axis=-1, keepdims=True) + EPS) * w


def _sigmoid(x):
    return 1.0 / (1.0 + jnp.exp(-x))


def _silu(x):
    return x * _sigmoid(x)


def _softplus(x):
    return jnp.maximum(x, 0.0) + jnp.log(1.0 + jnp.exp(-jnp.abs(x)))


def _dot(a, b):
    return jnp.dot(a.astype(BF16), b.astype(BF16), preferred_element_type=F32)


def _dot_nt(a, b):
    return lax.dot_general(a.astype(BF16), b.astype(BF16), (((1,), (1,)), ((), ())),
                           preferred_element_type=F32)


def _dot_f32(a, b):
    return jnp.dot(a, b, preferred_element_type=F32, precision=lax.Precision.HIGHEST)


def _params(*sem, flags=None):
    return pltpu.CompilerParams(dimension_semantics=sem, vmem_limit_bytes=VMEM_LIMIT_BYTES, flags=flags)


def _resident(shape):
    return pl.BlockSpec(shape, lambda *_: (0,) * len(shape), pipeline_mode=pl.Buffered(1))


def _norm_matmul_kernel(x_ref, nw_ref, w_ref, o_ref, *copies, full_precision, copy_blocks):
    xn = _rms(x_ref[...], nw_ref[...])
    res = _dot_f32(xn, w_ref[...]) if full_precision else _dot(xn, w_ref[...])
    o_ref[...] = res
    for ref, blk in zip(copies, copy_blocks):
        ref[...] = res[:, blk * GROUP:(blk + 1) * GROUP]


def norm_matmul(x, norm_w, w, tm, copy_blocks=()):
    n, d = x.shape
    c = w.shape[1]
    return pl.pallas_call(
        functools.partial(_norm_matmul_kernel, full_precision=(w.dtype == F32), copy_blocks=copy_blocks),
        grid=(n // tm,),
        in_specs=[pl.BlockSpec((tm, d), lambda i: (i, 0)),
                  _resident((1, d)),
                  _resident((d, c))],
        out_specs=[pl.BlockSpec((tm, c), lambda i: (i, 0))]
                  + [pl.BlockSpec((tm, GROUP), lambda i: (i, 0))] * len(copy_blocks),
        out_shape=[jax.ShapeDtypeStruct((n, c), F32)]
                  + [jax.ShapeDtypeStruct((n, GROUP), F32)] * len(copy_blocks),
        compiler_params=_params("parallel"),
        name="norm_matmul",
    )(x, norm_w.reshape(1, d), w)


def _mix_out_and_query(x, mixed, nw, wout_ref, wxq_ref):
    x = x + _rms(_dot(mixed, wout_ref[...]), nw[0:1])
    return x, _dot(_rms(x, nw[1:2]), wxq_ref[...])


def _memory_attention(q, head_k, head_v):
    xdh = q.shape[1] // X_HEADS
    heads = []
    for h in range(X_HEADS):
        s = _dot_nt(q[:, h * xdh:(h + 1) * xdh], head_k(h)) * (xdh ** -0.5)
        p = jnp.exp(s - jnp.max(s, axis=-1, keepdims=True))
        heads.append(_dot(p, head_v(h)) / jnp.sum(p, axis=-1, keepdims=True))
    return jnp.concatenate(heads, axis=-1)


def _lane_heads(ref):
    xdh = ref.shape[-1] // X_HEADS
    return lambda h: ref[0, :, h * xdh:(h + 1) * xdh]


def _attn_out_and_ffn(x, att, nw, wxo_ref, wgu_ref, wdown_ref):
    x = x + _rms(_dot(att, wxo_ref[...]), nw[2:3])
    gu = _dot(_rms(x, nw[3:4]), wgu_ref[...])
    hid = gu.shape[1] // 2
    act = _silu(gu[:, :hid]) * gu[:, hid:]
    return x + _rms(_dot(act, wdown_ref[...]), nw[4:5])


def _post_mixer_kernel(x_ref, oa_ref, ob_ref, oc_ref, od_ref, mk_ref, mv_ref, norms_ref, wout_ref,
                       wxq_ref, wxo_ref, wgu_ref, wdown_ref, o_ref):
    nw = norms_ref[...]
    mixed = jnp.concatenate([oa_ref[0], ob_ref[0], oc_ref[0], od_ref[0]], axis=-1)
    x, q = _mix_out_and_query(x_ref[0], mixed, nw, wout_ref, wxq_ref)
    att = _memory_attention(q, _lane_heads(mk_ref), _lane_heads(mv_ref))
    o_ref[0] = _attn_out_and_ffn(x, att, nw, wxo_ref, wgu_ref, wdown_ref)


def post_mixer(x, mix_parts, mk, mv, norms, w_out, w_xq, w_xo, w_gu, w_down, tm):
    b, L, d = x.shape
    m = mk.shape[1]
    row = pl.BlockSpec((1, tm, d), lambda i, j: (i, j, 0))
    part = pl.BlockSpec((1, tm, GROUP), lambda i, j: (i, j, 0))
    mem = pl.BlockSpec((1, m, d), lambda i, j: (i, 0, 0))
    return pl.pallas_call(
        _post_mixer_kernel,
        grid=(b, L // tm),
        in_specs=[row, part, part, part, part, mem, mem, _resident(norms.shape),
                  _resident(w_out.shape), _resident(w_xq.shape), _resident(w_xo.shape),
                  _resident(w_gu.shape), _resident(w_down.shape)],
        out_specs=row,
        out_shape=jax.ShapeDtypeStruct((b, L, d), F32),
        compiler_params=_params("parallel", "parallel"),
        name="post_mixer",
    )(x, *mix_parts, mk, mv, norms, w_out, w_xq, w_xo, w_gu, w_down)


def _sample_pre_kernel(x_ref, oa_ref, ob_ref, oc_ref, od_ref, norms_ref, wout_ref, wxq_ref,
                       x1_ref, q_ref):
    mixed = jnp.concatenate([oa_ref[...], ob_ref[...], oc_ref[...], od_ref[...]], axis=-1)
    x1, q = _mix_out_and_query(x_ref[...], mixed, norms_ref[...], wout_ref, wxq_ref)
    x1_ref[...] = x1
    q_ref[...] = q


def _sample_attn_kernel(q_ref, mk_ref, mv_ref, o_ref, q8, *, t):
    q8[0:t, :] = q_ref[0]
    q8[t:, :] = jnp.zeros((q8.shape[0] - t, q8.shape[1]), F32)
    o_ref[0] = _memory_attention(q8[...], _lane_heads(mk_ref), _lane_heads(mv_ref))[0:t]


def _sample_post_kernel(x_ref, att_ref, norms_ref, wxo_ref, wgu_ref, wdown_ref, o_ref):
    o_ref[...] = _attn_out_and_ffn(x_ref[...], att_ref[...], norms_ref[...], wxo_ref, wgu_ref,
                                   wdown_ref)


def post_mixer_sample(x, mix_parts, mk, mv, norms, w_out, w_xq, w_xo, w_gu, w_down):
    b, t, d = x.shape
    n = b * t
    m = mk.shape[1]
    flat = lambda a: a.reshape(n, a.shape[-1])
    full = lambda shp: pl.BlockSpec(shp, lambda i: (0,) * len(shp))
    x1, q = pl.pallas_call(
        _sample_pre_kernel,
        grid=(1,),
        in_specs=[full((n, d))] + [full((n, GROUP))] * 4
                 + [full(norms.shape), _resident(w_out.shape), _resident(w_xq.shape)],
        out_specs=[full((n, d)), full((n, d))],
        out_shape=[jax.ShapeDtypeStruct((n, d), F32)] * 2,
        compiler_params=_params("arbitrary"),
        name="sample_pre",
    )(flat(x), *[flat(p) for p in mix_parts], norms, w_out, w_xq)
    seq = pl.BlockSpec((1, t, d), lambda i: (i, 0, 0))
    mem = pl.BlockSpec((1, m, d), lambda i: (i, 0, 0))
    att = pl.pallas_call(
        functools.partial(_sample_attn_kernel, t=t),
        grid=(b,),
        in_specs=[seq, mem, mem],
        out_specs=seq,
        out_shape=jax.ShapeDtypeStruct((b, t, d), F32),
        scratch_shapes=[pltpu.VMEM((SUBLANES, d), F32)],
        compiler_params=_params("parallel"),
        name="sample_attn",
    )(q.reshape(b, t, d), mk, mv)
    out = pl.pallas_call(
        _sample_post_kernel,
        grid=(1,),
        in_specs=[full((n, d)), full((n, d)), full(norms.shape), _resident(w_xo.shape),
                  _resident(w_gu.shape), _resident(w_down.shape)],
        out_specs=full((n, d)),
        out_shape=jax.ShapeDtypeStruct((n, d), F32),
        compiler_params=_params("arbitrary"),
        name="sample_post",
    )(x1, flat(att), norms, w_xo, w_gu, w_down)
    return out.reshape(b, t, d)


def _tril(n, strict=False):
    r = lax.broadcasted_iota(jnp.int32, (n, n), 0)
    c = lax.broadcasted_iota(jnp.int32, (n, n), 1)
    return (r > c) if strict else (r >= c)


def _head_expand(n_rows, n_cols, width, offset=0):
    r = lax.broadcasted_iota(jnp.int32, (n_rows, n_cols), 0)
    c = lax.broadcasted_iota(jnp.int32, (n_rows, n_cols), 1)
    return (r == (c // width) + offset).astype(F32)


def _stage_rows(dst, src_ref, l_blk, row0=0):
    n = dst.shape[0]
    dst[row0:row0 + l_blk, :] = src_ref[0]
    if row0 + l_blk < n:
        dst[row0 + l_blk:n, :] = jnp.zeros((n - row0 - l_blk, dst.shape[1]), F32)


def _causal_conv(buf, cw_ref, cs, base=0):
    out = cw_ref[0:1, :] * buf[base + 5:base + 5 + cs, :]
    for j in range(1, CONV_W):
        out = out + cw_ref[j:j + 1, :] * buf[base + 5 + j:base + 5 + j + cs, :]
    return out


def _ssd_kernel(z_ref, x_ref, b_ref, c_ref, sm_ref, conv0_ref, h0_ref, cw_ref, cb_ref, vec_ref,
                o_ref, conv1_ref, h1_ref, buf, smbuf, hst, *, cs, l_blk, n_chunks):
    c_idx = pl.program_id(1)

    @pl.when(c_idx == 0)
    def _():
        buf[5:8, :] = conv0_ref[0]
        hst[...] = h0_ref[0]

    _stage_rows(buf.at[:, 0:GROUP], x_ref, l_blk, 8)
    _stage_rows(buf.at[:, GROUP:2 * GROUP], b_ref, l_blk, 8)
    _stage_rows(buf.at[:, 2 * GROUP:3 * GROUP], c_ref, l_blk, 8)
    _stage_rows(smbuf, sm_ref, l_blk)

    xbc = _silu(_causal_conv(buf, cw_ref, cs) + cb_ref[...])
    conv_tail = buf[5 + l_blk:8 + l_blk, :]
    buf[5:8, :] = conv_tail
    xs = xbc[:, 0:GROUP]
    bm = xbc[:, GROUP:2 * GROUP]
    cm = xbc[:, 2 * GROUP:3 * GROUP]

    dt = _softplus(smbuf[...] + vec_ref[0:1, 0:LANES])
    if l_blk < cs:
        rows = lax.broadcasted_iota(jnp.int32, (cs, LANES), 0)
        dt = jnp.where(rows < l_blk, dt, 0.0)
    a = dt * (-jnp.exp(vec_ref[1:2, 0:LANES]))
    acum = _dot_f32(_tril(cs).astype(F32), a)
    expand = _head_expand(LANES, GROUP, HEAD_DIM)
    dt_w = _dot_f32(dt, expand)
    acum_w = _dot_f32(acum, expand)
    a_last = acum_w[cs - 1:cs, :]
    xdt = xs * dt_w
    e_acum = jnp.exp(acum_w)
    xw = xdt * jnp.exp(a_last - acum_w)
    e_last = jnp.exp(a_last)
    acum_t = acum.T
    causal = _tril(cs)

    ys = []
    for h in range(HEADS):
        g = h // (HEADS // 2)
        hs = slice(h * HEAD_DIM, (h + 1) * HEAD_DIM)
        gs = slice(g * SSM_DSTATE, (g + 1) * SSM_DSTATE)
        st = hst[h]
        dec = jnp.exp(jnp.where(causal, acum[:, h:h + 1] - acum_t[h:h + 1, :], NEG_INF))
        y = _dot(_dot_nt(cm[:, gs], bm[:, gs]) * dec, xdt[:, hs])
        y = y + _dot_nt(cm[:, gs], st) * e_acum[:, hs]
        ys.append(y)
        hst[h] = e_last[:, h * HEAD_DIM:h * HEAD_DIM + 1] * st + _dot(xw[:, hs].T, bm[:, gs])
    y = jnp.concatenate(ys, axis=-1) + vec_ref[2:3, :] * xs
    o_ref[0] = _rms(y[0:l_blk] * _silu(z_ref[0]), vec_ref[3:4, :])

    @pl.when(c_idx == n_chunks - 1)
    def _():
        conv1_ref[0] = conv_tail
        h1_ref[0] = hst[...]


def ssd_mixer(proj, conv0, h0, conv_w, conv_b, dt_bias, a_log, d_skip, norm_w):
    b, L, _ = proj.shape
    cs = SSM_CHUNK
    l_blk = min(cs, L)
    n_chunks = L // l_blk
    vec = jnp.zeros((8, GROUP), F32)
    vec = vec.at[0, :HEADS].set(dt_bias).at[1, :HEADS].set(a_log)
    vec = vec.at[2].set(jnp.repeat(d_skip, HEAD_DIM)).at[3].set(norm_w)
    col = lambda k: pl.BlockSpec((1, l_blk, GROUP), lambda i, j, k=k: (i, j, k))
    per_b = lambda shp: pl.BlockSpec((1,) + shp, lambda i, j: (i,) + (0,) * len(shp))
    c3 = 3 * GROUP
    return pl.pallas_call(
        functools.partial(_ssd_kernel, cs=cs, l_blk=l_blk, n_chunks=n_chunks),
        grid=(b, n_chunks),
        in_specs=[col(COL_SSM_Z), col(COL_SSM_X), col(COL_SSM_X + 1), col(COL_SSM_X + 2),
                  pl.BlockSpec((1, l_blk, LANES), lambda i, j: (i, j, COL_SMALL)),
                  per_b((CONV_W - 1, c3)), per_b((HEADS, HEAD_DIM, SSM_DSTATE)),
                  _resident((CONV_W, c3)), _resident((1, c3)), _resident((8, GROUP))],
        out_specs=[pl.BlockSpec((1, l_blk, GROUP), lambda i, j: (i, j, 0)),
                   per_b((CONV_W - 1, c3)), per_b((HEADS, HEAD_DIM, SSM_DSTATE))],
        out_shape=[jax.ShapeDtypeStruct((b, L, GROUP), F32),
                   jax.ShapeDtypeStruct((b, CONV_W - 1, c3), F32),
                   jax.ShapeDtypeStruct((b, HEADS, HEAD_DIM, SSM_DSTATE), F32)],
        scratch_shapes=[pltpu.VMEM((cs + 8, c3), F32), pltpu.VMEM((cs, LANES), F32),
                        pltpu.VMEM((HEADS, HEAD_DIM, SSM_DSTATE), F32)],
        compiler_params=_params("parallel", "arbitrary"),
        name="ssd_mixer",
    )(proj, proj, proj, proj, proj, conv0, h0, conv_w, conv_b.reshape(1, c3), vec)


def _dot_tn(a, b):
    return lax.dot_general(a.astype(BF16), b.astype(BF16), (((0,), (0,)), ((), ())),
                           preferred_element_type=F32)


def _head_blocks(n, width):
    r = lax.broadcasted_iota(jnp.int32, (n, n), 0)
    c = lax.broadcasted_iota(jnp.int32, (n, n), 1)
    return ((r // width) == (c // width)).astype(F32)


def _dot_split(x, m01, passes, m01_left=False):
    m = m01.astype(BF16)
    acc = None
    for _ in range(passes):
        hi = x.astype(BF16)
        part = jnp.dot(m, hi, preferred_element_type=F32) if m01_left else jnp.dot(hi, m, preferred_element_type=F32)
        acc = part if acc is None else acc + part
        x = x - hi.astype(F32)
    return acc


def _head_rms_gate(o, gate, nw_row, l_blk):
    ms = _dot_split(o * o, _head_blocks(GROUP, HEAD_DIM), 2) * (1.0 / HEAD_DIM)
    return (o * lax.rsqrt(ms + EPS) * nw_row)[0:l_blk] * _silu(gate)


def _hgrn_kernel(q_ref, f_ref, i_ref, g_ref, s0_ref, lbraw_ref, nw_ref, o_ref, s1_ref,
                 stage, sst, *, cs, l_blk, n_chunks, layer):
    c_idx = pl.program_id(1)

    @pl.when(c_idx == 0)
    def _():
        sst[...] = s0_ref[0]

    if l_blk < cs:
        _stage_rows(stage.at[0], q_ref, l_blk)
        _stage_rows(stage.at[1], f_ref, l_blk)
        _stage_rows(stage.at[2], i_ref, l_blk)
        q, fx, v = stage[0], stage[1], stage[2]
    else:
        q, fx, v = q_ref[0], f_ref[0], i_ref[0]

    raw = lbraw_ref[...]
    e = jnp.exp(raw - jnp.max(raw, axis=0, keepdims=True))
    sm = e / jnp.sum(e, axis=0, keepdims=True)
    lb = jnp.zeros((1, GROUP), F32)
    for i in range(1, layer + 1):
        lb = lb + sm[i:i + 1, :]

    log_sig = jnp.minimum(fx, 0.0) - jnp.log1p(jnp.exp(-jnp.abs(fx)))
    la = jnp.log(lb)
    lbb = jnp.log1p(-lb) + log_sig
    log_f = jnp.maximum(la, lbb) + jnp.log1p(jnp.exp(-jnp.abs(la - lbb)))
    k = (1.0 - lb) * _sigmoid(-fx)
    if l_blk < cs:
        rows = lax.broadcasted_iota(jnp.int32, (cs, GROUP), 0)
        log_f = jnp.where(rows < l_blk, log_f, 0.0)
        k = jnp.where(rows < l_blk, k, 0.0)

    bcum = _dot_f32(_tril(cs).astype(F32), log_f)
    b_mid = bcum[cs // 2 - 1:cs // 2, :]
    b_last = bcum[cs - 1:cs, :]
    qe = q * jnp.exp(bcum - b_mid)
    ke = k * jnp.exp(b_mid - bcum)
    qs = q * jnp.exp(bcum)
    kd = k * jnp.exp(b_last - bcum)
    e_last = jnp.exp(b_last)
    causal = _tril(cs)

    os_ = []
    for h in range(HEADS):
        hs = slice(h * HEAD_DIM, (h + 1) * HEAD_DIM)
        st_t = sst[h]
        att = jnp.where(causal, _dot_nt(qe[:, hs], ke[:, hs]), 0.0)
        os_.append(_dot(att, v[:, hs]) + _dot_nt(qs[:, hs], st_t))
        sst[h] = e_last[:, hs] * st_t + _dot_tn(v[:, hs], kd[:, hs])
    o = jnp.concatenate(os_, axis=-1)
    o_ref[0] = _head_rms_gate(o, g_ref[0], nw_ref[...], l_blk)

    @pl.when(c_idx == n_chunks - 1)
    def _():
        s1_ref[0] = sst[...]


def hgrn_mixer(proj, s0, lb_raw, norm_w, layer):
    b, L, _ = proj.shape
    cs = LIN_CHUNK
    l_blk = min(cs, L)
    n_chunks = L // l_blk
    col = lambda k: pl.BlockSpec((1, l_blk, GROUP), lambda i, j, k=k: (i, j, k))
    st = pl.BlockSpec((1, HEADS, HEAD_DIM, HEAD_DIM), lambda i, j: (i, 0, 0, 0))
    return pl.pallas_call(
        functools.partial(_hgrn_kernel, cs=cs, l_blk=l_blk, n_chunks=n_chunks, layer=layer),
        grid=(b, n_chunks),
        in_specs=[col(COL_HGRN), col(COL_HGRN + 1), col(COL_HGRN + 2), col(COL_HGRN + 3), st,
                  _resident(lb_raw.shape), _resident((1, GROUP))],
        out_specs=[pl.BlockSpec((1, l_blk, GROUP), lambda i, j: (i, j, 0)), st],
        out_shape=[jax.ShapeDtypeStruct((b, L, GROUP), F32),
                   jax.ShapeDtypeStruct((b, HEADS, HEAD_DIM, HEAD_DIM), F32)],
        scratch_shapes=[pltpu.VMEM((3, cs, GROUP), F32),
                        pltpu.VMEM((HEADS, HEAD_DIM, HEAD_DIM), F32)],
        compiler_params=_params("parallel", "arbitrary"),
        name="hgrn_mixer",
    )(proj, proj, proj, proj, s0, lb_raw, jnp.tile(norm_w, HEADS).reshape(1, GROUP))


LIN_PREP_CHUNKS = 4


def _chunk_rows(x, cs, cps, row):
    return jnp.concatenate(
        [jnp.broadcast_to(x[i * cs + row:i * cs + row + 1, :], (cs, x.shape[1])) for i in range(cps)], axis=0)


def _hgrn_prep_kernel(q_ref, f_ref, i_ref, lbraw_ref, oi_ref, qs_ref, kd_ref, aux_ref, *, cs, cps, layer):
    rows = cs * cps
    q, fx, v = q_ref[0], f_ref[0], i_ref[0]
    raw = lbraw_ref[...]
    e = jnp.exp(raw - jnp.max(raw, axis=0, keepdims=True))
    sm = e / jnp.sum(e, axis=0, keepdims=True)
    lb = jnp.zeros((1, GROUP), F32)
    for i in range(1, layer + 1):
        lb = lb + sm[i:i + 1, :]
    log_sig = jnp.minimum(fx, 0.0) - jnp.log1p(jnp.exp(-jnp.abs(fx)))
    la = jnp.log(lb)
    lbb = jnp.log1p(-lb) + log_sig
    log_f = jnp.maximum(la, lbb) + jnp.log1p(jnp.exp(-jnp.abs(la - lbb)))
    k = (1.0 - lb) * _sigmoid(-fx)

    r = lax.broadcasted_iota(jnp.int32, (rows, rows), 0)
    c = lax.broadcasted_iota(jnp.int32, (rows, rows), 1)
    incl = jnp.logical_and(r >= c, (r // cs) == (c // cs))
    bcum = _dot_split(log_f, incl.astype(F32), 3, m01_left=True)
    b_mid = _chunk_rows(bcum, cs, cps, cs // 2 - 1)
    b_last = _chunk_rows(bcum, cs, cps, cs - 1)
    qe = q * jnp.exp(bcum - b_mid)
    ke = k * jnp.exp(b_mid - bcum)
    qs_ref[0] = q * jnp.exp(bcum)
    kd_ref[0] = k * jnp.exp(b_last - bcum)
    for i in range(cps):
        aux_ref[0, i * SUBLANES:(i + 1) * SUBLANES, :] = jnp.exp(b_last[i * cs:i * cs + SUBLANES, :])
    os_ = []
    for h in range(HEADS):
        hs = slice(h * HEAD_DIM, (h + 1) * HEAD_DIM)
        att = jnp.where(incl, _dot_nt(qe[:, hs], ke[:, hs]), 0.0)
        os_.append(_dot(att, v[:, hs]))
    oi_ref[0] = jnp.concatenate(os_, axis=-1)


def _hgrn_scan_kernel(oi_ref, qs_ref, kd_ref, aux_ref, v_ref, g_ref, s0_ref, nw_ref, o_ref, s1_ref, sst,
                      *, nb, n_chunks):
    c_idx = pl.program_id(1)
    blocks = _head_blocks(GROUP, HEAD_DIM)

    @pl.when(c_idx == 0)
    def _():
        sst[...] = jnp.zeros(sst.shape, F32)
        for bi in range(nb):
            for h in range(HEADS):
                hs = slice(h * HEAD_DIM, (h + 1) * HEAD_DIM)
                sst[bi, hs, hs] = s0_ref[bi, h]

    for bi in range(nb):
        st_t = sst[bi]
        o = oi_ref[bi] + _dot_nt(qs_ref[bi], st_t)
        sst[bi] = aux_ref[bi, 0:1, :] * st_t + _dot_tn(v_ref[bi], kd_ref[bi]) * blocks
        o_ref[bi] = _head_rms_gate(o, g_ref[bi], nw_ref[...], o.shape[0])

    @pl.when(c_idx == n_chunks - 1)
    def _():
        for bi in range(nb):
            for h in range(HEADS):
                hs = slice(h * HEAD_DIM, (h + 1) * HEAD_DIM)
                s1_ref[bi, h] = sst[bi, hs, hs]


def hgrn_mixer_long(proj, s0, lb_raw, norm_w, layer):
    b, L, _ = proj.shape
    cs = LIN_CHUNK
    cps = LIN_PREP_CHUNKS
    rows = cs * cps
    n_chunks = L // cs
    nb = 4 if b % 4 == 0 else 1
    col = lambda k: pl.BlockSpec((1, rows, GROUP), lambda i, j, k=k: (i, j, k))
    out = pl.BlockSpec((1, rows, GROUP), lambda i, j: (i, j, 0))
    wide = jax.ShapeDtypeStruct((b, L, GROUP), F32)
    oi, qs, kd, aux = pl.pallas_call(
        functools.partial(_hgrn_prep_kernel, cs=cs, cps=cps, layer=layer),
        grid=(b, L // rows),
        in_specs=[col(COL_HGRN), col(COL_HGRN + 1), col(COL_HGRN + 2), _resident(lb_raw.shape)],
        out_specs=[out, out, out, pl.BlockSpec((1, cps * SUBLANES, GROUP), lambda i, j: (i, j, 0))],
        out_shape=[wide, wide, wide, jax.ShapeDtypeStruct((b, n_chunks * SUBLANES, GROUP), F32)],
        compiler_params=_params("parallel", "parallel"),
        name="hgrn_prep",
    )(proj, proj, proj, lb_raw)
    blk = pl.BlockSpec((nb, cs, GROUP), lambda i, c: (i, c, 0))
    pcol = lambda k: pl.BlockSpec((nb, cs, GROUP), lambda i, c, k=k: (i, c, k))
    state = pl.BlockSpec((nb, HEADS, HEAD_DIM, HEAD_DIM), lambda i, c: (i, 0, 0, 0))
    return pl.pallas_call(
        functools.partial(_hgrn_scan_kernel, nb=nb, n_chunks=n_chunks),
        grid=(b // nb, n_chunks),
        in_specs=[blk, blk, blk, pl.BlockSpec((nb, SUBLANES, GROUP), lambda i, c: (i, c, 0)),
                  pcol(COL_HGRN + 2), pcol(COL_HGRN + 3), state, _resident((1, GROUP))],
        out_specs=[blk, state],
        out_shape=[wide, jax.ShapeDtypeStruct((b, HEADS, HEAD_DIM, HEAD_DIM), F32)],
        scratch_shapes=[pltpu.VMEM((nb, GROUP, GROUP), F32)],
        compiler_params=_params("parallel", "arbitrary"),
        name="hgrn_scan",
    )(oi, qs, kd, aux, proj, proj, s0, jnp.tile(norm_w, HEADS).reshape(1, GROUP))


def _doubling_level_masks(n, block):
    r = np.arange(n)[:, None]
    c = np.arange(n)[None, :]
    out, s = [], 1
    while s < block:
        out.append((r // (2 * s) == c // (2 * s)) & ((r // s) % 2 == 1) & ((c // s) % 2 == 0))
        s *= 2
    return np.stack(out).astype(np.float32)


def _unit_lower_inverse_minus_eye(a, block=None, level_masks_ref=None):
    n = a.shape[0]
    block = n if block is None else block
    r = lax.broadcasted_iota(jnp.int32, (n, n), 0)
    c = lax.broadcasted_iota(jnp.int32, (n, n), 1)
    dx = None
    s, level = 1, 0
    while s < block:
        if level_masks_ref is None:
            lower_left = jnp.logical_and((r // (2 * s)) == (c // (2 * s)),
                                         jnp.logical_and((r // s) % 2 == 1, (c // s) % 2 == 0))
            b = jnp.where(lower_left, a, 0.0)
        else:
            b = a * level_masks_ref[level]
        if dx is None:
            dx = -b
        else:
            m = b + _dot(dx, b)
            dx = dx - m - _dot(m, dx)
        s *= 2
        level += 1
    return dx


def _gdn_kernel(q_ref, k_ref, v_ref, z_ref, sm_ref, conv0_ref, s0_ref, cw_ref, vec_ref, nw_ref,
                o_ref, conv1_ref, s1_ref, buf, smbuf, sst, *, cs, l_blk, n_chunks):
    c_idx = pl.program_id(1)

    @pl.when(c_idx == 0)
    def _():
        buf[5:8, :] = conv0_ref[0]
        sst[...] = s0_ref[0]

    _stage_rows(buf.at[:, 0:GROUP], q_ref, l_blk, 8)
    _stage_rows(buf.at[:, GROUP:2 * GROUP], k_ref, l_blk, 8)
    _stage_rows(buf.at[:, 2 * GROUP:3 * GROUP], v_ref, l_blk, 8)
    _stage_rows(smbuf, sm_ref, l_blk)

    qkv = _silu(_causal_conv(buf, cw_ref, cs))
    conv_tail = buf[5 + l_blk:8 + l_blk, :]
    buf[5:8, :] = conv_tail
    q = qkv[:, 0:GROUP]
    k = qkv[:, GROUP:2 * GROUP]
    v = qkv[:, 2 * GROUP:3 * GROUP]
    blocks = _head_blocks(GROUP, HEAD_DIM)
    q = q * lax.rsqrt(_dot_f32(q * q, blocks) + EPS) * (HEAD_DIM ** -0.5)
    k = k * lax.rsqrt(_dot_f32(k * k, blocks) + EPS)

    sm = smbuf[...]
    beta = _sigmoid(sm)
    log_g = -jnp.exp(vec_ref[1:2, :]) * _softplus(sm + vec_ref[0:1, :])
    if l_blk < cs:
        rows = lax.broadcasted_iota(jnp.int32, (cs, LANES), 0)
        beta = jnp.where(rows < l_blk, beta, 0.0)
        log_g = jnp.where(rows < l_blk, log_g, 0.0)
    gam = _dot_f32(_tril(cs).astype(F32), log_g)
    beta_w = _dot_f32(beta, _head_expand(LANES, GROUP, HEAD_DIM, HEADS))
    gam_w = _dot_f32(gam, _head_expand(LANES, GROUP, HEAD_DIM, 2 * HEADS))
    g_last = gam_w[cs - 1:cs, :]
    e_gam = jnp.exp(gam_w)
    kb = k * beta_w
    vb = v * beta_w
    kbg = kb * e_gam
    qg = q * e_gam
    k_dec = k * jnp.exp(g_last - gam_w)
    e_last = jnp.exp(g_last)
    gam_t = gam.T
    incl = _tril(cs)
    strict = _tril(cs, strict=True)

    os_ = []
    for h in range(HEADS):
        hs = slice(h * HEAD_DIM, (h + 1) * HEAD_DIM)
        r = 2 * HEADS + h
        st = sst[h]
        dec_incl = jnp.exp(jnp.where(incl, gam[:, r:r + 1] - gam_t[r:r + 1, :], NEG_INF))
        a_mat = jnp.where(strict, _dot_nt(kb[:, hs], k[:, hs]) * dec_incl, 0.0)
        tx = _unit_lower_inverse_minus_eye(a_mat)
        u = vb[:, hs] + _dot(tx, vb[:, hs])
        w = kbg[:, hs] + _dot(tx, kbg[:, hs])
        v_new = u - _dot(w, st)
        qk = _dot_nt(q[:, hs], k[:, hs]) * dec_incl
        os_.append(_dot(qk, v_new) + _dot(qg[:, hs], st))
        sst[h] = e_last[:, h * HEAD_DIM:h * HEAD_DIM + 1] * st + _dot_tn(k_dec[:, hs], v_new)
    o = jnp.concatenate(os_, axis=-1)
    o_ref[0] = _head_rms_gate(o, z_ref[0], nw_ref[...], l_blk)

    @pl.when(c_idx == n_chunks - 1)
    def _():
        conv1_ref[0] = conv_tail
        s1_ref[0] = sst[...]


def gdn_mixer(proj, conv0, s0, conv_w, a_log, dt_bias, norm_w):
    b, L, _ = proj.shape
    cs = LIN_CHUNK
    l_blk = min(cs, L)
    n_chunks = L // l_blk
    vec = jnp.zeros((8, LANES), F32)
    vec = vec.at[0, 2 * HEADS:3 * HEADS].set(dt_bias).at[1, 2 * HEADS:3 * HEADS].set(a_log)
    col = lambda k: pl.BlockSpec((1, l_blk, GROUP), lambda i, j, k=k: (i, j, k))
    per_b = lambda shp: pl.BlockSpec((1,) + shp, lambda i, j: (i,) + (0,) * len(shp))
    c3 = 3 * GROUP
    return pl.pallas_call(
        functools.partial(_gdn_kernel, cs=cs, l_blk=l_blk, n_chunks=n_chunks),
        grid=(b, n_chunks),
        in_specs=[col(COL_GDN), col(COL_GDN + 1), col(COL_GDN + 2), col(COL_GDN + 3),
                  pl.BlockSpec((1, l_blk, LANES), lambda i, j: (i, j, COL_SMALL)),
                  per_b((CONV_W - 1, c3)), per_b((HEADS, HEAD_DIM, HEAD_DIM)),
                  _resident((CONV_W, c3)), _resident((8, LANES)), _resident((1, GROUP))],
        out_specs=[pl.BlockSpec((1, l_blk, GROUP), lambda i, j: (i, j, 0)),
                   per_b((CONV_W - 1, c3)), per_b((HEADS, HEAD_DIM, HEAD_DIM))],
        out_shape=[jax.ShapeDtypeStruct((b, L, GROUP), F32),
                   jax.ShapeDtypeStruct((b, CONV_W - 1, c3), F32),
                   jax.ShapeDtypeStruct((b, HEADS, HEAD_DIM, HEAD_DIM), F32)],
        scratch_shapes=[pltpu.VMEM((cs + 8, c3), F32), pltpu.VMEM((cs, LANES), F32),
                        pltpu.VMEM((HEADS, HEAD_DIM, HEAD_DIM), F32)],
        compiler_params=_params("parallel", "arbitrary"),
        name="gdn_mixer",
    )(proj, proj, proj, proj, proj, conv0, s0, conv_w, vec,
      jnp.tile(norm_w, HEADS).reshape(1, GROUP))


GDN_PREP_CHUNKS = 4


def _gdn_prep_kernel(q_ref, k_ref, v_ref, pq_ref, pk_ref, pv_ref, sm_ref, conv0_ref, cw_ref, vec_ref,
                     lvl_ref, u_ref, w_ref, qg_ref, kd_ref, qk_ref, aux_ref, buf, *, cs, cps, l_valid):
    j = pl.program_id(1)
    rows = cs * cps

    if l_valid is None:
        @pl.when(j == 0)
        def _():
            buf[5:8, :] = conv0_ref[0]

        @pl.when(j > 0)
        def _():
            for i, ref in enumerate((pq_ref, pk_ref, pv_ref)):
                buf[5:8, i * GROUP:(i + 1) * GROUP] = ref[0, SUBLANES - 3:SUBLANES, :]

        for i, ref in enumerate((q_ref, k_ref, v_ref)):
            buf[8:8 + rows, i * GROUP:(i + 1) * GROUP] = ref[0]
        qkv = _silu(_causal_conv(buf, cw_ref, rows))
    else:
        stride = cs + SUBLANES
        parts = []
        for n in range(cps):
            buf[n * stride + 5:n * stride + 8, :] = conv0_ref[0, n * (CONV_W - 1):(n + 1) * (CONV_W - 1), :]
            for i, ref in enumerate((q_ref, k_ref, v_ref)):
                buf[n * stride + 8:n * stride + 8 + cs, i * GROUP:(i + 1) * GROUP] = ref[0, n * cs:(n + 1) * cs, :]
            parts.append(_causal_conv(buf, cw_ref, cs, n * stride))
        qkv = _silu(jnp.concatenate(parts, axis=0))
    q = qkv[:, 0:GROUP]
    k = qkv[:, GROUP:2 * GROUP]
    v = qkv[:, 2 * GROUP:3 * GROUP]
    blocks = _head_blocks(GROUP, HEAD_DIM)
    q = q * lax.rsqrt(_dot_split(q * q, blocks, 2) + EPS) * (HEAD_DIM ** -0.5)
    k = k * lax.rsqrt(_dot_split(k * k, blocks, 2) + EPS)

    sm = sm_ref[0]
    beta = _sigmoid(sm)
    log_g = -jnp.exp(vec_ref[1:2, :]) * _softplus(sm + vec_ref[0:1, :])
    if l_valid is not None:
        valid = lax.broadcasted_iota(jnp.int32, (rows, LANES), 0) % cs < l_valid
        beta = jnp.where(valid, beta, 0.0)
        log_g = jnp.where(valid, log_g, 0.0)
    r = lax.broadcasted_iota(jnp.int32, (rows, rows), 0)
    c = lax.broadcasted_iota(jnp.int32, (rows, rows), 1)
    chunk_tril = jnp.logical_and(r >= c, (r // cs) == (c // cs)).astype(F32)
    gam = _dot_split(log_g, chunk_tril, 3, m01_left=True)
    beta_w = _dot_split(beta, _head_expand(LANES, GROUP, HEAD_DIM, HEADS), 3)
    gam_w = _dot_split(gam, _head_expand(LANES, GROUP, HEAD_DIM, 2 * HEADS), 3)
    g_last = jnp.concatenate(
        [jnp.broadcast_to(gam_w[(i + 1) * cs - 1:(i + 1) * cs, :], (cs, GROUP)) for i in range(cps)], axis=0)
    e_gam = jnp.exp(gam_w)
    kb = k * beta_w
    vb = v * beta_w
    kbg = kb * e_gam
    qg_ref[0] = q * e_gam
    kd_ref[0] = k * jnp.exp(g_last - gam_w)
    for i in range(cps):
        aux_ref[0, i * SUBLANES:(i + 1) * SUBLANES, :] = jnp.exp(g_last[i * cs:i * cs + SUBLANES, :])
    gam_t = gam.T
    incl = chunk_tril > 0.0

    us, ws, qks = [], [], []
    for h in range(HEADS):
        hs = slice(h * HEAD_DIM, (h + 1) * HEAD_DIM)
        lane = 2 * HEADS + h
        dec_incl = jnp.exp(jnp.where(incl, gam[:, lane:lane + 1] - gam_t[lane:lane + 1, :], NEG_INF))
        a_mat = _dot_nt(kb[:, hs], k[:, hs]) * dec_incl
        dx = _unit_lower_inverse_minus_eye(a_mat, cs, lvl_ref)
        uw = _dot(dx, jnp.concatenate([vb[:, hs], kbg[:, hs]], axis=-1))
        us.append(vb[:, hs] + uw[:, 0:HEAD_DIM])
        ws.append(kbg[:, hs] + uw[:, HEAD_DIM:2 * HEAD_DIM])
        qk = _dot_nt(q[:, hs], k[:, hs]) * dec_incl
        qks.append(jnp.concatenate([qk[i * cs:(i + 1) * cs, i * cs:(i + 1) * cs] for i in range(cps)], axis=0))
    u_ref[0] = jnp.concatenate(us, axis=-1)
    w_ref[0] = jnp.concatenate(ws, axis=-1)
    qk_ref[0] = jnp.concatenate(qks, axis=-1)


def _gdn_scan_kernel(u_ref, w_ref, qg_ref, kd_ref, qk_ref, aux_ref, z_ref, s0_ref, nw_ref,
                     o_ref, s1_ref, sst, *, nb, n_chunks):
    c_idx = pl.program_id(1)
    blocks = _head_blocks(GROUP, HEAD_DIM)

    @pl.when(c_idx == 0)
    def _():
        sst[...] = jnp.zeros(sst.shape, F32)
        for bi in range(nb):
            for h in range(HEADS):
                hs = slice(h * HEAD_DIM, (h + 1) * HEAD_DIM)
                sst[bi, hs, hs] = s0_ref[bi, h]

    for bi in range(nb):
        st = sst[bi]
        v_new = u_ref[bi] - _dot(w_ref[bi], st)
        v_bd = jnp.concatenate([v_new] * HEADS, axis=0) * blocks
        o = _dot(qk_ref[bi], v_bd) + _dot(qg_ref[bi], st)
        sst[bi] = aux_ref[bi, 0:1, :] * st + _dot_tn(kd_ref[bi], v_new) * blocks
        o_ref[bi] = _head_rms_gate(o, z_ref[bi], nw_ref[...], o.shape[0])

    @pl.when(c_idx == n_chunks - 1)
    def _():
        for bi in range(nb):
            for h in range(HEADS):
                hs = slice(h * HEAD_DIM, (h + 1) * HEAD_DIM)
                s1_ref[bi, h] = sst[bi, hs, hs]


def _gdn_prep_call(proj, conv0, conv_w, a_log, dt_bias, l_valid):
    g, L, _ = proj.shape
    cs = LIN_CHUNK
    cps = GDN_PREP_CHUNKS
    rows = cs * cps
    assert L % rows == 0 and (l_valid is None or L == rows)
    vec = jnp.zeros((8, LANES), F32)
    vec = vec.at[0, 2 * HEADS:3 * HEADS].set(dt_bias).at[1, 2 * HEADS:3 * HEADS].set(a_log)
    c3 = 3 * GROUP
    col = lambda k: pl.BlockSpec((1, rows, GROUP), lambda i, j, k=k: (i, j, k))
    prev = lambda k: pl.BlockSpec(
        (1, SUBLANES, GROUP), lambda i, j, k=k: (i, jnp.maximum(j * (rows // SUBLANES) - 1, 0), k))
    out = pl.BlockSpec((1, rows, GROUP), lambda i, j: (i, j, 0))
    wide = jax.ShapeDtypeStruct((g, L, GROUP), F32)
    level_masks = jnp.asarray(_doubling_level_masks(rows, cs))
    return pl.pallas_call(
        functools.partial(_gdn_prep_kernel, cs=cs, cps=cps, l_valid=l_valid),
        grid=(g, L // rows),
        in_specs=[col(COL_GDN), col(COL_GDN + 1), col(COL_GDN + 2),
                  prev(COL_GDN), prev(COL_GDN + 1), prev(COL_GDN + 2),
                  pl.BlockSpec((1, rows, LANES), lambda i, j: (i, j, COL_SMALL)),
                  pl.BlockSpec((1,) + conv0.shape[1:], lambda i, j: (i, 0, 0)),
                  _resident((CONV_W, c3)), _resident((8, LANES)), _resident(level_masks.shape)],
        out_specs=[out, out, out, out, out,
                   pl.BlockSpec((1, cps * SUBLANES, GROUP), lambda i, j: (i, j, 0))],
        out_shape=[wide, wide, wide, wide, wide,
                   jax.ShapeDtypeStruct((g, L // cs * SUBLANES, GROUP), F32)],
        scratch_shapes=[pltpu.VMEM((cps * (cs + SUBLANES), c3), F32)],
        compiler_params=_params("parallel", "parallel"),
        name="gdn_prep",
    )(proj, proj, proj, proj, proj, proj, proj, conv0, conv_w, vec, level_masks)


GDN_SCAN_SEQS = 4


def _gdn_scan_call(prep, proj, s0, norm_w):
    b, L, _ = proj.shape
    cs = LIN_CHUNK
    nb = GDN_SCAN_SEQS
    n_chunks = L // cs
    assert b % nb == 0 and L % cs == 0
    blk = pl.BlockSpec((nb, cs, GROUP), lambda i, c: (i, c, 0))
    state = pl.BlockSpec((nb, HEADS, HEAD_DIM, HEAD_DIM), lambda i, c: (i, 0, 0, 0))
    return pl.pallas_call(
        functools.partial(_gdn_scan_kernel, nb=nb, n_chunks=n_chunks),
        grid=(b // nb, n_chunks),
        in_specs=[blk, blk, blk, blk, blk,
                  pl.BlockSpec((nb, SUBLANES, GROUP), lambda i, c: (i, c, 0)),
                  pl.BlockSpec((nb, cs, GROUP), lambda i, c: (i, c, COL_GDN + 3)),
                  state, _resident((1, GROUP))],
        out_specs=[blk, state],
        out_shape=[jax.ShapeDtypeStruct((b, L, GROUP), F32),
                   jax.ShapeDtypeStruct((b, HEADS, HEAD_DIM, HEAD_DIM), F32)],
        scratch_shapes=[pltpu.VMEM((nb, GROUP, GROUP), F32)],
        compiler_params=_params("parallel", "arbitrary"),
        name="gdn_scan",
    )(*prep, proj, s0, jnp.tile(norm_w, HEADS).reshape(1, GROUP))


def gdn_mixer_long(proj, conv0, s0, conv_w, a_log, dt_bias, norm_w):
    L = proj.shape[1]
    prep = _gdn_prep_call(proj, conv0, conv_w, a_log, dt_bias, None)
    o, s1 = _gdn_scan_call(prep, proj, s0, norm_w)
    conv1 = proj[:, L - (CONV_W - 1):, COL_GDN * GROUP:(COL_GDN + 3) * GROUP]
    return o, conv1, s1


def gdn_mixer_short(proj, conv0, s0, conv_w, a_log, dt_bias, norm_w):
    b, L, cols = proj.shape
    cs = LIN_CHUNK
    cps = GDN_PREP_CHUNKS
    assert CONV_W - 1 <= L <= cs and b % cps == 0
    padded = jnp.pad(proj, ((0, 0), (0, cs - L), (0, 0)))
    prep = _gdn_prep_call(padded.reshape(b // cps, cps * cs, cols),
                          conv0.reshape(b // cps, cps * (CONV_W - 1), 3 * GROUP), conv_w, a_log, dt_bias, L)
    prep = [p.reshape(b, -1, GROUP) for p in prep]
    o, s1 = _gdn_scan_call(prep, padded, s0, norm_w)
    conv1 = proj[:, L - (CONV_W - 1):, COL_GDN * GROUP:(COL_GDN + 3) * GROUP]
    return o[:, :L], conv1, s1


def _moba_prompt_kernel(q_ref, k_ref, v_ref, o_ref, ks, vts, kmean, sel_t, o_t, q_tb, alibi, ml,
                        s_buf, p_buf, *, nb):
    blk = MOBA_BLOCK
    qi = pl.program_id(1)

    @pl.when(qi == 0)
    def _():
        for j in range(nb):
            kj = k_ref[0, j * blk:(j + 1) * blk, :]
            kmean[j:j + 1, :] = jnp.mean(kj, axis=0, keepdims=True)
            kjb = kj.astype(BF16)
            for h in range(HEADS):
                ks[h, j] = kjb[:, h * HEAD_DIM:(h + 1) * HEAD_DIM]
            vts[j] = v_ref[0, j * blk:(j + 1) * blk, :].T.astype(BF16)

    q_t = (q_ref[0] * (HEAD_DIM ** -0.5)).T
    q_tb[...] = (q_t * LOG2_E).astype(BF16)
    blk_row = lax.broadcasted_iota(jnp.int32, (nb, blk), 0)
    rel = (lax.broadcasted_iota(jnp.int32, (blk, blk), 1)
           - lax.broadcasted_iota(jnp.int32, (blk, blk), 0)).astype(F32)
    for h in range(HEADS):
        alibi[h] = rel * (-ALIBI_SLOPES[h] * LOG2_E)

    for h in range(HEADS):
        hs = slice(h * HEAD_DIM, (h + 1) * HEAD_DIM)
        slope = ALIBI_SLOPES[h]
        gate = jnp.where(blk_row < qi, _dot_f32(kmean[:, hs], q_t[hs, :]), NEG_INF)
        sel = jnp.full((nb, blk), NEG_INF, F32)
        for _ in range(MOBA_TOPK):
            top = jnp.max(gate, axis=0, keepdims=True)
            first = jnp.min(jnp.where(gate == top, blk_row, nb), axis=0, keepdims=True)
            pick = blk_row == first
            sel = jnp.where(jnp.logical_and(pick, blk_row < qi), 0.0, sel)
            gate = jnp.where(pick, NEG_INF, gate)
        sel_t[h] = sel
        ml[h, 0:1, :] = jnp.full((1, blk), NEG_INF, F32)
        ml[h, 1:2, :] = jnp.zeros((1, blk), F32)
    o_t[...] = jnp.zeros(o_t.shape, F32)

    def block_step(j, own):
        for h in range(HEADS):
            hs = slice(h * HEAD_DIM, (h + 1) * HEAD_DIM)
            s_buf[h] = _dot(ks[h, j], q_tb[hs, :])
        for h in range(HEADS):
            if own:
                causal = (lax.broadcasted_iota(jnp.int32, (blk, blk), 1)
                          >= lax.broadcasted_iota(jnp.int32, (blk, blk), 0))
                s = jnp.where(causal, s_buf[h] + alibi[h], NEG_INF)
            else:
                off = (qi - j).astype(F32) * (ALIBI_SLOPES[h] * LOG2_E * blk)
                s = s_buf[h] + alibi[h] + (sel_t[h, pl.ds(j, 1), :] - off)
            m = ml[h, 0:1, :]
            m_new = jnp.maximum(m, jnp.max(s, axis=0, keepdims=True))
            p = jnp.exp2(s - m_new)
            p_buf[h] = p.astype(BF16)
            alpha = jnp.exp2(m - m_new)
            ml[h, 0:1, :] = m_new
            ml[h, 1:2, :] = alpha * ml[h, 1:2, :] + jnp.sum(p, axis=0, keepdims=True)
            ml[h, 2:3, :] = alpha
        for h in range(HEADS):
            hs = slice(h * HEAD_DIM, (h + 1) * HEAD_DIM)
            o_t[hs, :] = (ml[h, 2:3, :] * o_t[hs, :]
                          + jnp.dot(vts[j, hs, :], p_buf[h], preferred_element_type=F32))

    block_step(qi, True)

    def body(j, carry):
        block_step(j, False)
        return carry

    lax.fori_loop(0, qi, body, 0)
    for h in range(HEADS):
        hs = slice(h * HEAD_DIM, (h + 1) * HEAD_DIM)
        o_t[hs, :] = o_t[hs, :] / ml[h, 1:2, :]
    o_ref[0] = o_t[...].T


def moba_prompt(proj):
    b, S, _ = proj.shape
    blk = MOBA_BLOCK
    nb = S // blk
    seq = lambda k: pl.BlockSpec((1, S, GROUP), lambda i, j, k=k: (i, 0, k))
    return pl.pallas_call(
        functools.partial(_moba_prompt_kernel, nb=nb),
        grid=(b, nb),
        in_specs=[pl.BlockSpec((1, blk, GROUP), lambda i, j: (i, j, COL_MOBA)),
                  seq(COL_MOBA + 1), seq(COL_MOBA + 2)],
        out_specs=pl.BlockSpec((1, blk, GROUP), lambda i, j: (i, j, 0)),
        out_shape=jax.ShapeDtypeStruct((b, S, GROUP), F32),
        scratch_shapes=[pltpu.VMEM((HEADS, nb, blk, HEAD_DIM), BF16),
                        pltpu.VMEM((nb, GROUP, blk), BF16),
                        pltpu.VMEM((nb, GROUP), F32),
                        pltpu.VMEM((HEADS, nb, blk), F32),
                        pltpu.VMEM((GROUP, blk), F32),
                        pltpu.VMEM((GROUP, blk), BF16),
                        pltpu.VMEM((HEADS, blk, blk), F32),
                        pltpu.VMEM((HEADS, SUBLANES, blk), F32),
                        pltpu.VMEM((HEADS, blk, blk), F32),
                        pltpu.VMEM((HEADS, blk, blk), BF16)],
        compiler_params=_params("parallel", "arbitrary"),
        name="moba_prompt",
    )(proj, proj, proj)


PAGES_PER_BLOCK = MOBA_BLOCK // PAGE_SIZE
SELECT_PAGE_BUFFERS = 64
SELECT_UNROLL = 4


def _moba_select_kernel(pt_ref, q_ref, kc_ref, idx_ref, pages, kmean_t, q8, sems, *,
                        layer, n_pages, t):
    b = pl.program_id(0)
    nbuf = SELECT_PAGE_BUFFERS
    n_blocks = n_pages // PAGES_PER_BLOCK

    def page_copy(p, slot):
        return pltpu.make_async_copy(kc_ref.at[layer, pt_ref[b, p]], pages.at[slot], sems.at[slot])

    for s in range(nbuf):
        page_copy(s, s).start()

    kmean_t[...] = jnp.zeros(kmean_t.shape, F32)
    blk_of_lane = lax.broadcasted_iota(jnp.int32, kmean_t.shape, 1)

    pages_per_trip = SELECT_UNROLL * PAGES_PER_BLOCK

    def body(trip, carry):
        p0 = trip * pages_per_trip
        for i in range(pages_per_trip):
            page_copy(p0 + i, (p0 + i) % nbuf).wait()
        km = kmean_t[...]
        for u in range(SELECT_UNROLL):
            tot = jnp.zeros((GROUP, PAGE_SIZE), F32)
            for pp in range(PAGES_PER_BLOCK):
                tot = tot + pages[(p0 + u * PAGES_PER_BLOCK + pp) % nbuf]
            mean = jnp.sum(tot, axis=1, keepdims=True) * (1.0 / MOBA_BLOCK)
            km = jnp.where(blk_of_lane == trip * SELECT_UNROLL + u, mean, km)
        kmean_t[...] = km
        for i in range(pages_per_trip):
            @pl.when(p0 + i + nbuf < n_pages)
            def _(i=i):
                page_copy(p0 + i + nbuf, (p0 + i) % nbuf).start()
        return carry

    lax.fori_loop(0, n_blocks // SELECT_UNROLL, body, 0)

    q8[0:t, :] = q_ref[0]
    q8[t:, :] = jnp.zeros((SUBLANES - t, GROUP), F32)
    blk_lane = lax.broadcasted_iota(jnp.int32, (SUBLANES, n_blocks), 1)
    out_lane = lax.broadcasted_iota(jnp.int32, (SUBLANES, LANES), 1)
    res = jnp.zeros((SUBLANES, LANES), jnp.int32)
    for h in range(HEADS):
        hs = slice(h * HEAD_DIM, (h + 1) * HEAD_DIM)
        gate = _dot_f32(q8[:, hs], kmean_t[hs, 0:n_blocks])
        for r in range(MOBA_TOPK):
            top = jnp.max(gate, axis=1, keepdims=True)
            first = jnp.min(jnp.where(gate == top, blk_lane, n_blocks), axis=1, keepdims=True)
            res = jnp.where(out_lane == h * MOBA_TOPK + r, first, res)
            gate = jnp.where(blk_lane == first, NEG_INF, gate)
    idx_ref[0] = res


def _moba_sample_attn_kernel(pt_ref, idx_ref, q_ref, kn_ref, vn_ref, kc_ref, vc_ref, o_ref,
                             kbuf, vbuf, st8, o8, ksem, vsem, *, layer, past, t):
    b = pl.program_id(0)
    n_seq = pl.num_programs(0)
    n_sel = MOBA_TOPK * MOBA_BLOCK
    pairs = [(tok, h) for h in range(HEADS) for tok in range(t)]

    def block_of(seq, tok, h, r):
        return idx_ref[seq, tok * (HEADS * MOBA_TOPK) + h * MOBA_TOPK + r]

    def copies(seq, i, half):
        tok, h = pairs[i]
        out = []
        for r in range(MOBA_TOPK):
            blk = block_of(seq, tok, h, r)
            for pp in range(PAGES_PER_BLOCK):
                phys = pt_ref[seq, blk * PAGES_PER_BLOCK + pp]
                lanes = pl.ds((r * PAGES_PER_BLOCK + pp) * PAGE_SIZE, PAGE_SIZE)
                out.append(pltpu.make_async_copy(kc_ref.at[layer, phys, h], kbuf.at[half, i, :, lanes],
                                                 ksem.at[half, i]))
                out.append(pltpu.make_async_copy(vc_ref.at[layer, phys, h], vbuf.at[half, i, :, lanes],
                                                 vsem.at[half, i]))
        return out

    def start_all(seq, half):
        for i in range(len(pairs)):
            for c in copies(seq, i, half):
                c.start()

    half = b % 2

    @pl.when(b == 0)
    def _():
        start_all(b, half)

    @pl.when(b + 1 < n_seq)
    def _():
        start_all(b + 1, 1 - half)

    for i, ref in enumerate((q_ref, kn_ref, vn_ref)):
        st8[i, 0:t, :] = ref[0]
        st8[i, t:, :] = jnp.zeros((SUBLANES - t, GROUP), F32)

    lane = lax.broadcasted_iota(jnp.int32, (1, n_sel), 1)
    row = lax.broadcasted_iota(jnp.int32, (SUBLANES, 1), 0)
    for i in range(len(pairs)):
        for c in copies(b, i, half):
            c.wait()
    for i, (tok, h) in enumerate(pairs):
        hs = slice(h * HEAD_DIM, (h + 1) * HEAD_DIM)
        slope = ALIBI_SLOPES[h]
        qrow = st8[0, tok:tok + 1, hs] * (HEAD_DIM ** -0.5)
        s_sel = _dot(jnp.broadcast_to(qrow, (SUBLANES, HEAD_DIM)), kbuf[half, i])[0:1]
        blk = jnp.where(lane < MOBA_BLOCK, block_of(b, tok, h, 0),
                        jnp.where(lane < 2 * MOBA_BLOCK, block_of(b, tok, h, 1), block_of(b, tok, h, 2)))
        pos = blk * MOBA_BLOCK + (lane % MOBA_BLOCK)
        s_sel = s_sel - slope * (past + tok - pos).astype(F32)
        s_own = jnp.sum(st8[1, :, hs] * qrow, axis=1, keepdims=True)
        s_own = jnp.where(row <= tok, s_own - slope * (tok - row).astype(F32), NEG_INF)
        m = jnp.maximum(jnp.max(s_sel, axis=1, keepdims=True), jnp.max(s_own, axis=0, keepdims=True))
        p_sel = jnp.exp(s_sel - m)
        p_own = jnp.exp(s_own - m)
        l = jnp.sum(p_sel, axis=1, keepdims=True) + jnp.sum(p_own, axis=0, keepdims=True)
        o = _dot_nt(jnp.broadcast_to(p_sel, (SUBLANES, n_sel)), vbuf[half, i])[0:1]
        o = o + jnp.sum(p_own * st8[2, :, hs], axis=0, keepdims=True)
        o8[tok:tok + 1, hs] = o / l
    o_ref[0] = o8[0:t, :]


def moba_sample(proj, k_cache, v_cache, page_table, layer):
    db, t, _ = proj.shape
    depth, pool = k_cache.shape[:2]
    n_pages = page_table.shape[1]
    n_blocks = n_pages // PAGES_PER_BLOCK
    past = n_pages * PAGE_SIZE
    assert past % MOBA_BLOCK == 0 and n_pages >= SELECT_PAGE_BUFFERS and t <= SUBLANES
    assert n_blocks % SELECT_UNROLL == 0 and SELECT_PAGE_BUFFERS % (SELECT_UNROLL * PAGES_PER_BLOCK) == 0
    assert MOBA_TOPK <= n_blocks <= LANES
    tok = lambda k: pl.BlockSpec((1, t, GROUP), lambda i, *_: (i, 0, k))
    hbm = pl.BlockSpec(memory_space=pl.ANY)
    n_pairs = t * HEADS
    idx = pl.pallas_call(
        functools.partial(_moba_select_kernel, layer=layer, n_pages=n_pages, t=t),
        grid_spec=pltpu.PrefetchScalarGridSpec(
            num_scalar_prefetch=1, grid=(db,),
            in_specs=[tok(COL_MOBA), hbm],
            out_specs=pl.BlockSpec((1, SUBLANES, LANES), lambda i, *_: (i, 0, 0)),
            scratch_shapes=[pltpu.VMEM((SELECT_PAGE_BUFFERS, GROUP, PAGE_SIZE), F32),
                            pltpu.VMEM((GROUP, LANES), F32),
                            pltpu.VMEM((SUBLANES, GROUP), F32),
                            pltpu.SemaphoreType.DMA((SELECT_PAGE_BUFFERS,))]),
        out_shape=jax.ShapeDtypeStruct((db, SUBLANES, LANES), jnp.int32),
        compiler_params=_params("arbitrary"),
        name="moba_sample_select",
    )(page_table, proj, k_cache.reshape(depth, pool, GROUP, PAGE_SIZE))
    idx = idx[:, :t, :HEADS * MOBA_TOPK].reshape(db, t * HEADS * MOBA_TOPK)
    return pl.pallas_call(
        functools.partial(_moba_sample_attn_kernel, layer=layer, past=past, t=t),
        grid_spec=pltpu.PrefetchScalarGridSpec(
            num_scalar_prefetch=2, grid=(db,),
            in_specs=[tok(COL_MOBA), tok(COL_MOBA + 1), tok(COL_MOBA + 2), hbm, hbm],
            out_specs=pl.BlockSpec((1, t, GROUP), lambda i, *_: (i, 0, 0)),
            scratch_shapes=[pltpu.VMEM((2, n_pairs, HEAD_DIM, MOBA_TOPK * MOBA_BLOCK), F32),
                            pltpu.VMEM((2, n_pairs, HEAD_DIM, MOBA_TOPK * MOBA_BLOCK), F32),
                            pltpu.VMEM((3, SUBLANES, GROUP), F32),
                            pltpu.VMEM((SUBLANES, GROUP), F32),
                            pltpu.SemaphoreType.DMA((2, n_pairs)),
                            pltpu.SemaphoreType.DMA((2, n_pairs))]),
        out_shape=jax.ShapeDtypeStruct((db, t, GROUP), F32),
        compiler_params=_params("arbitrary"),
        name="moba_sample_attn",
    )(page_table, idx, proj, proj, proj, k_cache, v_cache)


def _rearrange_w_in(w):
    d = w.shape[0]
    dt0 = COL_HGRN * GROUP
    ba0 = dt0 + HEADS + (COL_GDN + 4 - COL_HGRN) * GROUP
    wide = jnp.concatenate([w[:, :dt0], w[:, dt0 + HEADS:ba0]], axis=1)
    narrow = jnp.concatenate([w[:, dt0:dt0 + HEADS], w[:, ba0:ba0 + 2 * HEADS],
                              jnp.zeros((d, LANES - N_SMALL), w.dtype)], axis=1)
    return jnp.concatenate([wide, narrow], axis=1)


def _layer(x, mem_k, mem_v, states, lp, layer, attend, prompt):
    b, L, d = x.shape
    ssm_conv0, ssm0, hgrn0, gdn_conv0, gdn0 = states
    w_in = lp['w_in'].astype(BF16) if prompt else lp['w_in']
    proj, k, v = norm_matmul(x.reshape(b * L, d), lp['n_mix_pre'], w_in, min(256, b * L),
                             copy_blocks=(COL_MOBA + 1, COL_MOBA + 2))
    proj = proj.reshape(b, L, -1)
    o_a = attend(proj)
    o_b, ssm_conv1, ssm1 = ssd_mixer(proj, ssm_conv0, ssm0, lp['ssm_conv_w'], lp['ssm_conv_b'],
                                     lp['ssm_dt_bias'], lp['ssm_a_log'], lp['ssm_d'], lp['ssm_norm'])
    hgrn = hgrn_mixer_long if L % (LIN_PREP_CHUNKS * LIN_CHUNK) == 0 else hgrn_mixer
    o_c, hgrn1 = hgrn(proj, jnp.swapaxes(hgrn0, -1, -2), lp['hgrn_lb_raw'], lp['hgrn_norm'], layer)
    gdn = gdn_mixer_long if L % (GDN_PREP_CHUNKS * LIN_CHUNK) == 0 else gdn_mixer_short
    o_d, gdn_conv1, gdn1 = gdn(proj, gdn_conv0, gdn0, lp['gdn_conv_w'], lp['gdn_a_log'],
                               lp['gdn_dt_bias'], lp['gdn_norm'])
    post = post_mixer if prompt else post_mixer_sample
    args = (x, (o_a, o_b, o_c, o_d), mem_k, mem_v, lp['norms'], lp['w_out'], lp['w_xq'], lp['w_xo'],
            lp['w_gu'], lp['w_down'])
    x = post(*args, 256) if prompt else post(*args)
    k = k.reshape(b, L, HEADS, HEAD_DIM)
    v = v.reshape(b, L, HEADS, HEAD_DIM)
    return x, (k, v, ssm_conv1, ssm1, jnp.swapaxes(hgrn1, -1, -2), gdn_conv1, gdn1)


def kernel(x_prompt, x_sample, mem_prompt, cache_moba_k, cache_moba_v, page_table, cache_mem_k, cache_mem_v, state_ssm_conv, state_ssm, state_hgrn, state_gdn_conv, state_gdn, n_mix_pre, n_mix_post, w_in, w_out, ssm_conv_w, ssm_conv_b, ssm_dt_bias, ssm_a_log, ssm_d, ssm_norm, hgrn_lb_raw, hgrn_norm, gdn_conv_w, gdn_a_log, gdn_dt_bias, gdn_norm, n_x_pre, n_x_post, mem_norm, w_xq, w_xkv, w_xo, n_f_pre, n_f_post, w_gu, w_down):
    depth = w_in.shape[0]
    bp, _, d = x_prompt.shape
    db = x_sample.shape[0]
    n_mem = mem_prompt.shape[1]
    kc = jnp.transpose(cache_moba_k, (0, 1, 3, 4, 2))
    vc = jnp.transpose(cache_moba_v, (0, 1, 3, 4, 2))
    zeros_p = (jnp.zeros((bp,) + state_ssm_conv.shape[2:], F32), jnp.zeros((bp,) + state_ssm.shape[2:], F32),
               jnp.zeros((bp,) + state_hgrn.shape[2:], F32), jnp.zeros((bp,) + state_gdn_conv.shape[2:], F32),
               jnp.zeros((bp,) + state_gdn.shape[2:], F32))
    yp, ys = x_prompt, x_sample
    outs_p, outs_s, mem_ks, mem_vs = [], [], [], []
    for l in range(depth):
        norms = jnp.zeros((SUBLANES, d), F32)
        for i, nrm in enumerate((n_mix_post, n_x_pre, n_x_post, n_f_pre, n_f_post)):
            norms = norms.at[i].set(nrm[l])
        lp = {'n_mix_pre': n_mix_pre[l], 'w_in': _rearrange_w_in(w_in[l]), 'norms': norms,
              'w_out': w_out[l].astype(BF16), 'w_xq': w_xq[l].astype(BF16), 'w_xo': w_xo[l].astype(BF16),
              'w_gu': w_gu[l].astype(BF16), 'w_down': w_down[l].astype(BF16),
              'ssm_conv_w': ssm_conv_w[l], 'ssm_conv_b': ssm_conv_b[l], 'ssm_dt_bias': ssm_dt_bias[l],
              'ssm_a_log': ssm_a_log[l], 'ssm_d': ssm_d[l], 'ssm_norm': ssm_norm[l],
              'hgrn_lb_raw': hgrn_lb_raw, 'hgrn_norm': hgrn_norm[l], 'gdn_conv_w': gdn_conv_w[l],
              'gdn_a_log': gdn_a_log[l], 'gdn_dt_bias': gdn_dt_bias[l], 'gdn_norm': gdn_norm[l]}
        mkv, = norm_matmul(mem_prompt.reshape(bp * n_mem, d), mem_norm[l], w_xkv[l].astype(BF16), 256)
        mk = mkv[:, :d].reshape(bp, n_mem, d)
        mv = mkv[:, d:].reshape(bp, n_mem, d)
        yp, st_p = _layer(yp, mk.astype(BF16), mv.astype(BF16), zeros_p, lp, l, moba_prompt, True)
        outs_p.append(st_p)
        mem_ks.append(mk.reshape(bp, n_mem, X_HEADS, d // X_HEADS))
        mem_vs.append(mv.reshape(bp, n_mem, X_HEADS, d // X_HEADS))
        states_s = (state_ssm_conv[l], state_ssm[l], state_hgrn[l], state_gdn_conv[l], state_gdn[l])
        attend_s = functools.partial(moba_sample, k_cache=kc, v_cache=vc, page_table=page_table, layer=l)
        ys, st_s = _layer(ys, cache_mem_k[l].reshape(db, n_mem, d).astype(BF16),
                          cache_mem_v[l].reshape(db, n_mem, d).astype(BF16), states_s, lp, l, attend_s, False)
        outs_s.append(st_s)
    stack = lambda outs, i: jnp.stack([o[i] for o in outs], axis=0)
    return (yp, ys, stack(outs_p, 0), stack(outs_s, 0), stack(outs_p, 1), stack(outs_s, 1),
            jnp.stack(mem_ks, axis=0), jnp.stack(mem_vs, axis=0),
            stack(outs_p, 2), stack(outs_s, 2), stack(outs_p, 3), stack(outs_s, 3),
            stack(outs_p, 4), stack(outs_s, 4), stack(outs_p, 5), stack(outs_s, 5),
            stack(outs_p, 6), stack(outs_s, 6))
```

```python
import functools

import numpy as np
import jax
import jax.numpy as jnp
from jax import lax
from jax.experimental import pallas as pl
from jax.experimental.pallas import tpu as pltpu

F32 = jnp.float32
BF16 = jnp.bfloat16

LANES = 128
SUBLANES = 8
VMEM_LIMIT_BYTES = 56 * 1024 * 1024

GROUP = 256
HEADS = 4
HEAD_DIM = 64
CONV_W = 4
SSM_DSTATE = 128
SSM_CHUNK = 128
LIN_CHUNK = 64
MOBA_BLOCK = 256
MOBA_TOPK = 3
PAGE_SIZE = 128
X_HEADS = 4
N_SMALL = 12
COL_MOBA = 0
COL_SSM_Z = 3
COL_SSM_X = 4
COL_HGRN = 7
COL_GDN = 11
N_WIDE = 15 * GROUP
COL_SMALL = N_WIDE // LANES
EPS = 1e-6
NEG_INF = float("-inf")
LOG2_E = 1.4426950408889634
ALIBI_SLOPES = tuple(2.0 ** (-8.0 * (h + 1) / HEADS) for h in range(HEADS))


def _rms(x, w):
    return x * lax.rsqrt(jnp.mean(x * x, axis=-1, keepdims=True) + EPS) * w


def _sigmoid(x):
    return 1.0 / (1.0 + jnp.exp(-x))


def _silu(x):
    return x * _sigmoid(x)


def _softplus(x):
    return jnp.maximum(x, 0.0) + jnp.log(1.0 + jnp.exp(-jnp.abs(x)))


def _dot(a, b):
    return jnp.dot(a.astype(BF16), b.astype(BF16), preferred_element_type=F32)


def _dot_nt(a, b):
    return lax.dot_general(a.astype(BF16), b.astype(BF16), (((1,), (1,)), ((), ())),
                           preferred_element_type=F32)


def _dot_f32(a, b):
    return jnp.dot(a, b, preferred_element_type=F32, precision=lax.Precision.HIGHEST)


def _params(*sem, flags=None):
    return pltpu.CompilerParams(dimension_semantics=sem, vmem_limit_bytes=VMEM_LIMIT_BYTES, flags=flags)


def _resident(shape):
    return pl.BlockSpec(shape, lambda *_: (0,) * len(shape), pipeline_mode=pl.Buffered(1))


def _norm_matmul_kernel(x_ref, nw_ref, w_ref, o_ref, *, full_precision):
    xn = _rms(x_ref[...], nw_ref[...])
    o_ref[...] = _dot_f32(xn, w_ref[...]) if full_precision else _dot(xn, w_ref[...])


def norm_matmul(x, norm_w, w, tm):
    n, d = x.shape
    c = w.shape[1]
    return pl.pallas_call(
        functools.partial(_norm_matmul_kernel, full_precision=(w.dtype == F32)),
        grid=(n // tm,),
        in_specs=[pl.BlockSpec((tm, d), lambda i: (i, 0)),
                  _resident((1, d)),
                  _resident((d, c))],
        out_specs=pl.BlockSpec((tm, c), lambda i: (i, 0)),
        out_shape=jax.ShapeDtypeStruct((n, c), F32),
        compiler_params=_params("parallel"),
        name="norm_matmul",
    )(x, norm_w.reshape(1, d), w)


POST_MIXER_ROWS = 512

def _mix_out_and_query(x, mixed, nw, wout_ref, wxq_ref):
    x = x + _rms(_dot(mixed, wout_ref[...]), nw[0:1])
    return x, _dot(_rms(x, nw[1:2]), wxq_ref[...])


def _memory_attention(q, head_k, head_v):
    xdh = q.shape[1] // X_HEADS
    heads = []
    for h in range(X_HEADS):
        s = _dot_nt(q[:, h * xdh:(h + 1) * xdh], head_k(h)) * (xdh ** -0.5)
        p = jnp.exp(s - jnp.max(s, axis=-1, keepdims=True))
        heads.append(_dot(p, head_v(h)) / jnp.sum(p, axis=-1, keepdims=True))
    return jnp.concatenate(heads, axis=-1)


def _lane_heads(ref):
    xdh = ref.shape[-1] // X_HEADS
    return lambda h: ref[0, :, h * xdh:(h + 1) * xdh]


def _attn_out_and_ffn(x, att, nw, wxo_ref, wgu_ref, wdown_ref):
    x = x + _rms(_dot(att, wxo_ref[...]), nw[2:3])
    gu = _dot(_rms(x, nw[3:4]), wgu_ref[...])
    hid = gu.shape[1] // 2
    act = _silu(gu[:, :hid]) * gu[:, hid:]
    return x + _rms(_dot(act, wdown_ref[...]), nw[4:5])


def _post_mixer_kernel(x_ref, oa_ref, ob_ref, oc_ref, od_ref, mk_ref, mv_ref, norms_ref, wout_ref,
                       wxq_ref, wxo_ref, wgu_ref, wdown_ref, o_ref):
    nw = norms_ref[...]
    mixed = jnp.concatenate([oa_ref[0], ob_ref[0], oc_ref[0], od_ref[0]], axis=-1)
    x, q = _mix_out_and_query(x_ref[0], mixed, nw, wout_ref, wxq_ref)
    att = _memory_attention(q, _lane_heads(mk_ref), _lane_heads(mv_ref))
    o_ref[0] = _attn_out_and_ffn(x, att, nw, wxo_ref, wgu_ref, wdown_ref)


def post_mixer(x, mix_parts, mk, mv, norms, w_out, w_xq, w_xo, w_gu, w_down, tm):
    b, L, d = x.shape
    m = mk.shape[1]
    row = pl.BlockSpec((1, tm, d), lambda i, j: (i, j, 0))
    part = pl.BlockSpec((1, tm, GROUP), lambda i, j: (i, j, 0))
    mem = pl.BlockSpec((1, m, d), lambda i, j: (i, 0, 0))
    return pl.pallas_call(
        _post_mixer_kernel,
        grid=(b, L // tm),
        in_specs=[row, part, part, part, part, mem, mem, _resident(norms.shape),
                  _resident(w_out.shape), _resident(w_xq.shape), _resident(w_xo.shape),
                  _resident(w_gu.shape), _resident(w_down.shape)],
        out_specs=row,
        out_shape=jax.ShapeDtypeStruct((b, L, d), F32),
        compiler_params=_params("parallel", "parallel"),
        name="post_mixer",
    )(x, *mix_parts, mk, mv, norms, w_out, w_xq, w_xo, w_gu, w_down)


def _sample_pre_kernel(x_ref, oa_ref, ob_ref, oc_ref, od_ref, norms_ref, wout_ref, wxq_ref,
                       x1_ref, q_ref):
    mixed = jnp.concatenate([oa_ref[...], ob_ref[...], oc_ref[...], od_ref[...]], axis=-1)
    x1, q = _mix_out_and_query(x_ref[...], mixed, norms_ref[...], wout_ref, wxq_ref)
    x1_ref[...] = x1
    q_ref[...] = q


def _sample_attn_kernel(q_ref, mk_ref, mv_ref, o_ref, q8, *, t):
    q8[0:t, :] = q_ref[0]
    q8[t:, :] = jnp.zeros((q8.shape[0] - t, q8.shape[1]), F32)
    o_ref[0] = _memory_attention(q8[...], _lane_heads(mk_ref), _lane_heads(mv_ref))[0:t]


def _sample_post_kernel(x_ref, att_ref, norms_ref, wxo_ref, wgu_ref, wdown_ref, o_ref):
    o_ref[...] = _attn_out_and_ffn(x_ref[...], att_ref[...], norms_ref[...], wxo_ref, wgu_ref,
                                   wdown_ref)


def post_mixer_sample(x, mix_parts, mk, mv, norms, w_out, w_xq, w_xo, w_gu, w_down):
    b, t, d = x.shape
    n = b * t
    m = mk.shape[1]
    flat = lambda a: a.reshape(n, a.shape[-1])
    full = lambda shp: pl.BlockSpec(shp, lambda i: (0,) * len(shp))
    x1, q = pl.pallas_call(
        _sample_pre_kernel,
        grid=(1,),
        in_specs=[full((n, d))] + [full((n, GROUP))] * 4
                 + [full(norms.shape), _resident(w_out.shape), _resident(w_xq.shape)],
        out_specs=[full((n, d)), full((n, d))],
        out_shape=[jax.ShapeDtypeStruct((n, d), F32)] * 2,
        compiler_params=_params("arbitrary"),
        name="sample_pre",
    )(flat(x), *[flat(p) for p in mix_parts], norms, w_out, w_xq)
    seq = pl.BlockSpec((1, t, d), lambda i: (i, 0, 0))
    mem = pl.BlockSpec((1, m, d), lambda i: (i, 0, 0))
    att = pl.pallas_call(
        functools.partial(_sample_attn_kernel, t=t),
        grid=(b,),
        in_specs=[seq, mem, mem],
        out_specs=seq,
        out_shape=jax.ShapeDtypeStruct((b, t, d), F32),
        scratch_shapes=[pltpu.VMEM((SUBLANES, d), F32)],
        compiler_params=_params("parallel"),
        name="sample_attn",
    )(q.reshape(b, t, d), mk, mv)
    out = pl.pallas_call(
        _sample_post_kernel,
        grid=(1,),
        in_specs=[full((n, d)), full((n, d)), full(norms.shape), _resident(w_xo.shape),
                  _resident(w_gu.shape), _resident(w_down.shape)],
        out_specs=full((n, d)),
        out_shape=jax.ShapeDtypeStruct((n, d), F32),
        compiler_params=_params("arbitrary"),
        name="sample_post",
    )(x1, flat(att), norms, w_xo, w_gu, w_down)
    return out.reshape(b, t, d)


def _tril(n, strict=False):
    r = lax.broadcasted_iota(jnp.int32, (n, n), 0)
    c = lax.broadcasted_iota(jnp.int32, (n, n), 1)
    return (r > c) if strict else (r >= c)


def _head_expand(n_rows, n_cols, width, offset=0):
    r = lax.broadcasted_iota(jnp.int32, (n_rows, n_cols), 0)
    c = lax.broadcasted_iota(jnp.int32, (n_rows, n_cols), 1)
    return (r == (c // width) + offset).astype(F32)


def _stage_rows(dst, src_ref, l_blk, row0=0):
    n = dst.shape[0]
    dst[row0:row0 + l_blk, :] = src_ref[0]
    if row0 + l_blk < n:
        dst[row0 + l_blk:n, :] = jnp.zeros((n - row0 - l_blk, dst.shape[1]), F32)


def _causal_conv(buf, cw_ref, cs, base=0):
    out = cw_ref[0:1, :] * buf[base + 5:base + 5 + cs, :]
    for j in range(1, CONV_W):
        out = out + cw_ref[j:j + 1, :] * buf[base + 5 + j:base + 5 + j + cs, :]
    return out


def _ssd_kernel(z_ref, x_ref, b_ref, c_ref, sm_ref, conv0_ref, h0_ref, cw_ref, cb_ref, vec_ref,
                o_ref, conv1_ref, h1_ref, buf, smbuf, hst, *, cs, l_blk, n_chunks):
    c_idx = pl.program_id(1)

    @pl.when(c_idx == 0)
    def _():
        buf[5:8, :] = conv0_ref[0]
        hst[...] = h0_ref[0]

    _stage_rows(buf.at[:, 0:GROUP], x_ref, l_blk, 8)
    _stage_rows(buf.at[:, GROUP:2 * GROUP], b_ref, l_blk, 8)
    _stage_rows(buf.at[:, 2 * GROUP:3 * GROUP], c_ref, l_blk, 8)
    _stage_rows(smbuf, sm_ref, l_blk)

    xbc = _silu(_causal_conv(buf, cw_ref, cs) + cb_ref[...])
    conv_tail = buf[5 + l_blk:8 + l_blk, :]
    buf[5:8, :] = conv_tail
    xs = xbc[:, 0:GROUP]
    bm = xbc[:, GROUP:2 * GROUP]
    cm = xbc[:, 2 * GROUP:3 * GROUP]

    dt = _softplus(smbuf[...] + vec_ref[0:1, 0:LANES])
    if l_blk < cs:
        rows = lax.broadcasted_iota(jnp.int32, (cs, LANES), 0)
        dt = jnp.where(rows < l_blk, dt, 0.0)
    a = dt * (-jnp.exp(vec_ref[1:2, 0:LANES]))
    acum = _dot_f32(_tril(cs).astype(F32), a)
    expand = _head_expand(LANES, GROUP, HEAD_DIM)
    dt_w = _dot_f32(dt, expand)
    acum_w = _dot_f32(acum, expand)
    a_last = acum_w[cs - 1:cs, :]
    xdt = xs * dt_w
    e_acum = jnp.exp(acum_w)
    xw = xdt * jnp.exp(a_last - acum_w)
    e_last = jnp.exp(a_last)
    acum_t = acum.T
    causal = _tril(cs)

    ys = []
    for h in range(HEADS):
        g = h // (HEADS // 2)
        hs = slice(h * HEAD_DIM, (h + 1) * HEAD_DIM)
        gs = slice(g * SSM_DSTATE, (g + 1) * SSM_DSTATE)
        st = hst[h]
        dec = jnp.exp(jnp.where(causal, acum[:, h:h + 1] - acum_t[h:h + 1, :], NEG_INF))
        y = _dot(_dot_nt(cm[:, gs], bm[:, gs]) * dec, xdt[:, hs])
        y = y + _dot_nt(cm[:, gs], st) * e_acum[:, hs]
        ys.append(y)
        hst[h] = e_last[:, h * HEAD_DIM:h * HEAD_DIM + 1] * st + _dot(xw[:, hs].T, bm[:, gs])
    y = jnp.concatenate(ys, axis=-1) + vec_ref[2:3, :] * xs
    o_ref[0] = _rms(y[0:l_blk] * _silu(z_ref[0]), vec_ref[3:4, :])

    @pl.when(c_idx == n_chunks - 1)
    def _():
        conv1_ref[0] = conv_tail
        h1_ref[0] = hst[...]


def ssd_mixer(proj, conv0, h0, conv_w, conv_b, dt_bias, a_log, d_skip, norm_w):
    b, L, _ = proj.shape
    cs = SSM_CHUNK
    l_blk = min(cs, L)
    n_chunks = L // l_blk
    vec = jnp.zeros((8, GROUP), F32)
    vec = vec.at[0, :HEADS].set(dt_bias).at[1, :HEADS].set(a_log)
    vec = vec.at[2].set(jnp.repeat(d_skip, HEAD_DIM)).at[3].set(norm_w)
    col = lambda k: pl.BlockSpec((1, l_blk, GROUP), lambda i, j, k=k: (i, j, k))
    per_b = lambda shp: pl.BlockSpec((1,) + shp, lambda i, j: (i,) + (0,) * len(shp))
    c3 = 3 * GROUP
    return pl.pallas_call(
        functools.partial(_ssd_kernel, cs=cs, l_blk=l_blk, n_chunks=n_chunks),
        grid=(b, n_chunks),
        in_specs=[col(COL_SSM_Z), col(COL_SSM_X), col(COL_SSM_X + 1), col(COL_SSM_X + 2),
                  pl.BlockSpec((1, l_blk, LANES), lambda i, j: (i, j, COL_SMALL)),
                  per_b((CONV_W - 1, c3)), per_b((HEADS, HEAD_DIM, SSM_DSTATE)),
                  _resident((CONV_W, c3)), _resident((1, c3)), _resident((8, GROUP))],
        out_specs=[pl.BlockSpec((1, l_blk, GROUP), lambda i, j: (i, j, 0)),
                   per_b((CONV_W - 1, c3)), per_b((HEADS, HEAD_DIM, SSM_DSTATE))],
        out_shape=[jax.ShapeDtypeStruct((b, L, GROUP), F32),
                   jax.ShapeDtypeStruct((b, CONV_W - 1, c3), F32),
                   jax.ShapeDtypeStruct((b, HEADS, HEAD_DIM, SSM_DSTATE), F32)],
        scratch_shapes=[pltpu.VMEM((cs + 8, c3), F32), pltpu.VMEM((cs, LANES), F32),
                        pltpu.VMEM((HEADS, HEAD_DIM, SSM_DSTATE), F32)],
        compiler_params=_params("parallel", "arbitrary"),
        name="ssd_mixer",
    )(proj, proj, proj, proj, proj, conv0, h0, conv_w, conv_b.reshape(1, c3), vec)


def _dot_tn(a, b):
    return lax.dot_general(a.astype(BF16), b.astype(BF16), (((0,), (0,)), ((), ())),
                           preferred_element_type=F32)


def _head_blocks(n, width):
    r = lax.broadcasted_iota(jnp.int32, (n, n), 0)
    c = lax.broadcasted_iota(jnp.int32, (n, n), 1)
    return ((r // width) == (c // width)).astype(F32)


def _dot_split(x, m01, passes, m01_left=False):
    m = m01.astype(BF16)
    acc = None
    for _ in range(passes):
        hi = x.astype(BF16)
        part = jnp.dot(m, hi, preferred_element_type=F32) if m01_left else jnp.dot(hi, m, preferred_element_type=F32)
        acc = part if acc is None else acc + part
        x = x - hi.astype(F32)
    return acc


def _head_rms_gate(o, gate, nw_row, l_blk):
    ms = _dot_split(o * o, _head_blocks(GROUP, HEAD_DIM), 2) * (1.0 / HEAD_DIM)
    return (o * lax.rsqrt(ms + EPS) * nw_row)[0:l_blk] * _silu(gate)


def _hgrn_kernel(q_ref, f_ref, i_ref, g_ref, s0_ref, lbraw_ref, nw_ref, o_ref, s1_ref,
                 stage, sst, *, cs, l_blk, n_chunks, layer):
    c_idx = pl.program_id(1)

    @pl.when(c_idx == 0)
    def _():
        sst[...] = s0_ref[0]

    if l_blk < cs:
        _stage_rows(stage.at[0], q_ref, l_blk)
        _stage_rows(stage.at[1], f_ref, l_blk)
        _stage_rows(stage.at[2], i_ref, l_blk)
        q, fx, v = stage[0], stage[1], stage[2]
    else:
        q, fx, v = q_ref[0], f_ref[0], i_ref[0]

    raw = lbraw_ref[...]
    e = jnp.exp(raw - jnp.max(raw, axis=0, keepdims=True))
    sm = e / jnp.sum(e, axis=0, keepdims=True)
    lb = jnp.zeros((1, GROUP), F32)
    for i in range(1, layer + 1):
        lb = lb + sm[i:i + 1, :]

    log_sig = jnp.minimum(fx, 0.0) - jnp.log1p(jnp.exp(-jnp.abs(fx)))
    la = jnp.log(lb)
    lbb = jnp.log1p(-lb) + log_sig
    log_f = jnp.maximum(la, lbb) + jnp.log1p(jnp.exp(-jnp.abs(la - lbb)))
    k = (1.0 - lb) * _sigmoid(-fx)
    if l_blk < cs:
        rows = lax.broadcasted_iota(jnp.int32, (cs, GROUP), 0)
        log_f = jnp.where(rows < l_blk, log_f, 0.0)
        k = jnp.where(rows < l_blk, k, 0.0)

    bcum = _dot_f32(_tril(cs).astype(F32), log_f)
    b_mid = bcum[cs // 2 - 1:cs // 2, :]
    b_last = bcum[cs - 1:cs, :]
    qe = q * jnp.exp(bcum - b_mid)
    ke = k * jnp.exp(b_mid - bcum)
    qs = q * jnp.exp(bcum)
    kd = k * jnp.exp(b_last - bcum)
    e_last = jnp.exp(b_last)
    causal = _tril(cs)

    os_ = []
    for h in range(HEADS):
        hs = slice(h * HEAD_DIM, (h + 1) * HEAD_DIM)
        st_t = sst[h]
        att = jnp.where(causal, _dot_nt(qe[:, hs], ke[:, hs]), 0.0)
        os_.append(_dot(att, v[:, hs]) + _dot_nt(qs[:, hs], st_t))
        sst[h] = e_last[:, hs] * st_t + _dot_tn(v[:, hs], kd[:, hs])
    o = jnp.concatenate(os_, axis=-1)
    o_ref[0] = _head_rms_gate(o, g_ref[0], nw_ref[...], l_blk)

    @pl.when(c_idx == n_chunks - 1)
    def _():
        s1_ref[0] = sst[...]


def hgrn_mixer(proj, s0, lb_raw, norm_w, layer):
    b, L, _ = proj.shape
    cs = LIN_CHUNK
    l_blk = min(cs, L)
    n_chunks = L // l_blk
    col = lambda k: pl.BlockSpec((1, l_blk, GROUP), lambda i, j, k=k: (i, j, k))
    st = pl.BlockSpec((1, HEADS, HEAD_DIM, HEAD_DIM), lambda i, j: (i, 0, 0, 0))
    return pl.pallas_call(
        functools.partial(_hgrn_kernel, cs=cs, l_blk=l_blk, n_chunks=n_chunks, layer=layer),
        grid=(b, n_chunks),
        in_specs=[col(COL_HGRN), col(COL_HGRN + 1), col(COL_HGRN + 2), col(COL_HGRN + 3), st,
                  _resident(lb_raw.shape), _resident((1, GROUP))],
        out_specs=[pl.BlockSpec((1, l_blk, GROUP), lambda i, j: (i, j, 0)), st],
        out_shape=[jax.ShapeDtypeStruct((b, L, GROUP), F32),
                   jax.ShapeDtypeStruct((b, HEADS, HEAD_DIM, HEAD_DIM), F32)],
        scratch_shapes=[pltpu.VMEM((3, cs, GROUP), F32),
                        pltpu.VMEM((HEADS, HEAD_DIM, HEAD_DIM), F32)],
        compiler_params=_params("parallel", "arbitrary"),
        name="hgrn_mixer",
    )(proj, proj, proj, proj, s0, lb_raw, jnp.tile(norm_w, HEADS).reshape(1, GROUP))


LIN_PREP_CHUNKS = 4


def _chunk_rows(x, cs, cps, row):
    return jnp.concatenate(
        [jnp.broadcast_to(x[i * cs + row:i * cs + row + 1, :], (cs, x.shape[1])) for i in range(cps)], axis=0)


def _hgrn_prep_kernel(q_ref, f_ref, i_ref, lbraw_ref, oi_ref, qs_ref, kd_ref, aux_ref, *, cs, cps, layer):
    rows = cs * cps
    q, fx, v = q_ref[0], f_ref[0], i_ref[0]
    raw = lbraw_ref[...]
    e = jnp.exp(raw - jnp.max(raw, axis=0, keepdims=True))
    sm = e / jnp.sum(e, axis=0, keepdims=True)
    lb = jnp.zeros((1, GROUP), F32)
    for i in range(1, layer + 1):
        lb = lb + sm[i:i + 1, :]
    log_sig = jnp.minimum(fx, 0.0) - jnp.log1p(jnp.exp(-jnp.abs(fx)))
    la = jnp.log(lb)
    lbb = jnp.log1p(-lb) + log_sig
    log_f = jnp.maximum(la, lbb) + jnp.log1p(jnp.exp(-jnp.abs(la - lbb)))
    k = (1.0 - lb) * _sigmoid(-fx)

    r = lax.broadcasted_iota(jnp.int32, (rows, rows), 0)
    c = lax.broadcasted_iota(jnp.int32, (rows, rows), 1)
    incl = jnp.logical_and(r >= c, (r // cs) == (c // cs))
    bcum = _dot_split(log_f, incl.astype(F32), 3, m01_left=True)
    b_mid = _chunk_rows(bcum, cs, cps, cs // 2 - 1)
    b_last = _chunk_rows(bcum, cs, cps, cs - 1)
    qe = q * jnp.exp(bcum - b_mid)
    ke = k * jnp.exp(b_mid - bcum)
    qs_ref[0] = q * jnp.exp(bcum)
    kd_ref[0] = k * jnp.exp(b_last - bcum)
    for i in range(cps):
        aux_ref[0, i * SUBLANES:(i + 1) * SUBLANES, :] = jnp.exp(b_last[i * cs:i * cs + SUBLANES, :])
    os_ = []
    for h in range(HEADS):
        hs = slice(h * HEAD_DIM, (h + 1) * HEAD_DIM)
        att = jnp.where(incl, _dot_nt(qe[:, hs], ke[:, hs]), 0.0)
        os_.append(_dot(att, v[:, hs]))
    oi_ref[0] = jnp.concatenate(os_, axis=-1)


def _hgrn_scan_kernel(oi_ref, qs_ref, kd_ref, aux_ref, v_ref, g_ref, s0_ref, nw_ref, o_ref, s1_ref, sst,
                      *, nb, n_chunks):
    c_idx = pl.program_id(1)
    blocks = _head_blocks(GROUP, HEAD_DIM)

    @pl.when(c_idx == 0)
    def _():
        sst[...] = jnp.zeros(sst.shape, F32)
        for bi in range(nb):
            for h in range(HEADS):
                hs = slice(h * HEAD_DIM, (h + 1) * HEAD_DIM)
                sst[bi, hs, hs] = s0_ref[bi, h]

    for bi in range(nb):
        st_t = sst[bi]
        o = oi_ref[bi] + _dot_nt(qs_ref[bi], st_t)
        sst[bi] = aux_ref[bi, 0:1, :] * st_t + _dot_tn(v_ref[bi], kd_ref[bi]) * blocks
        o_ref[bi] = _head_rms_gate(o, g_ref[bi], nw_ref[...], o.shape[0])

    @pl.when(c_idx == n_chunks - 1)
    def _():
        for bi in range(nb):
            for h in range(HEADS):
                hs = slice(h * HEAD_DIM, (h + 1) * HEAD_DIM)
                s1_ref[bi, h] = sst[bi, hs, hs]


def hgrn_mixer_long(proj, s0, lb_raw, norm_w, layer):
    b, L, _ = proj.shape
    cs = LIN_CHUNK
    cps = LIN_PREP_CHUNKS
    rows = cs * cps
    n_chunks = L // cs
    nb = 4 if b % 4 == 0 else 1
    col = lambda k: pl.BlockSpec((1, rows, GROUP), lambda i, j, k=k: (i, j, k))
    out = pl.BlockSpec((1, rows, GROUP), lambda i, j: (i, j, 0))
    wide = jax.ShapeDtypeStruct((b, L, GROUP), F32)
    oi, qs, kd, aux = pl.pallas_call(
        functools.partial(_hgrn_prep_kernel, cs=cs, cps=cps, layer=layer),
        grid=(b, L // rows),
        in_specs=[col(COL_HGRN), col(COL_HGRN + 1), col(COL_HGRN + 2), _resident(lb_raw.shape)],
        out_specs=[out, out, out, pl.BlockSpec((1, cps * SUBLANES, GROUP), lambda i, j: (i, j, 0))],
        out_shape=[wide, wide, wide, jax.ShapeDtypeStruct((b, n_chunks * SUBLANES, GROUP), F32)],
        compiler_params=_params("parallel", "parallel"),
        name="hgrn_prep",
    )(proj, proj, proj, lb_raw)
    blk = pl.BlockSpec((nb, cs, GROUP), lambda i, c: (i, c, 0))
    pcol = lambda k: pl.BlockSpec((nb, cs, GROUP), lambda i, c, k=k: (i, c, k))
    state = pl.BlockSpec((nb, HEADS, HEAD_DIM, HEAD_DIM), lambda i, c: (i, 0, 0, 0))
    return pl.pallas_call(
        functools.partial(_hgrn_scan_kernel, nb=nb, n_chunks=n_chunks),
        grid=(b // nb, n_chunks),
        in_specs=[blk, blk, blk, pl.BlockSpec((nb, SUBLANES, GROUP), lambda i, c: (i, c, 0)),
                  pcol(COL_HGRN + 2), pcol(COL_HGRN + 3), state, _resident((1, GROUP))],
        out_specs=[blk, state],
        out_shape=[wide, jax.ShapeDtypeStruct((b, HEADS, HEAD_DIM, HEAD_DIM), F32)],
        scratch_shapes=[pltpu.VMEM((nb, GROUP, GROUP), F32)],
        compiler_params=_params("parallel", "arbitrary"),
        name="hgrn_scan",
    )(oi, qs, kd, aux, proj, proj, s0, jnp.tile(norm_w, HEADS).reshape(1, GROUP))


def _doubling_level_masks(n, block):
    r = np.arange(n)[:, None]
    c = np.arange(n)[None, :]
    out, s = [], 1
    while s < block:
        out.append((r // (2 * s) == c // (2 * s)) & ((r // s) % 2 == 1) & ((c // s) % 2 == 0))
        s *= 2
    return np.stack(out).astype(np.float32)


def _unit_lower_inverse_minus_eye(a, block=None, level_masks_ref=None):
    n = a.shape[0]
    block = n if block is None else block
    r = lax.broadcasted_iota(jnp.int32, (n, n), 0)
    c = lax.broadcasted_iota(jnp.int32, (n, n), 1)
    dx = None
    s, level = 1, 0
    while s < block:
        if level_masks_ref is None:
            lower_left = jnp.logical_and((r // (2 * s)) == (c // (2 * s)),
                                         jnp.logical_and((r // s) % 2 == 1, (c // s) % 2 == 0))
            b = jnp.where(lower_left, a, 0.0)
        else:
            b = a * level_masks_ref[level]
        if dx is None:
            dx = -b
        else:
            m = b + _dot(dx, b)
            dx = dx - m - _dot(m, dx)
        s *= 2
        level += 1
    return dx


def _gdn_kernel(q_ref, k_ref, v_ref, z_ref, sm_ref, conv0_ref, s0_ref, cw_ref, vec_ref, nw_ref,
                o_ref, conv1_ref, s1_ref, buf, smbuf, sst, *, cs, l_blk, n_chunks):
    c_idx = pl.program_id(1)

    @pl.when(c_idx == 0)
    def _():
        buf[5:8, :] = conv0_ref[0]
        sst[...] = s0_ref[0]

    _stage_rows(buf.at[:, 0:GROUP], q_ref, l_blk, 8)
    _stage_rows(buf.at[:, GROUP:2 * GROUP], k_ref, l_blk, 8)
    _stage_rows(buf.at[:, 2 * GROUP:3 * GROUP], v_ref, l_blk, 8)
    _stage_rows(smbuf, sm_ref, l_blk)

    qkv = _silu(_causal_conv(buf, cw_ref, cs))
    conv_tail = buf[5 + l_blk:8 + l_blk, :]
    buf[5:8, :] = conv_tail
    q = qkv[:, 0:GROUP]
    k = qkv[:, GROUP:2 * GROUP]
    v = qkv[:, 2 * GROUP:3 * GROUP]
    blocks = _head_blocks(GROUP, HEAD_DIM)
    q = q * lax.rsqrt(_dot_f32(q * q, blocks) + EPS) * (HEAD_DIM ** -0.5)
    k = k * lax.rsqrt(_dot_f32(k * k, blocks) + EPS)

    sm = smbuf[...]
    beta = _sigmoid(sm)
    log_g = -jnp.exp(vec_ref[1:2, :]) * _softplus(sm + vec_ref[0:1, :])
    if l_blk < cs:
        rows = lax.broadcasted_iota(jnp.int32, (cs, LANES), 0)
        beta = jnp.where(rows < l_blk, beta, 0.0)
        log_g = jnp.where(rows < l_blk, log_g, 0.0)
    gam = _dot_f32(_tril(cs).astype(F32), log_g)
    beta_w = _dot_f32(beta, _head_expand(LANES, GROUP, HEAD_DIM, HEADS))
    gam_w = _dot_f32(gam, _head_expand(LANES, GROUP, HEAD_DIM, 2 * HEADS))
    g_last = gam_w[cs - 1:cs, :]
    e_gam = jnp.exp(gam_w)
    kb = k * beta_w
    vb = v * beta_w
    kbg = kb * e_gam
    qg = q * e_gam
    k_dec = k * jnp.exp(g_last - gam_w)
    e_last = jnp.exp(g_last)
    gam_t = gam.T
    incl = _tril(cs)
    strict = _tril(cs, strict=True)

    os_ = []
    for h in range(HEADS):
        hs = slice(h * HEAD_DIM, (h + 1) * HEAD_DIM)
        r = 2 * HEADS + h
        st = sst[h]
        dec_incl = jnp.exp(jnp.where(incl, gam[:, r:r + 1] - gam_t[r:r + 1, :], NEG_INF))
        a_mat = jnp.where(strict, _dot_nt(kb[:, hs], k[:, hs]) * dec_incl, 0.0)
        tx = _unit_lower_inverse_minus_eye(a_mat)
        u = vb[:, hs] + _dot(tx, vb[:, hs])
        w = kbg[:, hs] + _dot(tx, kbg[:, hs])
        v_new = u - _dot(w, st)
        qk = _dot_nt(q[:, hs], k[:, hs]) * dec_incl
        os_.append(_dot(qk, v_new) + _dot(qg[:, hs], st))
        sst[h] = e_last[:, h * HEAD_DIM:h * HEAD_DIM + 1] * st + _dot_tn(k_dec[:, hs], v_new)
    o = jnp.concatenate(os_, axis=-1)
    o_ref[0] = _head_rms_gate(o, z_ref[0], nw_ref[...], l_blk)

    @pl.when(c_idx == n_chunks - 1)
    def _():
        conv1_ref[0] = conv_tail
        s1_ref[0] = sst[...]


def gdn_mixer(proj, conv0, s0, conv_w, a_log, dt_bias, norm_w):
    b, L, _ = proj.shape
    cs = LIN_CHUNK
    l_blk = min(cs, L)
    n_chunks = L // l_blk
    vec = jnp.zeros((8, LANES), F32)
    vec = vec.at[0, 2 * HEADS:3 * HEADS].set(dt_bias).at[1, 2 * HEADS:3 * HEADS].set(a_log)
    col = lambda k: pl.BlockSpec((1, l_blk, GROUP), lambda i, j, k=k: (i, j, k))
    per_b = lambda shp: pl.BlockSpec((1,) + shp, lambda i, j: (i,) + (0,) * len(shp))
    c3 = 3 * GROUP
    return pl.pallas_call(
        functools.partial(_gdn_kernel, cs=cs, l_blk=l_blk, n_chunks=n_chunks),
        grid=(b, n_chunks),
        in_specs=[col(COL_GDN), col(COL_GDN + 1), col(COL_GDN + 2), col(COL_GDN + 3),
                  pl.BlockSpec((1, l_blk, LANES), lambda i, j: (i, j, COL_SMALL)),
                  per_b((CONV_W - 1, c3)), per_b((HEADS, HEAD_DIM, HEAD_DIM)),
                  _resident((CONV_W, c3)), _resident((8, LANES)), _resident((1, GROUP))],
        out_specs=[pl.BlockSpec((1, l_blk, GROUP), lambda i, j: (i, j, 0)),
                   per_b((CONV_W - 1, c3)), per_b((HEADS, HEAD_DIM, HEAD_DIM))],
        out_shape=[jax.ShapeDtypeStruct((b, L, GROUP), F32),
                   jax.ShapeDtypeStruct((b, CONV_W - 1, c3), F32),
                   jax.ShapeDtypeStruct((b, HEADS, HEAD_DIM, HEAD_DIM), F32)],
        scratch_shapes=[pltpu.VMEM((cs + 8, c3), F32), pltpu.VMEM((cs, LANES), F32),
                        pltpu.VMEM((HEADS, HEAD_DIM, HEAD_DIM), F32)],
        compiler_params=_params("parallel", "arbitrary"),
        name="gdn_mixer",
    )(proj, proj, proj, proj, proj, conv0, s0, conv_w, vec,
      jnp.tile(norm_w, HEADS).reshape(1, GROUP))


GDN_PREP_CHUNKS = 4


def _gdn_prep_kernel(q_ref, k_ref, v_ref, pq_ref, pk_ref, pv_ref, sm_ref, conv0_ref, cw_ref, vec_ref,
                     lvl_ref, u_ref, w_ref, qg_ref, kd_ref, qk_ref, aux_ref, buf, *, cs, cps, l_valid):
    j = pl.program_id(1)
    rows = cs * cps

    if l_valid is None:
        @pl.when(j == 0)
        def _():
            buf[5:8, :] = conv0_ref[0]

        @pl.when(j > 0)
        def _():
            for i, ref in enumerate((pq_ref, pk_ref, pv_ref)):
                buf[5:8, i * GROUP:(i + 1) * GROUP] = ref[0, SUBLANES - 3:SUBLANES, :]

        for i, ref in enumerate((q_ref, k_ref, v_ref)):
            buf[8:8 + rows, i * GROUP:(i + 1) * GROUP] = ref[0]
        qkv = _silu(_causal_conv(buf, cw_ref, rows))
    else:
        stride = cs + SUBLANES
        parts = []
        for n in range(cps):
            buf[n * stride + 5:n * stride + 8, :] = conv0_ref[0, n * (CONV_W - 1):(n + 1) * (CONV_W - 1), :]
            for i, ref in enumerate((q_ref, k_ref, v_ref)):
                buf[n * stride + 8:n * stride + 8 + cs, i * GROUP:(i + 1) * GROUP] = ref[0, n * cs:(n + 1) * cs, :]
            parts.append(_causal_conv(buf, cw_ref, cs, n * stride))
        qkv = _silu(jnp.concatenate(parts, axis=0))
    q = qkv[:, 0:GROUP]
    k = qkv[:, GROUP:2 * GROUP]
    v = qkv[:, 2 * GROUP:3 * GROUP]
    blocks = _head_blocks(GROUP, HEAD_DIM)
    q = q * lax.rsqrt(_dot_split(q * q, blocks, 2) + EPS) * (HEAD_DIM ** -0.5)
    k = k * lax.rsqrt(_dot_split(k * k, blocks, 2) + EPS)

    sm = sm_ref[0]
    beta = _sigmoid(sm)
    log_g = -jnp.exp(vec_ref[1:2, :]) * _softplus(sm + vec_ref[0:1, :])
    if l_valid is not None:
        valid = lax.broadcasted_iota(jnp.int32, (rows, LANES), 0) % cs < l_valid
        beta = jnp.where(valid, beta, 0.0)
        log_g = jnp.where(valid, log_g, 0.0)
    r = lax.broadcasted_iota(jnp.int32, (rows, rows), 0)
    c = lax.broadcasted_iota(jnp.int32, (rows, rows), 1)
    chunk_tril = jnp.logical_and(r >= c, (r // cs) == (c // cs)).astype(F32)
    gam = _dot_split(log_g, chunk_tril, 3, m01_left=True)
    beta_w = _dot_split(beta, _head_expand(LANES, GROUP, HEAD_DIM, HEADS), 3)
    gam_w = _dot_split(gam, _head_expand(LANES, GROUP, HEAD_DIM, 2 * HEADS), 3)
    g_last = jnp.concatenate(
        [jnp.broadcast_to(gam_w[(i + 1) * cs - 1:(i + 1) * cs, :], (cs, GROUP)) for i in range(cps)], axis=0)
    e_gam = jnp.exp(gam_w)
    kb = k * beta_w
    vb = v * beta_w
    kbg = kb * e_gam
    qg_ref[0] = q * e_gam
    kd_ref[0] = k * jnp.exp(g_last - gam_w)
    for i in range(cps):
        aux_ref[0, i * SUBLANES:(i + 1) * SUBLANES, :] = jnp.exp(g_last[i * cs:i * cs + SUBLANES, :])
    gam_t = gam.T
    incl = chunk_tril > 0.0

    us, ws, qks = [], [], []
    for h in range(HEADS):
        hs = slice(h * HEAD_DIM, (h + 1) * HEAD_DIM)
        lane = 2 * HEADS + h
        dec_incl = jnp.exp(jnp.where(incl, gam[:, lane:lane + 1] - gam_t[lane:lane + 1, :], NEG_INF))
        a_mat = _dot_nt(kb[:, hs], k[:, hs]) * dec_incl
        dx = _unit_lower_inverse_minus_eye(a_mat, cs, lvl_ref)
        uw = _dot(dx, jnp.concatenate([vb[:, hs], kbg[:, hs]], axis=-1))
        us.append(vb[:, hs] + uw[:, 0:HEAD_DIM])
        ws.append(kbg[:, hs] + uw[:, HEAD_DIM:2 * HEAD_DIM])
        qk = _dot_nt(q[:, hs], k[:, hs]) * dec_incl
        qks.append(jnp.concatenate([qk[i * cs:(i + 1) * cs, i * cs:(i + 1) * cs] for i in range(cps)], axis=0))
    u_ref[0] = jnp.concatenate(us, axis=-1)
    w_ref[0] = jnp.concatenate(ws, axis=-1)
    qk_ref[0] = jnp.concatenate(qks, axis=-1)


def _gdn_scan_kernel(u_ref, w_ref, qg_ref, kd_ref, qk_ref, aux_ref, z_ref, s0_ref, nw_ref,
                     o_ref, s1_ref, sst, *, nb, n_chunks):
    c_idx = pl.program_id(1)
    blocks = _head_blocks(GROUP, HEAD_DIM)

    @pl.when(c_idx == 0)
    def _():
        sst[...] = jnp.zeros(sst.shape, F32)
        for bi in range(nb):
            for h in range(HEADS):
                hs = slice(h * HEAD_DIM, (h + 1) * HEAD_DIM)
                sst[bi, hs, hs] = s0_ref[bi, h]

    for bi in range(nb):
        st = sst[bi]
        v_new = u_ref[bi] - _dot(w_ref[bi], st)
        v_bd = jnp.concatenate([v_new] * HEADS, axis=0) * blocks
        o = _dot(qk_ref[bi], v_bd) + _dot(qg_ref[bi], st)
        sst[bi] = aux_ref[bi, 0:1, :] * st + _dot_tn(kd_ref[bi], v_new) * blocks
        o_ref[bi] = _head_rms_gate(o, z_ref[bi], nw_ref[...], o.shape[0])

    @pl.when(c_idx == n_chunks - 1)
    def _():
        for bi in range(nb):
            for h in range(HEADS):
                hs = slice(h * HEAD_DIM, (h + 1) * HEAD_DIM)
                s1_ref[bi, h] = sst[bi, hs, hs]


def _gdn_prep_call(proj, conv0, conv_w, a_log, dt_bias, l_valid):
    g, L, _ = proj.shape
    cs = LIN_CHUNK
    cps = GDN_PREP_CHUNKS
    rows = cs * cps
    assert L % rows == 0 and (l_valid is None or L == rows)
    vec = jnp.zeros((8, LANES), F32)
    vec = vec.at[0, 2 * HEADS:3 * HEADS].set(dt_bias).at[1, 2 * HEADS:3 * HEADS].set(a_log)
    c3 = 3 * GROUP
    col = lambda k: pl.BlockSpec((1, rows, GROUP), lambda i, j, k=k: (i, j, k))
    prev = lambda k: pl.BlockSpec(
        (1, SUBLANES, GROUP), lambda i, j, k=k: (i, jnp.maximum(j * (rows // SUBLANES) - 1, 0), k))
    out = pl.BlockSpec((1, rows, GROUP), lambda i, j: (i, j, 0))
    wide = jax.ShapeDtypeStruct((g, L, GROUP), F32)
    level_masks = jnp.asarray(_doubling_level_masks(rows, cs))
    return pl.pallas_call(
        functools.partial(_gdn_prep_kernel, cs=cs, cps=cps, l_valid=l_valid),
        grid=(g, L // rows),
        in_specs=[col(COL_GDN), col(COL_GDN + 1), col(COL_GDN + 2),
                  prev(COL_GDN), prev(COL_GDN + 1), prev(COL_GDN + 2),
                  pl.BlockSpec((1, rows, LANES), lambda i, j: (i, j, COL_SMALL)),
                  pl.BlockSpec((1,) + conv0.shape[1:], lambda i, j: (i, 0, 0)),
                  _resident((CONV_W, c3)), _resident((8, LANES)), _resident(level_masks.shape)],
        out_specs=[out, out, out, out, out,
                   pl.BlockSpec((1, cps * SUBLANES, GROUP), lambda i, j: (i, j, 0))],
        out_shape=[wide, wide, wide, wide, wide,
                   jax.ShapeDtypeStruct((g, L // cs * SUBLANES, GROUP), F32)],
        scratch_shapes=[pltpu.VMEM((cps * (cs + SUBLANES), c3), F32)],
        compiler_params=_params("parallel", "parallel"),
        name="gdn_prep",
    )(proj, proj, proj, proj, proj, proj, proj, conv0, conv_w, vec, level_masks)


GDN_SCAN_SEQS = 4


def _gdn_scan_call(prep, proj, s0, norm_w):
    b, L, _ = proj.shape
    cs = LIN_CHUNK
    nb = GDN_SCAN_SEQS
    n_chunks = L // cs
    assert b % nb == 0 and L % cs == 0
    blk = pl.BlockSpec((nb, cs, GROUP), lambda i, c: (i, c, 0))
    state = pl.BlockSpec((nb, HEADS, HEAD_DIM, HEAD_DIM), lambda i, c: (i, 0, 0, 0))
    return pl.pallas_call(
        functools.partial(_gdn_scan_kernel, nb=nb, n_chunks=n_chunks),
        grid=(b // nb, n_chunks),
        in_specs=[blk, blk, blk, blk, blk,
                  pl.BlockSpec((nb, SUBLANES, GROUP), lambda i, c: (i, c, 0)),
                  pl.BlockSpec((nb, cs, GROUP), lambda i, c: (i, c, COL_GDN + 3)),
                  state, _resident((1, GROUP))],
        out_specs=[blk, state],
        out_shape=[jax.ShapeDtypeStruct((b, L, GROUP), F32),
                   jax.ShapeDtypeStruct((b, HEADS, HEAD_DIM, HEAD_DIM), F32)],
        scratch_shapes=[pltpu.VMEM((nb, GROUP, GROUP), F32)],
        compiler_params=_params("parallel", "arbitrary"),
        name="gdn_scan",
    )(*prep, proj, s0, jnp.tile(norm_w, HEADS).reshape(1, GROUP))


def gdn_mixer_long(proj, conv0, s0, conv_w, a_log, dt_bias, norm_w):
    L = proj.shape[1]
    prep = _gdn_prep_call(proj, conv0, conv_w, a_log, dt_bias, None)
    o, s1 = _gdn_scan_call(prep, proj, s0, norm_w)
    conv1 = proj[:, L - (CONV_W - 1):, COL_GDN * GROUP:(COL_GDN + 3) * GROUP]
    return o, conv1, s1


def gdn_mixer_short(proj, conv0, s0, conv_w, a_log, dt_bias, norm_w):
    b, L, cols = proj.shape
    cs = LIN_CHUNK
    cps = GDN_PREP_CHUNKS
    assert CONV_W - 1 <= L <= cs and b % cps == 0
    padded = jnp.pad(proj, ((0, 0), (0, cs - L), (0, 0)))
    prep = _gdn_prep_call(padded.reshape(b // cps, cps * cs, cols),
                          conv0.reshape(b // cps, cps * (CONV_W - 1), 3 * GROUP), conv_w, a_log, dt_bias, L)
    prep = [p.reshape(b, -1, GROUP) for p in prep]
    o, s1 = _gdn_scan_call(prep, padded, s0, norm_w)
    conv1 = proj[:, L - (CONV_W - 1):, COL_GDN * GROUP:(COL_GDN + 3) * GROUP]
    return o[:, :L], conv1, s1


def _moba_prompt_kernel(q_ref, k_ref, v_ref, o_ref, ks, vts, kmean, sel_t, o_t, q_tb, alibi, ml,
                        s_buf, p_buf, *, nb):
    blk = MOBA_BLOCK
    qi = pl.program_id(1)

    @pl.when(qi == 0)
    def _():
        for j in range(nb):
            kj = k_ref[0, j * blk:(j + 1) * blk, :]
            kmean[j:j + 1, :] = jnp.mean(kj, axis=0, keepdims=True)
            kjb = kj.astype(BF16)
            for h in range(HEADS):
                ks[h, j] = kjb[:, h * HEAD_DIM:(h + 1) * HEAD_DIM]
            vts[j] = v_ref[0, j * blk:(j + 1) * blk, :].T.astype(BF16)

    q_t = (q_ref[0] * (HEAD_DIM ** -0.5)).T
    q_tb[...] = (q_t * LOG2_E).astype(BF16)
    blk_row = lax.broadcasted_iota(jnp.int32, (nb, blk), 0)
    rel = (lax.broadcasted_iota(jnp.int32, (blk, blk), 1)
           - lax.broadcasted_iota(jnp.int32, (blk, blk), 0)).astype(F32)
    for h in range(HEADS):
        alibi[h] = rel * (-ALIBI_SLOPES[h] * LOG2_E)

    for h in range(HEADS):
        hs = slice(h * HEAD_DIM, (h + 1) * HEAD_DIM)
        slope = ALIBI_SLOPES[h]
        gate = jnp.where(blk_row < qi, _dot_f32(kmean[:, hs], q_t[hs, :]), NEG_INF)
        sel = jnp.full((nb, blk), NEG_INF, F32)
        for _ in range(MOBA_TOPK):
            top = jnp.max(gate, axis=0, keepdims=True)
            first = jnp.min(jnp.where(gate == top, blk_row, nb), axis=0, keepdims=True)
            pick = blk_row == first
            sel = jnp.where(jnp.logical_and(pick, blk_row < qi), 0.0, sel)
            gate = jnp.where(pick, NEG_INF, gate)
        sel_t[h] = sel
        ml[h, 0:1, :] = jnp.full((1, blk), NEG_INF, F32)
        ml[h, 1:2, :] = jnp.zeros((1, blk), F32)
    o_t[...] = jnp.zeros(o_t.shape, F32)

    def block_step(j, own):
        for h in range(HEADS):
            hs = slice(h * HEAD_DIM, (h + 1) * HEAD_DIM)
            s_buf[h] = _dot(ks[h, j], q_tb[hs, :])
        for h in range(HEADS):
            if own:
                causal = (lax.broadcasted_iota(jnp.int32, (blk, blk), 1)
                          >= lax.broadcasted_iota(jnp.int32, (blk, blk), 0))
                s = jnp.where(causal, s_buf[h] + alibi[h], NEG_INF)
            else:
                off = (qi - j).astype(F32) * (ALIBI_SLOPES[h] * LOG2_E * blk)
                s = s_buf[h] + alibi[h] + (sel_t[h, pl.ds(j, 1), :] - off)
            m = ml[h, 0:1, :]
            m_new = jnp.maximum(m, jnp.max(s, axis=0, keepdims=True))
            p = jnp.exp2(s - m_new)
            p_buf[h] = p.astype(BF16)
            alpha = jnp.exp2(m - m_new)
            ml[h, 0:1, :] = m_new
            ml[h, 1:2, :] = alpha * ml[h, 1:2, :] + jnp.sum(p, axis=0, keepdims=True)
            ml[h, 2:3, :] = alpha
        for h in range(HEADS):
            hs = slice(h * HEAD_DIM, (h + 1) * HEAD_DIM)
            o_t[hs, :] = (ml[h, 2:3, :] * o_t[hs, :]
                          + jnp.dot(vts[j, hs, :], p_buf[h], preferred_element_type=F32))

    block_step(qi, True)

    def body(j, carry):
        block_step(j, False)
        return carry

    lax.fori_loop(0, qi, body, 0)
    for h in range(HEADS):
        hs = slice(h * HEAD_DIM, (h + 1) * HEAD_DIM)
        o_t[hs, :] = o_t[hs, :] / ml[h, 1:2, :]
    o_ref[0] = o_t[...].T


def moba_prompt(proj):
    b, S, _ = proj.shape
    blk = MOBA_BLOCK
    nb = S // blk
    seq = lambda k: pl.BlockSpec((1, S, GROUP), lambda i, j, k=k: (i, 0, k))
    return pl.pallas_call(
        functools.partial(_moba_prompt_kernel, nb=nb),
        grid=(b, nb),
        in_specs=[pl.BlockSpec((1, blk, GROUP), lambda i, j: (i, j, COL_MOBA)),
                  seq(COL_MOBA + 1), seq(COL_MOBA + 2)],
        out_specs=pl.BlockSpec((1, blk, GROUP), lambda i, j: (i, j, 0)),
        out_shape=jax.ShapeDtypeStruct((b, S, GROUP), F32),
        scratch_shapes=[pltpu.VMEM((HEADS, nb, blk, HEAD_DIM), BF16),
                        pltpu.VMEM((nb, GROUP, blk), BF16),
                        pltpu.VMEM((nb, GROUP), F32),
                        pltpu.VMEM((HEADS, nb, blk), F32),
                        pltpu.VMEM((GROUP, blk), F32),
                        pltpu.VMEM((GROUP, blk), BF16),
                        pltpu.VMEM((HEADS, blk, blk), F32),
                        pltpu.VMEM((HEADS, SUBLANES, blk), F32),
                        pltpu.VMEM((HEADS, blk, blk), F32),
                        pltpu.VMEM((HEADS, blk, blk), BF16)],
        compiler_params=_params("parallel", "arbitrary"),
        name="moba_prompt",
    )(proj, proj, proj)


PAGES_PER_BLOCK = MOBA_BLOCK // PAGE_SIZE
SELECT_PAGE_BUFFERS = 64
SELECT_UNROLL = 4


def _moba_select_kernel(pt_ref, q_ref, kc_ref, idx_ref, pages, kmean_t, q8, sems, *,
                        layer, n_pages, t):
    b = pl.program_id(0)
    nbuf = SELECT_PAGE_BUFFERS
    n_blocks = n_pages // PAGES_PER_BLOCK

    def page_copy(p, slot):
        return pltpu.make_async_copy(kc_ref.at[layer, pt_ref[b, p]], pages.at[slot], sems.at[slot])

    for s in range(nbuf):
        page_copy(s, s).start()

    kmean_t[...] = jnp.zeros(kmean_t.shape, F32)
    blk_of_lane = lax.broadcasted_iota(jnp.int32, kmean_t.shape, 1)

    pages_per_trip = SELECT_UNROLL * PAGES_PER_BLOCK

    def body(trip, carry):
        p0 = trip * pages_per_trip
        for i in range(pages_per_trip):
            page_copy(p0 + i, (p0 + i) % nbuf).wait()
        km = kmean_t[...]
        for u in range(SELECT_UNROLL):
            tot = jnp.zeros((GROUP, PAGE_SIZE), F32)
            for pp in range(PAGES_PER_BLOCK):
                tot = tot + pages[(p0 + u * PAGES_PER_BLOCK + pp) % nbuf]
            mean = jnp.sum(tot, axis=1, keepdims=True) * (1.0 / MOBA_BLOCK)
            km = jnp.where(blk_of_lane == trip * SELECT_UNROLL + u, mean, km)
        kmean_t[...] = km
        for i in range(pages_per_trip):
            @pl.when(p0 + i + nbuf < n_pages)
            def _(i=i):
                page_copy(p0 + i + nbuf, (p0 + i) % nbuf).start()
        return carry

    lax.fori_loop(0, n_blocks // SELECT_UNROLL, body, 0)

    q8[0:t, :] = q_ref[0]
    q8[t:, :] = jnp.zeros((SUBLANES - t, GROUP), F32)
    blk_lane = lax.broadcasted_iota(jnp.int32, (SUBLANES, n_blocks), 1)
    out_lane = lax.broadcasted_iota(jnp.int32, (SUBLANES, LANES), 1)
    res = jnp.zeros((SUBLANES, LANES), jnp.int32)
    for h in range(HEADS):
        hs = slice(h * HEAD_DIM, (h + 1) * HEAD_DIM)
        gate = _dot_f32(q8[:, hs], kmean_t[hs, 0:n_blocks])
        for r in range(MOBA_TOPK):
            top = jnp.max(gate, axis=1, keepdims=True)
            first = jnp.min(jnp.where(gate == top, blk_lane, n_blocks), axis=1, keepdims=True)
            res = jnp.where(out_lane == h * MOBA_TOPK + r, first, res)
            gate = jnp.where(blk_lane == first, NEG_INF, gate)
    idx_ref[0] = res


def _moba_sample_attn_kernel(pt_ref, idx_ref, q_ref, kn_ref, vn_ref, kc_ref, vc_ref, o_ref,
                             kbuf, vbuf, st8, o8, rows_s, rows_o, ksem, vsem, *, layer, past, t):
    b = pl.program_id(0)
    n_seq = pl.num_programs(0)
    n_sel = MOBA_TOPK * MOBA_BLOCK
    pairs = [(tok, h) for h in range(HEADS) for tok in range(t)]

    def block_of(seq, tok, h, r):
        return idx_ref[seq, tok * (HEADS * MOBA_TOPK) + h * MOBA_TOPK + r]

    def copies(seq, i, half):
        tok, h = pairs[i]
        out = []
        for r in range(MOBA_TOPK):
            blk = block_of(seq, tok, h, r)
            for pp in range(PAGES_PER_BLOCK):
                phys = pt_ref[seq, blk * PAGES_PER_BLOCK + pp]
                lanes = pl.ds((r * PAGES_PER_BLOCK + pp) * PAGE_SIZE, PAGE_SIZE)
                out.append(pltpu.make_async_copy(kc_ref.at[layer, phys, h], kbuf.at[half, i, :, lanes],
                                                 ksem.at[half, i]))
                out.append(pltpu.make_async_copy(vc_ref.at[layer, phys, h], vbuf.at[half, i, :, lanes],
                                                 vsem.at[half, i]))
        return out

    def start_all(seq, half):
        for i in range(len(pairs)):
            for c in copies(seq, i, half):
                c.start()

    half = b % 2

    @pl.when(b == 0)
    def _():
        start_all(b, half)

    @pl.when(b + 1 < n_seq)
    def _():
        start_all(b + 1, 1 - half)

    for i, ref in enumerate((q_ref, kn_ref, vn_ref)):
        st8[i, 0:t, :] = ref[0]
        st8[i, t:, :] = jnp.zeros((SUBLANES - t, GROUP), F32)

    lane = lax.broadcasted_iota(jnp.int32, (1, n_sel), 1)
    row = lax.broadcasted_iota(jnp.int32, (SUBLANES, 1), 0)
    for i in range(len(pairs)):
        for c in copies(b, i, half):
            c.wait()
    for i, (tok, h) in enumerate(pairs):
        hs = slice(h * HEAD_DIM, (h + 1) * HEAD_DIM)
        qrow = st8[0, tok:tok + 1, hs] * (HEAD_DIM ** -0.5)
        rows_s[i:i + 1, :] = _dot(jnp.broadcast_to(qrow, (SUBLANES, HEAD_DIM)), kbuf[half, i])[0:1]
    for i, (tok, h) in enumerate(pairs):
        hs = slice(h * HEAD_DIM, (h + 1) * HEAD_DIM)
        slope = ALIBI_SLOPES[h]
        qrow = st8[0, tok:tok + 1, hs] * (HEAD_DIM ** -0.5)
        blk = jnp.where(lane < MOBA_BLOCK, block_of(b, tok, h, 0),
                        jnp.where(lane < 2 * MOBA_BLOCK, block_of(b, tok, h, 1), block_of(b, tok, h, 2)))
        pos = blk * MOBA_BLOCK + (lane % MOBA_BLOCK)
        s_sel = rows_s[i:i + 1, :] - slope * (past + tok - pos).astype(F32)
        s_own = jnp.sum(st8[1, :, hs] * qrow, axis=1, keepdims=True)
        s_own = jnp.where(row <= tok, s_own - slope * (tok - row).astype(F32), NEG_INF)
        m = jnp.maximum(jnp.max(s_sel, axis=1, keepdims=True), jnp.max(s_own, axis=0, keepdims=True))
        p_sel = jnp.exp(s_sel - m)
        p_own = jnp.exp(s_own - m)
        l = jnp.sum(p_sel, axis=1, keepdims=True) + jnp.sum(p_own, axis=0, keepdims=True)
        rows_s[i:i + 1, :] = p_sel
        rows_o[i:i + 1, 0:HEAD_DIM] = jnp.sum(p_own * st8[2, :, hs], axis=0, keepdims=True)
        rows_o[i:i + 1, HEAD_DIM:2 * HEAD_DIM] = jnp.broadcast_to(l, (1, HEAD_DIM))
    for i, (tok, h) in enumerate(pairs):
        hs = slice(h * HEAD_DIM, (h + 1) * HEAD_DIM)
        p8 = jnp.broadcast_to(rows_s[i:i + 1, :], (SUBLANES, n_sel))
        o = _dot_nt(p8, vbuf[half, i])[0:1] + rows_o[i:i + 1, 0:HEAD_DIM]
        o8[tok:tok + 1, hs] = o / rows_o[i:i + 1, HEAD_DIM:2 * HEAD_DIM]
    o_ref[0] = o8[0:t, :]


def moba_sample(proj, k_cache, v_cache, page_table, layer):
    db, t, _ = proj.shape
    depth, pool = k_cache.shape[:2]
    n_pages = page_table.shape[1]
    n_blocks = n_pages // PAGES_PER_BLOCK
    past = n_pages * PAGE_SIZE
    assert past % MOBA_BLOCK == 0 and n_pages >= SELECT_PAGE_BUFFERS and t <= SUBLANES
    assert n_blocks % SELECT_UNROLL == 0 and SELECT_PAGE_BUFFERS % (SELECT_UNROLL * PAGES_PER_BLOCK) == 0
    assert MOBA_TOPK <= n_blocks <= LANES
    tok = lambda k: pl.BlockSpec((1, t, GROUP), lambda i, *_: (i, 0, k))
    hbm = pl.BlockSpec(memory_space=pl.ANY)
    n_pairs = t * HEADS
    idx = pl.pallas_call(
        functools.partial(_moba_select_kernel, layer=layer, n_pages=n_pages, t=t),
        grid_spec=pltpu.PrefetchScalarGridSpec(
            num_scalar_prefetch=1, grid=(db,),
            in_specs=[tok(COL_MOBA), hbm],
            out_specs=pl.BlockSpec((1, SUBLANES, LANES), lambda i, *_: (i, 0, 0)),
            scratch_shapes=[pltpu.VMEM((SELECT_PAGE_BUFFERS, GROUP, PAGE_SIZE), F32),
                            pltpu.VMEM((GROUP, LANES), F32),
                            pltpu.VMEM((SUBLANES, GROUP), F32),
                            pltpu.SemaphoreType.DMA((SELECT_PAGE_BUFFERS,))]),
        out_shape=jax.ShapeDtypeStruct((db, SUBLANES, LANES), jnp.int32),
        compiler_params=_params("arbitrary"),
        name="moba_sample_select",
    )(page_table, proj, k_cache.reshape(depth, pool, GROUP, PAGE_SIZE))
    idx = idx[:, :t, :HEADS * MOBA_TOPK].reshape(db, t * HEADS * MOBA_TOPK)
    return pl.pallas_call(
        functools.partial(_moba_sample_attn_kernel, layer=layer, past=past, t=t),
        grid_spec=pltpu.PrefetchScalarGridSpec(
            num_scalar_prefetch=2, grid=(db,),
            in_specs=[tok(COL_MOBA), tok(COL_MOBA + 1), tok(COL_MOBA + 2), hbm, hbm],
            out_specs=pl.BlockSpec((1, t, GROUP), lambda i, *_: (i, 0, 0)),
            scratch_shapes=[pltpu.VMEM((2, n_pairs, HEAD_DIM, MOBA_TOPK * MOBA_BLOCK), F32),
                            pltpu.VMEM((2, n_pairs, HEAD_DIM, MOBA_TOPK * MOBA_BLOCK), F32),
                            pltpu.VMEM((3, SUBLANES, GROUP), F32),
                            pltpu.VMEM((SUBLANES, GROUP), F32),
                            pltpu.VMEM((n_pairs, MOBA_TOPK * MOBA_BLOCK), F32),
                            pltpu.VMEM((n_pairs, LANES), F32),
                            pltpu.SemaphoreType.DMA((2, n_pairs)),
                            pltpu.SemaphoreType.DMA((2, n_pairs))]),
        out_shape=jax.ShapeDtypeStruct((db, t, GROUP), F32),
        compiler_params=_params("arbitrary"),
        name="moba_sample_attn",
    )(page_table, idx, proj, proj, proj, k_cache, v_cache)


def _rearrange_w_in(w):
    d = w.shape[0]
    dt0 = COL_HGRN * GROUP
    ba0 = dt0 + HEADS + (COL_GDN + 4 - COL_HGRN) * GROUP
    wide = jnp.concatenate([w[:, :dt0], w[:, dt0 + HEADS:ba0]], axis=1)
    narrow = jnp.concatenate([w[:, dt0:dt0 + HEADS], w[:, ba0:ba0 + 2 * HEADS],
                              jnp.zeros((d, LANES - N_SMALL), w.dtype)], axis=1)
    return jnp.concatenate([wide, narrow], axis=1)


def _layer(x, mem_k, mem_v, states, lp, layer, attend, prompt):
    b, L, d = x.shape
    ssm_conv0, ssm0, hgrn0, gdn_conv0, gdn0 = states
    w_in = lp['w_in'].astype(BF16) if prompt else lp['w_in']
    proj = norm_matmul(x.reshape(b * L, d), lp['n_mix_pre'], w_in, min(256, b * L)).reshape(b, L, -1)
    o_a = attend(proj)
    o_b, ssm_conv1, ssm1 = ssd_mixer(proj, ssm_conv0, ssm0, lp['ssm_conv_w'], lp['ssm_conv_b'],
                                     lp['ssm_dt_bias'], lp['ssm_a_log'], lp['ssm_d'], lp['ssm_norm'])
    hgrn = hgrn_mixer_long if L % (LIN_PREP_CHUNKS * LIN_CHUNK) == 0 else hgrn_mixer
    o_c, hgrn1 = hgrn(proj, jnp.swapaxes(hgrn0, -1, -2), lp['hgrn_lb_raw'], lp['hgrn_norm'], layer)
    gdn = gdn_mixer_long if L % (GDN_PREP_CHUNKS * LIN_CHUNK) == 0 else gdn_mixer_short
    o_d, gdn_conv1, gdn1 = gdn(proj, gdn_conv0, gdn0, lp['gdn_conv_w'], lp['gdn_a_log'],
                               lp['gdn_dt_bias'], lp['gdn_norm'])
    post = post_mixer if prompt else post_mixer_sample
    args = (x, (o_a, o_b, o_c, o_d), mem_k, mem_v, lp['norms'], lp['w_out'], lp['w_xq'], lp['w_xo'],
            lp['w_gu'], lp['w_down'])
    x = post(*args, POST_MIXER_ROWS) if prompt else post(*args)
    k = proj[..., GROUP:2 * GROUP].reshape(b, L, HEADS, HEAD_DIM)
    v = proj[..., 2 * GROUP:3 * GROUP].reshape(b, L, HEADS, HEAD_DIM)
    return x, (k, v, ssm_conv1, ssm1, jnp.swapaxes(hgrn1, -1, -2), gdn_conv1, gdn1)


def kernel(x_prompt, x_sample, mem_prompt, cache_moba_k, cache_moba_v, page_table, cache_mem_k, cache_mem_v, state_ssm_conv, state_ssm, state_hgrn, state_gdn_conv, state_gdn, n_mix_pre, n_mix_post, w_in, w_out, ssm_conv_w, ssm_conv_b, ssm_dt_bias, ssm_a_log, ssm_d, ssm_norm, hgrn_lb_raw, hgrn_norm, gdn_conv_w, gdn_a_log, gdn_dt_bias, gdn_norm, n_x_pre, n_x_post, mem_norm, w_xq, w_xkv, w_xo, n_f_pre, n_f_post, w_gu, w_down):
    depth = w_in.shape[0]
    bp, _, d = x_prompt.shape
    db = x_sample.shape[0]
    n_mem = mem_prompt.shape[1]
    kc = jnp.transpose(cache_moba_k, (0, 1, 3, 4, 2))
    vc = jnp.transpose(cache_moba_v, (0, 1, 3, 4, 2))
    zeros_p = (jnp.zeros((bp,) + state_ssm_conv.shape[2:], F32), jnp.zeros((bp,) + state_ssm.shape[2:], F32),
               jnp.zeros((bp,) + state_hgrn.shape[2:], F32), jnp.zeros((bp,) + state_gdn_conv.shape[2:], F32),
               jnp.zeros((bp,) + state_gdn.shape[2:], F32))
    yp, ys = x_prompt, x_sample
    outs_p, outs_s, mem_ks, mem_vs = [], [], [], []
    for l in range(depth):
        norms = jnp.zeros((SUBLANES, d), F32)
        for i, nrm in enumerate((n_mix_post, n_x_pre, n_x_post, n_f_pre, n_f_post)):
            norms = norms.at[i].set(nrm[l])
        lp = {'n_mix_pre': n_mix_pre[l], 'w_in': _rearrange_w_in(w_in[l]), 'norms': norms,
              'w_out': w_out[l].astype(BF16), 'w_xq': w_xq[l].astype(BF16), 'w_xo': w_xo[l].astype(BF16),
              'w_gu': w_gu[l].astype(BF16), 'w_down': w_down[l].astype(BF16),
              'ssm_conv_w': ssm_conv_w[l], 'ssm_conv_b': ssm_conv_b[l], 'ssm_dt_bias': ssm_dt_bias[l],
              'ssm_a_log': ssm_a_log[l], 'ssm_d': ssm_d[l], 'ssm_norm': ssm_norm[l],
              'hgrn_lb_raw': hgrn_lb_raw, 'hgrn_norm': hgrn_norm[l], 'gdn_conv_w': gdn_conv_w[l],
              'gdn_a_log': gdn_a_log[l], 'gdn_dt_bias': gdn_dt_bias[l], 'gdn_norm': gdn_norm[l]}
        mkv = norm_matmul(mem_prompt.reshape(bp * n_mem, d), mem_norm[l], w_xkv[l].astype(BF16), 256)
        mk = mkv[:, :d].reshape(bp, n_mem, d)
        mv = mkv[:, d:].reshape(bp, n_mem, d)
        yp, st_p = _layer(yp, mk.astype(BF16), mv.astype(BF16), zeros_p, lp, l, moba_prompt, True)
        outs_p.append(st_p)
        mem_ks.append(mk.reshape(bp, n_mem, X_HEADS, d // X_HEADS))
        mem_vs.append(mv.reshape(bp, n_mem, X_HEADS, d // X_HEADS))
        states_s = (state_ssm_conv[l], state_ssm[l], state_hgrn[l], state_gdn_conv[l], state_gdn[l])
        attend_s = functools.partial(moba_sample, k_cache=kc, v_cache=vc, page_table=page_table, layer=l)
        ys, st_s = _layer(ys, cache_mem_k[l].reshape(db, n_mem, d).astype(BF16),
                          cache_mem_v[l].reshape(db, n_mem, d).astype(BF16), states_s, lp, l, attend_s, False)
        outs_s.append(st_s)
    stack = lambda outs, i: jnp.stack([o[i] for o in outs], axis=0)
    return (yp, ys, stack(outs_p, 0), stack(outs_s, 0), stack(outs_p, 1), stack(outs_s, 1),
            jnp.stack(mem_ks, axis=0), jnp.stack(mem_vs, axis=0),
            stack(outs_p, 2), stack(outs_s, 2), stack(outs_p, 3), stack(outs_s, 3),
            stack(outs_p, 4), stack(outs_s, 4), stack(outs_p, 5), stack(outs_s, 5),
            stack(outs_p, 6), stack(outs_s, 6))
```

```python
import functools

import numpy as np
import jax
import jax.numpy as jnp
from jax import lax
from jax.experimental import pallas as pl
from jax.experimental.pallas import tpu as pltpu

F32 = jnp.float32
BF16 = jnp.bfloat16

LANES = 128
SUBLANES = 8
VMEM_LIMIT_BYTES = 56 * 1024 * 1024

GROUP = 256
HEADS = 4
HEAD_DIM = 64
CONV_W = 4
SSM_DSTATE = 128
SSM_CHUNK = 128
LIN_CHUNK = 64
MOBA_BLOCK = 256
MOBA_TOPK = 3
PAGE_SIZE = 128
X_HEADS = 4
N_SMALL = 12
COL_MOBA = 0
COL_SSM_Z = 3
COL_SSM_X = 4
COL_HGRN = 7
COL_GDN = 11
N_WIDE = 15 * GROUP
COL_SMALL = N_WIDE // LANES
EPS = 1e-6
NEG_INF = float("-inf")
LOG2_E = 1.4426950408889634
ALIBI_SLOPES = tuple(2.0 ** (-8.0 * (h + 1) / HEADS) for h in range(HEADS))


def _rms(x, w):
    return x * lax.rsqrt(jnp.mean(x * x, axis=-1, keepdims=True) + EPS) * w


def _sigmoid(x):
    return 1.0 / (1.0 + jnp.exp(-x))


def _silu(x):
    return x * _sigmoid(x)


def _softplus(x):
    return jnp.maximum(x, 0.0) + jnp.log(1.0 + jnp.exp(-jnp.abs(x)))


def _dot(a, b):
    return jnp.dot(a.astype(BF16), b.astype(BF16), preferred_element_type=F32)


def _dot_nt(a, b):
    return lax.dot_general(a.astype(BF16), b.astype(BF16), (((1,), (1,)), ((), ())),
                           preferred_element_type=F32)


def _dot_f32(a, b):
    return jnp.dot(a, b, preferred_element_type=F32, precision=lax.Precision.HIGHEST)


def _params(*sem, flags=None):
    return pltpu.CompilerParams(dimension_semantics=sem, vmem_limit_bytes=VMEM_LIMIT_BYTES, flags=flags)


def _resident(shape):
    return pl.BlockSpec(shape, lambda *_: (0,) * len(shape), pipeline_mode=pl.Buffered(1))


def _norm_matmul_kernel(x_ref, nw_ref, w_ref, o_ref, *, full_precision):
    xn = _rms(x_ref[...], nw_ref[...])
    o_ref[...] = _dot_f32(xn, w_ref[...]) if full_precision else _dot(xn, w_ref[...])


def norm_matmul(x, norm_w, w, tm):
    n, d = x.shape
    c = w.shape[1]
    return pl.pallas_call(
        functools.partial(_norm_matmul_kernel, full_precision=(w.dtype == F32)),
        grid=(n // tm,),
        in_specs=[pl.BlockSpec((tm, d), lambda i: (i, 0)),
                  _resident((1, d)),
                  _resident((d, c))],
        out_specs=pl.BlockSpec((tm, c), lambda i: (i, 0)),
        out_shape=jax.ShapeDtypeStruct((n, c), F32),
        compiler_params=_params("parallel"),
        name="norm_matmul",
    )(x, norm_w.reshape(1, d), w)


POST_MIXER_ROWS = 512
IN_PROJ_ROWS = 512

def _mix_out_and_query(x, mixed, nw, wout_ref, wxq_ref):
    x = x + _rms(_dot(mixed, wout_ref[...]), nw[0:1])
    return x, _dot(_rms(x, nw[1:2]), wxq_ref[...])


def _memory_attention(q, head_k, head_v):
    xdh = q.shape[1] // X_HEADS
    heads = []
    for h in range(X_HEADS):
        s = _dot_nt(q[:, h * xdh:(h + 1) * xdh], head_k(h)) * (xdh ** -0.5)
        p = jnp.exp(s - jnp.max(s, axis=-1, keepdims=True))
        heads.append(_dot(p, head_v(h)) / jnp.sum(p, axis=-1, keepdims=True))
    return jnp.concatenate(heads, axis=-1)


def _lane_heads(ref):
    xdh = ref.shape[-1] // X_HEADS
    return lambda h: ref[0, :, h * xdh:(h + 1) * xdh]


def _attn_out_and_ffn(x, att, nw, wxo_ref, wgu_ref, wdown_ref):
    x = x + _rms(_dot(att, wxo_ref[...]), nw[2:3])
    gu = _dot(_rms(x, nw[3:4]), wgu_ref[...])
    hid = gu.shape[1] // 2
    act = _silu(gu[:, :hid]) * gu[:, hid:]
    return x + _rms(_dot(act, wdown_ref[...]), nw[4:5])


def _post_mixer_kernel(x_ref, oa_ref, ob_ref, oc_ref, od_ref, mk_ref, mv_ref, norms_ref, wout_ref,
                       wxq_ref, wxo_ref, wgu_ref, wdown_ref, o_ref):
    nw = norms_ref[...]
    mixed = jnp.concatenate([oa_ref[0], ob_ref[0], oc_ref[0], od_ref[0]], axis=-1)
    x, q = _mix_out_and_query(x_ref[0], mixed, nw, wout_ref, wxq_ref)
    att = _memory_attention(q, _lane_heads(mk_ref), _lane_heads(mv_ref))
    o_ref[0] = _attn_out_and_ffn(x, att, nw, wxo_ref, wgu_ref, wdown_ref)


def post_mixer(x, mix_parts, mk, mv, norms, w_out, w_xq, w_xo, w_gu, w_down, tm):
    b, L, d = x.shape
    m = mk.shape[1]
    row = pl.BlockSpec((1, tm, d), lambda i, j: (i, j, 0))
    part = pl.BlockSpec((1, tm, GROUP), lambda i, j: (i, j, 0))
    mem = pl.BlockSpec((1, m, d), lambda i, j: (i, 0, 0))
    return pl.pallas_call(
        _post_mixer_kernel,
        grid=(b, L // tm),
        in_specs=[row, part, part, part, part, mem, mem, _resident(norms.shape),
                  _resident(w_out.shape), _resident(w_xq.shape), _resident(w_xo.shape),
                  _resident(w_gu.shape), _resident(w_down.shape)],
        out_specs=row,
        out_shape=jax.ShapeDtypeStruct((b, L, d), F32),
        compiler_params=_params("parallel", "parallel"),
        name="post_mixer",
    )(x, *mix_parts, mk, mv, norms, w_out, w_xq, w_xo, w_gu, w_down)


def _sample_pre_kernel(x_ref, oa_ref, ob_ref, oc_ref, od_ref, norms_ref, wout_ref, wxq_ref,
                       x1_ref, q_ref):
    mixed = jnp.concatenate([oa_ref[...], ob_ref[...], oc_ref[...], od_ref[...]], axis=-1)
    x1, q = _mix_out_and_query(x_ref[...], mixed, norms_ref[...], wout_ref, wxq_ref)
    x1_ref[...] = x1
    q_ref[...] = q


def _sample_attn_kernel(q_ref, mk_ref, mv_ref, o_ref, q8, *, t):
    q8[0:t, :] = q_ref[0]
    q8[t:, :] = jnp.zeros((q8.shape[0] - t, q8.shape[1]), F32)
    o_ref[0] = _memory_attention(q8[...], _lane_heads(mk_ref), _lane_heads(mv_ref))[0:t]


def _sample_post_kernel(x_ref, att_ref, norms_ref, wxo_ref, wgu_ref, wdown_ref, o_ref):
    o_ref[...] = _attn_out_and_ffn(x_ref[...], att_ref[...], norms_ref[...], wxo_ref, wgu_ref,
                                   wdown_ref)


def post_mixer_sample(x, mix_parts, mk, mv, norms, w_out, w_xq, w_xo, w_gu, w_down):
    b, t, d = x.shape
    n = b * t
    m = mk.shape[1]
    flat = lambda a: a.reshape(n, a.shape[-1])
    full = lambda shp: pl.BlockSpec(shp, lambda i: (0,) * len(shp))
    x1, q = pl.pallas_call(
        _sample_pre_kernel,
        grid=(1,),
        in_specs=[full((n, d))] + [full((n, GROUP))] * 4
                 + [full(norms.shape), _resident(w_out.shape), _resident(w_xq.shape)],
        out_specs=[full((n, d)), full((n, d))],
        out_shape=[jax.ShapeDtypeStruct((n, d), F32)] * 2,
        compiler_params=_params("arbitrary"),
        name="sample_pre",
    )(flat(x), *[flat(p) for p in mix_parts], norms, w_out, w_xq)
    seq = pl.BlockSpec((1, t, d), lambda i: (i, 0, 0))
    mem = pl.BlockSpec((1, m, d), lambda i: (i, 0, 0))
    att = pl.pallas_call(
        functools.partial(_sample_attn_kernel, t=t),
        grid=(b,),
        in_specs=[seq, mem, mem],
        out_specs=seq,
        out_shape=jax.ShapeDtypeStruct((b, t, d), F32),
        scratch_shapes=[pltpu.VMEM((SUBLANES, d), F32)],
        compiler_params=_params("parallel"),
        name="sample_attn",
    )(q.reshape(b, t, d), mk, mv)
    out = pl.pallas_call(
        _sample_post_kernel,
        grid=(1,),
        in_specs=[full((n, d)), full((n, d)), full(norms.shape), _resident(w_xo.shape),
                  _resident(w_gu.shape), _resident(w_down.shape)],
        out_specs=full((n, d)),
        out_shape=jax.ShapeDtypeStruct((n, d), F32),
        compiler_params=_params("arbitrary"),
        name="sample_post",
    )(x1, flat(att), norms, w_xo, w_gu, w_down)
    return out.reshape(b, t, d)


def _tril(n, strict=False):
    r = lax.broadcasted_iota(jnp.int32, (n, n), 0)
    c = lax.broadcasted_iota(jnp.int32, (n, n), 1)
    return (r > c) if strict else (r >= c)


def _head_expand(n_rows, n_cols, width, offset=0):
    r = lax.broadcasted_iota(jnp.int32, (n_rows, n_cols), 0)
    c = lax.broadcasted_iota(jnp.int32, (n_rows, n_cols), 1)
    return (r == (c // width) + offset).astype(F32)


def _stage_rows(dst, src_ref, l_blk, row0=0):
    n = dst.shape[0]
    dst[row0:row0 + l_blk, :] = src_ref[0]
    if row0 + l_blk < n:
        dst[row0 + l_blk:n, :] = jnp.zeros((n - row0 - l_blk, dst.shape[1]), F32)


def _causal_conv(buf, cw_ref, cs, base=0):
    out = cw_ref[0:1, :] * buf[base + 5:base + 5 + cs, :]
    for j in range(1, CONV_W):
        out = out + cw_ref[j:j + 1, :] * buf[base + 5 + j:base + 5 + j + cs, :]
    return out


def _ssd_kernel(z_ref, x_ref, b_ref, c_ref, sm_ref, conv0_ref, h0_ref, cw_ref, cb_ref, vec_ref,
                o_ref, conv1_ref, h1_ref, buf, smbuf, hst, *, cs, l_blk, n_chunks):
    c_idx = pl.program_id(1)

    @pl.when(c_idx == 0)
    def _():
        buf[5:8, :] = conv0_ref[0]
        hst[...] = h0_ref[0]

    _stage_rows(buf.at[:, 0:GROUP], x_ref, l_blk, 8)
    _stage_rows(buf.at[:, GROUP:2 * GROUP], b_ref, l_blk, 8)
    _stage_rows(buf.at[:, 2 * GROUP:3 * GROUP], c_ref, l_blk, 8)
    _stage_rows(smbuf, sm_ref, l_blk)

    xbc = _silu(_causal_conv(buf, cw_ref, cs) + cb_ref[...])
    conv_tail = buf[5 + l_blk:8 + l_blk, :]
    buf[5:8, :] = conv_tail
    xs = xbc[:, 0:GROUP]
    bm = xbc[:, GROUP:2 * GROUP]
    cm = xbc[:, 2 * GROUP:3 * GROUP]

    dt = _softplus(smbuf[...] + vec_ref[0:1, 0:LANES])
    if l_blk < cs:
        rows = lax.broadcasted_iota(jnp.int32, (cs, LANES), 0)
        dt = jnp.where(rows < l_blk, dt, 0.0)
    a = dt * (-jnp.exp(vec_ref[1:2, 0:LANES]))
    acum = _dot_f32(_tril(cs).astype(F32), a)
    expand = _head_expand(LANES, GROUP, HEAD_DIM)
    dt_w = _dot_f32(dt, expand)
    acum_w = _dot_f32(acum, expand)
    a_last = acum_w[cs - 1:cs, :]
    xdt = xs * dt_w
    e_acum = jnp.exp(acum_w)
    xw = xdt * jnp.exp(a_last - acum_w)
    e_last = jnp.exp(a_last)
    acum_t = acum.T
    causal = _tril(cs)

    ys = []
    for h in range(HEADS):
        g = h // (HEADS // 2)
        hs = slice(h * HEAD_DIM, (h + 1) * HEAD_DIM)
        gs = slice(g * SSM_DSTATE, (g + 1) * SSM_DSTATE)
        st = hst[h]
        dec = jnp.exp(jnp.where(causal, acum[:, h:h + 1] - acum_t[h:h + 1, :], NEG_INF))
        y = _dot(_dot_nt(cm[:, gs], bm[:, gs]) * dec, xdt[:, hs])
        y = y + _dot_nt(cm[:, gs], st) * e_acum[:, hs]
        ys.append(y)
        hst[h] = e_last[:, h * HEAD_DIM:h * HEAD_DIM + 1] * st + _dot(xw[:, hs].T, bm[:, gs])
    y = jnp.concatenate(ys, axis=-1) + vec_ref[2:3, :] * xs
    o_ref[0] = _rms(y[0:l_blk] * _silu(z_ref[0]), vec_ref[3:4, :])

    @pl.when(c_idx == n_chunks - 1)
    def _():
        conv1_ref[0] = conv_tail
        h1_ref[0] = hst[...]


def ssd_mixer(proj, conv0, h0, conv_w, conv_b, dt_bias, a_log, d_skip, norm_w):
    b, L, _ = proj.shape
    cs = SSM_CHUNK
    l_blk = min(cs, L)
    n_chunks = L // l_blk
    vec = jnp.zeros((8, GROUP), F32)
    vec = vec.at[0, :HEADS].set(dt_bias).at[1, :HEADS].set(a_log)
    vec = vec.at[2].set(jnp.repeat(d_skip, HEAD_DIM)).at[3].set(norm_w)
    col = lambda k: pl.BlockSpec((1, l_blk, GROUP), lambda i, j, k=k: (i, j, k))
    per_b = lambda shp: pl.BlockSpec((1,) + shp, lambda i, j: (i,) + (0,) * len(shp))
    c3 = 3 * GROUP
    return pl.pallas_call(
        functools.partial(_ssd_kernel, cs=cs, l_blk=l_blk, n_chunks=n_chunks),
        grid=(b, n_chunks),
        in_specs=[col(COL_SSM_Z), col(COL_SSM_X), col(COL_SSM_X + 1), col(COL_SSM_X + 2),
                  pl.BlockSpec((1, l_blk, LANES), lambda i, j: (i, j, COL_SMALL)),
                  per_b((CONV_W - 1, c3)), per_b((HEADS, HEAD_DIM, SSM_DSTATE)),
                  _resident((CONV_W, c3)), _resident((1, c3)), _resident((8, GROUP))],
        out_specs=[pl.BlockSpec((1, l_blk, GROUP), lambda i, j: (i, j, 0)),
                   per_b((CONV_W - 1, c3)), per_b((HEADS, HEAD_DIM, SSM_DSTATE))],
        out_shape=[jax.ShapeDtypeStruct((b, L, GROUP), F32),
                   jax.ShapeDtypeStruct((b, CONV_W - 1, c3), F32),
                   jax.ShapeDtypeStruct((b, HEADS, HEAD_DIM, SSM_DSTATE), F32)],
        scratch_shapes=[pltpu.VMEM((cs + 8, c3), F32), pltpu.VMEM((cs, LANES), F32),
                        pltpu.VMEM((HEADS, HEAD_DIM, SSM_DSTATE), F32)],
        compiler_params=_params("parallel", "arbitrary"),
        name="ssd_mixer",
    )(proj, proj, proj, proj, proj, conv0, h0, conv_w, conv_b.reshape(1, c3), vec)


def _dot_tn(a, b):
    return lax.dot_general(a.astype(BF16), b.astype(BF16), (((0,), (0,)), ((), ())),
                           preferred_element_type=F32)


def _head_blocks(n, width):
    r = lax.broadcasted_iota(jnp.int32, (n, n), 0)
    c = lax.broadcasted_iota(jnp.int32, (n, n), 1)
    return ((r // width) == (c // width)).astype(F32)


def _dot_split(x, m01, passes, m01_left=False):
    m = m01.astype(BF16)
    acc = None
    for _ in range(passes):
        hi = x.astype(BF16)
        part = jnp.dot(m, hi, preferred_element_type=F32) if m01_left else jnp.dot(hi, m, preferred_element_type=F32)
        acc = part if acc is None else acc + part
        x = x - hi.astype(F32)
    return acc


def _head_rms_gate(o, gate, nw_row, l_blk):
    ms = _dot_split(o * o, _head_blocks(GROUP, HEAD_DIM), 2) * (1.0 / HEAD_DIM)
    return (o * lax.rsqrt(ms + EPS) * nw_row)[0:l_blk] * _silu(gate)


def _hgrn_kernel(q_ref, f_ref, i_ref, g_ref, s0_ref, lbraw_ref, nw_ref, o_ref, s1_ref,
                 stage, sst, *, cs, l_blk, n_chunks, layer):
    c_idx = pl.program_id(1)

    @pl.when(c_idx == 0)
    def _():
        sst[...] = s0_ref[0]

    if l_blk < cs:
        _stage_rows(stage.at[0], q_ref, l_blk)
        _stage_rows(stage.at[1], f_ref, l_blk)
        _stage_rows(stage.at[2], i_ref, l_blk)
        q, fx, v = stage[0], stage[1], stage[2]
    else:
        q, fx, v = q_ref[0], f_ref[0], i_ref[0]

    raw = lbraw_ref[...]
    e = jnp.exp(raw - jnp.max(raw, axis=0, keepdims=True))
    sm = e / jnp.sum(e, axis=0, keepdims=True)
    lb = jnp.zeros((1, GROUP), F32)
    for i in range(1, layer + 1):
        lb = lb + sm[i:i + 1, :]

    log_sig = jnp.minimum(fx, 0.0) - jnp.log1p(jnp.exp(-jnp.abs(fx)))
    la = jnp.log(lb)
    lbb = jnp.log1p(-lb) + log_sig
    log_f = jnp.maximum(la, lbb) + jnp.log1p(jnp.exp(-jnp.abs(la - lbb)))
    k = (1.0 - lb) * _sigmoid(-fx)
    if l_blk < cs:
        rows = lax.broadcasted_iota(jnp.int32, (cs, GROUP), 0)
        log_f = jnp.where(rows < l_blk, log_f, 0.0)
        k = jnp.where(rows < l_blk, k, 0.0)

    bcum = _dot_f32(_tril(cs).astype(F32), log_f)
    b_mid = bcum[cs // 2 - 1:cs // 2, :]
    b_last = bcum[cs - 1:cs, :]
    qe = q * jnp.exp(bcum - b_mid)
    ke = k * jnp.exp(b_mid - bcum)
    qs = q * jnp.exp(bcum)
    kd = k * jnp.exp(b_last - bcum)
    e_last = jnp.exp(b_last)
    causal = _tril(cs)

    os_ = []
    for h in range(HEADS):
        hs = slice(h * HEAD_DIM, (h + 1) * HEAD_DIM)
        st_t = sst[h]
        att = jnp.where(causal, _dot_nt(qe[:, hs], ke[:, hs]), 0.0)
        os_.append(_dot(att, v[:, hs]) + _dot_nt(qs[:, hs], st_t))
        sst[h] = e_last[:, hs] * st_t + _dot_tn(v[:, hs], kd[:, hs])
    o = jnp.concatenate(os_, axis=-1)
    o_ref[0] = _head_rms_gate(o, g_ref[0], nw_ref[...], l_blk)

    @pl.when(c_idx == n_chunks - 1)
    def _():
        s1_ref[0] = sst[...]


def hgrn_mixer(proj, s0, lb_raw, norm_w, layer):
    b, L, _ = proj.shape
    cs = LIN_CHUNK
    l_blk = min(cs, L)
    n_chunks = L // l_blk
    col = lambda k: pl.BlockSpec((1, l_blk, GROUP), lambda i, j, k=k: (i, j, k))
    st = pl.BlockSpec((1, HEADS, HEAD_DIM, HEAD_DIM), lambda i, j: (i, 0, 0, 0))
    return pl.pallas_call(
        functools.partial(_hgrn_kernel, cs=cs, l_blk=l_blk, n_chunks=n_chunks, layer=layer),
        grid=(b, n_chunks),
        in_specs=[col(COL_HGRN), col(COL_HGRN + 1), col(COL_HGRN + 2), col(COL_HGRN + 3), st,
                  _resident(lb_raw.shape), _resident((1, GROUP))],
        out_specs=[pl.BlockSpec((1, l_blk, GROUP), lambda i, j: (i, j, 0)), st],
        out_shape=[jax.ShapeDtypeStruct((b, L, GROUP), F32),
                   jax.ShapeDtypeStruct((b, HEADS, HEAD_DIM, HEAD_DIM), F32)],
        scratch_shapes=[pltpu.VMEM((3, cs, GROUP), F32),
                        pltpu.VMEM((HEADS, HEAD_DIM, HEAD_DIM), F32)],
        compiler_params=_params("parallel", "arbitrary"),
        name="hgrn_mixer",
    )(proj, proj, proj, proj, s0, lb_raw, jnp.tile(norm_w, HEADS).reshape(1, GROUP))


LIN_PREP_CHUNKS = 4


def _chunk_rows(x, cs, cps, row):
    return jnp.concatenate(
        [jnp.broadcast_to(x[i * cs + row:i * cs + row + 1, :], (cs, x.shape[1])) for i in range(cps)], axis=0)


def _hgrn_prep_kernel(q_ref, f_ref, i_ref, lbraw_ref, oi_ref, qs_ref, kd_ref, aux_ref, *, cs, cps, layer,
                      l_valid):
    rows = cs * cps
    q, fx, v = q_ref[0], f_ref[0], i_ref[0]
    raw = lbraw_ref[...]
    e = jnp.exp(raw - jnp.max(raw, axis=0, keepdims=True))
    sm = e / jnp.sum(e, axis=0, keepdims=True)
    lb = jnp.zeros((1, GROUP), F32)
    for i in range(1, layer + 1):
        lb = lb + sm[i:i + 1, :]
    log_sig = jnp.minimum(fx, 0.0) - jnp.log1p(jnp.exp(-jnp.abs(fx)))
    la = jnp.log(lb)
    lbb = jnp.log1p(-lb) + log_sig
    log_f = jnp.maximum(la, lbb) + jnp.log1p(jnp.exp(-jnp.abs(la - lbb)))
    k = (1.0 - lb) * _sigmoid(-fx)
    if l_valid is not None:
        valid = lax.broadcasted_iota(jnp.int32, (rows, GROUP), 0) % cs < l_valid
        log_f = jnp.where(valid, log_f, 0.0)
        k = jnp.where(valid, k, 0.0)

    r = lax.broadcasted_iota(jnp.int32, (rows, rows), 0)
    c = lax.broadcasted_iota(jnp.int32, (rows, rows), 1)
    incl = jnp.logical_and(r >= c, (r // cs) == (c // cs))
    bcum = _dot_split(log_f, incl.astype(F32), 3, m01_left=True)
    b_mid = _chunk_rows(bcum, cs, cps, cs // 2 - 1)
    b_last = _chunk_rows(bcum, cs, cps, cs - 1)
    qe = q * jnp.exp(bcum - b_mid)
    ke = k * jnp.exp(b_mid - bcum)
    qs_ref[0] = q * jnp.exp(bcum)
    kd_ref[0] = k * jnp.exp(b_last - bcum)
    for i in range(cps):
        aux_ref[0, i * SUBLANES:(i + 1) * SUBLANES, :] = jnp.exp(b_last[i * cs:i * cs + SUBLANES, :])
    os_ = []
    for h in range(HEADS):
        hs = slice(h * HEAD_DIM, (h + 1) * HEAD_DIM)
        att = jnp.where(incl, _dot_nt(qe[:, hs], ke[:, hs]), 0.0)
        os_.append(_dot(att, v[:, hs]))
    oi_ref[0] = jnp.concatenate(os_, axis=-1)


def _hgrn_scan_kernel(oi_ref, qs_ref, kd_ref, aux_ref, v_ref, g_ref, s0_ref, nw_ref, o_ref, s1_ref, sst,
                      *, nb, n_chunks):
    c_idx = pl.program_id(1)
    blocks = _head_blocks(GROUP, HEAD_DIM)

    @pl.when(c_idx == 0)
    def _():
        sst[...] = jnp.zeros(sst.shape, F32)
        for bi in range(nb):
            for h in range(HEADS):
                hs = slice(h * HEAD_DIM, (h + 1) * HEAD_DIM)
                sst[bi, hs, hs] = s0_ref[bi, h]

    for bi in range(nb):
        st_t = sst[bi]
        o = oi_ref[bi] + _dot_nt(qs_ref[bi], st_t)
        sst[bi] = aux_ref[bi, 0:1, :] * st_t + _dot_tn(v_ref[bi], kd_ref[bi]) * blocks
        o_ref[bi] = _head_rms_gate(o, g_ref[bi], nw_ref[...], o.shape[0])

    @pl.when(c_idx == n_chunks - 1)
    def _():
        for bi in range(nb):
            for h in range(HEADS):
                hs = slice(h * HEAD_DIM, (h + 1) * HEAD_DIM)
                s1_ref[bi, h] = sst[bi, hs, hs]


def _hgrn_prep_call(proj, lb_raw, layer, l_valid):
    g, L, _ = proj.shape
    cs = LIN_CHUNK
    cps = LIN_PREP_CHUNKS
    rows = cs * cps
    assert L % rows == 0
    col = lambda k: pl.BlockSpec((1, rows, GROUP), lambda i, j, k=k: (i, j, k))
    out = pl.BlockSpec((1, rows, GROUP), lambda i, j: (i, j, 0))
    wide = jax.ShapeDtypeStruct((g, L, GROUP), F32)
    return pl.pallas_call(
        functools.partial(_hgrn_prep_kernel, cs=cs, cps=cps, layer=layer, l_valid=l_valid),
        grid=(g, L // rows),
        in_specs=[col(COL_HGRN), col(COL_HGRN + 1), col(COL_HGRN + 2), _resident(lb_raw.shape)],
        out_specs=[out, out, out, pl.BlockSpec((1, cps * SUBLANES, GROUP), lambda i, j: (i, j, 0))],
        out_shape=[wide, wide, wide, jax.ShapeDtypeStruct((g, L // cs * SUBLANES, GROUP), F32)],
        compiler_params=_params("parallel", "parallel"),
        name="hgrn_prep",
    )(proj, proj, proj, lb_raw)


def _hgrn_scan_call(prep, proj, s0, norm_w):
    b, L, _ = proj.shape
    cs = LIN_CHUNK
    n_chunks = L // cs
    nb = 4 if b % 4 == 0 else 1
    blk = pl.BlockSpec((nb, cs, GROUP), lambda i, c: (i, c, 0))
    pcol = lambda k: pl.BlockSpec((nb, cs, GROUP), lambda i, c, k=k: (i, c, k))
    state = pl.BlockSpec((nb, HEADS, HEAD_DIM, HEAD_DIM), lambda i, c: (i, 0, 0, 0))
    return pl.pallas_call(
        functools.partial(_hgrn_scan_kernel, nb=nb, n_chunks=n_chunks),
        grid=(b // nb, n_chunks),
        in_specs=[blk, blk, blk, pl.BlockSpec((nb, SUBLANES, GROUP), lambda i, c: (i, c, 0)),
                  pcol(COL_HGRN + 2), pcol(COL_HGRN + 3), state, _resident((1, GROUP))],
        out_specs=[blk, state],
        out_shape=[jax.ShapeDtypeStruct((b, L, GROUP), F32),
                   jax.ShapeDtypeStruct((b, HEADS, HEAD_DIM, HEAD_DIM), F32)],
        scratch_shapes=[pltpu.VMEM((nb, GROUP, GROUP), F32)],
        compiler_params=_params("parallel", "arbitrary"),
        name="hgrn_scan",
    )(*prep, proj, proj, s0, jnp.tile(norm_w, HEADS).reshape(1, GROUP))


def hgrn_mixer_long(proj, s0, lb_raw, norm_w, layer):
    return _hgrn_scan_call(_hgrn_prep_call(proj, lb_raw, layer, None), proj, s0, norm_w)


def hgrn_mixer_short(proj, s0, lb_raw, norm_w, layer):
    b, L, cols = proj.shape
    cs = LIN_CHUNK
    cps = LIN_PREP_CHUNKS
    assert L <= cs and b % cps == 0
    padded = jnp.pad(proj, ((0, 0), (0, cs - L), (0, 0)))
    prep = _hgrn_prep_call(padded.reshape(b // cps, cps * cs, cols), lb_raw, layer, L)
    o, s1 = _hgrn_scan_call([p.reshape(b, -1, GROUP) for p in prep], padded, s0, norm_w)
    return o[:, :L], s1


def _doubling_level_masks(n, block):
    r = np.arange(n)[:, None]
    c = np.arange(n)[None, :]
    out, s = [], 1
    while s < block:
        out.append((r // (2 * s) == c // (2 * s)) & ((r // s) % 2 == 1) & ((c // s) % 2 == 0))
        s *= 2
    return np.stack(out).astype(np.float32)


def _unit_lower_inverse_minus_eye(a, block=None, level_masks_ref=None):
    n = a.shape[0]
    block = n if block is None else block
    r = lax.broadcasted_iota(jnp.int32, (n, n), 0)
    c = lax.broadcasted_iota(jnp.int32, (n, n), 1)
    dx = None
    s, level = 1, 0
    while s < block:
        if level_masks_ref is None:
            lower_left = jnp.logical_and((r // (2 * s)) == (c // (2 * s)),
                                         jnp.logical_and((r // s) % 2 == 1, (c // s) % 2 == 0))
            b = jnp.where(lower_left, a, 0.0)
        else:
            b = a * level_masks_ref[level]
        if dx is None:
            dx = -b
        else:
            m = b + _dot(dx, b)
            dx = dx - m - _dot(m, dx)
        s *= 2
        level += 1
    return dx


def _gdn_kernel(q_ref, k_ref, v_ref, z_ref, sm_ref, conv0_ref, s0_ref, cw_ref, vec_ref, nw_ref,
                o_ref, conv1_ref, s1_ref, buf, smbuf, sst, *, cs, l_blk, n_chunks):
    c_idx = pl.program_id(1)

    @pl.when(c_idx == 0)
    def _():
        buf[5:8, :] = conv0_ref[0]
        sst[...] = s0_ref[0]

    _stage_rows(buf.at[:, 0:GROUP], q_ref, l_blk, 8)
    _stage_rows(buf.at[:, GROUP:2 * GROUP], k_ref, l_blk, 8)
    _stage_rows(buf.at[:, 2 * GROUP:3 * GROUP], v_ref, l_blk, 8)
    _stage_rows(smbuf, sm_ref, l_blk)

    qkv = _silu(_causal_conv(buf, cw_ref, cs))
    conv_tail = buf[5 + l_blk:8 + l_blk, :]
    buf[5:8, :] = conv_tail
    q = qkv[:, 0:GROUP]
    k = qkv[:, GROUP:2 * GROUP]
    v = qkv[:, 2 * GROUP:3 * GROUP]
    blocks = _head_blocks(GROUP, HEAD_DIM)
    q = q * lax.rsqrt(_dot_f32(q * q, blocks) + EPS) * (HEAD_DIM ** -0.5)
    k = k * lax.rsqrt(_dot_f32(k * k, blocks) + EPS)

    sm = smbuf[...]
    beta = _sigmoid(sm)
    log_g = -jnp.exp(vec_ref[1:2, :]) * _softplus(sm + vec_ref[0:1, :])
    if l_blk < cs:
        rows = lax.broadcasted_iota(jnp.int32, (cs, LANES), 0)
        beta = jnp.where(rows < l_blk, beta, 0.0)
        log_g = jnp.where(rows < l_blk, log_g, 0.0)
    gam = _dot_f32(_tril(cs).astype(F32), log_g)
    beta_w = _dot_f32(beta, _head_expand(LANES, GROUP, HEAD_DIM, HEADS))
    gam_w = _dot_f32(gam, _head_expand(LANES, GROUP, HEAD_DIM, 2 * HEADS))
    g_last = gam_w[cs - 1:cs, :]
    e_gam = jnp.exp(gam_w)
    kb = k * beta_w
    vb = v * beta_w
    kbg = kb * e_gam
    qg = q * e_gam
    k_dec = k * jnp.exp(g_last - gam_w)
    e_last = jnp.exp(g_last)
    gam_t = gam.T
    incl = _tril(cs)
    strict = _tril(cs, strict=True)

    os_ = []
    for h in range(HEADS):
        hs = slice(h * HEAD_DIM, (h + 1) * HEAD_DIM)
        r = 2 * HEADS + h
        st = sst[h]
        dec_incl = jnp.exp(jnp.where(incl, gam[:, r:r + 1] - gam_t[r:r + 1, :], NEG_INF))
        a_mat = jnp.where(strict, _dot_nt(kb[:, hs], k[:, hs]) * dec_incl, 0.0)
        tx = _unit_lower_inverse_minus_eye(a_mat)
        u = vb[:, hs] + _dot(tx, vb[:, hs])
        w = kbg[:, hs] + _dot(tx, kbg[:, hs])
        v_new = u - _dot(w, st)
        qk = _dot_nt(q[:, hs], k[:, hs]) * dec_incl
        os_.append(_dot(qk, v_new) + _dot(qg[:, hs], st))
        sst[h] = e_last[:, h * HEAD_DIM:h * HEAD_DIM + 1] * st + _dot_tn(k_dec[:, hs], v_new)
    o = jnp.concatenate(os_, axis=-1)
    o_ref[0] = _head_rms_gate(o, z_ref[0], nw_ref[...], l_blk)

    @pl.when(c_idx == n_chunks - 1)
    def _():
        conv1_ref[0] = conv_tail
        s1_ref[0] = sst[...]


def gdn_mixer(proj, conv0, s0, conv_w, a_log, dt_bias, norm_w):
    b, L, _ = proj.shape
    cs = LIN_CHUNK
    l_blk = min(cs, L)
    n_chunks = L // l_blk
    vec = jnp.zeros((8, LANES), F32)
    vec = vec.at[0, 2 * HEADS:3 * HEADS].set(dt_bias).at[1, 2 * HEADS:3 * HEADS].set(a_log)
    col = lambda k: pl.BlockSpec((1, l_blk, GROUP), lambda i, j, k=k: (i, j, k))
    per_b = lambda shp: pl.BlockSpec((1,) + shp, lambda i, j: (i,) + (0,) * len(shp))
    c3 = 3 * GROUP
    return pl.pallas_call(
        functools.partial(_gdn_kernel, cs=cs, l_blk=l_blk, n_chunks=n_chunks),
        grid=(b, n_chunks),
        in_specs=[col(COL_GDN), col(COL_GDN + 1), col(COL_GDN + 2), col(COL_GDN + 3),
                  pl.BlockSpec((1, l_blk, LANES), lambda i, j: (i, j, COL_SMALL)),
                  per_b((CONV_W - 1, c3)), per_b((HEADS, HEAD_DIM, HEAD_DIM)),
                  _resident((CONV_W, c3)), _resident((8, LANES)), _resident((1, GROUP))],
        out_specs=[pl.BlockSpec((1, l_blk, GROUP), lambda i, j: (i, j, 0)),
                   per_b((CONV_W - 1, c3)), per_b((HEADS, HEAD_DIM, HEAD_DIM))],
        out_shape=[jax.ShapeDtypeStruct((b, L, GROUP), F32),
                   jax.ShapeDtypeStruct((b, CONV_W - 1, c3), F32),
                   jax.ShapeDtypeStruct((b, HEADS, HEAD_DIM, HEAD_DIM), F32)],
        scratch_shapes=[pltpu.VMEM((cs + 8, c3), F32), pltpu.VMEM((cs, LANES), F32),
                        pltpu.VMEM((HEADS, HEAD_DIM, HEAD_DIM), F32)],
        compiler_params=_params("parallel", "arbitrary"),
        name="gdn_mixer",
    )(proj, proj, proj, proj, proj, conv0, s0, conv_w, vec,
      jnp.tile(norm_w, HEADS).reshape(1, GROUP))


GDN_PREP_CHUNKS = 4


def _gdn_prep_kernel(q_ref, k_ref, v_ref, pq_ref, pk_ref, pv_ref, sm_ref, conv0_ref, cw_ref, vec_ref,
                     lvl_ref, u_ref, w_ref, qg_ref, kd_ref, qk_ref, aux_ref, buf, *, cs, cps, l_valid):
    j = pl.program_id(1)
    rows = cs * cps

    if l_valid is None:
        @pl.when(j == 0)
        def _():
            buf[5:8, :] = conv0_ref[0]

        @pl.when(j > 0)
        def _():
            for i, ref in enumerate((pq_ref, pk_ref, pv_ref)):
                buf[5:8, i * GROUP:(i + 1) * GROUP] = ref[0, SUBLANES - 3:SUBLANES, :]

        for i, ref in enumerate((q_ref, k_ref, v_ref)):
            buf[8:8 + rows, i * GROUP:(i + 1) * GROUP] = ref[0]
        qkv = _silu(_causal_conv(buf, cw_ref, rows))
    else:
        stride = cs + SUBLANES
        parts = []
        for n in range(cps):
            buf[n * stride + 5:n * stride + 8, :] = conv0_ref[0, n * (CONV_W - 1):(n + 1) * (CONV_W - 1), :]
            for i, ref in enumerate((q_ref, k_ref, v_ref)):
                buf[n * stride + 8:n * stride + 8 + cs, i * GROUP:(i + 1) * GROUP] = ref[0, n * cs:(n + 1) * cs, :]
            parts.append(_causal_conv(buf, cw_ref, cs, n * stride))
        qkv = _silu(jnp.concatenate(parts, axis=0))
    q = qkv[:, 0:GROUP]
    k = qkv[:, GROUP:2 * GROUP]
    v = qkv[:, 2 * GROUP:3 * GROUP]
    blocks = _head_blocks(GROUP, HEAD_DIM)
    q = q * lax.rsqrt(_dot_split(q * q, blocks, 2) + EPS) * (HEAD_DIM ** -0.5)
    k = k * lax.rsqrt(_dot_split(k * k, blocks, 2) + EPS)

    sm = sm_ref[0]
    beta = _sigmoid(sm)
    log_g = -jnp.exp(vec_ref[1:2, :]) * _softplus(sm + vec_ref[0:1, :])
    if l_valid is not None:
        valid = lax.broadcasted_iota(jnp.int32, (rows, LANES), 0) % cs < l_valid
        beta = jnp.where(valid, beta, 0.0)
        log_g = jnp.where(valid, log_g, 0.0)
    r = lax.broadcasted_iota(jnp.int32, (rows, rows), 0)
    c = lax.broadcasted_iota(jnp.int32, (rows, rows), 1)
    chunk_tril = jnp.logical_and(r >= c, (r // cs) == (c // cs)).astype(F32)
    gam = _dot_split(log_g, chunk_tril, 3, m01_left=True)
    beta_w = _dot_split(beta, _head_expand(LANES, GROUP, HEAD_DIM, HEADS), 3)
    gam_w = _dot_split(gam, _head_expand(LANES, GROUP, HEAD_DIM, 2 * HEADS), 3)
    g_last = jnp.concatenate(
        [jnp.broadcast_to(gam_w[(i + 1) * cs - 1:(i + 1) * cs, :], (cs, GROUP)) for i in range(cps)], axis=0)
    e_gam = jnp.exp(gam_w)
    kb = k * beta_w
    vb = v * beta_w
    kbg = kb * e_gam
    qg_ref[0] = q * e_gam
    kd_ref[0] = k * jnp.exp(g_last - gam_w)
    for i in range(cps):
        aux_ref[0, i * SUBLANES:(i + 1) * SUBLANES, :] = jnp.exp(g_last[i * cs:i * cs + SUBLANES, :])
    gam_t = gam.T
    incl = chunk_tril > 0.0

    us, ws, qks = [], [], []
    for h in range(HEADS):
        hs = slice(h * HEAD_DIM, (h + 1) * HEAD_DIM)
        lane = 2 * HEADS + h
        dec_incl = jnp.exp(jnp.where(incl, gam[:, lane:lane + 1] - gam_t[lane:lane + 1, :], NEG_INF))
        a_mat = _dot_nt(kb[:, hs], k[:, hs]) * dec_incl
        dx = _unit_lower_inverse_minus_eye(a_mat, cs, lvl_ref)
        uw = _dot(dx, jnp.concatenate([vb[:, hs], kbg[:, hs]], axis=-1))
        us.append(vb[:, hs] + uw[:, 0:HEAD_DIM])
        ws.append(kbg[:, hs] + uw[:, HEAD_DIM:2 * HEAD_DIM])
        qk = _dot_nt(q[:, hs], k[:, hs]) * dec_incl
        qks.append(jnp.concatenate([qk[i * cs:(i + 1) * cs, i * cs:(i + 1) * cs] for i in range(cps)], axis=0))
    u_ref[0] = jnp.concatenate(us, axis=-1)
    w_ref[0] = jnp.concatenate(ws, axis=-1)
    qk_ref[0] = jnp.concatenate(qks, axis=-1)


def _gdn_scan_kernel(u_ref, w_ref, qg_ref, kd_ref, qk_ref, aux_ref, z_ref, s0_ref, nw_ref,
                     o_ref, s1_ref, sst, *, nb, n_chunks):
    c_idx = pl.program_id(1)
    blocks = _head_blocks(GROUP, HEAD_DIM)

    @pl.when(c_idx == 0)
    def _():
        sst[...] = jnp.zeros(sst.shape, F32)
        for bi in range(nb):
            for h in range(HEADS):
                hs = slice(h * HEAD_DIM, (h + 1) * HEAD_DIM)
                sst[bi, hs, hs] = s0_ref[bi, h]

    for bi in range(nb):
        st = sst[bi]
        v_new = u_ref[bi] - _dot(w_ref[bi], st)
        v_bd = jnp.concatenate([v_new] * HEADS, axis=0) * blocks
        o = _dot(qk_ref[bi], v_bd) + _dot(qg_ref[bi], st)
        sst[bi] = aux_ref[bi, 0:1, :] * st + _dot_tn(kd_ref[bi], v_new) * blocks
        o_ref[bi] = _head_rms_gate(o, z_ref[bi], nw_ref[...], o.shape[0])

    @pl.when(c_idx == n_chunks - 1)
    def _():
        for bi in range(nb):
            for h in range(HEADS):
                hs = slice(h * HEAD_DIM, (h + 1) * HEAD_DIM)
                s1_ref[bi, h] = sst[bi, hs, hs]


def _gdn_prep_call(proj, conv0, conv_w, a_log, dt_bias, l_valid):
    g, L, _ = proj.shape
    cs = LIN_CHUNK
    cps = GDN_PREP_CHUNKS
    rows = cs * cps
    assert L % rows == 0 and (l_valid is None or L == rows)
    vec = jnp.zeros((8, LANES), F32)
    vec = vec.at[0, 2 * HEADS:3 * HEADS].set(dt_bias).at[1, 2 * HEADS:3 * HEADS].set(a_log)
    c3 = 3 * GROUP
    col = lambda k: pl.BlockSpec((1, rows, GROUP), lambda i, j, k=k: (i, j, k))
    prev = lambda k: pl.BlockSpec(
        (1, SUBLANES, GROUP), lambda i, j, k=k: (i, jnp.maximum(j * (rows // SUBLANES) - 1, 0), k))
    out = pl.BlockSpec((1, rows, GROUP), lambda i, j: (i, j, 0))
    wide = jax.ShapeDtypeStruct((g, L, GROUP), F32)
    level_masks = jnp.asarray(_doubling_level_masks(rows, cs))
    return pl.pallas_call(
        functools.partial(_gdn_prep_kernel, cs=cs, cps=cps, l_valid=l_valid),
        grid=(g, L // rows),
        in_specs=[col(COL_GDN), col(COL_GDN + 1), col(COL_GDN + 2),
                  prev(COL_GDN), prev(COL_GDN + 1), prev(COL_GDN + 2),
                  pl.BlockSpec((1, rows, LANES), lambda i, j: (i, j, COL_SMALL)),
                  pl.BlockSpec((1,) + conv0.shape[1:], lambda i, j: (i, 0, 0)),
                  _resident((CONV_W, c3)), _resident((8, LANES)), _resident(level_masks.shape)],
        out_specs=[out, out, out, out, out,
                   pl.BlockSpec((1, cps * SUBLANES, GROUP), lambda i, j: (i, j, 0))],
        out_shape=[wide, wide, wide, wide, wide,
                   jax.ShapeDtypeStruct((g, L // cs * SUBLANES, GROUP), F32)],
        scratch_shapes=[pltpu.VMEM((cps * (cs + SUBLANES), c3), F32)],
        compiler_params=_params("parallel", "parallel"),
        name="gdn_prep",
    )(proj, proj, proj, proj, proj, proj, proj, conv0, conv_w, vec, level_masks)


GDN_SCAN_SEQS = 4


def _gdn_scan_call(prep, proj, s0, norm_w):
    b, L, _ = proj.shape
    cs = LIN_CHUNK
    nb = GDN_SCAN_SEQS
    n_chunks = L // cs
    assert b % nb == 0 and L % cs == 0
    blk = pl.BlockSpec((nb, cs, GROUP), lambda i, c: (i, c, 0))
    state = pl.BlockSpec((nb, HEADS, HEAD_DIM, HEAD_DIM), lambda i, c: (i, 0, 0, 0))
    return pl.pallas_call(
        functools.partial(_gdn_scan_kernel, nb=nb, n_chunks=n_chunks),
        grid=(b // nb, n_chunks),
        in_specs=[blk, blk, blk, blk, blk,
                  pl.BlockSpec((nb, SUBLANES, GROUP), lambda i, c: (i, c, 0)),
                  pl.BlockSpec((nb, cs, GROUP), lambda i, c: (i, c, COL_GDN + 3)),
                  state, _resident((1, GROUP))],
        out_specs=[blk, state],
        out_shape=[jax.ShapeDtypeStruct((b, L, GROUP), F32),
                   jax.ShapeDtypeStruct((b, HEADS, HEAD_DIM, HEAD_DIM), F32)],
        scratch_shapes=[pltpu.VMEM((nb, GROUP, GROUP), F32)],
        compiler_params=_params("parallel", "arbitrary"),
        name="gdn_scan",
    )(*prep, proj, s0, jnp.tile(norm_w, HEADS).reshape(1, GROUP))


def gdn_mixer_long(proj, conv0, s0, conv_w, a_log, dt_bias, norm_w):
    L = proj.shape[1]
    prep = _gdn_prep_call(proj, conv0, conv_w, a_log, dt_bias, None)
    o, s1 = _gdn_scan_call(prep, proj, s0, norm_w)
    conv1 = proj[:, L - (CONV_W - 1):, COL_GDN * GROUP:(COL_GDN + 3) * GROUP]
    return o, conv1, s1


def gdn_mixer_short(proj, conv0, s0, conv_w, a_log, dt_bias, norm_w):
    b, L, cols = proj.shape
    cs = LIN_CHUNK
    cps = GDN_PREP_CHUNKS
    assert CONV_W - 1 <= L <= cs and b % cps == 0
    padded = jnp.pad(proj, ((0, 0), (0, cs - L), (0, 0)))
    prep = _gdn_prep_call(padded.reshape(b // cps, cps * cs, cols),
                          conv0.reshape(b // cps, cps * (CONV_W - 1), 3 * GROUP), conv_w, a_log, dt_bias, L)
    prep = [p.reshape(b, -1, GROUP) for p in prep]
    o, s1 = _gdn_scan_call(prep, padded, s0, norm_w)
    conv1 = proj[:, L - (CONV_W - 1):, COL_GDN * GROUP:(COL_GDN + 3) * GROUP]
    return o[:, :L], conv1, s1


def _moba_prompt_kernel(q_ref, k_ref, v_ref, o_ref, ks, vts, kmean, sel_t, o_t, q_tb, alibi, ml,
                        s_buf, p_buf, *, nb):
    blk = MOBA_BLOCK
    qi = pl.program_id(1)

    @pl.when(qi == 0)
    def _():
        for j in range(nb):
            kj = k_ref[0, j * blk:(j + 1) * blk, :]
            kmean[j:j + 1, :] = jnp.mean(kj, axis=0, keepdims=True)
            kjb = kj.astype(BF16)
            for h in range(HEADS):
                ks[h, j] = kjb[:, h * HEAD_DIM:(h + 1) * HEAD_DIM]
            vts[j] = v_ref[0, j * blk:(j + 1) * blk, :].T.astype(BF16)

    q_t = (q_ref[0] * (HEAD_DIM ** -0.5)).T
    q_tb[...] = (q_t * LOG2_E).astype(BF16)
    blk_row = lax.broadcasted_iota(jnp.int32, (nb, blk), 0)
    rel = (lax.broadcasted_iota(jnp.int32, (blk, blk), 1)
           - lax.broadcasted_iota(jnp.int32, (blk, blk), 0)).astype(F32)
    for h in range(HEADS):
        alibi[h] = rel * (-ALIBI_SLOPES[h] * LOG2_E)

    for h in range(HEADS):
        hs = slice(h * HEAD_DIM, (h + 1) * HEAD_DIM)
        slope = ALIBI_SLOPES[h]
        gate = jnp.where(blk_row < qi, _dot_f32(kmean[:, hs], q_t[hs, :]), NEG_INF)
        sel = jnp.full((nb, blk), NEG_INF, F32)
        for _ in range(MOBA_TOPK):
            top = jnp.max(gate, axis=0, keepdims=True)
            first = jnp.min(jnp.where(gate == top, blk_row, nb), axis=0, keepdims=True)
            pick = blk_row == first
            sel = jnp.where(jnp.logical_and(pick, blk_row < qi), 0.0, sel)
            gate = jnp.where(pick, NEG_INF, gate)
        sel_t[h] = sel
        ml[h, 0:1, :] = jnp.full((1, blk), NEG_INF, F32)
        ml[h, 1:2, :] = jnp.zeros((1, blk), F32)
    o_t[...] = jnp.zeros(o_t.shape, F32)

    def block_step(j, own):
        for h in range(HEADS):
            hs = slice(h * HEAD_DIM, (h + 1) * HEAD_DIM)
            s_buf[h] = _dot(ks[h, j], q_tb[hs, :])
        for h in range(HEADS):
            if own:
                causal = (lax.broadcasted_iota(jnp.int32, (blk, blk), 1)
                          >= lax.broadcasted_iota(jnp.int32, (blk, blk), 0))
                s = jnp.where(causal, s_buf[h] + alibi[h], NEG_INF)
            else:
                off = (qi - j).astype(F32) * (ALIBI_SLOPES[h] * LOG2_E * blk)
                s = s_buf[h] + alibi[h] + (sel_t[h, pl.ds(j, 1), :] - off)
            m = ml[h, 0:1, :]
            m_new = jnp.maximum(m, jnp.max(s, axis=0, keepdims=True))
            p = jnp.exp2(s - m_new)
            p_buf[h] = p.astype(BF16)
            alpha = jnp.exp2(m - m_new)
            ml[h, 0:1, :] = m_new
            ml[h, 1:2, :] = alpha * ml[h, 1:2, :] + jnp.sum(p, axis=0, keepdims=True)
            ml[h, 2:3, :] = alpha
        for h in range(HEADS):
            hs = slice(h * HEAD_DIM, (h + 1) * HEAD_DIM)
            o_t[hs, :] = (ml[h, 2:3, :] * o_t[hs, :]
                          + jnp.dot(vts[j, hs, :], p_buf[h], preferred_element_type=F32))

    block_step(qi, True)

    def body(j, carry):
        block_step(j, False)
        return carry

    lax.fori_loop(0, qi, body, 0)
    for h in range(HEADS):
        hs = slice(h * HEAD_DIM, (h + 1) * HEAD_DIM)
        o_t[hs, :] = o_t[hs, :] / ml[h, 1:2, :]
    o_ref[0] = o_t[...].T


def moba_prompt(proj):
    b, S, _ = proj.shape
    blk = MOBA_BLOCK
    nb = S // blk
    seq = lambda k: pl.BlockSpec((1, S, GROUP), lambda i, j, k=k: (i, 0, k))
    return pl.pallas_call(
        functools.partial(_moba_prompt_kernel, nb=nb),
        grid=(b, nb),
        in_specs=[pl.BlockSpec((1, blk, GROUP), lambda i, j: (i, j, COL_MOBA)),
                  seq(COL_MOBA + 1), seq(COL_MOBA + 2)],
        out_specs=pl.BlockSpec((1, blk, GROUP), lambda i, j: (i, j, 0)),
        out_shape=jax.ShapeDtypeStruct((b, S, GROUP), F32),
        scratch_shapes=[pltpu.VMEM((HEADS, nb, blk, HEAD_DIM), BF16),
                        pltpu.VMEM((nb, GROUP, blk), BF16),
                        pltpu.VMEM((nb, GROUP), F32),
                        pltpu.VMEM((HEADS, nb, blk), F32),
                        pltpu.VMEM((GROUP, blk), F32),
                        pltpu.VMEM((GROUP, blk), BF16),
                        pltpu.VMEM((HEADS, blk, blk), F32),
                        pltpu.VMEM((HEADS, SUBLANES, blk), F32),
                        pltpu.VMEM((HEADS, blk, blk), F32),
                        pltpu.VMEM((HEADS, blk, blk), BF16)],
        compiler_params=_params("parallel", "arbitrary"),
        name="moba_prompt",
    )(proj, proj, proj)


PAGES_PER_BLOCK = MOBA_BLOCK // PAGE_SIZE
SELECT_PAGE_BUFFERS = 64
SELECT_UNROLL = 8


def _moba_select_kernel(pt_ref, q_ref, kc_ref, idx_ref, pages, kmean_t, q8, sems, *,
                        layer, n_pages, t):
    b = pl.program_id(0)
    nbuf = SELECT_PAGE_BUFFERS
    n_blocks = n_pages // PAGES_PER_BLOCK

    def page_copy(p, slot):
        return pltpu.make_async_copy(kc_ref.at[layer, pt_ref[b, p]], pages.at[slot], sems.at[slot])

    for s in range(nbuf):
        page_copy(s, s).start()

    kmean_t[...] = jnp.zeros(kmean_t.shape, F32)
    blk_of_lane = lax.broadcasted_iota(jnp.int32, kmean_t.shape, 1)

    pages_per_trip = SELECT_UNROLL * PAGES_PER_BLOCK

    def body(trip, carry):
        p0 = trip * pages_per_trip
        for i in range(pages_per_trip):
            page_copy(p0 + i, (p0 + i) % nbuf).wait()
        km = kmean_t[...]
        for u in range(SELECT_UNROLL):
            tot = jnp.zeros((GROUP, PAGE_SIZE), F32)
            for pp in range(PAGES_PER_BLOCK):
                tot = tot + pages[(p0 + u * PAGES_PER_BLOCK + pp) % nbuf]
            mean = jnp.sum(tot, axis=1, keepdims=True) * (1.0 / MOBA_BLOCK)
            km = jnp.where(blk_of_lane == trip * SELECT_UNROLL + u, mean, km)
        kmean_t[...] = km
        for i in range(pages_per_trip):
            @pl.when(p0 + i + nbuf < n_pages)
            def _(i=i):
                page_copy(p0 + i + nbuf, (p0 + i) % nbuf).start()
        return carry

    lax.fori_loop(0, n_blocks // SELECT_UNROLL, body, 0)

    q8[0:t, :] = q_ref[0]
    q8[t:, :] = jnp.zeros((SUBLANES - t, GROUP), F32)
    blk_lane = lax.broadcasted_iota(jnp.int32, (SUBLANES, n_blocks), 1)
    out_lane = lax.broadcasted_iota(jnp.int32, (SUBLANES, LANES), 1)
    res = jnp.zeros((SUBLANES, LANES), jnp.int32)
    for h in range(HEADS):
        hs = slice(h * HEAD_DIM, (h + 1) * HEAD_DIM)
        gate = _dot_f32(q8[:, hs], kmean_t[hs, 0:n_blocks])
        for r in range(MOBA_TOPK):
            top = jnp.max(gate, axis=1, keepdims=True)
            first = jnp.min(jnp.where(gate == top, blk_lane, n_blocks), axis=1, keepdims=True)
            res = jnp.where(out_lane == h * MOBA_TOPK + r, first, res)
            gate = jnp.where(blk_lane == first, NEG_INF, gate)
    idx_ref[0] = res


def _moba_sample_attn_kernel(pt_ref, idx_ref, q_ref, kn_ref, vn_ref, kc_ref, vc_ref, o_ref,
                             kbuf, vbuf, st8, o8, rows_s, rows_o, ksem, vsem, *, layer, past, t):
    b = pl.program_id(0)
    n_seq = pl.num_programs(0)
    n_sel = MOBA_TOPK * MOBA_BLOCK
    pairs = [(tok, h) for h in range(HEADS) for tok in range(t)]

    def block_of(seq, tok, h, r):
        return idx_ref[seq, tok * (HEADS * MOBA_TOPK) + h * MOBA_TOPK + r]

    def copies(seq, i, half):
        tok, h = pairs[i]
        out = []
        for r in range(MOBA_TOPK):
            blk = block_of(seq, tok, h, r)
            for pp in range(PAGES_PER_BLOCK):
                phys = pt_ref[seq, blk * PAGES_PER_BLOCK + pp]
                lanes = pl.ds((r * PAGES_PER_BLOCK + pp) * PAGE_SIZE, PAGE_SIZE)
                out.append(pltpu.make_async_copy(kc_ref.at[layer, phys, h], kbuf.at[half, i, :, lanes],
                                                 ksem.at[half, i]))
                out.append(pltpu.make_async_copy(vc_ref.at[layer, phys, h], vbuf.at[half, i, :, lanes],
                                                 vsem.at[half, i]))
        return out

    def start_all(seq, half):
        for i in range(len(pairs)):
            for c in copies(seq, i, half):
                c.start()

    half = b % 2

    @pl.when(b == 0)
    def _():
        start_all(b, half)

    @pl.when(b + 1 < n_seq)
    def _():
        start_all(b + 1, 1 - half)

    for i, ref in enumerate((q_ref, kn_ref, vn_ref)):
        st8[i, 0:t, :] = ref[0]
        st8[i, t:, :] = jnp.zeros((SUBLANES - t, GROUP), F32)

    lane = lax.broadcasted_iota(jnp.int32, (1, n_sel), 1)
    row = lax.broadcasted_iota(jnp.int32, (SUBLANES, 1), 0)
    for i in range(len(pairs)):
        for c in copies(b, i, half):
            c.wait()
    for i, (tok, h) in enumerate(pairs):
        hs = slice(h * HEAD_DIM, (h + 1) * HEAD_DIM)
        qrow = st8[0, tok:tok + 1, hs] * (HEAD_DIM ** -0.5)
        rows_s[i:i + 1, :] = _dot(jnp.broadcast_to(qrow, (SUBLANES, HEAD_DIM)), kbuf[half, i])[0:1]
    for i, (tok, h) in enumerate(pairs):
        hs = slice(h * HEAD_DIM, (h + 1) * HEAD_DIM)
        slope = ALIBI_SLOPES[h]
        qrow = st8[0, tok:tok + 1, hs] * (HEAD_DIM ** -0.5)
        blk = jnp.where(lane < MOBA_BLOCK, block_of(b, tok, h, 0),
                        jnp.where(lane < 2 * MOBA_BLOCK, block_of(b, tok, h, 1), block_of(b, tok, h, 2)))
        pos = blk * MOBA_BLOCK + (lane % MOBA_BLOCK)
        s_sel = rows_s[i:i + 1, :] - slope * (past + tok - pos).astype(F32)
        s_own = jnp.sum(st8[1, :, hs] * qrow, axis=1, keepdims=True)
        s_own = jnp.where(row <= tok, s_own - slope * (tok - row).astype(F32), NEG_INF)
        m = jnp.maximum(jnp.max(s_sel, axis=1, keepdims=True), jnp.max(s_own, axis=0, keepdims=True))
        p_sel = jnp.exp(s_sel - m)
        p_own = jnp.exp(s_own - m)
        l = jnp.sum(p_sel, axis=1, keepdims=True) + jnp.sum(p_own, axis=0, keepdims=True)
        rows_s[i:i + 1, :] = p_sel
        rows_o[i:i + 1, 0:HEAD_DIM] = jnp.sum(p_own * st8[2, :, hs], axis=0, keepdims=True)
        rows_o[i:i + 1, HEAD_DIM:2 * HEAD_DIM] = jnp.broadcast_to(l, (1, HEAD_DIM))
    for i, (tok, h) in enumerate(pairs):
        hs = slice(h * HEAD_DIM, (h + 1) * HEAD_DIM)
        p8 = jnp.broadcast_to(rows_s[i:i + 1, :], (SUBLANES, n_sel))
        o = _dot_nt(p8, vbuf[half, i])[0:1] + rows_o[i:i + 1, 0:HEAD_DIM]
        o8[tok:tok + 1, hs] = o / rows_o[i:i + 1, HEAD_DIM:2 * HEAD_DIM]
    o_ref[0] = o8[0:t, :]


def moba_sample(proj, k_cache, v_cache, page_table, layer):
    db, t, _ = proj.shape
    depth, pool = k_cache.shape[:2]
    n_pages = page_table.shape[1]
    n_blocks = n_pages // PAGES_PER_BLOCK
    past = n_pages * PAGE_SIZE
    assert past % MOBA_BLOCK == 0 and n_pages >= SELECT_PAGE_BUFFERS and t <= SUBLANES
    assert n_blocks % SELECT_UNROLL == 0 and SELECT_PAGE_BUFFERS % (SELECT_UNROLL * PAGES_PER_BLOCK) == 0
    assert MOBA_TOPK <= n_blocks <= LANES
    tok = lambda k: pl.BlockSpec((1, t, GROUP), lambda i, *_: (i, 0, k))
    hbm = pl.BlockSpec(memory_space=pl.ANY)
    n_pairs = t * HEADS
    idx = pl.pallas_call(
        functools.partial(_moba_select_kernel, layer=layer, n_pages=n_pages, t=t),
        grid_spec=pltpu.PrefetchScalarGridSpec(
            num_scalar_prefetch=1, grid=(db,),
            in_specs=[tok(COL_MOBA), hbm],
            out_specs=pl.BlockSpec((1, SUBLANES, LANES), lambda i, *_: (i, 0, 0)),
            scratch_shapes=[pltpu.VMEM((SELECT_PAGE_BUFFERS, GROUP, PAGE_SIZE), F32),
                            pltpu.VMEM((GROUP, LANES), F32),
                            pltpu.VMEM((SUBLANES, GROUP), F32),
                            pltpu.SemaphoreType.DMA((SELECT_PAGE_BUFFERS,))]),
        out_shape=jax.ShapeDtypeStruct((db, SUBLANES, LANES), jnp.int32),
        compiler_params=_params("arbitrary"),
        name="moba_sample_select",
    )(page_table, proj, k_cache.reshape(depth, pool, GROUP, PAGE_SIZE))
    idx = idx[:, :t, :HEADS * MOBA_TOPK].reshape(db, t * HEADS * MOBA_TOPK)
    return pl.pallas_call(
        functools.partial(_moba_sample_attn_kernel, layer=layer, past=past, t=t),
        grid_spec=pltpu.PrefetchScalarGridSpec(
            num_scalar_prefetch=2, grid=(db,),
            in_specs=[tok(COL_MOBA), tok(COL_MOBA + 1), tok(COL_MOBA + 2), hbm, hbm],
            out_specs=pl.BlockSpec((1, t, GROUP), lambda i, *_: (i, 0, 0)),
            scratch_shapes=[pltpu.VMEM((2, n_pairs, HEAD_DIM, MOBA_TOPK * MOBA_BLOCK), F32),
                            pltpu.VMEM((2, n_pairs, HEAD_DIM, MOBA_TOPK * MOBA_BLOCK), F32),
                            pltpu.VMEM((3, SUBLANES, GROUP), F32),
                            pltpu.VMEM((SUBLANES, GROUP), F32),
                            pltpu.VMEM((n_pairs, MOBA_TOPK * MOBA_BLOCK), F32),
                            pltpu.VMEM((n_pairs, LANES), F32),
                            pltpu.SemaphoreType.DMA((2, n_pairs)),
                            pltpu.SemaphoreType.DMA((2, n_pairs))]),
        out_shape=jax.ShapeDtypeStruct((db, t, GROUP), F32),
        compiler_params=_params("arbitrary"),
        name="moba_sample_attn",
    )(page_table, idx, proj, proj, proj, k_cache, v_cache)


def _rearrange_w_in(w):
    d = w.shape[0]
    dt0 = COL_HGRN * GROUP
    ba0 = dt0 + HEADS + (COL_GDN + 4 - COL_HGRN) * GROUP
    wide = jnp.concatenate([w[:, :dt0], w[:, dt0 + HEADS:ba0]], axis=1)
    narrow = jnp.concatenate([w[:, dt0:dt0 + HEADS], w[:, ba0:ba0 + 2 * HEADS],
                              jnp.zeros((d, LANES - N_SMALL), w.dtype)], axis=1)
    return jnp.concatenate([wide, narrow], axis=1)


def _layer(x, mem_k, mem_v, states, lp, layer, attend, prompt):
    b, L, d = x.shape
    ssm_conv0, ssm0, hgrn0, gdn_conv0, gdn0 = states
    w_in = lp['w_in'].astype(BF16) if prompt else lp['w_in']
    proj = norm_matmul(x.reshape(b * L, d), lp['n_mix_pre'], w_in, min(IN_PROJ_ROWS, b * L)).reshape(b, L, -1)
    o_a = attend(proj)
    o_b, ssm_conv1, ssm1 = ssd_mixer(proj, ssm_conv0, ssm0, lp['ssm_conv_w'], lp['ssm_conv_b'],
                                     lp['ssm_dt_bias'], lp['ssm_a_log'], lp['ssm_d'], lp['ssm_norm'])
    hgrn = hgrn_mixer_long if L % (LIN_PREP_CHUNKS * LIN_CHUNK) == 0 else hgrn_mixer_short
    o_c, hgrn1 = hgrn(proj, jnp.swapaxes(hgrn0, -1, -2), lp['hgrn_lb_raw'], lp['hgrn_norm'], layer)
    gdn = gdn_mixer_long if L % (GDN_PREP_CHUNKS * LIN_CHUNK) == 0 else gdn_mixer_short
    o_d, gdn_conv1, gdn1 = gdn(proj, gdn_conv0, gdn0, lp['gdn_conv_w'], lp['gdn_a_log'],
                               lp['gdn_dt_bias'], lp['gdn_norm'])
    post = post_mixer if prompt else post_mixer_sample
    args = (x, (o_a, o_b, o_c, o_d), mem_k, mem_v, lp['norms'], lp['w_out'], lp['w_xq'], lp['w_xo'],
            lp['w_gu'], lp['w_down'])
    x = post(*args, POST_MIXER_ROWS) if prompt else post(*args)
    k = proj[..., GROUP:2 * GROUP].reshape(b, L, HEADS, HEAD_DIM)
    v = proj[..., 2 * GROUP:3 * GROUP].reshape(b, L, HEADS, HEAD_DIM)
    return x, (k, v, ssm_conv1, ssm1, jnp.swapaxes(hgrn1, -1, -2), gdn_conv1, gdn1)


def kernel(x_prompt, x_sample, mem_prompt, cache_moba_k, cache_moba_v, page_table, cache_mem_k, cache_mem_v, state_ssm_conv, state_ssm, state_hgrn, state_gdn_conv, state_gdn, n_mix_pre, n_mix_post, w_in, w_out, ssm_conv_w, ssm_conv_b, ssm_dt_bias, ssm_a_log, ssm_d, ssm_norm, hgrn_lb_raw, hgrn_norm, gdn_conv_w, gdn_a_log, gdn_dt_bias, gdn_norm, n_x_pre, n_x_post, mem_norm, w_xq, w_xkv, w_xo, n_f_pre, n_f_post, w_gu, w_down):
    depth = w_in.shape[0]
    bp, _, d = x_prompt.shape
    db = x_sample.shape[0]
    n_mem = mem_prompt.shape[1]
    kc = jnp.transpose(cache_moba_k, (0, 1, 3, 4, 2))
    vc = jnp.transpose(cache_moba_v, (0, 1, 3, 4, 2))
    zeros_p = (jnp.zeros((bp,) + state_ssm_conv.shape[2:], F32), jnp.zeros((bp,) + state_ssm.shape[2:], F32),
               jnp.zeros((bp,) + state_hgrn.shape[2:], F32), jnp.zeros((bp,) + state_gdn_conv.shape[2:], F32),
               jnp.zeros((bp,) + state_gdn.shape[2:], F32))
    yp, ys = x_prompt, x_sample
    outs_p, outs_s, mem_ks, mem_vs = [], [], [], []
    for l in range(depth):
        norms = jnp.zeros((SUBLANES, d), F32)
        for i, nrm in enumerate((n_mix_post, n_x_pre, n_x_post, n_f_pre, n_f_post)):
            norms = norms.at[i].set(nrm[l])
        lp = {'n_mix_pre': n_mix_pre[l], 'w_in': _rearrange_w_in(w_in[l]), 'norms': norms,
              'w_out': w_out[l].astype(BF16), 'w_xq': w_xq[l].astype(BF16), 'w_xo': w_xo[l].astype(BF16),
              'w_gu': w_gu[l].astype(BF16), 'w_down': w_down[l].astype(BF16),
              'ssm_conv_w': ssm_conv_w[l], 'ssm_conv_b': ssm_conv_b[l], 'ssm_dt_bias': ssm_dt_bias[l],
              'ssm_a_log': ssm_a_log[l], 'ssm_d': ssm_d[l], 'ssm_norm': ssm_norm[l],
              'hgrn_lb_raw': hgrn_lb_raw, 'hgrn_norm': hgrn_norm[l], 'gdn_conv_w': gdn_conv_w[l],
              'gdn_a_log': gdn_a_log[l], 'gdn_dt_bias': gdn_dt_bias[l], 'gdn_norm': gdn_norm[l]}
        mkv = norm_matmul(mem_prompt.reshape(bp * n_mem, d), mem_norm[l], w_xkv[l].astype(BF16), 256)
        mk = mkv[:, :d].reshape(bp, n_mem, d)
        mv = mkv[:, d:].reshape(bp, n_mem, d)
        yp, st_p = _layer(yp, mk.astype(BF16), mv.astype(BF16), zeros_p, lp, l, moba_prompt, True)
        outs_p.append(st_p)
        mem_ks.append(mk.reshape(bp, n_mem, X_HEADS, d // X_HEADS))
        mem_vs.append(mv.reshape(bp, n_mem, X_HEADS, d // X_HEADS))
        states_s = (state_ssm_conv[l], state_ssm[l], state_hgrn[l], state_gdn_conv[l], state_gdn[l])
        attend_s = functools.partial(moba_sample, k_cache=kc, v_cache=vc, page_table=page_table, layer=l)
        ys, st_s = _layer(ys, cache_mem_k[l].reshape(db, n_mem, d).astype(BF16),
                          cache_mem_v[l].reshape(db, n_mem, d).astype(BF16), states_s, lp, l, attend_s, False)
        outs_s.append(st_s)
    stack = lambda outs, i: jnp.stack([o[i] for o in outs], axis=0)
    return (yp, ys, stack(outs_p, 0), stack(outs_s, 0), stack(outs_p, 1), stack(outs_s, 1),
            jnp.stack(mem_ks, axis=0), jnp.stack(mem_vs, axis=0),
            stack(outs_p, 2), stack(outs_s, 2), stack(outs_p, 3), stack(outs_s, 3),
            stack(outs_p, 4), stack(outs_s, 4), stack(outs_p, 5), stack(outs_s, 5),
            stack(outs_p, 6), stack(outs_s, 6))
```

```python
import functools

import numpy as np
import jax
import jax.numpy as jnp
from jax import lax
from jax.experimental import pallas as pl
from jax.experimental.pallas import tpu as pltpu

F32 = jnp.float32
BF16 = jnp.bfloat16

LANES = 128
SUBLANES = 8
VMEM_LIMIT_BYTES = 56 * 1024 * 1024

GROUP = 256
HEADS = 4
HEAD_DIM = 64
CONV_W = 4
SSM_DSTATE = 128
SSM_CHUNK = 128
LIN_CHUNK = 64
MOBA_BLOCK = 256
MOBA_TOPK = 3
PAGE_SIZE = 128
X_HEADS = 4
N_SMALL = 12
COL_MOBA = 0
COL_SSM_Z = 3
COL_SSM_X = 4
COL_HGRN = 7
COL_GDN = 11
N_WIDE = 15 * GROUP
COL_SMALL = N_WIDE // LANES
EPS = 1e-6
NEG_INF = float("-inf")
LOG2_E = 1.4426950408889634
ALIBI_SLOPES = tuple(2.0 ** (-8.0 * (h + 1) / HEADS) for h in range(HEADS))


def _rms(x, w):
    return x * lax.rsqrt(jnp.mean(x * x, axis=-1, keepdims=True) + EPS) * w


def _sigmoid(x):
    return 1.0 / (1.0 + jnp.exp(-x))


def _silu(x):
    return x * _sigmoid(x)


def _softplus(x):
    return jnp.maximum(x, 0.0) + jnp.log(1.0 + jnp.exp(-jnp.abs(x)))


def _dot(a, b):
    return jnp.dot(a.astype(BF16), b.astype(BF16), preferred_element_type=F32)


def _dot_nt(a, b):
    return lax.dot_general(a.astype(BF16), b.astype(BF16), (((1,), (1,)), ((), ())),
                           preferred_element_type=F32)


def _dot_f32(a, b):
    return jnp.dot(a, b, preferred_element_type=F32, precision=lax.Precision.HIGHEST)


def _params(*sem, flags=None):
    return pltpu.CompilerParams(dimension_semantics=sem, vmem_limit_bytes=VMEM_LIMIT_BYTES, flags=flags)


def _resident(shape):
    return pl.BlockSpec(shape, lambda *_: (0,) * len(shape), pipeline_mode=pl.Buffered(1))


def _norm_matmul_kernel(x_ref, nw_ref, w_ref, o_ref, *, full_precision):
    xn = _rms(x_ref[...], nw_ref[...])
    o_ref[...] = _dot_f32(xn, w_ref[...]) if full_precision else _dot(xn, w_ref[...])


def norm_matmul(x, norm_w, w, tm):
    n, d = x.shape
    c = w.shape[1]
    return pl.pallas_call(
        functools.partial(_norm_matmul_kernel, full_precision=(w.dtype == F32)),
        grid=(n // tm,),
        in_specs=[pl.BlockSpec((tm, d), lambda i: (i, 0)),
                  _resident((1, d)),
                  _resident((d, c))],
        out_specs=pl.BlockSpec((tm, c), lambda i: (i, 0)),
        out_shape=jax.ShapeDtypeStruct((n, c), F32),
        compiler_params=_params("parallel"),
        name="norm_matmul",
    )(x, norm_w.reshape(1, d), w)


POST_MIXER_ROWS = 512
IN_PROJ_ROWS = 512

def _mix_out_and_query(x, mixed, nw, wout_ref, wxq_ref):
    x = x + _rms(_dot(mixed, wout_ref[...]), nw[0:1])
    return x, _dot(_rms(x, nw[1:2]), wxq_ref[...])


def _memory_attention(q, head_k, head_v):
    xdh = q.shape[1] // X_HEADS
    heads = []
    for h in range(X_HEADS):
        s = _dot_nt(q[:, h * xdh:(h + 1) * xdh], head_k(h)) * (xdh ** -0.5)
        p = jnp.exp(s - jnp.max(s, axis=-1, keepdims=True))
        heads.append(_dot(p, head_v(h)) / jnp.sum(p, axis=-1, keepdims=True))
    return jnp.concatenate(heads, axis=-1)


def _lane_heads(ref):
    xdh = ref.shape[-1] // X_HEADS
    return lambda h: ref[0, :, h * xdh:(h + 1) * xdh]


def _attn_out_and_ffn(x, att, nw, wxo_ref, wgu_ref, wdown_ref):
    x = x + _rms(_dot(att, wxo_ref[...]), nw[2:3])
    gu = _dot(_rms(x, nw[3:4]), wgu_ref[...])
    hid = gu.shape[1] // 2
    act = _silu(gu[:, :hid]) * gu[:, hid:]
    return x + _rms(_dot(act, wdown_ref[...]), nw[4:5])


def _post_mixer_kernel(x_ref, oa_ref, ob_ref, oc_ref, od_ref, mk_ref, mv_ref, norms_ref, wout_ref,
                       wxq_ref, wxo_ref, wgu_ref, wdown_ref, o_ref):
    nw = norms_ref[...]
    mixed = jnp.concatenate([oa_ref[0], ob_ref[0], oc_ref[0], od_ref[0]], axis=-1)
    x, q = _mix_out_and_query(x_ref[0], mixed, nw, wout_ref, wxq_ref)
    att = _memory_attention(q, _lane_heads(mk_ref), _lane_heads(mv_ref))
    o_ref[0] = _attn_out_and_ffn(x, att, nw, wxo_ref, wgu_ref, wdown_ref)


def post_mixer(x, mix_parts, mk, mv, norms, w_out, w_xq, w_xo, w_gu, w_down, tm):
    b, L, d = x.shape
    m = mk.shape[1]
    row = pl.BlockSpec((1, tm, d), lambda i, j: (i, j, 0))
    part = pl.BlockSpec((1, tm, GROUP), lambda i, j: (i, j, 0))
    mem = pl.BlockSpec((1, m, d), lambda i, j: (i, 0, 0))
    return pl.pallas_call(
        _post_mixer_kernel,
        grid=(b, L // tm),
        in_specs=[row, part, part, part, part, mem, mem, _resident(norms.shape),
                  _resident(w_out.shape), _resident(w_xq.shape), _resident(w_xo.shape),
                  _resident(w_gu.shape), _resident(w_down.shape)],
        out_specs=row,
        out_shape=jax.ShapeDtypeStruct((b, L, d), F32),
        compiler_params=_params("parallel", "parallel"),
        name="post_mixer",
    )(x, *mix_parts, mk, mv, norms, w_out, w_xq, w_xo, w_gu, w_down)


def _sample_pre_kernel(x_ref, oa_ref, ob_ref, oc_ref, od_ref, norms_ref, wout_ref, wxq_ref,
                       x1_ref, q_ref):
    mixed = jnp.concatenate([oa_ref[...], ob_ref[...], oc_ref[...], od_ref[...]], axis=-1)
    x1, q = _mix_out_and_query(x_ref[...], mixed, norms_ref[...], wout_ref, wxq_ref)
    x1_ref[...] = x1
    q_ref[...] = q


def _sample_attn_kernel(q_ref, mk_ref, mv_ref, o_ref, q8, *, t):
    q8[0:t, :] = q_ref[0]
    q8[t:, :] = jnp.zeros((q8.shape[0] - t, q8.shape[1]), F32)
    o_ref[0] = _memory_attention(q8[...], _lane_heads(mk_ref), _lane_heads(mv_ref))[0:t]


def _sample_post_kernel(x_ref, att_ref, norms_ref, wxo_ref, wgu_ref, wdown_ref, o_ref):
    o_ref[...] = _attn_out_and_ffn(x_ref[...], att_ref[...], norms_ref[...], wxo_ref, wgu_ref,
                                   wdown_ref)


def post_mixer_sample(x, mix_parts, mk, mv, norms, w_out, w_xq, w_xo, w_gu, w_down):
    b, t, d = x.shape
    n = b * t
    m = mk.shape[1]
    flat = lambda a: a.reshape(n, a.shape[-1])
    full = lambda shp: pl.BlockSpec(shp, lambda i: (0,) * len(shp))
    x1, q = pl.pallas_call(
        _sample_pre_kernel,
        grid=(1,),
        in_specs=[full((n, d))] + [full((n, GROUP))] * 4
                 + [full(norms.shape), _resident(w_out.shape), _resident(w_xq.shape)],
        out_specs=[full((n, d)), full((n, d))],
        out_shape=[jax.ShapeDtypeStruct((n, d), F32)] * 2,
        compiler_params=_params("arbitrary"),
        name="sample_pre",
    )(flat(x), *[flat(p) for p in mix_parts], norms, w_out, w_xq)
    seq = pl.BlockSpec((1, t, d), lambda i: (i, 0, 0))
    mem = pl.BlockSpec((1, m, d), lambda i: (i, 0, 0))
    att = pl.pallas_call(
        functools.partial(_sample_attn_kernel, t=t),
        grid=(b,),
        in_specs=[seq, mem, mem],
        out_specs=seq,
        out_shape=jax.ShapeDtypeStruct((b, t, d), F32),
        scratch_shapes=[pltpu.VMEM((SUBLANES, d), F32)],
        compiler_params=_params("parallel"),
        name="sample_attn",
    )(q.reshape(b, t, d), mk, mv)
    out = pl.pallas_call(
        _sample_post_kernel,
        grid=(1,),
        in_specs=[full((n, d)), full((n, d)), full(norms.shape), _resident(w_xo.shape),
                  _resident(w_gu.shape), _resident(w_down.shape)],
        out_specs=full((n, d)),
        out_shape=jax.ShapeDtypeStruct((n, d), F32),
        compiler_params=_params("arbitrary"),
        name="sample_post",
    )(x1, flat(att), norms, w_xo, w_gu, w_down)
    return out.reshape(b, t, d)


def _tril(n, strict=False):
    r = lax.broadcasted_iota(jnp.int32, (n, n), 0)
    c = lax.broadcasted_iota(jnp.int32, (n, n), 1)
    return (r > c) if strict else (r >= c)


def _head_expand(n_rows, n_cols, width, offset=0):
    r = lax.broadcasted_iota(jnp.int32, (n_rows, n_cols), 0)
    c = lax.broadcasted_iota(jnp.int32, (n_rows, n_cols), 1)
    return (r == (c // width) + offset).astype(F32)


def _stage_rows(dst, src_ref, l_blk, row0=0):
    n = dst.shape[0]
    dst[row0:row0 + l_blk, :] = src_ref[0]
    if row0 + l_blk < n:
        dst[row0 + l_blk:n, :] = jnp.zeros((n - row0 - l_blk, dst.shape[1]), F32)


def _causal_conv(buf, cw_ref, cs, base=0):
    out = cw_ref[0:1, :] * buf[base + 5:base + 5 + cs, :]
    for j in range(1, CONV_W):
        out = out + cw_ref[j:j + 1, :] * buf[base + 5 + j:base + 5 + j + cs, :]
    return out


def _ssd_kernel(z_ref, x_ref, b_ref, c_ref, sm_ref, conv0_ref, h0_ref, cw_ref, cb_ref, vec_ref,
                o_ref, conv1_ref, h1_ref, buf, smbuf, hst, *, cs, l_blk, n_chunks):
    c_idx = pl.program_id(1)

    @pl.when(c_idx == 0)
    def _():
        buf[5:8, :] = conv0_ref[0]
        hst[...] = h0_ref[0]

    _stage_rows(buf.at[:, 0:GROUP], x_ref, l_blk, 8)
    _stage_rows(buf.at[:, GROUP:2 * GROUP], b_ref, l_blk, 8)
    _stage_rows(buf.at[:, 2 * GROUP:3 * GROUP], c_ref, l_blk, 8)
    _stage_rows(smbuf, sm_ref, l_blk)

    xbc = _silu(_causal_conv(buf, cw_ref, cs) + cb_ref[...])
    conv_tail = buf[5 + l_blk:8 + l_blk, :]
    buf[5:8, :] = conv_tail
    xs = xbc[:, 0:GROUP]
    bm = xbc[:, GROUP:2 * GROUP]
    cm = xbc[:, 2 * GROUP:3 * GROUP]

    dt = _softplus(smbuf[...] + vec_ref[0:1, 0:LANES])
    if l_blk < cs:
        rows = lax.broadcasted_iota(jnp.int32, (cs, LANES), 0)
        dt = jnp.where(rows < l_blk, dt, 0.0)
    a = dt * (-jnp.exp(vec_ref[1:2, 0:LANES]))
    acum = _dot_split(a, _tril(cs).astype(F32), 3, m01_left=True)
    expand = _head_expand(LANES, GROUP, HEAD_DIM)
    dt_w = _dot_split(dt, expand, 3)
    acum_w = _dot_split(acum, expand, 3)
    a_last = acum_w[cs - 1:cs, :]
    xdt = xs * dt_w
    e_acum = jnp.exp(acum_w)
    xw = xdt * jnp.exp(a_last - acum_w)
    e_last = jnp.exp(a_last)
    acum_t = acum.T
    causal = _tril(cs)

    ys = []
    for h in range(HEADS):
        g = h // (HEADS // 2)
        hs = slice(h * HEAD_DIM, (h + 1) * HEAD_DIM)
        gs = slice(g * SSM_DSTATE, (g + 1) * SSM_DSTATE)
        st = hst[h]
        dec = jnp.exp(jnp.where(causal, acum[:, h:h + 1] - acum_t[h:h + 1, :], NEG_INF))
        y = _dot(_dot_nt(cm[:, gs], bm[:, gs]) * dec, xdt[:, hs])
        y = y + _dot_nt(cm[:, gs], st) * e_acum[:, hs]
        ys.append(y)
        hst[h] = e_last[:, h * HEAD_DIM:h * HEAD_DIM + 1] * st + _dot(xw[:, hs].T, bm[:, gs])
    y = jnp.concatenate(ys, axis=-1) + vec_ref[2:3, :] * xs
    o_ref[0] = _rms(y[0:l_blk] * _silu(z_ref[0]), vec_ref[3:4, :])

    @pl.when(c_idx == n_chunks - 1)
    def _():
        conv1_ref[0] = conv_tail
        h1_ref[0] = hst[...]


def ssd_mixer(proj, conv0, h0, conv_w, conv_b, dt_bias, a_log, d_skip, norm_w):
    b, L, _ = proj.shape
    cs = SSM_CHUNK
    l_blk = min(cs, L)
    n_chunks = L // l_blk
    vec = jnp.zeros((8, GROUP), F32)
    vec = vec.at[0, :HEADS].set(dt_bias).at[1, :HEADS].set(a_log)
    vec = vec.at[2].set(jnp.repeat(d_skip, HEAD_DIM)).at[3].set(norm_w)
    col = lambda k: pl.BlockSpec((1, l_blk, GROUP), lambda i, j, k=k: (i, j, k))
    per_b = lambda shp: pl.BlockSpec((1,) + shp, lambda i, j: (i,) + (0,) * len(shp))
    c3 = 3 * GROUP
    return pl.pallas_call(
        functools.partial(_ssd_kernel, cs=cs, l_blk=l_blk, n_chunks=n_chunks),
        grid=(b, n_chunks),
        in_specs=[col(COL_SSM_Z), col(COL_SSM_X), col(COL_SSM_X + 1), col(COL_SSM_X + 2),
                  pl.BlockSpec((1, l_blk, LANES), lambda i, j: (i, j, COL_SMALL)),
                  per_b((CONV_W - 1, c3)), per_b((HEADS, HEAD_DIM, SSM_DSTATE)),
                  _resident((CONV_W, c3)), _resident((1, c3)), _resident((8, GROUP))],
        out_specs=[pl.BlockSpec((1, l_blk, GROUP), lambda i, j: (i, j, 0)),
                   per_b((CONV_W - 1, c3)), per_b((HEADS, HEAD_DIM, SSM_DSTATE))],
        out_shape=[jax.ShapeDtypeStruct((b, L, GROUP), F32),
                   jax.ShapeDtypeStruct((b, CONV_W - 1, c3), F32),
                   jax.ShapeDtypeStruct((b, HEADS, HEAD_DIM, SSM_DSTATE), F32)],
        scratch_shapes=[pltpu.VMEM((cs + 8, c3), F32), pltpu.VMEM((cs, LANES), F32),
                        pltpu.VMEM((HEADS, HEAD_DIM, SSM_DSTATE), F32)],
        compiler_params=_params("parallel", "arbitrary"),
        name="ssd_mixer",
    )(proj, proj, proj, proj, proj, conv0, h0, conv_w, conv_b.reshape(1, c3), vec)


def _dot_tn(a, b):
    return lax.dot_general(a.astype(BF16), b.astype(BF16), (((0,), (0,)), ((), ())),
                           preferred_element_type=F32)


def _head_blocks(n, width):
    r = lax.broadcasted_iota(jnp.int32, (n, n), 0)
    c = lax.broadcasted_iota(jnp.int32, (n, n), 1)
    return ((r // width) == (c // width)).astype(F32)


def _dot_split(x, m01, passes, m01_left=False):
    m = m01.astype(BF16)
    acc = None
    for _ in range(passes):
        hi = x.astype(BF16)
        part = jnp.dot(m, hi, preferred_element_type=F32) if m01_left else jnp.dot(hi, m, preferred_element_type=F32)
        acc = part if acc is None else acc + part
        x = x - hi.astype(F32)
    return acc


def _head_rms_gate(o, gate, nw_row, l_blk):
    ms = _dot_split(o * o, _head_blocks(GROUP, HEAD_DIM), 2) * (1.0 / HEAD_DIM)
    return (o * lax.rsqrt(ms + EPS) * nw_row)[0:l_blk] * _silu(gate)


def _hgrn_kernel(q_ref, f_ref, i_ref, g_ref, s0_ref, lbraw_ref, nw_ref, o_ref, s1_ref,
                 stage, sst, *, cs, l_blk, n_chunks, layer):
    c_idx = pl.program_id(1)

    @pl.when(c_idx == 0)
    def _():
        sst[...] = s0_ref[0]

    if l_blk < cs:
        _stage_rows(stage.at[0], q_ref, l_blk)
        _stage_rows(stage.at[1], f_ref, l_blk)
        _stage_rows(stage.at[2], i_ref, l_blk)
        q, fx, v = stage[0], stage[1], stage[2]
    else:
        q, fx, v = q_ref[0], f_ref[0], i_ref[0]

    raw = lbraw_ref[...]
    e = jnp.exp(raw - jnp.max(raw, axis=0, keepdims=True))
    sm = e / jnp.sum(e, axis=0, keepdims=True)
    lb = jnp.zeros((1, GROUP), F32)
    for i in range(1, layer + 1):
        lb = lb + sm[i:i + 1, :]

    log_sig = jnp.minimum(fx, 0.0) - jnp.log1p(jnp.exp(-jnp.abs(fx)))
    la = jnp.log(lb)
    lbb = jnp.log1p(-lb) + log_sig
    log_f = jnp.maximum(la, lbb) + jnp.log1p(jnp.exp(-jnp.abs(la - lbb)))
    k = (1.0 - lb) * _sigmoid(-fx)
    if l_blk < cs:
        rows = lax.broadcasted_iota(jnp.int32, (cs, GROUP), 0)
        log_f = jnp.where(rows < l_blk, log_f, 0.0)
        k = jnp.where(rows < l_blk, k, 0.0)

    bcum = _dot_f32(_tril(cs).astype(F32), log_f)
    b_mid = bcum[cs // 2 - 1:cs // 2, :]
    b_last = bcum[cs - 1:cs, :]
    qe = q * jnp.exp(bcum - b_mid)
    ke = k * jnp.exp(b_mid - bcum)
    qs = q * jnp.exp(bcum)
    kd = k * jnp.exp(b_last - bcum)
    e_last = jnp.exp(b_last)
    causal = _tril(cs)

    os_ = []
    for h in range(HEADS):
        hs = slice(h * HEAD_DIM, (h + 1) * HEAD_DIM)
        st_t = sst[h]
        att = jnp.where(causal, _dot_nt(qe[:, hs], ke[:, hs]), 0.0)
        os_.append(_dot(att, v[:, hs]) + _dot_nt(qs[:, hs], st_t))
        sst[h] = e_last[:, hs] * st_t + _dot_tn(v[:, hs], kd[:, hs])
    o = jnp.concatenate(os_, axis=-1)
    o_ref[0] = _head_rms_gate(o, g_ref[0], nw_ref[...], l_blk)

    @pl.when(c_idx == n_chunks - 1)
    def _():
        s1_ref[0] = sst[...]


def hgrn_mixer(proj, s0, lb_raw, norm_w, layer):
    b, L, _ = proj.shape
    cs = LIN_CHUNK
    l_blk = min(cs, L)
    n_chunks = L // l_blk
    col = lambda k: pl.BlockSpec((1, l_blk, GROUP), lambda i, j, k=k: (i, j, k))
    st = pl.BlockSpec((1, HEADS, HEAD_DIM, HEAD_DIM), lambda i, j: (i, 0, 0, 0))
    return pl.pallas_call(
        functools.partial(_hgrn_kernel, cs=cs, l_blk=l_blk, n_chunks=n_chunks, layer=layer),
        grid=(b, n_chunks),
        in_specs=[col(COL_HGRN), col(COL_HGRN + 1), col(COL_HGRN + 2), col(COL_HGRN + 3), st,
                  _resident(lb_raw.shape), _resident((1, GROUP))],
        out_specs=[pl.BlockSpec((1, l_blk, GROUP), lambda i, j: (i, j, 0)), st],
        out_shape=[jax.ShapeDtypeStruct((b, L, GROUP), F32),
                   jax.ShapeDtypeStruct((b, HEADS, HEAD_DIM, HEAD_DIM), F32)],
        scratch_shapes=[pltpu.VMEM((3, cs, GROUP), F32),
                        pltpu.VMEM((HEADS, HEAD_DIM, HEAD_DIM), F32)],
        compiler_params=_params("parallel", "arbitrary"),
        name="hgrn_mixer",
    )(proj, proj, proj, proj, s0, lb_raw, jnp.tile(norm_w, HEADS).reshape(1, GROUP))


LIN_PREP_CHUNKS = 4


def _chunk_rows(x, cs, cps, row):
    return jnp.concatenate(
        [jnp.broadcast_to(x[i * cs + row:i * cs + row + 1, :], (cs, x.shape[1])) for i in range(cps)], axis=0)


def _hgrn_prep_kernel(q_ref, f_ref, i_ref, lbraw_ref, oi_ref, qs_ref, kd_ref, aux_ref, *, cs, cps, layer,
                      l_valid):
    rows = cs * cps
    q, fx, v = q_ref[0], f_ref[0], i_ref[0]
    raw = lbraw_ref[...]
    e = jnp.exp(raw - jnp.max(raw, axis=0, keepdims=True))
    sm = e / jnp.sum(e, axis=0, keepdims=True)
    lb = jnp.zeros((1, GROUP), F32)
    for i in range(1, layer + 1):
        lb = lb + sm[i:i + 1, :]
    log_sig = jnp.minimum(fx, 0.0) - jnp.log1p(jnp.exp(-jnp.abs(fx)))
    la = jnp.log(lb)
    lbb = jnp.log1p(-lb) + log_sig
    log_f = jnp.maximum(la, lbb) + jnp.log1p(jnp.exp(-jnp.abs(la - lbb)))
    k = (1.0 - lb) * _sigmoid(-fx)
    if l_valid is not None:
        valid = lax.broadcasted_iota(jnp.int32, (rows, GROUP), 0) % cs < l_valid
        log_f = jnp.where(valid, log_f, 0.0)
        k = jnp.where(valid, k, 0.0)

    r = lax.broadcasted_iota(jnp.int32, (rows, rows), 0)
    c = lax.broadcasted_iota(jnp.int32, (rows, rows), 1)
    incl = jnp.logical_and(r >= c, (r // cs) == (c // cs))
    bcum = _dot_split(log_f, incl.astype(F32), 3, m01_left=True)
    b_mid = _chunk_rows(bcum, cs, cps, cs // 2 - 1)
    b_last = _chunk_rows(bcum, cs, cps, cs - 1)
    qe = q * jnp.exp(bcum - b_mid)
    ke = k * jnp.exp(b_mid - bcum)
    qs_ref[0] = q * jnp.exp(bcum)
    kd_ref[0] = k * jnp.exp(b_last - bcum)
    for i in range(cps):
        aux_ref[0, i * SUBLANES:(i + 1) * SUBLANES, :] = jnp.exp(b_last[i * cs:i * cs + SUBLANES, :])
    os_ = []
    for h in range(HEADS):
        hs = slice(h * HEAD_DIM, (h + 1) * HEAD_DIM)
        att = jnp.where(incl, _dot_nt(qe[:, hs], ke[:, hs]), 0.0)
        os_.append(_dot(att, v[:, hs]))
    oi_ref[0] = jnp.concatenate(os_, axis=-1)


def _hgrn_scan_kernel(oi_ref, qs_ref, kd_ref, aux_ref, v_ref, g_ref, s0_ref, nw_ref, o_ref, s1_ref, sst,
                      *, nb, n_chunks):
    c_idx = pl.program_id(1)
    blocks = _head_blocks(GROUP, HEAD_DIM)

    @pl.when(c_idx == 0)
    def _():
        sst[...] = jnp.zeros(sst.shape, F32)
        for bi in range(nb):
            for h in range(HEADS):
                hs = slice(h * HEAD_DIM, (h + 1) * HEAD_DIM)
                sst[bi, hs, hs] = s0_ref[bi, h]

    for bi in range(nb):
        st_t = sst[bi]
        o = oi_ref[bi] + _dot_nt(qs_ref[bi], st_t)
        sst[bi] = aux_ref[bi, 0:1, :] * st_t + _dot_tn(v_ref[bi], kd_ref[bi]) * blocks
        o_ref[bi] = _head_rms_gate(o, g_ref[bi], nw_ref[...], o.shape[0])

    @pl.when(c_idx == n_chunks - 1)
    def _():
        for bi in range(nb):
            for h in range(HEADS):
                hs = slice(h * HEAD_DIM, (h + 1) * HEAD_DIM)
                s1_ref[bi, h] = sst[bi, hs, hs]


def _hgrn_prep_call(proj, lb_raw, layer, l_valid):
    g, L, _ = proj.shape
    cs = LIN_CHUNK
    cps = LIN_PREP_CHUNKS
    rows = cs * cps
    assert L % rows == 0
    col = lambda k: pl.BlockSpec((1, rows, GROUP), lambda i, j, k=k: (i, j, k))
    out = pl.BlockSpec((1, rows, GROUP), lambda i, j: (i, j, 0))
    wide = jax.ShapeDtypeStruct((g, L, GROUP), F32)
    return pl.pallas_call(
        functools.partial(_hgrn_prep_kernel, cs=cs, cps=cps, layer=layer, l_valid=l_valid),
        grid=(g, L // rows),
        in_specs=[col(COL_HGRN), col(COL_HGRN + 1), col(COL_HGRN + 2), _resident(lb_raw.shape)],
        out_specs=[out, out, out, pl.BlockSpec((1, cps * SUBLANES, GROUP), lambda i, j: (i, j, 0))],
        out_shape=[wide, wide, wide, jax.ShapeDtypeStruct((g, L // cs * SUBLANES, GROUP), F32)],
        compiler_params=_params("parallel", "parallel"),
        name="hgrn_prep",
    )(proj, proj, proj, lb_raw)


def _hgrn_scan_call(prep, proj, s0, norm_w):
    b, L, _ = proj.shape
    cs = LIN_CHUNK
    n_chunks = L // cs
    nb = 4 if b % 4 == 0 else 1
    blk = pl.BlockSpec((nb, cs, GROUP), lambda i, c: (i, c, 0))
    pcol = lambda k: pl.BlockSpec((nb, cs, GROUP), lambda i, c, k=k: (i, c, k))
    state = pl.BlockSpec((nb, HEADS, HEAD_DIM, HEAD_DIM), lambda i, c: (i, 0, 0, 0))
    return pl.pallas_call(
        functools.partial(_hgrn_scan_kernel, nb=nb, n_chunks=n_chunks),
        grid=(b // nb, n_chunks),
        in_specs=[blk, blk, blk, pl.BlockSpec((nb, SUBLANES, GROUP), lambda i, c: (i, c, 0)),
                  pcol(COL_HGRN + 2), pcol(COL_HGRN + 3), state, _resident((1, GROUP))],
        out_specs=[blk, state],
        out_shape=[jax.ShapeDtypeStruct((b, L, GROUP), F32),
                   jax.ShapeDtypeStruct((b, HEADS, HEAD_DIM, HEAD_DIM), F32)],
        scratch_shapes=[pltpu.VMEM((nb, GROUP, GROUP), F32)],
        compiler_params=_params("parallel", "arbitrary"),
        name="hgrn_scan",
    )(*prep, proj, proj, s0, jnp.tile(norm_w, HEADS).reshape(1, GROUP))


def hgrn_mixer_long(proj, s0, lb_raw, norm_w, layer):
    return _hgrn_scan_call(_hgrn_prep_call(proj, lb_raw, layer, None), proj, s0, norm_w)


def hgrn_mixer_short(proj, s0, lb_raw, norm_w, layer):
    b, L, cols = proj.shape
    cs = LIN_CHUNK
    cps = LIN_PREP_CHUNKS
    assert L <= cs and b % cps == 0
    padded = jnp.pad(proj, ((0, 0), (0, cs - L), (0, 0)))
    prep = _hgrn_prep_call(padded.reshape(b // cps, cps * cs, cols), lb_raw, layer, L)
    o, s1 = _hgrn_scan_call([p.reshape(b, -1, GROUP) for p in prep], padded, s0, norm_w)
    return o[:, :L], s1


def _doubling_level_masks(n, block):
    r = np.arange(n)[:, None]
    c = np.arange(n)[None, :]
    out, s = [], 1
    while s < block:
        out.append((r // (2 * s) == c // (2 * s)) & ((r // s) % 2 == 1) & ((c // s) % 2 == 0))
        s *= 2
    return np.stack(out).astype(np.float32)


def _unit_lower_inverse_minus_eye(a, block=None, level_masks_ref=None):
    n = a.shape[0]
    block = n if block is None else block
    r = lax.broadcasted_iota(jnp.int32, (n, n), 0)
    c = lax.broadcasted_iota(jnp.int32, (n, n), 1)
    dx = None
    s, level = 1, 0
    while s < block:
        if level_masks_ref is None:
            lower_left = jnp.logical_and((r // (2 * s)) == (c // (2 * s)),
                                         jnp.logical_and((r // s) % 2 == 1, (c // s) % 2 == 0))
            b = jnp.where(lower_left, a, 0.0)
        else:
            b = a * level_masks_ref[level]
        if dx is None:
            dx = -b
        else:
            m = b + _dot(dx, b)
            dx = dx - m - _dot(m, dx)
        s *= 2
        level += 1
    return dx


def _gdn_kernel(q_ref, k_ref, v_ref, z_ref, sm_ref, conv0_ref, s0_ref, cw_ref, vec_ref, nw_ref,
                o_ref, conv1_ref, s1_ref, buf, smbuf, sst, *, cs, l_blk, n_chunks):
    c_idx = pl.program_id(1)

    @pl.when(c_idx == 0)
    def _():
        buf[5:8, :] = conv0_ref[0]
        sst[...] = s0_ref[0]

    _stage_rows(buf.at[:, 0:GROUP], q_ref, l_blk, 8)
    _stage_rows(buf.at[:, GROUP:2 * GROUP], k_ref, l_blk, 8)
    _stage_rows(buf.at[:, 2 * GROUP:3 * GROUP], v_ref, l_blk, 8)
    _stage_rows(smbuf, sm_ref, l_blk)

    qkv = _silu(_causal_conv(buf, cw_ref, cs))
    conv_tail = buf[5 + l_blk:8 + l_blk, :]
    buf[5:8, :] = conv_tail
    q = qkv[:, 0:GROUP]
    k = qkv[:, GROUP:2 * GROUP]
    v = qkv[:, 2 * GROUP:3 * GROUP]
    blocks = _head_blocks(GROUP, HEAD_DIM)
    q = q * lax.rsqrt(_dot_f32(q * q, blocks) + EPS) * (HEAD_DIM ** -0.5)
    k = k * lax.rsqrt(_dot_f32(k * k, blocks) + EPS)

    sm = smbuf[...]
    beta = _sigmoid(sm)
    log_g = -jnp.exp(vec_ref[1:2, :]) * _softplus(sm + vec_ref[0:1, :])
    if l_blk < cs:
        rows = lax.broadcasted_iota(jnp.int32, (cs, LANES), 0)
        beta = jnp.where(rows < l_blk, beta, 0.0)
        log_g = jnp.where(rows < l_blk, log_g, 0.0)
    gam = _dot_f32(_tril(cs).astype(F32), log_g)
    beta_w = _dot_f32(beta, _head_expand(LANES, GROUP, HEAD_DIM, HEADS))
    gam_w = _dot_f32(gam, _head_expand(LANES, GROUP, HEAD_DIM, 2 * HEADS))
    g_last = gam_w[cs - 1:cs, :]
    e_gam = jnp.exp(gam_w)
    kb = k * beta_w
    vb = v * beta_w
    kbg = kb * e_gam
    qg = q * e_gam
    k_dec = k * jnp.exp(g_last - gam_w)
    e_last = jnp.exp(g_last)
    gam_t = gam.T
    incl = _tril(cs)
    strict = _tril(cs, strict=True)

    os_ = []
    for h in range(HEADS):
        hs = slice(h * HEAD_DIM, (h + 1) * HEAD_DIM)
        r = 2 * HEADS + h
        st = sst[h]
        dec_incl = jnp.exp(jnp.where(incl, gam[:, r:r + 1] - gam_t[r:r + 1, :], NEG_INF))
        a_mat = jnp.where(strict, _dot_nt(kb[:, hs], k[:, hs]) * dec_incl, 0.0)
        tx = _unit_lower_inverse_minus_eye(a_mat)
        u = vb[:, hs] + _dot(tx, vb[:, hs])
        w = kbg[:, hs] + _dot(tx, kbg[:, hs])
        v_new = u - _dot(w, st)
        qk = _dot_nt(q[:, hs], k[:, hs]) * dec_incl
        os_.append(_dot(qk, v_new) + _dot(qg[:, hs], st))
        sst[h] = e_last[:, h * HEAD_DIM:h * HEAD_DIM + 1] * st + _dot_tn(k_dec[:, hs], v_new)
    o = jnp.concatenate(os_, axis=-1)
    o_ref[0] = _head_rms_gate(o, z_ref[0], nw_ref[...], l_blk)

    @pl.when(c_idx == n_chunks - 1)
    def _():
        conv1_ref[0] = conv_tail
        s1_ref[0] = sst[...]


def gdn_mixer(proj, conv0, s0, conv_w, a_log, dt_bias, norm_w):
    b, L, _ = proj.shape
    cs = LIN_CHUNK
    l_blk = min(cs, L)
    n_chunks = L // l_blk
    vec = jnp.zeros((8, LANES), F32)
    vec = vec.at[0, 2 * HEADS:3 * HEADS].set(dt_bias).at[1, 2 * HEADS:3 * HEADS].set(a_log)
    col = lambda k: pl.BlockSpec((1, l_blk, GROUP), lambda i, j, k=k: (i, j, k))
    per_b = lambda shp: pl.BlockSpec((1,) + shp, lambda i, j: (i,) + (0,) * len(shp))
    c3 = 3 * GROUP
    return pl.pallas_call(
        functools.partial(_gdn_kernel, cs=cs, l_blk=l_blk, n_chunks=n_chunks),
        grid=(b, n_chunks),
        in_specs=[col(COL_GDN), col(COL_GDN + 1), col(COL_GDN + 2), col(COL_GDN + 3),
                  pl.BlockSpec((1, l_blk, LANES), lambda i, j: (i, j, COL_SMALL)),
                  per_b((CONV_W - 1, c3)), per_b((HEADS, HEAD_DIM, HEAD_DIM)),
                  _resident((CONV_W, c3)), _resident((8, LANES)), _resident((1, GROUP))],
        out_specs=[pl.BlockSpec((1, l_blk, GROUP), lambda i, j: (i, j, 0)),
                   per_b((CONV_W - 1, c3)), per_b((HEADS, HEAD_DIM, HEAD_DIM))],
        out_shape=[jax.ShapeDtypeStruct((b, L, GROUP), F32),
                   jax.ShapeDtypeStruct((b, CONV_W - 1, c3), F32),
                   jax.ShapeDtypeStruct((b, HEADS, HEAD_DIM, HEAD_DIM), F32)],
        scratch_shapes=[pltpu.VMEM((cs + 8, c3), F32), pltpu.VMEM((cs, LANES), F32),
                        pltpu.VMEM((HEADS, HEAD_DIM, HEAD_DIM), F32)],
        compiler_params=_params("parallel", "arbitrary"),
        name="gdn_mixer",
    )(proj, proj, proj, proj, proj, conv0, s0, conv_w, vec,
      jnp.tile(norm_w, HEADS).reshape(1, GROUP))


GDN_PREP_CHUNKS = 4


def _gdn_prep_kernel(q_ref, k_ref, v_ref, pq_ref, pk_ref, pv_ref, sm_ref, conv0_ref, cw_ref, vec_ref,
                     lvl_ref, u_ref, w_ref, qg_ref, kd_ref, qk_ref, aux_ref, buf, *, cs, cps, l_valid):
    j = pl.program_id(1)
    rows = cs * cps

    if l_valid is None:
        @pl.when(j == 0)
        def _():
            buf[5:8, :] = conv0_ref[0]

        @pl.when(j > 0)
        def _():
            for i, ref in enumerate((pq_ref, pk_ref, pv_ref)):
                buf[5:8, i * GROUP:(i + 1) * GROUP] = ref[0, SUBLANES - 3:SUBLANES, :]

        for i, ref in enumerate((q_ref, k_ref, v_ref)):
            buf[8:8 + rows, i * GROUP:(i + 1) * GROUP] = ref[0]
        qkv = _silu(_causal_conv(buf, cw_ref, rows))
    else:
        stride = cs + SUBLANES
        parts = []
        for n in range(cps):
            buf[n * stride + 5:n * stride + 8, :] = conv0_ref[0, n * (CONV_W - 1):(n + 1) * (CONV_W - 1), :]
            for i, ref in enumerate((q_ref, k_ref, v_ref)):
                buf[n * stride + 8:n * stride + 8 + cs, i * GROUP:(i + 1) * GROUP] = ref[0, n * cs:(n + 1) * cs, :]
            parts.append(_causal_conv(buf, cw_ref, cs, n * stride))
        qkv = _silu(jnp.concatenate(parts, axis=0))
    q = qkv[:, 0:GROUP]
    k = qkv[:, GROUP:2 * GROUP]
    v = qkv[:, 2 * GROUP:3 * GROUP]
    blocks = _head_blocks(GROUP, HEAD_DIM)
    q = q * lax.rsqrt(_dot_split(q * q, blocks, 2) + EPS) * (HEAD_DIM ** -0.5)
    k = k * lax.rsqrt(_dot_split(k * k, blocks, 2) + EPS)

    sm = sm_ref[0]
    beta = _sigmoid(sm)
    log_g = -jnp.exp(vec_ref[1:2, :]) * _softplus(sm + vec_ref[0:1, :])
    if l_valid is not None:
        valid = lax.broadcasted_iota(jnp.int32, (rows, LANES), 0) % cs < l_valid
        beta = jnp.where(valid, beta, 0.0)
        log_g = jnp.where(valid, log_g, 0.0)
    r = lax.broadcasted_iota(jnp.int32, (rows, rows), 0)
    c = lax.broadcasted_iota(jnp.int32, (rows, rows), 1)
    chunk_tril = jnp.logical_and(r >= c, (r // cs) == (c // cs)).astype(F32)
    gam = _dot_split(log_g, chunk_tril, 3, m01_left=True)
    beta_w = _dot_split(beta, _head_expand(LANES, GROUP, HEAD_DIM, HEADS), 3)
    gam_w = _dot_split(gam, _head_expand(LANES, GROUP, HEAD_DIM, 2 * HEADS), 3)
    g_last = jnp.concatenate(
        [jnp.broadcast_to(gam_w[(i + 1) * cs - 1:(i + 1) * cs, :], (cs, GROUP)) for i in range(cps)], axis=0)
    e_gam = jnp.exp(gam_w)
    kb = k * beta_w
    vb = v * beta_w
    kbg = kb * e_gam
    qg_ref[0] = q * e_gam
    kd_ref[0] = k * jnp.exp(g_last - gam_w)
    for i in range(cps):
        aux_ref[0, i * SUBLANES:(i + 1) * SUBLANES, :] = jnp.exp(g_last[i * cs:i * cs + SUBLANES, :])
    gam_t = gam.T
    incl = chunk_tril > 0.0

    us, ws, qks = [], [], []
    for h in range(HEADS):
        hs = slice(h * HEAD_DIM, (h + 1) * HEAD_DIM)
        lane = 2 * HEADS + h
        dec_incl = jnp.exp(jnp.where(incl, gam[:, lane:lane + 1] - gam_t[lane:lane + 1, :], NEG_INF))
        a_mat = _dot_nt(kb[:, hs], k[:, hs]) * dec_incl
        dx = _unit_lower_inverse_minus_eye(a_mat, cs, lvl_ref)
        uw = _dot(dx, jnp.concatenate([vb[:, hs], kbg[:, hs]], axis=-1))
        us.append(vb[:, hs] + uw[:, 0:HEAD_DIM])
        ws.append(kbg[:, hs] + uw[:, HEAD_DIM:2 * HEAD_DIM])
        qk = _dot_nt(q[:, hs], k[:, hs]) * dec_incl
        qks.append(jnp.concatenate([qk[i * cs:(i + 1) * cs, i * cs:(i + 1) * cs] for i in range(cps)], axis=0))
    u_ref[0] = jnp.concatenate(us, axis=-1)
    w_ref[0] = jnp.concatenate(ws, axis=-1)
    qk_ref[0] = jnp.concatenate(qks, axis=-1)


def _gdn_scan_kernel(u_ref, w_ref, qg_ref, kd_ref, qk_ref, aux_ref, z_ref, s0_ref, nw_ref,
                     o_ref, s1_ref, sst, *, nb, n_chunks):
    c_idx = pl.program_id(1)
    blocks = _head_blocks(GROUP, HEAD_DIM)

    @pl.when(c_idx == 0)
    def _():
        sst[...] = jnp.zeros(sst.shape, F32)
        for bi in range(nb):
            for h in range(HEADS):
                hs = slice(h * HEAD_DIM, (h + 1) * HEAD_DIM)
                sst[bi, hs, hs] = s0_ref[bi, h]

    for bi in range(nb):
        st = sst[bi]
        v_new = u_ref[bi] - _dot(w_ref[bi], st)
        v_bd = jnp.concatenate([v_new] * HEADS, axis=0) * blocks
        o = _dot(qk_ref[bi], v_bd) + _dot(qg_ref[bi], st)
        sst[bi] = aux_ref[bi, 0:1, :] * st + _dot_tn(kd_ref[bi], v_new) * blocks
        o_ref[bi] = _head_rms_gate(o, z_ref[bi], nw_ref[...], o.shape[0])

    @pl.when(c_idx == n_chunks - 1)
    def _():
        for bi in range(nb):
            for h in range(HEADS):
                hs = slice(h * HEAD_DIM, (h + 1) * HEAD_DIM)
                s1_ref[bi, h] = sst[bi, hs, hs]


def _gdn_prep_call(proj, conv0, conv_w, a_log, dt_bias, l_valid):
    g, L, _ = proj.shape
    cs = LIN_CHUNK
    cps = GDN_PREP_CHUNKS
    rows = cs * cps
    assert L % rows == 0 and (l_valid is None or L == rows)
    vec = jnp.zeros((8, LANES), F32)
    vec = vec.at[0, 2 * HEADS:3 * HEADS].set(dt_bias).at[1, 2 * HEADS:3 * HEADS].set(a_log)
    c3 = 3 * GROUP
    col = lambda k: pl.BlockSpec((1, rows, GROUP), lambda i, j, k=k: (i, j, k))
    prev = lambda k: pl.BlockSpec(
        (1, SUBLANES, GROUP), lambda i, j, k=k: (i, jnp.maximum(j * (rows // SUBLANES) - 1, 0), k))
    out = pl.BlockSpec((1, rows, GROUP), lambda i, j: (i, j, 0))
    wide = jax.ShapeDtypeStruct((g, L, GROUP), F32)
    level_masks = jnp.asarray(_doubling_level_masks(rows, cs))
    return pl.pallas_call(
        functools.partial(_gdn_prep_kernel, cs=cs, cps=cps, l_valid=l_valid),
        grid=(g, L // rows),
        in_specs=[col(COL_GDN), col(COL_GDN + 1), col(COL_GDN + 2),
                  prev(COL_GDN), prev(COL_GDN + 1), prev(COL_GDN + 2),
                  pl.BlockSpec((1, rows, LANES), lambda i, j: (i, j, COL_SMALL)),
                  pl.BlockSpec((1,) + conv0.shape[1:], lambda i, j: (i, 0, 0)),
                  _resident((CONV_W, c3)), _resident((8, LANES)), _resident(level_masks.shape)],
        out_specs=[out, out, out, out, out,
                   pl.BlockSpec((1, cps * SUBLANES, GROUP), lambda i, j: (i, j, 0))],
        out_shape=[wide, wide, wide, wide, wide,
                   jax.ShapeDtypeStruct((g, L // cs * SUBLANES, GROUP), F32)],
        scratch_shapes=[pltpu.VMEM((cps * (cs + SUBLANES), c3), F32)],
        compiler_params=_params("parallel", "parallel"),
        name="gdn_prep",
    )(proj, proj, proj, proj, proj, proj, proj, conv0, conv_w, vec, level_masks)


GDN_SCAN_SEQS = 4


def _gdn_scan_call(prep, proj, s0, norm_w):
    b, L, _ = proj.shape
    cs = LIN_CHUNK
    nb = GDN_SCAN_SEQS
    n_chunks = L // cs
    assert b % nb == 0 and L % cs == 0
    blk = pl.BlockSpec((nb, cs, GROUP), lambda i, c: (i, c, 0))
    state = pl.BlockSpec((nb, HEADS, HEAD_DIM, HEAD_DIM), lambda i, c: (i, 0, 0, 0))
    return pl.pallas_call(
        functools.partial(_gdn_scan_kernel, nb=nb, n_chunks=n_chunks),
        grid=(b // nb, n_chunks),
        in_specs=[blk, blk, blk, blk, blk,
                  pl.BlockSpec((nb, SUBLANES, GROUP), lambda i, c: (i, c, 0)),
                  pl.BlockSpec((nb, cs, GROUP), lambda i, c: (i, c, COL_GDN + 3)),
                  state, _resident((1, GROUP))],
        out_specs=[blk, state],
        out_shape=[jax.ShapeDtypeStruct((b, L, GROUP), F32),
                   jax.ShapeDtypeStruct((b, HEADS, HEAD_DIM, HEAD_DIM), F32)],
        scratch_shapes=[pltpu.VMEM((nb, GROUP, GROUP), F32)],
        compiler_params=_params("parallel", "arbitrary"),
        name="gdn_scan",
    )(*prep, proj, s0, jnp.tile(norm_w, HEADS).reshape(1, GROUP))


def gdn_mixer_long(proj, conv0, s0, conv_w, a_log, dt_bias, norm_w):
    L = proj.shape[1]
    prep = _gdn_prep_call(proj, conv0, conv_w, a_log, dt_bias, None)
    o, s1 = _gdn_scan_call(prep, proj, s0, norm_w)
    conv1 = proj[:, L - (CONV_W - 1):, COL_GDN * GROUP:(COL_GDN + 3) * GROUP]
    return o, conv1, s1


def gdn_mixer_short(proj, conv0, s0, conv_w, a_log, dt_bias, norm_w):
    b, L, cols = proj.shape
    cs = LIN_CHUNK
    cps = GDN_PREP_CHUNKS
    assert CONV_W - 1 <= L <= cs and b % cps == 0
    padded = jnp.pad(proj, ((0, 0), (0, cs - L), (0, 0)))
    prep = _gdn_prep_call(padded.reshape(b // cps, cps * cs, cols),
                          conv0.reshape(b // cps, cps * (CONV_W - 1), 3 * GROUP), conv_w, a_log, dt_bias, L)
    prep = [p.reshape(b, -1, GROUP) for p in prep]
    o, s1 = _gdn_scan_call(prep, padded, s0, norm_w)
    conv1 = proj[:, L - (CONV_W - 1):, COL_GDN * GROUP:(COL_GDN + 3) * GROUP]
    return o[:, :L], conv1, s1


def _moba_prompt_kernel(q_ref, k_ref, v_ref, o_ref, ks, vts, kmean, sel_t, o_t, q_tb, alibi, ml,
                        s_buf, p_buf, *, nb):
    blk = MOBA_BLOCK
    qi = pl.program_id(1)

    @pl.when(qi == 0)
    def _():
        for j in range(nb):
            kj = k_ref[0, j * blk:(j + 1) * blk, :]
            kmean[j:j + 1, :] = jnp.mean(kj, axis=0, keepdims=True)
            kjb = kj.astype(BF16)
            for h in range(HEADS):
                ks[h, j] = kjb[:, h * HEAD_DIM:(h + 1) * HEAD_DIM]
            vts[j] = v_ref[0, j * blk:(j + 1) * blk, :].T.astype(BF16)

    q_t = (q_ref[0] * (HEAD_DIM ** -0.5)).T
    q_tb[...] = (q_t * LOG2_E).astype(BF16)
    blk_row = lax.broadcasted_iota(jnp.int32, (nb, blk), 0)
    rel = (lax.broadcasted_iota(jnp.int32, (blk, blk), 1)
           - lax.broadcasted_iota(jnp.int32, (blk, blk), 0)).astype(F32)
    for h in range(HEADS):
        alibi[h] = rel * (-ALIBI_SLOPES[h] * LOG2_E)

    for h in range(HEADS):
        hs = slice(h * HEAD_DIM, (h + 1) * HEAD_DIM)
        slope = ALIBI_SLOPES[h]
        gate = jnp.where(blk_row < qi, _dot_f32(kmean[:, hs], q_t[hs, :]), NEG_INF)
        sel = jnp.full((nb, blk), NEG_INF, F32)
        for _ in range(MOBA_TOPK):
            top = jnp.max(gate, axis=0, keepdims=True)
            first = jnp.min(jnp.where(gate == top, blk_row, nb), axis=0, keepdims=True)
            pick = blk_row == first
            sel = jnp.where(jnp.logical_and(pick, blk_row < qi), 0.0, sel)
            gate = jnp.where(pick, NEG_INF, gate)
        sel_t[h] = sel
        ml[h, 0:1, :] = jnp.full((1, blk), NEG_INF, F32)
        ml[h, 1:2, :] = jnp.zeros((1, blk), F32)
    o_t[...] = jnp.zeros(o_t.shape, F32)

    def block_step(j, own):
        for h in range(HEADS):
            hs = slice(h * HEAD_DIM, (h + 1) * HEAD_DIM)
            s_buf[h] = _dot(ks[h, j], q_tb[hs, :])
        for h in range(HEADS):
            if own:
                causal = (lax.broadcasted_iota(jnp.int32, (blk, blk), 1)
                          >= lax.broadcasted_iota(jnp.int32, (blk, blk), 0))
                s = jnp.where(causal, s_buf[h] + alibi[h], NEG_INF)
            else:
                off = (qi - j).astype(F32) * (ALIBI_SLOPES[h] * LOG2_E * blk)
                s = s_buf[h] + alibi[h] + (sel_t[h, pl.ds(j, 1), :] - off)
            m = ml[h, 0:1, :]
            m_new = jnp.maximum(m, jnp.max(s, axis=0, keepdims=True))
            p = jnp.exp2(s - m_new)
            p_buf[h] = p.astype(BF16)
            alpha = jnp.exp2(m - m_new)
            ml[h, 0:1, :] = m_new
            ml[h, 1:2, :] = alpha * ml[h, 1:2, :] + jnp.sum(p, axis=0, keepdims=True)
            ml[h, 2:3, :] = alpha
        for h in range(HEADS):
            hs = slice(h * HEAD_DIM, (h + 1) * HEAD_DIM)
            o_t[hs, :] = (ml[h, 2:3, :] * o_t[hs, :]
                          + jnp.dot(vts[j, hs, :], p_buf[h], preferred_element_type=F32))

    block_step(qi, True)

    def body(j, carry):
        block_step(j, False)
        return carry

    lax.fori_loop(0, qi, body, 0)
    for h in range(HEADS):
        hs = slice(h * HEAD_DIM, (h + 1) * HEAD_DIM)
        o_t[hs, :] = o_t[hs, :] / ml[h, 1:2, :]
    o_ref[0] = o_t[...].T


def moba_prompt(proj):
    b, S, _ = proj.shape
    blk = MOBA_BLOCK
    nb = S // blk
    seq = lambda k: pl.BlockSpec((1, S, GROUP), lambda i, j, k=k: (i, 0, k))
    return pl.pallas_call(
        functools.partial(_moba_prompt_kernel, nb=nb),
        grid=(b, nb),
        in_specs=[pl.BlockSpec((1, blk, GROUP), lambda i, j: (i, j, COL_MOBA)),
                  seq(COL_MOBA + 1), seq(COL_MOBA + 2)],
        out_specs=pl.BlockSpec((1, blk, GROUP), lambda i, j: (i, j, 0)),
        out_shape=jax.ShapeDtypeStruct((b, S, GROUP), F32),
        scratch_shapes=[pltpu.VMEM((HEADS, nb, blk, HEAD_DIM), BF16),
                        pltpu.VMEM((nb, GROUP, blk), BF16),
                        pltpu.VMEM((nb, GROUP), F32),
                        pltpu.VMEM((HEADS, nb, blk), F32),
                        pltpu.VMEM((GROUP, blk), F32),
                        pltpu.VMEM((GROUP, blk), BF16),
                        pltpu.VMEM((HEADS, blk, blk), F32),
                        pltpu.VMEM((HEADS, SUBLANES, blk), F32),
                        pltpu.VMEM((HEADS, blk, blk), F32),
                        pltpu.VMEM((HEADS, blk, blk), BF16)],
        compiler_params=_params("parallel", "arbitrary"),
        name="moba_prompt",
    )(proj, proj, proj)


PAGES_PER_BLOCK = MOBA_BLOCK // PAGE_SIZE
SELECT_PAGE_BUFFERS = 64
SELECT_UNROLL = 8


def _moba_select_kernel(pt_ref, q_ref, kc_ref, idx_ref, pages, kmean_t, q8, sems, *,
                        layer, n_pages, t):
    b = pl.program_id(0)
    nbuf = SELECT_PAGE_BUFFERS
    n_blocks = n_pages // PAGES_PER_BLOCK

    def page_copy(p, slot):
        return pltpu.make_async_copy(kc_ref.at[layer, pt_ref[b, p]], pages.at[slot], sems.at[slot])

    for s in range(nbuf):
        page_copy(s, s).start()

    kmean_t[...] = jnp.zeros(kmean_t.shape, F32)
    blk_of_lane = lax.broadcasted_iota(jnp.int32, kmean_t.shape, 1)

    pages_per_trip = SELECT_UNROLL * PAGES_PER_BLOCK

    def body(trip, carry):
        p0 = trip * pages_per_trip
        for i in range(pages_per_trip):
            page_copy(p0 + i, (p0 + i) % nbuf).wait()
        km = kmean_t[...]
        for u in range(SELECT_UNROLL):
            tot = jnp.zeros((GROUP, PAGE_SIZE), F32)
            for pp in range(PAGES_PER_BLOCK):
                tot = tot + pages[(p0 + u * PAGES_PER_BLOCK + pp) % nbuf]
            mean = jnp.sum(tot, axis=1, keepdims=True) * (1.0 / MOBA_BLOCK)
            km = jnp.where(blk_of_lane == trip * SELECT_UNROLL + u, mean, km)
        kmean_t[...] = km
        for i in range(pages_per_trip):
            @pl.when(p0 + i + nbuf < n_pages)
            def _(i=i):
                page_copy(p0 + i + nbuf, (p0 + i) % nbuf).start()
        return carry

    lax.fori_loop(0, n_blocks // SELECT_UNROLL, body, 0)

    q8[0:t, :] = q_ref[0]
    q8[t:, :] = jnp.zeros((SUBLANES - t, GROUP), F32)
    blk_lane = lax.broadcasted_iota(jnp.int32, (SUBLANES, n_blocks), 1)
    out_lane = lax.broadcasted_iota(jnp.int32, (SUBLANES, LANES), 1)
    res = jnp.zeros((SUBLANES, LANES), jnp.int32)
    for h in range(HEADS):
        hs = slice(h * HEAD_DIM, (h + 1) * HEAD_DIM)
        gate = _dot_f32(q8[:, hs], kmean_t[hs, 0:n_blocks])
        for r in range(MOBA_TOPK):
            top = jnp.max(gate, axis=1, keepdims=True)
            first = jnp.min(jnp.where(gate == top, blk_lane, n_blocks), axis=1, keepdims=True)
            res = jnp.where(out_lane == h * MOBA_TOPK + r, first, res)
            gate = jnp.where(blk_lane == first, NEG_INF, gate)
    idx_ref[0] = res


def _moba_sample_attn_kernel(pt_ref, idx_ref, q_ref, kn_ref, vn_ref, kc_ref, vc_ref, o_ref,
                             kbuf, vbuf, st8, o8, rows_s, rows_o, ksem, vsem, *, layer, past, t):
    b = pl.program_id(0)
    n_seq = pl.num_programs(0)
    n_sel = MOBA_TOPK * MOBA_BLOCK
    pairs = [(tok, h) for h in range(HEADS) for tok in range(t)]

    def block_of(seq, tok, h, r):
        return idx_ref[seq, tok * (HEADS * MOBA_TOPK) + h * MOBA_TOPK + r]

    def copies(seq, i, half):
        tok, h = pairs[i]
        out = []
        for r in range(MOBA_TOPK):
            blk = block_of(seq, tok, h, r)
            for pp in range(PAGES_PER_BLOCK):
                phys = pt_ref[seq, blk * PAGES_PER_BLOCK + pp]
                lanes = pl.ds((r * PAGES_PER_BLOCK + pp) * PAGE_SIZE, PAGE_SIZE)
                out.append(pltpu.make_async_copy(kc_ref.at[layer, phys, h], kbuf.at[half, i, :, lanes],
                                                 ksem.at[half, i]))
                out.append(pltpu.make_async_copy(vc_ref.at[layer, phys, h], vbuf.at[half, i, :, lanes],
                                                 vsem.at[half, i]))
        return out

    def start_all(seq, half):
        for i in range(len(pairs)):
            for c in copies(seq, i, half):
                c.start()

    half = b % 2

    @pl.when(b == 0)
    def _():
        start_all(b, half)

    @pl.when(b + 1 < n_seq)
    def _():
        start_all(b + 1, 1 - half)

    for i, ref in enumerate((q_ref, kn_ref, vn_ref)):
        st8[i, 0:t, :] = ref[0]
        st8[i, t:, :] = jnp.zeros((SUBLANES - t, GROUP), F32)

    lane = lax.broadcasted_iota(jnp.int32, (1, n_sel), 1)
    row = lax.broadcasted_iota(jnp.int32, (SUBLANES, 1), 0)
    for i in range(len(pairs)):
        for c in copies(b, i, half):
            c.wait()
    for i, (tok, h) in enumerate(pairs):
        hs = slice(h * HEAD_DIM, (h + 1) * HEAD_DIM)
        qrow = st8[0, tok:tok + 1, hs] * (HEAD_DIM ** -0.5)
        rows_s[i:i + 1, :] = _dot(jnp.broadcast_to(qrow, (SUBLANES, HEAD_DIM)), kbuf[half, i])[0:1]
    for i, (tok, h) in enumerate(pairs):
        hs = slice(h * HEAD_DIM, (h + 1) * HEAD_DIM)
        slope = ALIBI_SLOPES[h]
        qrow = st8[0, tok:tok + 1, hs] * (HEAD_DIM ** -0.5)
        blk = jnp.where(lane < MOBA_BLOCK, block_of(b, tok, h, 0),
                        jnp.where(lane < 2 * MOBA_BLOCK, block_of(b, tok, h, 1), block_of(b, tok, h, 2)))
        pos = blk * MOBA_BLOCK + (lane % MOBA_BLOCK)
        s_sel = rows_s[i:i + 1, :] - slope * (past + tok - pos).astype(F32)
        s_own = jnp.sum(st8[1, :, hs] * qrow, axis=1, keepdims=True)
        s_own = jnp.where(row <= tok, s_own - slope * (tok - row).astype(F32), NEG_INF)
        m = jnp.maximum(jnp.max(s_sel, axis=1, keepdims=True), jnp.max(s_own, axis=0, keepdims=True))
        p_sel = jnp.exp(s_sel - m)
        p_own = jnp.exp(s_own - m)
        l = jnp.sum(p_sel, axis=1, keepdims=True) + jnp.sum(p_own, axis=0, keepdims=True)
        rows_s[i:i + 1, :] = p_sel
        rows_o[i:i + 1, 0:HEAD_DIM] = jnp.sum(p_own * st8[2, :, hs], axis=0, keepdims=True)
        rows_o[i:i + 1, HEAD_DIM:2 * HEAD_DIM] = jnp.broadcast_to(l, (1, HEAD_DIM))
    for i, (tok, h) in enumerate(pairs):
        hs = slice(h * HEAD_DIM, (h + 1) * HEAD_DIM)
        p8 = jnp.broadcast_to(rows_s[i:i + 1, :], (SUBLANES, n_sel))
        o = _dot_nt(p8, vbuf[half, i])[0:1] + rows_o[i:i + 1, 0:HEAD_DIM]
        o8[tok:tok + 1, hs] = o / rows_o[i:i + 1, HEAD_DIM:2 * HEAD_DIM]
    o_ref[0] = o8[0:t, :]


def moba_sample(proj, k_cache, v_cache, page_table, layer):
    db, t, _ = proj.shape
    depth, pool = k_cache.shape[:2]
    n_pages = page_table.shape[1]
    n_blocks = n_pages // PAGES_PER_BLOCK
    past = n_pages * PAGE_SIZE
    assert past % MOBA_BLOCK == 0 and n_pages >= SELECT_PAGE_BUFFERS and t <= SUBLANES
    assert n_blocks % SELECT_UNROLL == 0 and SELECT_PAGE_BUFFERS % (SELECT_UNROLL * PAGES_PER_BLOCK) == 0
    assert MOBA_TOPK <= n_blocks <= LANES
    tok = lambda k: pl.BlockSpec((1, t, GROUP), lambda i, *_: (i, 0, k))
    hbm = pl.BlockSpec(memory_space=pl.ANY)
    n_pairs = t * HEADS
    idx = pl.pallas_call(
        functools.partial(_moba_select_kernel, layer=layer, n_pages=n_pages, t=t),
        grid_spec=pltpu.PrefetchScalarGridSpec(
            num_scalar_prefetch=1, grid=(db,),
            in_specs=[tok(COL_MOBA), hbm],
            out_specs=pl.BlockSpec((1, SUBLANES, LANES), lambda i, *_: (i, 0, 0)),
            scratch_shapes=[pltpu.VMEM((SELECT_PAGE_BUFFERS, GROUP, PAGE_SIZE), F32),
                            pltpu.VMEM((GROUP, LANES), F32),
                            pltpu.VMEM((SUBLANES, GROUP), F32),
                            pltpu.SemaphoreType.DMA((SELECT_PAGE_BUFFERS,))]),
        out_shape=jax.ShapeDtypeStruct((db, SUBLANES, LANES), jnp.int32),
        compiler_params=_params("arbitrary"),
        name="moba_sample_select",
    )(page_table, proj, k_cache.reshape(depth, pool, GROUP, PAGE_SIZE))
    idx = idx[:, :t, :HEADS * MOBA_TOPK].reshape(db, t * HEADS * MOBA_TOPK)
    return pl.pallas_call(
        functools.partial(_moba_sample_attn_kernel, layer=layer, past=past, t=t),
        grid_spec=pltpu.PrefetchScalarGridSpec(
            num_scalar_prefetch=2, grid=(db,),
            in_specs=[tok(COL_MOBA), tok(COL_MOBA + 1), tok(COL_MOBA + 2), hbm, hbm],
            out_specs=pl.BlockSpec((1, t, GROUP), lambda i, *_: (i, 0, 0)),
            scratch_shapes=[pltpu.VMEM((2, n_pairs, HEAD_DIM, MOBA_TOPK * MOBA_BLOCK), F32),
                            pltpu.VMEM((2, n_pairs, HEAD_DIM, MOBA_TOPK * MOBA_BLOCK), F32),
                            pltpu.VMEM((3, SUBLANES, GROUP), F32),
                            pltpu.VMEM((SUBLANES, GROUP), F32),
                            pltpu.VMEM((n_pairs, MOBA_TOPK * MOBA_BLOCK), F32),
                            pltpu.VMEM((n_pairs, LANES), F32),
                            pltpu.SemaphoreType.DMA((2, n_pairs)),
                            pltpu.SemaphoreType.DMA((2, n_pairs))]),
        out_shape=jax.ShapeDtypeStruct((db, t, GROUP), F32),
        compiler_params=_params("arbitrary"),
        name="moba_sample_attn",
    )(page_table, idx, proj, proj, proj, k_cache, v_cache)


def _rearrange_w_in(w):
    d = w.shape[0]
    dt0 = COL_HGRN * GROUP
    ba0 = dt0 + HEADS + (COL_GDN + 4 - COL_HGRN) * GROUP
    wide = jnp.concatenate([w[:, :dt0], w[:, dt0 + HEADS:ba0]], axis=1)
    narrow = jnp.concatenate([w[:, dt0:dt0 + HEADS], w[:, ba0:ba0 + 2 * HEADS],
                              jnp.zeros((d, LANES - N_SMALL), w.dtype)], axis=1)
    return jnp.concatenate([wide, narrow], axis=1)


def _layer(x, mem_k, mem_v, states, lp, layer, attend, prompt):
    b, L, d = x.shape
    ssm_conv0, ssm0, hgrn0, gdn_conv0, gdn0 = states
    w_in = lp['w_in'].astype(BF16) if prompt else lp['w_in']
    proj = norm_matmul(x.reshape(b * L, d), lp['n_mix_pre'], w_in, min(IN_PROJ_ROWS, b * L)).reshape(b, L, -1)
    o_a = attend(proj)
    o_b, ssm_conv1, ssm1 = ssd_mixer(proj, ssm_conv0, ssm0, lp['ssm_conv_w'], lp['ssm_conv_b'],
                                     lp['ssm_dt_bias'], lp['ssm_a_log'], lp['ssm_d'], lp['ssm_norm'])
    hgrn = hgrn_mixer_long if L % (LIN_PREP_CHUNKS * LIN_CHUNK) == 0 else hgrn_mixer_short
    o_c, hgrn1 = hgrn(proj, jnp.swapaxes(hgrn0, -1, -2), lp['hgrn_lb_raw'], lp['hgrn_norm'], layer)
    gdn = gdn_mixer_long if L % (GDN_PREP_CHUNKS * LIN_CHUNK) == 0 else gdn_mixer_short
    o_d, gdn_conv1, gdn1 = gdn(proj, gdn_conv0, gdn0, lp['gdn_conv_w'], lp['gdn_a_log'],
                               lp['gdn_dt_bias'], lp['gdn_norm'])
    post = post_mixer if prompt else post_mixer_sample
    args = (x, (o_a, o_b, o_c, o_d), mem_k, mem_v, lp['norms'], lp['w_out'], lp['w_xq'], lp['w_xo'],
            lp['w_gu'], lp['w_down'])
    x = post(*args, POST_MIXER_ROWS) if prompt else post(*args)
    k = proj[..., GROUP:2 * GROUP].reshape(b, L, HEADS, HEAD_DIM)
    v = proj[..., 2 * GROUP:3 * GROUP].reshape(b, L, HEADS, HEAD_DIM)
    return x, (k, v, ssm_conv1, ssm1, jnp.swapaxes(hgrn1, -1, -2), gdn_conv1, gdn1)


def kernel(x_prompt, x_sample, mem_prompt, cache_moba_k, cache_moba_v, page_table, cache_mem_k, cache_mem_v, state_ssm_conv, state_ssm, state_hgrn, state_gdn_conv, state_gdn, n_mix_pre, n_mix_post, w_in, w_out, ssm_conv_w, ssm_conv_b, ssm_dt_bias, ssm_a_log, ssm_d, ssm_norm, hgrn_lb_raw, hgrn_norm, gdn_conv_w, gdn_a_log, gdn_dt_bias, gdn_norm, n_x_pre, n_x_post, mem_norm, w_xq, w_xkv, w_xo, n_f_pre, n_f_post, w_gu, w_down):
    depth = w_in.shape[0]
    bp, _, d = x_prompt.shape
    db = x_sample.shape[0]
    n_mem = mem_prompt.shape[1]
    kc = jnp.transpose(cache_moba_k, (0, 1, 3, 4, 2))
    vc = jnp.transpose(cache_moba_v, (0, 1, 3, 4, 2))
    zeros_p = (jnp.zeros((bp,) + state_ssm_conv.shape[2:], F32), jnp.zeros((bp,) + state_ssm.shape[2:], F32),
               jnp.zeros((bp,) + state_hgrn.shape[2:], F32), jnp.zeros((bp,) + state_gdn_conv.shape[2:], F32),
               jnp.zeros((bp,) + state_gdn.shape[2:], F32))
    yp, ys = x_prompt, x_sample
    outs_p, outs_s, mem_ks, mem_vs = [], [], [], []
    for l in range(depth):
        norms = jnp.zeros((SUBLANES, d), F32)
        for i, nrm in enumerate((n_mix_post, n_x_pre, n_x_post, n_f_pre, n_f_post)):
            norms = norms.at[i].set(nrm[l])
        lp = {'n_mix_pre': n_mix_pre[l], 'w_in': _rearrange_w_in(w_in[l]), 'norms': norms,
              'w_out': w_out[l].astype(BF16), 'w_xq': w_xq[l].astype(BF16), 'w_xo': w_xo[l].astype(BF16),
              'w_gu': w_gu[l].astype(BF16), 'w_down': w_down[l].astype(BF16),
              'ssm_conv_w': ssm_conv_w[l], 'ssm_conv_b': ssm_conv_b[l], 'ssm_dt_bias': ssm_dt_bias[l],
              'ssm_a_log': ssm_a_log[l], 'ssm_d': ssm_d[l], 'ssm_norm': ssm_norm[l],
              'hgrn_lb_raw': hgrn_lb_raw, 'hgrn_norm': hgrn_norm[l], 'gdn_conv_w': gdn_conv_w[l],
              'gdn_a_log': gdn_a_log[l], 'gdn_dt_bias': gdn_dt_bias[l], 'gdn_norm': gdn_norm[l]}
        mkv = norm_matmul(mem_prompt.reshape(bp * n_mem, d), mem_norm[l], w_xkv[l].astype(BF16), 256)
        mk = mkv[:, :d].reshape(bp, n_mem, d)
        mv = mkv[:, d:].reshape(bp, n_mem, d)
        yp, st_p = _layer(yp, mk.astype(BF16), mv.astype(BF16), zeros_p, lp, l, moba_prompt, True)
        outs_p.append(st_p)
        mem_ks.append(mk.reshape(bp, n_mem, X_HEADS, d // X_HEADS))
        mem_vs.append(mv.reshape(bp, n_mem, X_HEADS, d // X_HEADS))
        states_s = (state_ssm_conv[l], state_ssm[l], state_hgrn[l], state_gdn_conv[l], state_gdn[l])
        attend_s = functools.partial(moba_sample, k_cache=kc, v_cache=vc, page_table=page_table, layer=l)
        ys, st_s = _layer(ys, cache_mem_k[l].reshape(db, n_mem, d).astype(BF16),
                          cache_mem_v[l].reshape(db, n_mem, d).astype(BF16), states_s, lp, l, attend_s, False)
        outs_s.append(st_s)
    stack = lambda outs, i: jnp.stack([o[i] for o in outs], axis=0)
    return (yp, ys, stack(outs_p, 0), stack(outs_s, 0), stack(outs_p, 1), stack(outs_s, 1),
            jnp.stack(mem_ks, axis=0), jnp.stack(mem_vs, axis=0),
            stack(outs_p, 2), stack(outs_s, 2), stack(outs_p, 3), stack(outs_s, 3),
            stack(outs_p, 4), stack(outs_s, 4), stack(outs_p, 5), stack(outs_s, 5),
            stack(outs_p, 6), stack(outs_s, 6))
```

```python
import functools

import numpy as np
import jax
import jax.numpy as jnp
from jax import lax
from jax.experimental import pallas as pl
from jax.experimental.pallas import tpu as pltpu

F32 = jnp.float32
BF16 = jnp.bfloat16

LANES = 128
SUBLANES = 8
VMEM_LIMIT_BYTES = 56 * 1024 * 1024

GROUP = 256
HEADS = 4
HEAD_DIM = 64
CONV_W = 4
SSM_DSTATE = 128
SSM_CHUNK = 128
LIN_CHUNK = 64
MOBA_BLOCK = 256
MOBA_TOPK = 3
PAGE_SIZE = 128
X_HEADS = 4
N_SMALL = 12
COL_MOBA = 0
COL_SSM_Z = 3
COL_SSM_X = 4
COL_HGRN = 7
COL_GDN = 11
N_WIDE = 15 * GROUP
COL_SMALL = N_WIDE // LANES
EPS = 1e-6
NEG_INF = float("-inf")
LOG2_E = 1.4426950408889634
ALIBI_SLOPES = tuple(2.0 ** (-8.0 * (h + 1) / HEADS) for h in range(HEADS))


def _rms(x, w):
    return x * lax.rsqrt(jnp.mean(x * x, axis=-1, keepdims=True) + EPS) * w


def _sigmoid(x):
    return 1.0 / (1.0 + jnp.exp(-x))


def _silu(x):
    return x * _sigmoid(x)


def _softplus(x):
    return jnp.maximum(x, 0.0) + jnp.log(1.0 + jnp.exp(-jnp.abs(x)))


def _dot(a, b):
    return jnp.dot(a.astype(BF16), b.astype(BF16), preferred_element_type=F32)


def _dot_nt(a, b):
    return lax.dot_general(a.astype(BF16), b.astype(BF16), (((1,), (1,)), ((), ())),
                           preferred_element_type=F32)


def _dot_f32(a, b):
    return jnp.dot(a, b, preferred_element_type=F32, precision=lax.Precision.HIGHEST)


def _params(*sem, flags=None):
    return pltpu.CompilerParams(dimension_semantics=sem, vmem_limit_bytes=VMEM_LIMIT_BYTES, flags=flags)


def _resident(shape):
    return pl.BlockSpec(shape, lambda *_: (0,) * len(shape), pipeline_mode=pl.Buffered(1))


def _norm_matmul_kernel(x_ref, nw_ref, w_ref, o_ref, *, full_precision):
    xn = _rms(x_ref[...], nw_ref[...])
    o_ref[...] = _dot_f32(xn, w_ref[...]) if full_precision else _dot(xn, w_ref[...])


def norm_matmul(x, norm_w, w, tm):
    n, d = x.shape
    c = w.shape[1]
    return pl.pallas_call(
        functools.partial(_norm_matmul_kernel, full_precision=(w.dtype == F32)),
        grid=(n // tm,),
        in_specs=[pl.BlockSpec((tm, d), lambda i: (i, 0)),
                  _resident((1, d)),
                  _resident((d, c))],
        out_specs=pl.BlockSpec((tm, c), lambda i: (i, 0)),
        out_shape=jax.ShapeDtypeStruct((n, c), F32),
        compiler_params=_params("parallel"),
        name="norm_matmul",
    )(x, norm_w.reshape(1, d), w)


POST_MIXER_ROWS = 512
IN_PROJ_ROWS = 512

def _mix_out_and_query(x, mixed, nw, wout_ref, wxq_ref):
    x = x + _rms(_dot(mixed, wout_ref[...]), nw[0:1])
    return x, _dot(_rms(x, nw[1:2]), wxq_ref[...])


def _memory_attention(q, head_k, head_v):
    xdh = q.shape[1] // X_HEADS
    heads = []
    for h in range(X_HEADS):
        s = _dot_nt(q[:, h * xdh:(h + 1) * xdh], head_k(h)) * (xdh ** -0.5)
        p = jnp.exp(s - jnp.max(s, axis=-1, keepdims=True))
        heads.append(_dot(p, head_v(h)) / jnp.sum(p, axis=-1, keepdims=True))
    return jnp.concatenate(heads, axis=-1)


def _lane_heads(ref):
    xdh = ref.shape[-1] // X_HEADS
    return lambda h: ref[0, :, h * xdh:(h + 1) * xdh]


def _attn_out_and_ffn(x, att, nw, wxo_ref, wgu_ref, wdown_ref):
    x = x + _rms(_dot(att, wxo_ref[...]), nw[2:3])
    gu = _dot(_rms(x, nw[3:4]), wgu_ref[...])
    hid = gu.shape[1] // 2
    act = _silu(gu[:, :hid]) * gu[:, hid:]
    return x + _rms(_dot(act, wdown_ref[...]), nw[4:5])


def _post_mixer_kernel(x_ref, oa_ref, ob_ref, oc_ref, od_ref, mk_ref, mv_ref, norms_ref, wout_ref,
                       wxq_ref, wxo_ref, wgu_ref, wdown_ref, o_ref):
    nw = norms_ref[...]
    mixed = jnp.concatenate([oa_ref[0], ob_ref[0], oc_ref[0], od_ref[0]], axis=-1)
    x, q = _mix_out_and_query(x_ref[0], mixed, nw, wout_ref, wxq_ref)
    att = _memory_attention(q, _lane_heads(mk_ref), _lane_heads(mv_ref))
    o_ref[0] = _attn_out_and_ffn(x, att, nw, wxo_ref, wgu_ref, wdown_ref)


def post_mixer(x, mix_parts, mk, mv, norms, w_out, w_xq, w_xo, w_gu, w_down, tm):
    b, L, d = x.shape
    m = mk.shape[1]
    row = pl.BlockSpec((1, tm, d), lambda i, j: (i, j, 0))
    part = pl.BlockSpec((1, tm, GROUP), lambda i, j: (i, j, 0))
    mem = pl.BlockSpec((1, m, d), lambda i, j: (i, 0, 0))
    return pl.pallas_call(
        _post_mixer_kernel,
        grid=(b, L // tm),
        in_specs=[row, part, part, part, part, mem, mem, _resident(norms.shape),
                  _resident(w_out.shape), _resident(w_xq.shape), _resident(w_xo.shape),
                  _resident(w_gu.shape), _resident(w_down.shape)],
        out_specs=row,
        out_shape=jax.ShapeDtypeStruct((b, L, d), F32),
        compiler_params=_params("parallel", "parallel"),
        name="post_mixer",
    )(x, *mix_parts, mk, mv, norms, w_out, w_xq, w_xo, w_gu, w_down)


def _sample_pre_kernel(x_ref, oa_ref, ob_ref, oc_ref, od_ref, norms_ref, wout_ref, wxq_ref,
                       x1_ref, q_ref):
    mixed = jnp.concatenate([oa_ref[...], ob_ref[...], oc_ref[...], od_ref[...]], axis=-1)
    x1, q = _mix_out_and_query(x_ref[...], mixed, norms_ref[...], wout_ref, wxq_ref)
    x1_ref[...] = x1
    q_ref[...] = q


def _sample_attn_kernel(q_ref, mk_ref, mv_ref, o_ref, q8, *, t):
    q8[0:t, :] = q_ref[0]
    q8[t:, :] = jnp.zeros((q8.shape[0] - t, q8.shape[1]), F32)
    o_ref[0] = _memory_attention(q8[...], _lane_heads(mk_ref), _lane_heads(mv_ref))[0:t]


def _sample_post_kernel(x_ref, att_ref, norms_ref, wxo_ref, wgu_ref, wdown_ref, o_ref):
    o_ref[...] = _attn_out_and_ffn(x_ref[...], att_ref[...], norms_ref[...], wxo_ref, wgu_ref,
                                   wdown_ref)


def post_mixer_sample(x, mix_parts, mk, mv, norms, w_out, w_xq, w_xo, w_gu, w_down):
    b, t, d = x.shape
    n = b * t
    m = mk.shape[1]
    flat = lambda a: a.reshape(n, a.shape[-1])
    full = lambda shp: pl.BlockSpec(shp, lambda i: (0,) * len(shp))
    x1, q = pl.pallas_call(
        _sample_pre_kernel,
        grid=(1,),
        in_specs=[full((n, d))] + [full((n, GROUP))] * 4
                 + [full(norms.shape), _resident(w_out.shape), _resident(w_xq.shape)],
        out_specs=[full((n, d)), full((n, d))],
        out_shape=[jax.ShapeDtypeStruct((n, d), F32)] * 2,
        compiler_params=_params("arbitrary"),
        name="sample_pre",
    )(flat(x), *[flat(p) for p in mix_parts], norms, w_out, w_xq)
    seq = pl.BlockSpec((1, t, d), lambda i: (i, 0, 0))
    mem = pl.BlockSpec((1, m, d), lambda i: (i, 0, 0))
    att = pl.pallas_call(
        functools.partial(_sample_attn_kernel, t=t),
        grid=(b,),
        in_specs=[seq, mem, mem],
        out_specs=seq,
        out_shape=jax.ShapeDtypeStruct((b, t, d), F32),
        scratch_shapes=[pltpu.VMEM((SUBLANES, d), F32)],
        compiler_params=_params("parallel"),
        name="sample_attn",
    )(q.reshape(b, t, d), mk, mv)
    out = pl.pallas_call(
        _sample_post_kernel,
        grid=(1,),
        in_specs=[full((n, d)), full((n, d)), full(norms.shape), _resident(w_xo.shape),
                  _resident(w_gu.shape), _resident(w_down.shape)],
        out_specs=full((n, d)),
        out_shape=jax.ShapeDtypeStruct((n, d), F32),
        compiler_params=_params("arbitrary"),
        name="sample_post",
    )(x1, flat(att), norms, w_xo, w_gu, w_down)
    return out.reshape(b, t, d)


def _tril(n, strict=False):
    r = lax.broadcasted_iota(jnp.int32, (n, n), 0)
    c = lax.broadcasted_iota(jnp.int32, (n, n), 1)
    return (r > c) if strict else (r >= c)


def _head_expand(n_rows, n_cols, width, offset=0):
    r = lax.broadcasted_iota(jnp.int32, (n_rows, n_cols), 0)
    c = lax.broadcasted_iota(jnp.int32, (n_rows, n_cols), 1)
    return (r == (c // width) + offset).astype(F32)


def _stage_rows(dst, src_ref, l_blk, row0=0):
    n = dst.shape[0]
    dst[row0:row0 + l_blk, :] = src_ref[0]
    if row0 + l_blk < n:
        dst[row0 + l_blk:n, :] = jnp.zeros((n - row0 - l_blk, dst.shape[1]), F32)


def _causal_conv(buf, cw_ref, cs, base=0):
    out = cw_ref[0:1, :] * buf[base + 5:base + 5 + cs, :]
    for j in range(1, CONV_W):
        out = out + cw_ref[j:j + 1, :] * buf[base + 5 + j:base + 5 + j + cs, :]
    return out


def _ssd_kernel(z_ref, x_ref, b_ref, c_ref, sm_ref, conv0_ref, h0_ref, cw_ref, cb_ref, vec_ref,
                o_ref, conv1_ref, h1_ref, buf, smbuf, hst, *, cs, l_blk, n_chunks):
    c_idx = pl.program_id(1)

    @pl.when(c_idx == 0)
    def _():
        buf[5:8, :] = conv0_ref[0]
        hst[...] = h0_ref[0]

    _stage_rows(buf.at[:, 0:GROUP], x_ref, l_blk, 8)
    _stage_rows(buf.at[:, GROUP:2 * GROUP], b_ref, l_blk, 8)
    _stage_rows(buf.at[:, 2 * GROUP:3 * GROUP], c_ref, l_blk, 8)
    _stage_rows(smbuf, sm_ref, l_blk)

    xbc = _silu(_causal_conv(buf, cw_ref, cs) + cb_ref[...])
    conv_tail = buf[5 + l_blk:8 + l_blk, :]
    buf[5:8, :] = conv_tail
    xs = xbc[:, 0:GROUP]
    bm = xbc[:, GROUP:2 * GROUP]
    cm = xbc[:, 2 * GROUP:3 * GROUP]

    dt = _softplus(smbuf[...] + vec_ref[0:1, 0:LANES])
    if l_blk < cs:
        rows = lax.broadcasted_iota(jnp.int32, (cs, LANES), 0)
        dt = jnp.where(rows < l_blk, dt, 0.0)
    a = dt * (-jnp.exp(vec_ref[1:2, 0:LANES]))
    acum = _dot_split(a, _tril(cs).astype(F32), 3, m01_left=True)
    expand = _head_expand(LANES, GROUP, HEAD_DIM)
    dt_w = _dot_split(dt, expand, 3)
    acum_w = _dot_split(acum, expand, 3)
    a_last = acum_w[cs - 1:cs, :]
    xdt = xs * dt_w
    e_acum = jnp.exp(acum_w)
    xw = xdt * jnp.exp(a_last - acum_w)
    e_last = jnp.exp(a_last)
    acum_t = acum.T
    causal = _tril(cs)

    ys = []
    for h in range(HEADS):
        g = h // (HEADS // 2)
        hs = slice(h * HEAD_DIM, (h + 1) * HEAD_DIM)
        gs = slice(g * SSM_DSTATE, (g + 1) * SSM_DSTATE)
        st = hst[h]
        dec = jnp.exp(jnp.where(causal, acum[:, h:h + 1] - acum_t[h:h + 1, :], NEG_INF))
        y = _dot(_dot_nt(cm[:, gs], bm[:, gs]) * dec, xdt[:, hs])
        y = y + _dot_nt(cm[:, gs], st) * e_acum[:, hs]
        ys.append(y)
        hst[h] = e_last[:, h * HEAD_DIM:h * HEAD_DIM + 1] * st + _dot(xw[:, hs].T, bm[:, gs])
    y = jnp.concatenate(ys, axis=-1) + vec_ref[2:3, :] * xs
    o_ref[0] = _rms(y[0:l_blk] * _silu(z_ref[0]), vec_ref[3:4, :])

    @pl.when(c_idx == n_chunks - 1)
    def _():
        conv1_ref[0] = conv_tail
        h1_ref[0] = hst[...]


def ssd_mixer(proj, conv0, h0, conv_w, conv_b, dt_bias, a_log, d_skip, norm_w):
    b, L, _ = proj.shape
    cs = SSM_CHUNK
    l_blk = min(cs, L)
    n_chunks = L // l_blk
    vec = jnp.zeros((8, GROUP), F32)
    vec = vec.at[0, :HEADS].set(dt_bias).at[1, :HEADS].set(a_log)
    vec = vec.at[2].set(jnp.repeat(d_skip, HEAD_DIM)).at[3].set(norm_w)
    col = lambda k: pl.BlockSpec((1, l_blk, GROUP), lambda i, j, k=k: (i, j, k))
    per_b = lambda shp: pl.BlockSpec((1,) + shp, lambda i, j: (i,) + (0,) * len(shp))
    c3 = 3 * GROUP
    return pl.pallas_call(
        functools.partial(_ssd_kernel, cs=cs, l_blk=l_blk, n_chunks=n_chunks),
        grid=(b, n_chunks),
        in_specs=[col(COL_SSM_Z), col(COL_SSM_X), col(COL_SSM_X + 1), col(COL_SSM_X + 2),
                  pl.BlockSpec((1, l_blk, LANES), lambda i, j: (i, j, COL_SMALL)),
                  per_b((CONV_W - 1, c3)), per_b((HEADS, HEAD_DIM, SSM_DSTATE)),
                  _resident((CONV_W, c3)), _resident((1, c3)), _resident((8, GROUP))],
        out_specs=[pl.BlockSpec((1, l_blk, GROUP), lambda i, j: (i, j, 0)),
                   per_b((CONV_W - 1, c3)), per_b((HEADS, HEAD_DIM, SSM_DSTATE))],
        out_shape=[jax.ShapeDtypeStruct((b, L, GROUP), F32),
                   jax.ShapeDtypeStruct((b, CONV_W - 1, c3), F32),
                   jax.ShapeDtypeStruct((b, HEADS, HEAD_DIM, SSM_DSTATE), F32)],
        scratch_shapes=[pltpu.VMEM((cs + 8, c3), F32), pltpu.VMEM((cs, LANES), F32),
                        pltpu.VMEM((HEADS, HEAD_DIM, SSM_DSTATE), F32)],
        compiler_params=_params("parallel", "arbitrary"),
        name="ssd_mixer",
    )(proj, proj, proj, proj, proj, conv0, h0, conv_w, conv_b.reshape(1, c3), vec)


def _dot_tn(a, b):
    return lax.dot_general(a.astype(BF16), b.astype(BF16), (((0,), (0,)), ((), ())),
                           preferred_element_type=F32)


def _head_blocks(n, width):
    r = lax.broadcasted_iota(jnp.int32, (n, n), 0)
    c = lax.broadcasted_iota(jnp.int32, (n, n), 1)
    return ((r // width) == (c // width)).astype(F32)


def _dot_split(x, m01, passes, m01_left=False):
    m = m01.astype(BF16)
    acc = None
    for _ in range(passes):
        hi = x.astype(BF16)
        part = jnp.dot(m, hi, preferred_element_type=F32) if m01_left else jnp.dot(hi, m, preferred_element_type=F32)
        acc = part if acc is None else acc + part
        x = x - hi.astype(F32)
    return acc


def _head_rms_gate(o, gate, nw_row, l_blk):
    ms = _dot_split(o * o, _head_blocks(GROUP, HEAD_DIM), 2) * (1.0 / HEAD_DIM)
    return (o * lax.rsqrt(ms + EPS) * nw_row)[0:l_blk] * _silu(gate)


def _hgrn_kernel(q_ref, f_ref, i_ref, g_ref, s0_ref, lbraw_ref, nw_ref, o_ref, s1_ref,
                 stage, sst, *, cs, l_blk, n_chunks, layer):
    c_idx = pl.program_id(1)

    @pl.when(c_idx == 0)
    def _():
        sst[...] = s0_ref[0]

    if l_blk < cs:
        _stage_rows(stage.at[0], q_ref, l_blk)
        _stage_rows(stage.at[1], f_ref, l_blk)
        _stage_rows(stage.at[2], i_ref, l_blk)
        q, fx, v = stage[0], stage[1], stage[2]
    else:
        q, fx, v = q_ref[0], f_ref[0], i_ref[0]

    raw = lbraw_ref[...]
    e = jnp.exp(raw - jnp.max(raw, axis=0, keepdims=True))
    sm = e / jnp.sum(e, axis=0, keepdims=True)
    lb = jnp.zeros((1, GROUP), F32)
    for i in range(1, layer + 1):
        lb = lb + sm[i:i + 1, :]

    log_sig = jnp.minimum(fx, 0.0) - jnp.log1p(jnp.exp(-jnp.abs(fx)))
    la = jnp.log(lb)
    lbb = jnp.log1p(-lb) + log_sig
    log_f = jnp.maximum(la, lbb) + jnp.log1p(jnp.exp(-jnp.abs(la - lbb)))
    k = (1.0 - lb) * _sigmoid(-fx)
    if l_blk < cs:
        rows = lax.broadcasted_iota(jnp.int32, (cs, GROUP), 0)
        log_f = jnp.where(rows < l_blk, log_f, 0.0)
        k = jnp.where(rows < l_blk, k, 0.0)

    bcum = _dot_f32(_tril(cs).astype(F32), log_f)
    b_mid = bcum[cs // 2 - 1:cs // 2, :]
    b_last = bcum[cs - 1:cs, :]
    qe = q * jnp.exp(bcum - b_mid)
    ke = k * jnp.exp(b_mid - bcum)
    qs = q * jnp.exp(bcum)
    kd = k * jnp.exp(b_last - bcum)
    e_last = jnp.exp(b_last)
    causal = _tril(cs)

    os_ = []
    for h in range(HEADS):
        hs = slice(h * HEAD_DIM, (h + 1) * HEAD_DIM)
        st_t = sst[h]
        att = jnp.where(causal, _dot_nt(qe[:, hs], ke[:, hs]), 0.0)
        os_.append(_dot(att, v[:, hs]) + _dot_nt(qs[:, hs], st_t))
        sst[h] = e_last[:, hs] * st_t + _dot_tn(v[:, hs], kd[:, hs])
    o = jnp.concatenate(os_, axis=-1)
    o_ref[0] = _head_rms_gate(o, g_ref[0], nw_ref[...], l_blk)

    @pl.when(c_idx == n_chunks - 1)
    def _():
        s1_ref[0] = sst[...]


def hgrn_mixer(proj, s0, lb_raw, norm_w, layer):
    b, L, _ = proj.shape
    cs = LIN_CHUNK
    l_blk = min(cs, L)
    n_chunks = L // l_blk
    col = lambda k: pl.BlockSpec((1, l_blk, GROUP), lambda i, j, k=k: (i, j, k))
    st = pl.BlockSpec((1, HEADS, HEAD_DIM, HEAD_DIM), lambda i, j: (i, 0, 0, 0))
    return pl.pallas_call(
        functools.partial(_hgrn_kernel, cs=cs, l_blk=l_blk, n_chunks=n_chunks, layer=layer),
        grid=(b, n_chunks),
        in_specs=[col(COL_HGRN), col(COL_HGRN + 1), col(COL_HGRN + 2), col(COL_HGRN + 3), st,
                  _resident(lb_raw.shape), _resident((1, GROUP))],
        out_specs=[pl.BlockSpec((1, l_blk, GROUP), lambda i, j: (i, j, 0)), st],
        out_shape=[jax.ShapeDtypeStruct((b, L, GROUP), F32),
                   jax.ShapeDtypeStruct((b, HEADS, HEAD_DIM, HEAD_DIM), F32)],
        scratch_shapes=[pltpu.VMEM((3, cs, GROUP), F32),
                        pltpu.VMEM((HEADS, HEAD_DIM, HEAD_DIM), F32)],
        compiler_params=_params("parallel", "arbitrary"),
        name="hgrn_mixer",
    )(proj, proj, proj, proj, s0, lb_raw, jnp.tile(norm_w, HEADS).reshape(1, GROUP))


LIN_PREP_CHUNKS = 4


def _chunk_rows(x, cs, cps, row):
    return jnp.concatenate(
        [jnp.broadcast_to(x[i * cs + row:i * cs + row + 1, :], (cs, x.shape[1])) for i in range(cps)], axis=0)


def _hgrn_prep_kernel(q_ref, f_ref, i_ref, lbraw_ref, oi_ref, qs_ref, kd_ref, aux_ref, *, cs, cps, layer,
                      l_valid):
    rows = cs * cps
    q, fx, v = q_ref[0], f_ref[0], i_ref[0]
    raw = lbraw_ref[...]
    e = jnp.exp(raw - jnp.max(raw, axis=0, keepdims=True))
    sm = e / jnp.sum(e, axis=0, keepdims=True)
    lb = jnp.zeros((1, GROUP), F32)
    for i in range(1, layer + 1):
        lb = lb + sm[i:i + 1, :]
    log_sig = jnp.minimum(fx, 0.0) - jnp.log1p(jnp.exp(-jnp.abs(fx)))
    la = jnp.log(lb)
    lbb = jnp.log1p(-lb) + log_sig
    log_f = jnp.maximum(la, lbb) + jnp.log1p(jnp.exp(-jnp.abs(la - lbb)))
    k = (1.0 - lb) * _sigmoid(-fx)
    if l_valid is not None:
        valid = lax.broadcasted_iota(jnp.int32, (rows, GROUP), 0) % cs < l_valid
        log_f = jnp.where(valid, log_f, 0.0)
        k = jnp.where(valid, k, 0.0)

    r = lax.broadcasted_iota(jnp.int32, (rows, rows), 0)
    c = lax.broadcasted_iota(jnp.int32, (rows, rows), 1)
    incl = jnp.logical_and(r >= c, (r // cs) == (c // cs))
    bcum = _dot_split(log_f, incl.astype(F32), 3, m01_left=True)
    b_mid = _chunk_rows(bcum, cs, cps, cs // 2 - 1)
    b_last = _chunk_rows(bcum, cs, cps, cs - 1)
    qe = q * jnp.exp(bcum - b_mid)
    ke = k * jnp.exp(b_mid - bcum)
    qs_ref[0] = q * jnp.exp(bcum)
    kd_ref[0] = k * jnp.exp(b_last - bcum)
    for i in range(cps):
        aux_ref[0, i * SUBLANES:(i + 1) * SUBLANES, :] = jnp.exp(b_last[i * cs:i * cs + SUBLANES, :])
    os_ = []
    for h in range(HEADS):
        hs = slice(h * HEAD_DIM, (h + 1) * HEAD_DIM)
        att = jnp.where(incl, _dot_nt(qe[:, hs], ke[:, hs]), 0.0)
        os_.append(_dot(att, v[:, hs]))
    oi_ref[0] = jnp.concatenate(os_, axis=-1)


def _hgrn_scan_kernel(oi_ref, qs_ref, kd_ref, aux_ref, v_ref, g_ref, s0_ref, nw_ref, o_ref, s1_ref, sst,
                      *, nb, n_chunks):
    c_idx = pl.program_id(1)
    blocks = _head_blocks(GROUP, HEAD_DIM)

    @pl.when(c_idx == 0)
    def _():
        sst[...] = jnp.zeros(sst.shape, F32)
        for bi in range(nb):
            for h in range(HEADS):
                hs = slice(h * HEAD_DIM, (h + 1) * HEAD_DIM)
                sst[bi, hs, hs] = s0_ref[bi, h]

    for bi in range(nb):
        st_t = sst[bi]
        o = oi_ref[bi] + _dot_nt(qs_ref[bi], st_t)
        sst[bi] = aux_ref[bi, 0:1, :] * st_t + _dot_tn(v_ref[bi], kd_ref[bi]) * blocks
        o_ref[bi] = _head_rms_gate(o, g_ref[bi], nw_ref[...], o.shape[0])

    @pl.when(c_idx == n_chunks - 1)
    def _():
        for bi in range(nb):
            for h in range(HEADS):
                hs = slice(h * HEAD_DIM, (h + 1) * HEAD_DIM)
                s1_ref[bi, h] = sst[bi, hs, hs]


def _hgrn_prep_call(proj, lb_raw, layer, l_valid):
    g, L, _ = proj.shape
    cs = LIN_CHUNK
    cps = LIN_PREP_CHUNKS
    rows = cs * cps
    assert L % rows == 0
    col = lambda k: pl.BlockSpec((1, rows, GROUP), lambda i, j, k=k: (i, j, k))
    out = pl.BlockSpec((1, rows, GROUP), lambda i, j: (i, j, 0))
    wide = jax.ShapeDtypeStruct((g, L, GROUP), F32)
    return pl.pallas_call(
        functools.partial(_hgrn_prep_kernel, cs=cs, cps=cps, layer=layer, l_valid=l_valid),
        grid=(g, L // rows),
        in_specs=[col(COL_HGRN), col(COL_HGRN + 1), col(COL_HGRN + 2), _resident(lb_raw.shape)],
        out_specs=[out, out, out, pl.BlockSpec((1, cps * SUBLANES, GROUP), lambda i, j: (i, j, 0))],
        out_shape=[wide, wide, wide, jax.ShapeDtypeStruct((g, L // cs * SUBLANES, GROUP), F32)],
        compiler_params=_params("parallel", "parallel"),
        name="hgrn_prep",
    )(proj, proj, proj, lb_raw)


def _hgrn_scan_call(prep, proj, s0, norm_w):
    b, L, _ = proj.shape
    cs = LIN_CHUNK
    n_chunks = L // cs
    nb = 4 if b % 4 == 0 else 1
    blk = pl.BlockSpec((nb, cs, GROUP), lambda i, c: (i, c, 0))
    pcol = lambda k: pl.BlockSpec((nb, cs, GROUP), lambda i, c, k=k: (i, c, k))
    state = pl.BlockSpec((nb, HEADS, HEAD_DIM, HEAD_DIM), lambda i, c: (i, 0, 0, 0))
    return pl.pallas_call(
        functools.partial(_hgrn_scan_kernel, nb=nb, n_chunks=n_chunks),
        grid=(b // nb, n_chunks),
        in_specs=[blk, blk, blk, pl.BlockSpec((nb, SUBLANES, GROUP), lambda i, c: (i, c, 0)),
                  pcol(COL_HGRN + 2), pcol(COL_HGRN + 3), state, _resident((1, GROUP))],
        out_specs=[blk, state],
        out_shape=[jax.ShapeDtypeStruct((b, L, GROUP), F32),
                   jax.ShapeDtypeStruct((b, HEADS, HEAD_DIM, HEAD_DIM), F32)],
        scratch_shapes=[pltpu.VMEM((nb, GROUP, GROUP), F32)],
        compiler_params=_params("parallel", "arbitrary"),
        name="hgrn_scan",
    )(*prep, proj, proj, s0, jnp.tile(norm_w, HEADS).reshape(1, GROUP))


def hgrn_mixer_long(proj, s0, lb_raw, norm_w, layer):
    return _hgrn_scan_call(_hgrn_prep_call(proj, lb_raw, layer, None), proj, s0, norm_w)


def hgrn_mixer_short(proj, s0, lb_raw, norm_w, layer):
    b, L, cols = proj.shape
    cs = LIN_CHUNK
    cps = LIN_PREP_CHUNKS
    assert L <= cs and b % cps == 0
    padded = jnp.pad(proj, ((0, 0), (0, cs - L), (0, 0)))
    prep = _hgrn_prep_call(padded.reshape(b // cps, cps * cs, cols), lb_raw, layer, L)
    o, s1 = _hgrn_scan_call([p.reshape(b, -1, GROUP) for p in prep], padded, s0, norm_w)
    return o[:, :L], s1


def _doubling_level_masks(n, block):
    r = np.arange(n)[:, None]
    c = np.arange(n)[None, :]
    out, s = [], 1
    while s < block:
        out.append((r // (2 * s) == c // (2 * s)) & ((r // s) % 2 == 1) & ((c // s) % 2 == 0))
        s *= 2
    return np.stack(out).astype(np.float32)


def _unit_lower_inverse_minus_eye(a, block=None, level_masks_ref=None):
    n = a.shape[0]
    block = n if block is None else block
    r = lax.broadcasted_iota(jnp.int32, (n, n), 0)
    c = lax.broadcasted_iota(jnp.int32, (n, n), 1)
    dx = None
    s, level = 1, 0
    while s < block:
        if level_masks_ref is None:
            lower_left = jnp.logical_and((r // (2 * s)) == (c // (2 * s)),
                                         jnp.logical_and((r // s) % 2 == 1, (c // s) % 2 == 0))
            b = jnp.where(lower_left, a, 0.0)
        else:
            b = a * level_masks_ref[level]
        if dx is None:
            dx = -b
        else:
            m = b + _dot(dx, b)
            dx = dx - m - _dot(m, dx)
        s *= 2
        level += 1
    return dx


def _gdn_kernel(q_ref, k_ref, v_ref, z_ref, sm_ref, conv0_ref, s0_ref, cw_ref, vec_ref, nw_ref,
                o_ref, conv1_ref, s1_ref, buf, smbuf, sst, *, cs, l_blk, n_chunks):
    c_idx = pl.program_id(1)

    @pl.when(c_idx == 0)
    def _():
        buf[5:8, :] = conv0_ref[0]
        sst[...] = s0_ref[0]

    _stage_rows(buf.at[:, 0:GROUP], q_ref, l_blk, 8)
    _stage_rows(buf.at[:, GROUP:2 * GROUP], k_ref, l_blk, 8)
    _stage_rows(buf.at[:, 2 * GROUP:3 * GROUP], v_ref, l_blk, 8)
    _stage_rows(smbuf, sm_ref, l_blk)

    qkv = _silu(_causal_conv(buf, cw_ref, cs))
    conv_tail = buf[5 + l_blk:8 + l_blk, :]
    buf[5:8, :] = conv_tail
    q = qkv[:, 0:GROUP]
    k = qkv[:, GROUP:2 * GROUP]
    v = qkv[:, 2 * GROUP:3 * GROUP]
    blocks = _head_blocks(GROUP, HEAD_DIM)
    q = q * lax.rsqrt(_dot_f32(q * q, blocks) + EPS) * (HEAD_DIM ** -0.5)
    k = k * lax.rsqrt(_dot_f32(k * k, blocks) + EPS)

    sm = smbuf[...]
    beta = _sigmoid(sm)
    log_g = -jnp.exp(vec_ref[1:2, :]) * _softplus(sm + vec_ref[0:1, :])
    if l_blk < cs:
        rows = lax.broadcasted_iota(jnp.int32, (cs, LANES), 0)
        beta = jnp.where(rows < l_blk, beta, 0.0)
        log_g = jnp.where(rows < l_blk, log_g, 0.0)
    gam = _dot_f32(_tril(cs).astype(F32), log_g)
    beta_w = _dot_f32(beta, _head_expand(LANES, GROUP, HEAD_DIM, HEADS))
    gam_w = _dot_f32(gam, _head_expand(LANES, GROUP, HEAD_DIM, 2 * HEADS))
    g_last = gam_w[cs - 1:cs, :]
    e_gam = jnp.exp(gam_w)
    kb = k * beta_w
    vb = v * beta_w
    kbg = kb * e_gam
    qg = q * e_gam
    k_dec = k * jnp.exp(g_last - gam_w)
    e_last = jnp.exp(g_last)
    gam_t = gam.T
    incl = _tril(cs)
    strict = _tril(cs, strict=True)

    os_ = []
    for h in range(HEADS):
        hs = slice(h * HEAD_DIM, (h + 1) * HEAD_DIM)
        r = 2 * HEADS + h
        st = sst[h]
        dec_incl = jnp.exp(jnp.where(incl, gam[:, r:r + 1] - gam_t[r:r + 1, :], NEG_INF))
        a_mat = jnp.where(strict, _dot_nt(kb[:, hs], k[:, hs]) * dec_incl, 0.0)
        tx = _unit_lower_inverse_minus_eye(a_mat)
        u = vb[:, hs] + _dot(tx, vb[:, hs])
        w = kbg[:, hs] + _dot(tx, kbg[:, hs])
        v_new = u - _dot(w, st)
        qk = _dot_nt(q[:, hs], k[:, hs]) * dec_incl
        os_.append(_dot(qk, v_new) + _dot(qg[:, hs], st))
        sst[h] = e_last[:, h * HEAD_DIM:h * HEAD_DIM + 1] * st + _dot_tn(k_dec[:, hs], v_new)
    o = jnp.concatenate(os_, axis=-1)
    o_ref[0] = _head_rms_gate(o, z_ref[0], nw_ref[...], l_blk)

    @pl.when(c_idx == n_chunks - 1)
    def _():
        conv1_ref[0] = conv_tail
        s1_ref[0] = sst[...]


def gdn_mixer(proj, conv0, s0, conv_w, a_log, dt_bias, norm_w):
    b, L, _ = proj.shape
    cs = LIN_CHUNK
    l_blk = min(cs, L)
    n_chunks = L // l_blk
    vec = jnp.zeros((8, LANES), F32)
    vec = vec.at[0, 2 * HEADS:3 * HEADS].set(dt_bias).at[1, 2 * HEADS:3 * HEADS].set(a_log)
    col = lambda k: pl.BlockSpec((1, l_blk, GROUP), lambda i, j, k=k: (i, j, k))
    per_b = lambda shp: pl.BlockSpec((1,) + shp, lambda i, j: (i,) + (0,) * len(shp))
    c3 = 3 * GROUP
    return pl.pallas_call(
        functools.partial(_gdn_kernel, cs=cs, l_blk=l_blk, n_chunks=n_chunks),
        grid=(b, n_chunks),
        in_specs=[col(COL_GDN), col(COL_GDN + 1), col(COL_GDN + 2), col(COL_GDN + 3),
                  pl.BlockSpec((1, l_blk, LANES), lambda i, j: (i, j, COL_SMALL)),
                  per_b((CONV_W - 1, c3)), per_b((HEADS, HEAD_DIM, HEAD_DIM)),
                  _resident((CONV_W, c3)), _resident((8, LANES)), _resident((1, GROUP))],
        out_specs=[pl.BlockSpec((1, l_blk, GROUP), lambda i, j: (i, j, 0)),
                   per_b((CONV_W - 1, c3)), per_b((HEADS, HEAD_DIM, HEAD_DIM))],
        out_shape=[jax.ShapeDtypeStruct((b, L, GROUP), F32),
                   jax.ShapeDtypeStruct((b, CONV_W - 1, c3), F32),
                   jax.ShapeDtypeStruct((b, HEADS, HEAD_DIM, HEAD_DIM), F32)],
        scratch_shapes=[pltpu.VMEM((cs + 8, c3), F32), pltpu.VMEM((cs, LANES), F32),
                        pltpu.VMEM((HEADS, HEAD_DIM, HEAD_DIM), F32)],
        compiler_params=_params("parallel", "arbitrary"),
        name="gdn_mixer",
    )(proj, proj, proj, proj, proj, conv0, s0, conv_w, vec,
      jnp.tile(norm_w, HEADS).reshape(1, GROUP))


GDN_PREP_CHUNKS = 4


def _gdn_prep_kernel(q_ref, k_ref, v_ref, pq_ref, pk_ref, pv_ref, sm_ref, conv0_ref, cw_ref, vec_ref,
                     lvl_ref, u_ref, w_ref, qg_ref, kd_ref, qk_ref, aux_ref, buf, *, cs, cps, l_valid):
    j = pl.program_id(1)
    rows = cs * cps

    if l_valid is None:
        @pl.when(j == 0)
        def _():
            buf[5:8, :] = conv0_ref[0]

        @pl.when(j > 0)
        def _():
            for i, ref in enumerate((pq_ref, pk_ref, pv_ref)):
                buf[5:8, i * GROUP:(i + 1) * GROUP] = ref[0, SUBLANES - 3:SUBLANES, :]

        for i, ref in enumerate((q_ref, k_ref, v_ref)):
            buf[8:8 + rows, i * GROUP:(i + 1) * GROUP] = ref[0]
        qkv = _silu(_causal_conv(buf, cw_ref, rows))
    else:
        stride = cs + SUBLANES
        parts = []
        for n in range(cps):
            buf[n * stride + 5:n * stride + 8, :] = conv0_ref[0, n * (CONV_W - 1):(n + 1) * (CONV_W - 1), :]
            for i, ref in enumerate((q_ref, k_ref, v_ref)):
                buf[n * stride + 8:n * stride + 8 + cs, i * GROUP:(i + 1) * GROUP] = ref[0, n * cs:(n + 1) * cs, :]
            parts.append(_causal_conv(buf, cw_ref, cs, n * stride))
        qkv = _silu(jnp.concatenate(parts, axis=0))
    q = qkv[:, 0:GROUP]
    k = qkv[:, GROUP:2 * GROUP]
    v = qkv[:, 2 * GROUP:3 * GROUP]
    blocks = _head_blocks(GROUP, HEAD_DIM)
    q = q * lax.rsqrt(_dot_split(q * q, blocks, 2) + EPS) * (HEAD_DIM ** -0.5)
    k = k * lax.rsqrt(_dot_split(k * k, blocks, 2) + EPS)

    sm = sm_ref[0]
    beta = _sigmoid(sm)
    log_g = -jnp.exp(vec_ref[1:2, :]) * _softplus(sm + vec_ref[0:1, :])
    if l_valid is not None:
        valid = lax.broadcasted_iota(jnp.int32, (rows, LANES), 0) % cs < l_valid
        beta = jnp.where(valid, beta, 0.0)
        log_g = jnp.where(valid, log_g, 0.0)
    r = lax.broadcasted_iota(jnp.int32, (rows, rows), 0)
    c = lax.broadcasted_iota(jnp.int32, (rows, rows), 1)
    chunk_tril = jnp.logical_and(r >= c, (r // cs) == (c // cs)).astype(F32)
    gam = _dot_split(log_g, chunk_tril, 3, m01_left=True)
    beta_w = _dot_split(beta, _head_expand(LANES, GROUP, HEAD_DIM, HEADS), 3)
    gam_w = _dot_split(gam, _head_expand(LANES, GROUP, HEAD_DIM, 2 * HEADS), 3)
    g_last = jnp.concatenate(
        [jnp.broadcast_to(gam_w[(i + 1) * cs - 1:(i + 1) * cs, :], (cs, GROUP)) for i in range(cps)], axis=0)
    e_gam = jnp.exp(gam_w)
    kb = k * beta_w
    vb = v * beta_w
    kbg = kb * e_gam
    qg_ref[0] = q * e_gam
    kd_ref[0] = k * jnp.exp(g_last - gam_w)
    for i in range(cps):
        aux_ref[0, i * SUBLANES:(i + 1) * SUBLANES, :] = jnp.exp(g_last[i * cs:i * cs + SUBLANES, :])
    gam_t = gam.T
    incl = chunk_tril > 0.0

    us, ws, qks = [], [], []
    for h in range(HEADS):
        hs = slice(h * HEAD_DIM, (h + 1) * HEAD_DIM)
        lane = 2 * HEADS + h
        dec_incl = jnp.exp(jnp.where(incl, gam[:, lane:lane + 1] - gam_t[lane:lane + 1, :], NEG_INF))
        a_mat = _dot_nt(kb[:, hs], k[:, hs]) * dec_incl
        dx = _unit_lower_inverse_minus_eye(a_mat, cs, lvl_ref)
        uw = _dot(dx, jnp.concatenate([vb[:, hs], kbg[:, hs]], axis=-1))
        us.append(vb[:, hs] + uw[:, 0:HEAD_DIM])
        ws.append(kbg[:, hs] + uw[:, HEAD_DIM:2 * HEAD_DIM])
        qk = _dot_nt(q[:, hs], k[:, hs]) * dec_incl
        qks.append(jnp.concatenate([qk[i * cs:(i + 1) * cs, i * cs:(i + 1) * cs] for i in range(cps)], axis=0))
    u_ref[0] = jnp.concatenate(us, axis=-1)
    w_ref[0] = jnp.concatenate(ws, axis=-1)
    qk_ref[0] = jnp.concatenate(qks, axis=-1)


def _gdn_scan_kernel(u_ref, w_ref, qg_ref, kd_ref, qk_ref, aux_ref, z_ref, s0_ref, nw_ref,
                     o_ref, s1_ref, sst, *, nb, n_chunks):
    c_idx = pl.program_id(1)
    blocks = _head_blocks(GROUP, HEAD_DIM)

    @pl.when(c_idx == 0)
    def _():
        sst[...] = jnp.zeros(sst.shape, F32)
        for bi in range(nb):
            for h in range(HEADS):
                hs = slice(h * HEAD_DIM, (h + 1) * HEAD_DIM)
                sst[bi, hs, hs] = s0_ref[bi, h]

    for bi in range(nb):
        st = sst[bi]
        v_new = u_ref[bi] - _dot(w_ref[bi], st)
        v_bd = jnp.concatenate([v_new] * HEADS, axis=0) * blocks
        o = _dot(qk_ref[bi], v_bd) + _dot(qg_ref[bi], st)
        sst[bi] = aux_ref[bi, 0:1, :] * st + _dot_tn(kd_ref[bi], v_new) * blocks
        o_ref[bi] = _head_rms_gate(o, z_ref[bi], nw_ref[...], o.shape[0])

    @pl.when(c_idx == n_chunks - 1)
    def _():
        for bi in range(nb):
            for h in range(HEADS):
                hs = slice(h * HEAD_DIM, (h + 1) * HEAD_DIM)
                s1_ref[bi, h] = sst[bi, hs, hs]


def _gdn_prep_call(proj, conv0, conv_w, a_log, dt_bias, l_valid):
    g, L, _ = proj.shape
    cs = LIN_CHUNK
    cps = GDN_PREP_CHUNKS
    rows = cs * cps
    assert L % rows == 0 and (l_valid is None or L == rows)
    vec = jnp.zeros((8, LANES), F32)
    vec = vec.at[0, 2 * HEADS:3 * HEADS].set(dt_bias).at[1, 2 * HEADS:3 * HEADS].set(a_log)
    c3 = 3 * GROUP
    col = lambda k: pl.BlockSpec((1, rows, GROUP), lambda i, j, k=k: (i, j, k))
    prev = lambda k: pl.BlockSpec(
        (1, SUBLANES, GROUP), lambda i, j, k=k: (i, jnp.maximum(j * (rows // SUBLANES) - 1, 0), k))
    out = pl.BlockSpec((1, rows, GROUP), lambda i, j: (i, j, 0))
    wide = jax.ShapeDtypeStruct((g, L, GROUP), F32)
    level_masks = jnp.asarray(_doubling_level_masks(rows, cs))
    return pl.pallas_call(
        functools.partial(_gdn_prep_kernel, cs=cs, cps=cps, l_valid=l_valid),
        grid=(g, L // rows),
        in_specs=[col(COL_GDN), col(COL_GDN + 1), col(COL_GDN + 2),
                  prev(COL_GDN), prev(COL_GDN + 1), prev(COL_GDN + 2),
                  pl.BlockSpec((1, rows, LANES), lambda i, j: (i, j, COL_SMALL)),
                  pl.BlockSpec((1,) + conv0.shape[1:], lambda i, j: (i, 0, 0)),
                  _resident((CONV_W, c3)), _resident((8, LANES)), _resident(level_masks.shape)],
        out_specs=[out, out, out, out, out,
                   pl.BlockSpec((1, cps * SUBLANES, GROUP), lambda i, j: (i, j, 0))],
        out_shape=[wide, wide, wide, wide, wide,
                   jax.ShapeDtypeStruct((g, L // cs * SUBLANES, GROUP), F32)],
        scratch_shapes=[pltpu.VMEM((cps * (cs + SUBLANES), c3), F32)],
        compiler_params=_params("parallel", "parallel"),
        name="gdn_prep",
    )(proj, proj, proj, proj, proj, proj, proj, conv0, conv_w, vec, level_masks)


GDN_SCAN_SEQS = 4


def _gdn_scan_call(prep, proj, s0, norm_w):
    b, L, _ = proj.shape
    cs = LIN_CHUNK
    nb = GDN_SCAN_SEQS
    n_chunks = L // cs
    assert b % nb == 0 and L % cs == 0
    blk = pl.BlockSpec((nb, cs, GROUP), lambda i, c: (i, c, 0))
    state = pl.BlockSpec((nb, HEADS, HEAD_DIM, HEAD_DIM), lambda i, c: (i, 0, 0, 0))
    return pl.pallas_call(
        functools.partial(_gdn_scan_kernel, nb=nb, n_chunks=n_chunks),
        grid=(b // nb, n_chunks),
        in_specs=[blk, blk, blk, blk, blk,
                  pl.BlockSpec((nb, SUBLANES, GROUP), lambda i, c: (i, c, 0)),
                  pl.BlockSpec((nb, cs, GROUP), lambda i, c: (i, c, COL_GDN + 3)),
                  state, _resident((1, GROUP))],
        out_specs=[blk, state],
        out_shape=[jax.ShapeDtypeStruct((b, L, GROUP), F32),
                   jax.ShapeDtypeStruct((b, HEADS, HEAD_DIM, HEAD_DIM), F32)],
        scratch_shapes=[pltpu.VMEM((nb, GROUP, GROUP), F32)],
        compiler_params=_params("parallel", "arbitrary"),
        name="gdn_scan",
    )(*prep, proj, s0, jnp.tile(norm_w, HEADS).reshape(1, GROUP))


def gdn_mixer_long(proj, conv0, s0, conv_w, a_log, dt_bias, norm_w):
    L = proj.shape[1]
    prep = _gdn_prep_call(proj, conv0, conv_w, a_log, dt_bias, None)
    o, s1 = _gdn_scan_call(prep, proj, s0, norm_w)
    conv1 = proj[:, L - (CONV_W - 1):, COL_GDN * GROUP:(COL_GDN + 3) * GROUP]
    return o, conv1, s1


def gdn_mixer_short(proj, conv0, s0, conv_w, a_log, dt_bias, norm_w):
    b, L, cols = proj.shape
    cs = LIN_CHUNK
    cps = GDN_PREP_CHUNKS
    assert CONV_W - 1 <= L <= cs and b % cps == 0
    padded = jnp.pad(proj, ((0, 0), (0, cs - L), (0, 0)))
    prep = _gdn_prep_call(padded.reshape(b // cps, cps * cs, cols),
                          conv0.reshape(b // cps, cps * (CONV_W - 1), 3 * GROUP), conv_w, a_log, dt_bias, L)
    prep = [p.reshape(b, -1, GROUP) for p in prep]
    o, s1 = _gdn_scan_call(prep, padded, s0, norm_w)
    conv1 = proj[:, L - (CONV_W - 1):, COL_GDN * GROUP:(COL_GDN + 3) * GROUP]
    return o[:, :L], conv1, s1


def _moba_prompt_kernel(q_ref, k_ref, v_ref, o_ref, ks, vts, kmean, sel_t, o_t, q_tb, alibi, ml,
                        s_buf, p_buf, *, nb):
    blk = MOBA_BLOCK
    qi = pl.program_id(1)

    @pl.when(qi == 0)
    def _():
        for j in range(nb):
            kj = k_ref[0, j * blk:(j + 1) * blk, :]
            kmean[j:j + 1, :] = jnp.mean(kj, axis=0, keepdims=True)
            kjb = kj.astype(BF16)
            for h in range(HEADS):
                ks[h, j] = kjb[:, h * HEAD_DIM:(h + 1) * HEAD_DIM]
            vts[j] = v_ref[0, j * blk:(j + 1) * blk, :].T.astype(BF16)

    q_t = (q_ref[0] * (HEAD_DIM ** -0.5)).T
    q_tb[...] = (q_t * LOG2_E).astype(BF16)
    blk_row = lax.broadcasted_iota(jnp.int32, (nb, blk), 0)
    rel = (lax.broadcasted_iota(jnp.int32, (blk, blk), 1)
           - lax.broadcasted_iota(jnp.int32, (blk, blk), 0)).astype(F32)
    for h in range(HEADS):
        alibi[h] = rel * (-ALIBI_SLOPES[h] * LOG2_E)

    for h in range(HEADS):
        hs = slice(h * HEAD_DIM, (h + 1) * HEAD_DIM)
        slope = ALIBI_SLOPES[h]
        gate = jnp.where(blk_row < qi, _dot_f32(kmean[:, hs], q_t[hs, :]), NEG_INF)
        sel = jnp.full((nb, blk), NEG_INF, F32)
        for _ in range(MOBA_TOPK):
            top = jnp.max(gate, axis=0, keepdims=True)
            first = jnp.min(jnp.where(gate == top, blk_row, nb), axis=0, keepdims=True)
            pick = blk_row == first
            sel = jnp.where(jnp.logical_and(pick, blk_row < qi), 0.0, sel)
            gate = jnp.where(pick, NEG_INF, gate)
        sel_t[h] = sel
        ml[h, 0:1, :] = jnp.full((1, blk), NEG_INF, F32)
        ml[h, 1:2, :] = jnp.zeros((1, blk), F32)
    o_t[...] = jnp.zeros(o_t.shape, F32)

    def block_step(j, own):
        for h in range(HEADS):
            hs = slice(h * HEAD_DIM, (h + 1) * HEAD_DIM)
            s_buf[h] = _dot(ks[h, j], q_tb[hs, :])
        for h in range(HEADS):
            if own:
                causal = (lax.broadcasted_iota(jnp.int32, (blk, blk), 1)
                          >= lax.broadcasted_iota(jnp.int32, (blk, blk), 0))
                s = jnp.where(causal, s_buf[h] + alibi[h], NEG_INF)
            else:
                off = (qi - j).astype(F32) * (ALIBI_SLOPES[h] * LOG2_E * blk)
                s = s_buf[h] + alibi[h] + (sel_t[h, pl.ds(j, 1), :] - off)
            m = ml[h, 0:1, :]
            m_new = jnp.maximum(m, jnp.max(s, axis=0, keepdims=True))
            p = jnp.exp2(s - m_new)
            p_buf[h] = p.astype(BF16)
            alpha = jnp.exp2(m - m_new)
            ml[h, 0:1, :] = m_new
            ml[h, 1:2, :] = alpha * ml[h, 1:2, :] + jnp.sum(p, axis=0, keepdims=True)
            ml[h, 2:3, :] = alpha
        for h in range(HEADS):
            hs = slice(h * HEAD_DIM, (h + 1) * HEAD_DIM)
            o_t[hs, :] = (ml[h, 2:3, :] * o_t[hs, :]
                          + jnp.dot(vts[j, hs, :], p_buf[h], preferred_element_type=F32))

    block_step(qi, True)

    def body(j, carry):
        block_step(j, False)
        return carry

    lax.fori_loop(0, qi, body, 0)
    for h in range(HEADS):
        hs = slice(h * HEAD_DIM, (h + 1) * HEAD_DIM)
        o_t[hs, :] = o_t[hs, :] / ml[h, 1:2, :]
    o_ref[0] = o_t[...].T


def moba_prompt(proj):
    b, S, _ = proj.shape
    blk = MOBA_BLOCK
    nb = S // blk
    seq = lambda k: pl.BlockSpec((1, S, GROUP), lambda i, j, k=k: (i, 0, k))
    return pl.pallas_call(
        functools.partial(_moba_prompt_kernel, nb=nb),
        grid=(b, nb),
        in_specs=[pl.BlockSpec((1, blk, GROUP), lambda i, j: (i, j, COL_MOBA)),
                  seq(COL_MOBA + 1), seq(COL_MOBA + 2)],
        out_specs=pl.BlockSpec((1, blk, GROUP), lambda i, j: (i, j, 0)),
        out_shape=jax.ShapeDtypeStruct((b, S, GROUP), F32),
        scratch_shapes=[pltpu.VMEM((HEADS, nb, blk, HEAD_DIM), BF16),
                        pltpu.VMEM((nb, GROUP, blk), BF16),
                        pltpu.VMEM((nb, GROUP), F32),
                        pltpu.VMEM((HEADS, nb, blk), F32),
                        pltpu.VMEM((GROUP, blk), F32),
                        pltpu.VMEM((GROUP, blk), BF16),
                        pltpu.VMEM((HEADS, blk, blk), F32),
                        pltpu.VMEM((HEADS, SUBLANES, blk), F32),
                        pltpu.VMEM((HEADS, blk, blk), F32),
                        pltpu.VMEM((HEADS, blk, blk), BF16)],
        compiler_params=_params("parallel", "arbitrary"),
        name="moba_prompt",
    )(proj, proj, proj)


PAGES_PER_BLOCK = MOBA_BLOCK // PAGE_SIZE
SELECT_PAGE_BUFFERS = 64
SELECT_UNROLL = 8


def _moba_select_kernel(pt_ref, q_ref, kc_ref, idx_ref, pages, kmean_t, q8, sems, *,
                        layer, n_pages, t):
    b = pl.program_id(0)
    nbuf = SELECT_PAGE_BUFFERS
    n_blocks = n_pages // PAGES_PER_BLOCK

    def page_copy(p, slot):
        return pltpu.make_async_copy(kc_ref.at[layer, pt_ref[b, p]], pages.at[slot], sems.at[slot])

    for s in range(nbuf):
        page_copy(s, s).start(priority=s % 2)

    kmean_t[...] = jnp.zeros(kmean_t.shape, F32)
    blk_of_lane = lax.broadcasted_iota(jnp.int32, kmean_t.shape, 1)

    pages_per_trip = SELECT_UNROLL * PAGES_PER_BLOCK

    def body(trip, carry):
        p0 = trip * pages_per_trip
        for i in range(pages_per_trip):
            page_copy(p0 + i, (p0 + i) % nbuf).wait()
        km = kmean_t[...]
        for u in range(SELECT_UNROLL):
            tot = jnp.zeros((GROUP, PAGE_SIZE), F32)
            for pp in range(PAGES_PER_BLOCK):
                tot = tot + pages[(p0 + u * PAGES_PER_BLOCK + pp) % nbuf]
            mean = jnp.sum(tot, axis=1, keepdims=True) * (1.0 / MOBA_BLOCK)
            km = jnp.where(blk_of_lane == trip * SELECT_UNROLL + u, mean, km)
        kmean_t[...] = km
        for i in range(pages_per_trip):
            @pl.when(p0 + i + nbuf < n_pages)
            def _(i=i):
                page_copy(p0 + i + nbuf, (p0 + i) % nbuf).start(priority=i % 2)
        return carry

    lax.fori_loop(0, n_blocks // SELECT_UNROLL, body, 0)

    q8[0:t, :] = q_ref[0]
    q8[t:, :] = jnp.zeros((SUBLANES - t, GROUP), F32)
    blk_lane = lax.broadcasted_iota(jnp.int32, (SUBLANES, n_blocks), 1)
    out_lane = lax.broadcasted_iota(jnp.int32, (SUBLANES, LANES), 1)
    res = jnp.zeros((SUBLANES, LANES), jnp.int32)
    for h in range(HEADS):
        hs = slice(h * HEAD_DIM, (h + 1) * HEAD_DIM)
        gate = _dot_f32(q8[:, hs], kmean_t[hs, 0:n_blocks])
        for r in range(MOBA_TOPK):
            top = jnp.max(gate, axis=1, keepdims=True)
            first = jnp.min(jnp.where(gate == top, blk_lane, n_blocks), axis=1, keepdims=True)
            res = jnp.where(out_lane == h * MOBA_TOPK + r, first, res)
            gate = jnp.where(blk_lane == first, NEG_INF, gate)
    idx_ref[0] = res


def _moba_sample_attn_kernel(pt_ref, idx_ref, q_ref, kn_ref, vn_ref, kc_ref, vc_ref, o_ref,
                             kbuf, vbuf, st8, o8, rows_s, rows_o, ksem, vsem, *, layer, past, t):
    b = pl.program_id(0)
    n_seq = pl.num_programs(0)
    n_sel = MOBA_TOPK * MOBA_BLOCK
    pairs = [(tok, h) for h in range(HEADS) for tok in range(t)]

    def block_of(seq, tok, h, r):
        return idx_ref[seq, tok * (HEADS * MOBA_TOPK) + h * MOBA_TOPK + r]

    def copies(seq, i, half):
        tok, h = pairs[i]
        out = []
        for r in range(MOBA_TOPK):
            blk = block_of(seq, tok, h, r)
            for pp in range(PAGES_PER_BLOCK):
                phys = pt_ref[seq, blk * PAGES_PER_BLOCK + pp]
                lanes = pl.ds((r * PAGES_PER_BLOCK + pp) * PAGE_SIZE, PAGE_SIZE)
                out.append(pltpu.make_async_copy(kc_ref.at[layer, phys, h], kbuf.at[half, i, :, lanes],
                                                 ksem.at[half, i]))
                out.append(pltpu.make_async_copy(vc_ref.at[layer, phys, h], vbuf.at[half, i, :, lanes],
                                                 vsem.at[half, i]))
        return out

    def start_all(seq, half):
        for i in range(len(pairs)):
            for n, c in enumerate(copies(seq, i, half)):
                c.start(priority=n % 2)

    half = b % 2

    @pl.when(b == 0)
    def _():
        start_all(b, half)

    @pl.when(b + 1 < n_seq)
    def _():
        start_all(b + 1, 1 - half)

    for i, ref in enumerate((q_ref, kn_ref, vn_ref)):
        st8[i, 0:t, :] = ref[0]
        st8[i, t:, :] = jnp.zeros((SUBLANES - t, GROUP), F32)

    lane = lax.broadcasted_iota(jnp.int32, (1, n_sel), 1)
    row = lax.broadcasted_iota(jnp.int32, (SUBLANES, 1), 0)
    for i in range(len(pairs)):
        for c in copies(b, i, half):
            c.wait()
    for i, (tok, h) in enumerate(pairs):
        hs = slice(h * HEAD_DIM, (h + 1) * HEAD_DIM)
        qrow = st8[0, tok:tok + 1, hs] * (HEAD_DIM ** -0.5)
        rows_s[i:i + 1, :] = _dot(jnp.broadcast_to(qrow, (SUBLANES, HEAD_DIM)), kbuf[half, i])[0:1]
    for i, (tok, h) in enumerate(pairs):
        hs = slice(h * HEAD_DIM, (h + 1) * HEAD_DIM)
        slope = ALIBI_SLOPES[h]
        qrow = st8[0, tok:tok + 1, hs] * (HEAD_DIM ** -0.5)
        blk = jnp.where(lane < MOBA_BLOCK, block_of(b, tok, h, 0),
                        jnp.where(lane < 2 * MOBA_BLOCK, block_of(b, tok, h, 1), block_of(b, tok, h, 2)))
        pos = blk * MOBA_BLOCK + (lane % MOBA_BLOCK)
        s_sel = rows_s[i:i + 1, :] - slope * (past + tok - pos).astype(F32)
        s_own = jnp.sum(st8[1, :, hs] * qrow, axis=1, keepdims=True)
        s_own = jnp.where(row <= tok, s_own - slope * (tok - row).astype(F32), NEG_INF)
        m = jnp.maximum(jnp.max(s_sel, axis=1, keepdims=True), jnp.max(s_own, axis=0, keepdims=True))
        p_sel = jnp.exp(s_sel - m)
        p_own = jnp.exp(s_own - m)
        l = jnp.sum(p_sel, axis=1, keepdims=True) + jnp.sum(p_own, axis=0, keepdims=True)
        rows_s[i:i + 1, :] = p_sel
        rows_o[i:i + 1, 0:HEAD_DIM] = jnp.sum(p_own * st8[2, :, hs], axis=0, keepdims=True)
        rows_o[i:i + 1, HEAD_DIM:2 * HEAD_DIM] = jnp.broadcast_to(l, (1, HEAD_DIM))
    for i, (tok, h) in enumerate(pairs):
        hs = slice(h * HEAD_DIM, (h + 1) * HEAD_DIM)
        p8 = jnp.broadcast_to(rows_s[i:i + 1, :], (SUBLANES, n_sel))
        o = _dot_nt(p8, vbuf[half, i])[0:1] + rows_o[i:i + 1, 0:HEAD_DIM]
        o8[tok:tok + 1, hs] = o / rows_o[i:i + 1, HEAD_DIM:2 * HEAD_DIM]
    o_ref[0] = o8[0:t, :]


def moba_sample(proj, k_cache, v_cache, page_table, layer):
    db, t, _ = proj.shape
    depth, pool = k_cache.shape[:2]
    n_pages = page_table.shape[1]
    n_blocks = n_pages // PAGES_PER_BLOCK
    past = n_pages * PAGE_SIZE
    assert past % MOBA_BLOCK == 0 and n_pages >= SELECT_PAGE_BUFFERS and t <= SUBLANES
    assert n_blocks % SELECT_UNROLL == 0 and SELECT_PAGE_BUFFERS % (SELECT_UNROLL * PAGES_PER_BLOCK) == 0
    assert MOBA_TOPK <= n_blocks <= LANES
    tok = lambda k: pl.BlockSpec((1, t, GROUP), lambda i, *_: (i, 0, k))
    hbm = pl.BlockSpec(memory_space=pl.ANY)
    n_pairs = t * HEADS
    idx = pl.pallas_call(
        functools.partial(_moba_select_kernel, layer=layer, n_pages=n_pages, t=t),
        grid_spec=pltpu.PrefetchScalarGridSpec(
            num_scalar_prefetch=1, grid=(db,),
            in_specs=[tok(COL_MOBA), hbm],
            out_specs=pl.BlockSpec((1, SUBLANES, LANES), lambda i, *_: (i, 0, 0)),
            scratch_shapes=[pltpu.VMEM((SELECT_PAGE_BUFFERS, GROUP, PAGE_SIZE), F32),
                            pltpu.VMEM((GROUP, LANES), F32),
                            pltpu.VMEM((SUBLANES, GROUP), F32),
                            pltpu.SemaphoreType.DMA((SELECT_PAGE_BUFFERS,))]),
        out_shape=jax.ShapeDtypeStruct((db, SUBLANES, LANES), jnp.int32),
        compiler_params=_params("arbitrary"),
        name="moba_sample_select",
    )(page_table, proj, k_cache.reshape(depth, pool, GROUP, PAGE_SIZE))
    idx = idx[:, :t, :HEADS * MOBA_TOPK].reshape(db, t * HEADS * MOBA_TOPK)
    return pl.pallas_call(
        functools.partial(_moba_sample_attn_kernel, layer=layer, past=past, t=t),
        grid_spec=pltpu.PrefetchScalarGridSpec(
            num_scalar_prefetch=2, grid=(db,),
            in_specs=[tok(COL_MOBA), tok(COL_MOBA + 1), tok(COL_MOBA + 2), hbm, hbm],
            out_specs=pl.BlockSpec((1, t, GROUP), lambda i, *_: (i, 0, 0)),
            scratch_shapes=[pltpu.VMEM((2, n_pairs, HEAD_DIM, MOBA_TOPK * MOBA_BLOCK), F32),
                            pltpu.VMEM((2, n_pairs, HEAD_DIM, MOBA_TOPK * MOBA_BLOCK), F32),
                            pltpu.VMEM((3, SUBLANES, GROUP), F32),
                            pltpu.VMEM((SUBLANES, GROUP), F32),
                            pltpu.VMEM((n_pairs, MOBA_TOPK * MOBA_BLOCK), F32),
                            pltpu.VMEM((n_pairs, LANES), F32),
                            pltpu.SemaphoreType.DMA((2, n_pairs)),
                            pltpu.SemaphoreType.DMA((2, n_pairs))]),
        out_shape=jax.ShapeDtypeStruct((db, t, GROUP), F32),
        compiler_params=_params("arbitrary"),
        name="moba_sample_attn",
    )(page_table, idx, proj, proj, proj, k_cache, v_cache)


def _rearrange_w_in(w):
    d = w.shape[0]
    dt0 = COL_HGRN * GROUP
    ba0 = dt0 + HEADS + (COL_GDN + 4 - COL_HGRN) * GROUP
    wide = jnp.concatenate([w[:, :dt0], w[:, dt0 + HEADS:ba0]], axis=1)
    narrow = jnp.concatenate([w[:, dt0:dt0 + HEADS], w[:, ba0:ba0 + 2 * HEADS],
                              jnp.zeros((d, LANES - N_SMALL), w.dtype)], axis=1)
    return jnp.concatenate([wide, narrow], axis=1)


def _layer(x, mem_k, mem_v, states, lp, layer, attend, prompt):
    b, L, d = x.shape
    ssm_conv0, ssm0, hgrn0, gdn_conv0, gdn0 = states
    w_in = lp['w_in'].astype(BF16) if prompt else lp['w_in']
    proj = norm_matmul(x.reshape(b * L, d), lp['n_mix_pre'], w_in, min(IN_PROJ_ROWS, b * L)).reshape(b, L, -1)
    o_a = attend(proj)
    o_b, ssm_conv1, ssm1 = ssd_mixer(proj, ssm_conv0, ssm0, lp['ssm_conv_w'], lp['ssm_conv_b'],
                                     lp['ssm_dt_bias'], lp['ssm_a_log'], lp['ssm_d'], lp['ssm_norm'])
    hgrn = hgrn_mixer_long if L % (LIN_PREP_CHUNKS * LIN_CHUNK) == 0 else hgrn_mixer_short
    o_c, hgrn1 = hgrn(proj, jnp.swapaxes(hgrn0, -1, -2), lp['hgrn_lb_raw'], lp['hgrn_norm'], layer)
    gdn = gdn_mixer_long if L % (GDN_PREP_CHUNKS * LIN_CHUNK) == 0 else gdn_mixer_short
    o_d, gdn_conv1, gdn1 = gdn(proj, gdn_conv0, gdn0, lp['gdn_conv_w'], lp['gdn_a_log'],
                               lp['gdn_dt_bias'], lp['gdn_norm'])
    post = post_mixer if prompt else post_mixer_sample
    args = (x, (o_a, o_b, o_c, o_d), mem_k, mem_v, lp['norms'], lp['w_out'], lp['w_xq'], lp['w_xo'],
            lp['w_gu'], lp['w_down'])
    x = post(*args, POST_MIXER_ROWS) if prompt else post(*args)
    k = proj[..., GROUP:2 * GROUP].reshape(b, L, HEADS, HEAD_DIM)
    v = proj[..., 2 * GROUP:3 * GROUP].reshape(b, L, HEADS, HEAD_DIM)
    return x, (k, v, ssm_conv1, ssm1, jnp.swapaxes(hgrn1, -1, -2), gdn_conv1, gdn1)


def kernel(x_prompt, x_sample, mem_prompt, cache_moba_k, cache_moba_v, page_table, cache_mem_k, cache_mem_v, state_ssm_conv, state_ssm, state_hgrn, state_gdn_conv, state_gdn, n_mix_pre, n_mix_post, w_in, w_out, ssm_conv_w, ssm_conv_b, ssm_dt_bias, ssm_a_log, ssm_d, ssm_norm, hgrn_lb_raw, hgrn_norm, gdn_conv_w, gdn_a_log, gdn_dt_bias, gdn_norm, n_x_pre, n_x_post, mem_norm, w_xq, w_xkv, w_xo, n_f_pre, n_f_post, w_gu, w_down):
    depth = w_in.shape[0]
    bp, _, d = x_prompt.shape
    db = x_sample.shape[0]
    n_mem = mem_prompt.shape[1]
    kc = jnp.transpose(cache_moba_k, (0, 1, 3, 4, 2))
    vc = jnp.transpose(cache_moba_v, (0, 1, 3, 4, 2))
    zeros_p = (jnp.zeros((bp,) + state_ssm_conv.shape[2:], F32), jnp.zeros((bp,) + state_ssm.shape[2:], F32),
               jnp.zeros((bp,) + state_hgrn.shape[2:], F32), jnp.zeros((bp,) + state_gdn_conv.shape[2:], F32),
               jnp.zeros((bp,) + state_gdn.shape[2:], F32))
    yp, ys = x_prompt, x_sample
    outs_p, outs_s, mem_ks, mem_vs = [], [], [], []
    for l in range(depth):
        norms = jnp.zeros((SUBLANES, d), F32)
        for i, nrm in enumerate((n_mix_post, n_x_pre, n_x_post, n_f_pre, n_f_post)):
            norms = norms.at[i].set(nrm[l])
        lp = {'n_mix_pre': n_mix_pre[l], 'w_in': _rearrange_w_in(w_in[l]), 'norms': norms,
              'w_out': w_out[l].astype(BF16), 'w_xq': w_xq[l].astype(BF16), 'w_xo': w_xo[l].astype(BF16),
              'w_gu': w_gu[l].astype(BF16), 'w_down': w_down[l].astype(BF16),
              'ssm_conv_w': ssm_conv_w[l], 'ssm_conv_b': ssm_conv_b[l], 'ssm_dt_bias': ssm_dt_bias[l],
              'ssm_a_log': ssm_a_log[l], 'ssm_d': ssm_d[l], 'ssm_norm': ssm_norm[l],
              'hgrn_lb_raw': hgrn_lb_raw, 'hgrn_norm': hgrn_norm[l], 'gdn_conv_w': gdn_conv_w[l],
              'gdn_a_log': gdn_a_log[l], 'gdn_dt_bias': gdn_dt_bias[l], 'gdn_norm': gdn_norm[l]}
        mkv = norm_matmul(mem_prompt.reshape(bp * n_mem, d), mem_norm[l], w_xkv[l].astype(BF16), 256)
        mk = mkv[:, :d].reshape(bp, n_mem, d)
        mv = mkv[:, d:].reshape(bp, n_mem, d)
        yp, st_p = _layer(yp, mk.astype(BF16), mv.astype(BF16), zeros_p, lp, l, moba_prompt, True)
        outs_p.append(st_p)
        mem_ks.append(mk.reshape(bp, n_mem, X_HEADS, d // X_HEADS))
        mem_vs.append(mv.reshape(bp, n_mem, X_HEADS, d // X_HEADS))
        states_s = (state_ssm_conv[l], state_ssm[l], state_hgrn[l], state_gdn_conv[l], state_gdn[l])
        attend_s = functools.partial(moba_sample, k_cache=kc, v_cache=vc, page_table=page_table, layer=l)
        ys, st_s = _layer(ys, cache_mem_k[l].reshape(db, n_mem, d).astype(BF16),
                          cache_mem_v[l].reshape(db, n_mem, d).astype(BF16), states_s, lp, l, attend_s, False)
        outs_s.append(st_s)
    stack = lambda outs, i: jnp.stack([o[i] for o in outs], axis=0)
    return (yp, ys, stack(outs_p, 0), stack(outs_s, 0), stack(outs_p, 1), stack(outs_s, 1),
            jnp.stack(mem_ks, axis=0), jnp.stack(mem_vs, axis=0),
            stack(outs_p, 2), stack(outs_s, 2), stack(outs_p, 3), stack(outs_s, 3),
            stack(outs_p, 4), stack(outs_s, 4), stack(outs_p, 5), stack(outs_s, 5),
            stack(outs_p, 6), stack(outs_s, 6))
```
